```python
import jax, jax.numpy as jnp
from jax import lax
import numpy as np

D_MODEL = 1024
BATCH = 1
SEQ = 16384
DEPTH = 2
DEC_BATCH = 128
DEC_SEQ = 4
PAST_LEN = 16384
PAGE_SIZE = 128

N_MIXERS = 2
N_HEADS = 16
N_KV_HEADS = 2
HEAD_DIM = 64
Q_PER_KV = N_HEADS // N_KV_HEADS
QKV_DIM = (N_HEADS + 2 * N_KV_HEADS) * HEAD_DIM
WINDOW = 128
BLOCK = WINDOW
ROPE_THETA = 10000.0
CONV_WIDTH = 3
N_GROUPS = 4
EXPERTS_PER_GROUP = 4
N_EXPERTS = N_GROUPS * EXPERTS_PER_GROUP
TOP_K_IN_GROUP = 2
D_FF_EXPERT = 512
RMS_EPS = 1e-6
N_ATTN_LAYERS = (DEPTH + N_MIXERS - 1) // N_MIXERS
N_CONV_LAYERS = DEPTH // N_MIXERS

kernel_name = "hybrid_swa_shortconv_hmoe_step"


def rmsnorm(x, g):
    xf = x.astype(jnp.float32)
    y = xf * lax.rsqrt(jnp.mean(xf * xf, axis=-1, keepdims=True) + RMS_EPS)
    return (y * g.astype(jnp.float32)).astype(x.dtype)


def rope(x, pos):
    half = HEAD_DIM // 2
    inv_freq = ROPE_THETA ** (-jnp.arange(half, dtype=jnp.float32) / half)
    ang = pos.astype(jnp.float32)[:, None] * inv_freq[None, :]
    cos = jnp.cos(ang)[None, :, None, :]
    sin = jnp.sin(ang)[None, :, None, :]
    xf = x.astype(jnp.float32)
    x1, x2 = xf[..., :half], xf[..., half:]
    out = jnp.concatenate([x1 * cos - x2 * sin, x2 * cos + x1 * sin], axis=-1)
    return out.astype(x.dtype)


def qkv_project(h, w_qkv, q_norm, k_norm, pos):
    B, T, _ = h.shape
    qkv = h @ w_qkv
    q = qkv[..., :N_HEADS * HEAD_DIM].reshape(B, T, N_HEADS, HEAD_DIM)
    k = qkv[..., N_HEADS * HEAD_DIM:(N_HEADS + N_KV_HEADS) * HEAD_DIM].reshape(B, T, N_KV_HEADS, HEAD_DIM)
    v = qkv[..., (N_HEADS + N_KV_HEADS) * HEAD_DIM:].reshape(B, T, N_KV_HEADS, HEAD_DIM)
    q = rope(rmsnorm(q, q_norm), pos)
    k = rope(rmsnorm(k, k_norm), pos)
    return q, k, v


def sink_softmax(scores, mask, sinks):
    scores = jnp.where(mask, scores, -jnp.inf)
    sink = jnp.broadcast_to(sinks.astype(jnp.float32).reshape(N_KV_HEADS, Q_PER_KV, 1, 1),
                            scores.shape[:-1] + (1,))
    probs = jax.nn.softmax(jnp.concatenate([scores, sink], axis=-1), axis=-1)
    return probs[..., :-1]


def swa_prompt(h, w_qkv, q_norm, k_norm, sinks, w_o):
    B, S, _ = h.shape
    nb = S // BLOCK
    pos = jnp.arange(S, dtype=jnp.int32)
    q, k, v = qkv_project(h, w_qkv, q_norm, k_norm, pos)
    qb = q.reshape(B, nb, BLOCK, N_KV_HEADS, Q_PER_KV, HEAD_DIM)

    def with_prev(t):
        tb = t.reshape(B, nb, BLOCK, N_KV_HEADS, HEAD_DIM)
        prev = jnp.concatenate([jnp.zeros_like(tb[:, :1]), tb[:, :-1]], axis=1)
        return jnp.concatenate([prev, tb], axis=2)

    kk, vv = with_prev(k), with_prev(v)
    scale = HEAD_DIM ** -0.5
    scores = jnp.einsum('bnqhgd,bnkhd->bnhgqk', qb, kk,
                        preferred_element_type=jnp.float32) * scale
    qi = jnp.arange(BLOCK)[:, None]
    ki = jnp.arange(2 * BLOCK)[None, :]
    dist = qi + BLOCK - ki
    band = (dist >= 0) & (dist <= WINDOW)
    has_prev = (jnp.arange(nb)[:, None, None] > 0) | (ki >= BLOCK)[None]
    mask = (band[None] & has_prev)[None, :, None, None]
    probs = sink_softmax(scores, mask, sinks)
    out = jnp.einsum('bnhgqk,bnkhd->bnqhgd', probs.astype(vv.dtype), vv)
    y = out.reshape(B, S, N_HEADS * HEAD_DIM) @ w_o
    return y, k[:, -WINDOW:], v[:, -WINDOW:]


def swa_sample(h, buf_k, buf_v, w_qkv, q_norm, k_norm, sinks, w_o):
    B, T, _ = h.shape
    n_buf = buf_k.shape[1]
    pos = PAST_LEN + jnp.arange(T, dtype=jnp.int32)
    q, k, v = qkv_project(h, w_qkv, q_norm, k_norm, pos)
    kk = jnp.concatenate([buf_k.astype(k.dtype), k], axis=1)
    vv = jnp.concatenate([buf_v.astype(v.dtype), v], axis=1)
    key_pos = PAST_LEN - n_buf + jnp.arange(n_buf + T, dtype=jnp.int32)
    qg = q.reshape(B, T, N_KV_HEADS, Q_PER_KV, HEAD_DIM)
    scale = HEAD_DIM ** -0.5
    scores = jnp.einsum('bthgd,bshd->bhgts', qg, kk,
                        preferred_element_type=jnp.float32) * scale
    dist = pos[:, None] - key_pos[None, :]
    mask = (dist >= 0) & (dist <= WINDOW)
    probs = sink_softmax(scores, mask, sinks)
    out = jnp.einsum('bhgts,bshd->bthgd', probs.astype(vv.dtype), vv)
    y = out.reshape(B, T, N_HEADS * HEAD_DIM) @ w_o
    return y, kk[:, -WINDOW:], vv[:, -WINDOW:]


def short_conv(h, buf, w_in, conv_w, w_out):
    T = h.shape[1]
    bcu = h @ w_in
    b_gate = bcu[..., :D_MODEL]
    c_gate = bcu[..., D_MODEL:2 * D_MODEL]
    u = bcu[..., 2 * D_MODEL:]
    cu = c_gate * u
    padded = jnp.concatenate([buf.astype(cu.dtype), cu], axis=1)
    conv = sum(conv_w[j] * padded[:, j:j + T] for j in range(CONV_WIDTH))
    y = (b_gate * conv) @ w_out
    return y, padded[:, -(CONV_WIDTH - 1):]


def hier_moe(h, w_group, b_group, w_router, b_router, w_gate, w_up, w_down):
    B, T, D = h.shape
    ht = h.reshape(B * T, D)
    g_logits = (ht @ w_group + b_group).astype(jnp.float32)
    g_prob = jax.nn.softmax(g_logits, axis=-1)
    g_sel = jnp.argmax(g_logits, axis=-1)
    g_w = jnp.take_along_axis(g_prob, g_sel[:, None], axis=-1)
    e_logits = (ht @ w_router + b_router).astype(jnp.float32).reshape(B * T, N_GROUPS, EXPERTS_PER_GROUP)
    e_in_group = jnp.take_along_axis(e_logits, g_sel[:, None, None], axis=1)[:, 0]
    top_v, top_i = lax.top_k(e_in_group, TOP_K_IN_GROUP)
    top_w = jax.nn.softmax(top_v, axis=-1) * g_w
    expert_id = g_sel[:, None] * EXPERTS_PER_GROUP + top_i
    gates = jnp.sum(jax.nn.one_hot(expert_id, N_EXPERTS, dtype=jnp.float32) * top_w[..., None], axis=1)
    a = jnp.einsum('nd,edf->nef', ht, w_gate)
    u = jnp.einsum('nd,edf->nef', ht, w_up)
    act = jax.nn.silu(a) * u * gates[..., None].astype(h.dtype)
    out = jnp.einsum('nef,efd->nd', act, w_down)
    return out.reshape(B, T, D)


def setup_inputs(seed: int = 0) -> dict:
    key = jax.random.key(seed)
    ks = jax.random.split(key, 24)
    f32 = jnp.float32
    nrm = lambda k, shape, s: jax.random.normal(k, shape, f32) * s
    return {
        "x_prompt": nrm(ks[0], (BATCH, SEQ, D_MODEL), 1.0),
        "x_sample": nrm(ks[1], (DEC_BATCH, DEC_SEQ, D_MODEL), 1.0),
        "cache_k": nrm(ks[2], (N_ATTN_LAYERS, DEC_BATCH, WINDOW, N_KV_HEADS, HEAD_DIM), 1.0),
        "cache_v": nrm(ks[3], (N_ATTN_LAYERS, DEC_BATCH, WINDOW, N_KV_HEADS, HEAD_DIM), 1.0),
        "state_conv": nrm(ks[4], (N_CONV_LAYERS, DEC_BATCH, CONV_WIDTH - 1, D_MODEL), 1.0),
        "norm_mix": 1.0 + nrm(ks[5], (DEPTH, D_MODEL), 0.02),
        "w_qkv": nrm(ks[6], (N_ATTN_LAYERS, D_MODEL, QKV_DIM), D_MODEL ** -0.5),
        "q_norm": 1.0 + nrm(ks[7], (N_ATTN_LAYERS, HEAD_DIM), 0.02),
        "k_norm": 1.0 + nrm(ks[8], (N_ATTN_LAYERS, HEAD_DIM), 0.02),
        "sinks": nrm(ks[9], (N_ATTN_LAYERS, N_HEADS), 0.5),
        "w_o": nrm(ks[10], (N_ATTN_LAYERS, N_HEADS * HEAD_DIM, D_MODEL), (N_HEADS * HEAD_DIM) ** -0.5),
        "w_in": nrm(ks[11], (N_CONV_LAYERS, D_MODEL, 3 * D_MODEL), D_MODEL ** -0.5),
        "conv_w": nrm(ks[12], (N_CONV_LAYERS, CONV_WIDTH, D_MODEL), CONV_WIDTH ** -0.5),
        "w_out": nrm(ks[13], (N_CONV_LAYERS, D_MODEL, D_MODEL), D_MODEL ** -0.5),
        "norm_ffn": 1.0 + nrm(ks[14], (DEPTH, D_MODEL), 0.02),
        "w_group": nrm(ks[15], (DEPTH, D_MODEL, N_GROUPS), D_MODEL ** -0.5),
        "b_group": nrm(ks[16], (DEPTH, N_GROUPS), 0.01),
        "w_router": nrm(ks[17], (DEPTH, D_MODEL, N_EXPERTS), D_MODEL ** -0.5),
        "b_router": nrm(ks[18], (DEPTH, N_EXPERTS), 0.01),
        "w_gate": nrm(ks[19], (DEPTH, N_EXPERTS, D_MODEL, D_FF_EXPERT), D_MODEL ** -0.5),
        "w_up": nrm(ks[20], (DEPTH, N_EXPERTS, D_MODEL, D_FF_EXPERT), D_MODEL ** -0.5),
        "w_down": nrm(ks[21], (DEPTH, N_EXPERTS, D_FF_EXPERT, D_MODEL), D_FF_EXPERT ** -0.5),
    }


def reference(x_prompt, x_sample, cache_k, cache_v, state_conv, norm_mix, w_qkv, q_norm, k_norm,
              sinks, w_o, w_in, conv_w, w_out, norm_ffn, w_group, b_group, w_router, b_router,
              w_gate, w_up, w_down):
    xp, xs = x_prompt, x_sample
    kp_list, vp_list, cp_list = [], [], []
    ks_list, vs_list, cs_list = [], [], []
    for i in range(DEPTH):
        j = i // N_MIXERS
        hp = rmsnorm(xp, norm_mix[i])
        hs = rmsnorm(xs, norm_mix[i])
        if i % N_MIXERS == 0:
            yp, kp, vp = swa_prompt(hp, w_qkv[j], q_norm[j], k_norm[j], sinks[j], w_o[j])
            ys, kn, vn = swa_sample(hs, cache_k[j], cache_v[j], w_qkv[j], q_norm[j], k_norm[j], sinks[j], w_o[j])
            kp_list.append(kp); vp_list.append(vp)
            ks_list.append(kn); vs_list.append(vn)
        else:
            zero_buf = jnp.zeros((hp.shape[0], CONV_WIDTH - 1, D_MODEL), hp.dtype)
            yp, cp = short_conv(hp, zero_buf, w_in[j], conv_w[j], w_out[j])
            ys, cn = short_conv(hs, state_conv[j], w_in[j], conv_w[j], w_out[j])
            cp_list.append(cp); cs_list.append(cn)
        xp = xp + yp
        xs = xs + ys
        xp = xp + hier_moe(rmsnorm(xp, norm_ffn[i]), w_group[i], b_group[i], w_router[i], b_router[i],
                           w_gate[i], w_up[i], w_down[i])
        xs = xs + hier_moe(rmsnorm(xs, norm_ffn[i]), w_group[i], b_group[i], w_router[i], b_router[i],
                           w_gate[i], w_up[i], w_down[i])
    new_k_prompt = jnp.stack(kp_list, axis=0)
    new_v_prompt = jnp.stack(vp_list, axis=0)
    new_conv_prompt = jnp.stack(cp_list, axis=0)
    new_k_sample = jnp.stack(ks_list, axis=0)
    new_v_sample = jnp.stack(vs_list, axis=0)
    new_conv_sample = jnp.stack(cs_list, axis=0)
    return (xp, xs, new_k_prompt, new_v_prompt, new_conv_prompt, new_k_sample, new_v_sample, new_conv_sample)
```

```python
import functools

import jax
import jax.numpy as jnp
from jax import lax
from jax.experimental import pallas as pl
from jax.experimental.pallas import tpu as pltpu

D = 1024
NP = 16384
NB = 128
TS = 4
NS = NB * TS
N = NP + NS
PAST = 16384
H = 16
KVH = 2
G = H // KVH
HD = 64
HALF = HD // 2
QKV = (H + 2 * KVH) * HD
WIN = 128
THETA = 10000.0
NGRP = 4
EPG = 4
NE = NGRP * EPG
F = 512
EPS = 1e-6
SCALE = HD ** -0.5

T = 512
NT = N // T
TQ = 128
BB = 8
LANES = 128
VMEM_LIMIT = 50 * 1024 * 1024

f32 = jnp.float32
bf16 = jnp.bfloat16


def _params(*sem):
    return pltpu.CompilerParams(dimension_semantics=sem, vmem_limit_bytes=VMEM_LIMIT)


def _rms(x, g):
    ms = jnp.mean(x * x, axis=-1, keepdims=True)
    return x * lax.rsqrt(ms + EPS) * g


def _qkv_kernel(x_ref, g_ref, wT_ref, qn_ref, kn_ref, cos_ref, sin_ref,
                qT_ref, ktok_ref, vtok_ref, vT_ref):
    h = _rms(x_ref[...], g_ref[...]).astype(bf16)
    qkvT = lax.dot_general(wT_ref[...], h, (((1,), (1,)), ((), ())),
                           preferred_element_type=f32)
    cos = cos_ref[...]
    sin = sin_ref[...]

    def norm_rope(blk, gcol):
        ms = jnp.mean(blk * blk, axis=0, keepdims=True)
        y = blk * lax.rsqrt(ms + EPS) * gcol
        y1 = y[:HALF]
        y2 = y[HALF:]
        return y1 * cos - y2 * sin, y2 * cos + y1 * sin

    qn = qn_ref[...]
    for hd in range(H):
        o1, o2 = norm_rope(qkvT[hd * HD:(hd + 1) * HD], qn)
        qT_ref[hd * HD:hd * HD + HALF, :] = (o1 * SCALE).astype(bf16)
        qT_ref[hd * HD + HALF:(hd + 1) * HD, :] = (o2 * SCALE).astype(bf16)
    kn = kn_ref[...]
    ks = []
    for j in range(KVH):
        o1, o2 = norm_rope(qkvT[H * HD + j * HD:H * HD + (j + 1) * HD], kn)
        ks += [o1, o2]
    kT = jnp.concatenate(ks, axis=0)
    ktok_ref[...] = kT.T
    vT = qkvT[(H + KVH) * HD:]
    vtok_ref[...] = vT.T
    vT_ref[...] = vT.astype(bf16)


def _qkv(x, g, wT, qn, kn, cos, sin):
    return pl.pallas_call(
        _qkv_kernel,
        grid=(NT,),
        in_specs=[
            pl.BlockSpec((T, D), lambda i: (i, 0)),
            pl.BlockSpec((1, D), lambda i: (0, 0)),
            pl.BlockSpec((QKV, D), lambda i: (0, 0)),
            pl.BlockSpec((HD, 1), lambda i: (0, 0)),
            pl.BlockSpec((HD, 1), lambda i: (0, 0)),
            pl.BlockSpec((HALF, T), lambda i: (0, i)),
            pl.BlockSpec((HALF, T), lambda i: (0, i)),
        ],
        out_specs=[
            pl.BlockSpec((H * HD, T), lambda i: (0, i)),
            pl.BlockSpec((T, KVH * HD), lambda i: (i, 0)),
            pl.BlockSpec((T, KVH * HD), lambda i: (i, 0)),
            pl.BlockSpec((KVH * HD, T), lambda i: (0, i)),
        ],
        out_shape=[
            jax.ShapeDtypeStruct((H * HD, N), bf16),
            jax.ShapeDtypeStruct((N, KVH * HD), f32),
            jax.ShapeDtypeStruct((N, KVH * HD), f32),
            jax.ShapeDtypeStruct((KVH * HD, N), bf16),
        ],
        compiler_params=_params("parallel"),
        name="qkv",
    )(x, g, wT, qn, kn, cos, sin)


def _attn_prompt_kernel(qT_ref, kp_ref, kc_ref, vp_ref, vc_ref, sink_ref, o_ref):
    j = pl.program_id(0)
    kk = jnp.concatenate([kp_ref[...], kc_ref[...]], axis=0).astype(bf16)
    vv = jnp.concatenate([vp_ref[...], vc_ref[...]], axis=1)
    s_idx = lax.broadcasted_iota(jnp.int32, (2 * TQ, G * TQ), 0)
    t_idx = lax.broadcasted_iota(jnp.int32, (2 * TQ, G * TQ), 1) & (TQ - 1)
    dist = t_idx + TQ - s_idx
    valid = (dist >= 0) & (dist <= WIN) & ((s_idx >= TQ) | (j > 0))
    pieces = []
    for g in range(KVH):
        qg = jnp.concatenate(
            [qT_ref[(g * G + hh) * HD:(g * G + hh + 1) * HD, :] for hh in range(G)],
            axis=1)
        zeros = jnp.zeros_like(qg)
        rhs = jnp.concatenate([qg, zeros] if g == 0 else [zeros, qg], axis=0)
        sT = jnp.dot(kk, rhs, preferred_element_type=f32)
        sT = jnp.where(valid, sT, -jnp.inf)
        sink = sink_ref[g]
        m = jnp.maximum(jnp.max(sT, axis=0, keepdims=True), sink)
        p = jnp.exp(sT - m)
        l = jnp.sum(p, axis=0, keepdims=True) + jnp.exp(sink - m)
        oT = jnp.dot(vv[g * HD:(g + 1) * HD, :], p.astype(bf16),
                     preferred_element_type=f32)
        oT = oT / l
        pieces += [oT[:, hh * TQ:(hh + 1) * TQ] for hh in range(G)]
    oT_all = jnp.concatenate(pieces, axis=0)
    o_ref[...] = oT_all.T.astype(bf16)


def _attn_prompt(qT, ktok, vT, sink_rows):
    nq = NP // TQ
    return pl.pallas_call(
        _attn_prompt_kernel,
        grid=(nq,),
        in_specs=[
            pl.BlockSpec((H * HD, TQ), lambda j: (0, j)),
            pl.BlockSpec((TQ, KVH * HD), lambda j: (jnp.maximum(j - 1, 0), 0)),
            pl.BlockSpec((TQ, KVH * HD), lambda j: (j, 0)),
            pl.BlockSpec((KVH * HD, TQ), lambda j: (0, jnp.maximum(j - 1, 0))),
            pl.BlockSpec((KVH * HD, TQ), lambda j: (0, j)),
            pl.BlockSpec((KVH, 1, G * TQ), lambda j: (0, 0, 0)),
        ],
        out_specs=pl.BlockSpec((TQ, H * HD), lambda j: (j, 0)),
        out_shape=jax.ShapeDtypeStruct((NP, H * HD), bf16),
        compiler_params=_params("parallel"),
        name="attn_prompt",
    )(qT, ktok, ktok, vT, vT, sink_rows)


def _attn_sample_kernel(q_ref, kc_ref, vc_ref, kn_ref, vn_ref, sink_ref, o_ref):
    rows = H * TS
    t1 = lax.broadcasted_iota(jnp.int32, (rows, WIN), 0) & (TS - 1)
    s1 = lax.broadcasted_iota(jnp.int32, (rows, WIN), 1)
    valid1 = s1 >= t1
    t2 = lax.broadcasted_iota(jnp.int32, (rows, 8), 0) & (TS - 1)
    s2 = lax.broadcasted_iota(jnp.int32, (rows, 8), 1)
    valid2 = s2 <= t2
    sink = sink_ref[...]
    nt = (((1,), (1,)), ((), ()))
    for b in range(BB):
        q = q_ref[b]
        sc = lax.dot_general(q, kc_ref[b].astype(bf16), nt, preferred_element_type=f32)
        sn = lax.dot_general(q, kn_ref[b].astype(bf16), nt, preferred_element_type=f32)
        sc = jnp.where(valid1, sc, -jnp.inf)
        sn = jnp.where(valid2, sn, -jnp.inf)
        m = jnp.maximum(jnp.maximum(jnp.max(sc, axis=-1, keepdims=True),
                                    jnp.max(sn, axis=-1, keepdims=True)), sink)
        pc = jnp.exp(sc - m)
        pn = jnp.exp(sn - m)
        l = (jnp.sum(pc, axis=-1, keepdims=True) + jnp.sum(pn, axis=-1, keepdims=True)
             + jnp.exp(sink - m))
        o = (jnp.dot(pc.astype(bf16), vc_ref[b].astype(bf16), preferred_element_type=f32)
             + jnp.dot(pn.astype(bf16), vn_ref[b].astype(bf16), preferred_element_type=f32))
        o_ref[b] = o / l


def _attn_sample(qbd, kc, vc, kn, vn, sink_col):
    rows = H * TS
    return pl.pallas_call(
        _attn_sample_kernel,
        grid=(NB // BB,),
        in_specs=[
            pl.BlockSpec((BB, rows, KVH * HD), lambda i: (i, 0, 0)),
            pl.BlockSpec((BB, WIN, KVH * HD), lambda i: (i, 0, 0)),
            pl.BlockSpec((BB, WIN, KVH * HD), lambda i: (i, 0, 0)),
            pl.BlockSpec((BB, 8, KVH * HD), lambda i: (i, 0, 0)),
            pl.BlockSpec((BB, 8, KVH * HD), lambda i: (i, 0, 0)),
            pl.BlockSpec((rows, 1), lambda i: (0, 0)),
        ],
        out_specs=pl.BlockSpec((BB, rows, KVH * HD), lambda i: (i, 0, 0)),
        out_shape=jax.ShapeDtypeStruct((NB, rows, KVH * HD), f32),
        compiler_params=_params("parallel"),
        name="attn_sample",
    )(qbd, kc, vc, kn, vn, sink_col)


def _proj_kernel(x_ref, o_ref, w_ref, y_ref):
    y_ref[...] = x_ref[...] + jnp.dot(o_ref[...], w_ref[...], preferred_element_type=f32)


def _proj(x, o, w):
    return pl.pallas_call(
        _proj_kernel,
        grid=(NT,),
        in_specs=[
            pl.BlockSpec((T, D), lambda i: (i, 0)),
            pl.BlockSpec((T, D), lambda i: (i, 0)),
            pl.BlockSpec((D, D), lambda i: (0, 0)),
        ],
        out_specs=pl.BlockSpec((T, D), lambda i: (i, 0)),
        out_shape=jax.ShapeDtypeStruct((N, D), f32),
        compiler_params=_params("parallel"),
        name="proj",
    )(x, o, w)


def _router_kernel(x_ref, g_ref, whi_ref, wlo_ref, b_ref, h_ref, gates_ref):
    hf = _rms(x_ref[...], g_ref[...])
    h_hi = hf.astype(bf16)
    h_ref[...] = h_hi
    h_lo = (hf - h_hi.astype(f32)).astype(bf16)
    whi = whi_ref[...]
    logits = (jnp.dot(h_hi, whi, preferred_element_type=f32)
              + jnp.dot(h_lo, whi, preferred_element_type=f32)
              + jnp.dot(h_hi, wlo_ref[...], preferred_element_type=f32)
              + b_ref[...])
    lane = lax.broadcasted_iota(jnp.int32, logits.shape, 1)
    lane_f = lane.astype(f32)
    big = float(LANES)
    gl = jnp.where(lane < NGRP, logits, -jnp.inf)
    gmax = jnp.max(gl, axis=-1, keepdims=True)
    gsel = jnp.min(jnp.where(gl == gmax, lane_f, big), axis=-1, keepdims=True)
    gsum = jnp.sum(jnp.exp(gl - gmax), axis=-1, keepdims=True)
    g_w = 1.0 / gsum
    lane_grp = ((lane - NGRP) >> 2).astype(f32)
    emask = (lane >= NGRP) & (lane < NGRP + NE) & (lane_grp == gsel)
    el = jnp.where(emask, logits, -jnp.inf)
    v1 = jnp.max(el, axis=-1, keepdims=True)
    i1 = jnp.min(jnp.where(el == v1, lane_f, big), axis=-1, keepdims=True)
    el2 = jnp.where(lane_f == i1, -jnp.inf, el)
    v2 = jnp.max(el2, axis=-1, keepdims=True)
    i2 = jnp.min(jnp.where(el2 == v2, lane_f, big), axis=-1, keepdims=True)
    e1 = jnp.exp(v2 - v1)
    den = 1.0 + e1
    w1 = (1.0 / den) * g_w
    w2 = (e1 / den) * g_w
    gates_ref[...] = jnp.where(lane_f == i1, w1, 0.0) + jnp.where(lane_f == i2, w2, 0.0)


def _router(x, g, whi, wlo, b):
    return pl.pallas_call(
        _router_kernel,
        grid=(NT,),
        in_specs=[
            pl.BlockSpec((T, D), lambda i: (i, 0)),
            pl.BlockSpec((1, D), lambda i: (0, 0)),
            pl.BlockSpec((D, LANES), lambda i: (0, 0)),
            pl.BlockSpec((D, LANES), lambda i: (0, 0)),
            pl.BlockSpec((1, LANES), lambda i: (0, 0)),
        ],
        out_specs=[
            pl.BlockSpec((T, D), lambda i: (i, 0)),
            pl.BlockSpec((T, LANES), lambda i: (i, 0)),
        ],
        out_shape=[
            jax.ShapeDtypeStruct((N, D), bf16),
            jax.ShapeDtypeStruct((N, LANES), f32),
        ],
        compiler_params=_params("parallel"),
        name="router",
    )(x, g, whi, wlo, b)


def _moe_kernel(h_ref, gates_ref, x_ref, wg_ref, wu_ref, wd_ref, o_ref):
    e = pl.program_id(1)

    @pl.when(e == 0)
    def _():
        o_ref[...] = x_ref[...]

    h = h_ref[...]
    a = jnp.dot(h, wg_ref[...], preferred_element_type=f32)
    u = jnp.dot(h, wu_ref[...], preferred_element_type=f32)
    gates = gates_ref[...]
    lane = lax.broadcasted_iota(jnp.int32, gates.shape, 1)
    gcol = jnp.sum(jnp.where(lane == e + NGRP, gates, 0.0), axis=-1, keepdims=True)
    act = (a * (1.0 / (1.0 + jnp.exp(-a)))) * u * gcol
    o_ref[...] += jnp.dot(act.astype(bf16), wd_ref[...], preferred_element_type=f32)


def _moe(h, gates, x, wg, wu, wd):
    return pl.pallas_call(
        _moe_kernel,
        grid=(NT, NE),
        in_specs=[
            pl.BlockSpec((T, D), lambda i, e: (i, 0)),
            pl.BlockSpec((T, LANES), lambda i, e: (i, 0)),
            pl.BlockSpec((T, D), lambda i, e: (i, 0)),
            pl.BlockSpec((None, D, F), lambda i, e: (e, 0, 0)),
            pl.BlockSpec((None, D, F), lambda i, e: (e, 0, 0)),
            pl.BlockSpec((None, F, D), lambda i, e: (e, 0, 0)),
        ],
        out_specs=pl.BlockSpec((T, D), lambda i, e: (i, 0)),
        out_shape=jax.ShapeDtypeStruct((N, D), f32),
        compiler_params=_params("parallel", "arbitrary"),
        name="moe",
    )(h, gates, x, wg, wu, wd)


def _conv_kernel(x_ref, g_ref, win_ref, cw_ref, wout_ref, p1_ref, p2_ref,
                 y_ref, cu_ref, pad_ref):
    i = pl.program_id(0)

    @pl.when(i == 0)
    def _():
        pad_ref[0:8, :] = jnp.zeros((8, D), f32)

    x = x_ref[...]
    h = _rms(x, g_ref[...]).astype(bf16)
    bcu = jnp.dot(h, win_ref[...], preferred_element_type=f32)
    b = bcu[:, :D]
    cu = bcu[:, D:2 * D] * bcu[:, 2 * D:]
    pad_ref[8:T + 8, :] = cu
    m1 = pad_ref[7:T + 7, :]
    m2 = pad_ref[6:T + 6, :]
    t = lax.broadcasted_iota(jnp.int32, (T, 1), 0) & (TS - 1)
    is_sample = i == NT - 1
    m1 = jnp.where(is_sample & (t == 0), p1_ref[...], m1)
    m2 = jnp.where(is_sample & (t < 2), p2_ref[...], m2)
    cw = cw_ref[...]
    conv = cw[0:1] * m2 + cw[1:2] * m1 + cw[2:3] * cu
    y = jnp.dot((b * conv).astype(bf16), wout_ref[...], preferred_element_type=f32)
    y_ref[...] = x + y
    cu_ref[...] = cu
    pad_ref[0:8, :] = cu[T - 8:, :]


def _conv(x, g, win, cw, wout, p1, p2):
    return pl.pallas_call(
        _conv_kernel,
        grid=(NT,),
        in_specs=[
            pl.BlockSpec((T, D), lambda i: (i, 0)),
            pl.BlockSpec((1, D), lambda i: (0, 0)),
            pl.BlockSpec((D, 3 * D), lambda i: (0, 0)),
            pl.BlockSpec((3, D), lambda i: (0, 0)),
            pl.BlockSpec((D, D), lambda i: (0, 0)),
            pl.BlockSpec((T, D), lambda i: (0, 0)),
            pl.BlockSpec((T, D), lambda i: (0, 0)),
        ],
        out_specs=[
            pl.BlockSpec((T, D), lambda i: (i, 0)),
            pl.BlockSpec((T, D), lambda i: (jnp.where(i == NT - 1, 1, 0), 0)),
        ],
        out_shape=[
            jax.ShapeDtypeStruct((N, D), f32),
            jax.ShapeDtypeStruct((2 * T, D), f32),
        ],
        scratch_shapes=[pltpu.VMEM((T + 8, D), f32)],
        compiler_params=_params("arbitrary"),
        name="conv",
    )(x, g, win, cw, wout, p1, p2)


def _rope_tables():
    inv_freq = THETA ** (-jnp.arange(HALF, dtype=f32) / HALF)
    pos = jnp.concatenate([
        jnp.arange(NP, dtype=jnp.int32),
        PAST + jnp.tile(jnp.arange(TS, dtype=jnp.int32), NB),
    ]).astype(f32)
    ang = inv_freq[:, None] * pos[None, :]
    return jnp.cos(ang), jnp.sin(ang)


def _router_weights(w_group, b_group, w_router, b_router):
    pad = LANES - NGRP - NE
    w = jnp.concatenate([w_group, w_router, jnp.zeros((D, pad), f32)], axis=1)
    b = jnp.concatenate([b_group, b_router, jnp.zeros((pad,), f32)])[None, :]
    whi = w.astype(bf16)
    wlo = (w - whi.astype(f32)).astype(bf16)
    return whi, wlo, b


def _moe_layer(x, i, norm_ffn, w_group, b_group, w_router, b_router, w_gate, w_up, w_down):
    whi, wlo, b = _router_weights(w_group[i], b_group[i], w_router[i], b_router[i])
    h, gates = _router(x, norm_ffn[i][None, :], whi, wlo, b)
    return _moe(h, gates, x, w_gate[i].astype(bf16), w_up[i].astype(bf16),
                w_down[i].astype(bf16))


def kernel(x_prompt, x_sample, cache_k, cache_v, state_conv, norm_mix, w_qkv, q_norm, k_norm,
           sinks, w_o, w_in, conv_w, w_out, norm_ffn, w_group, b_group, w_router, b_router,
           w_gate, w_up, w_down):
    x = jnp.concatenate([x_prompt.reshape(NP, D), x_sample.reshape(NS, D)], axis=0)
    moe_w = (norm_ffn, w_group, b_group, w_router, b_router, w_gate, w_up, w_down)

    cos, sin = _rope_tables()
    qT, ktok, vtok, vT = _qkv(x, norm_mix[0][None, :], w_qkv[0].T.astype(bf16),
                              q_norm[0][:, None], k_norm[0][:, None], cos, sin)
    sink_rows = jnp.repeat(sinks[0].reshape(KVH, G), TQ, axis=1)[:, None, :]
    o_p = _attn_prompt(qT, ktok, vT, sink_rows)

    qs = qT[:, NP:].reshape(KVH, G, HD, NB, TS).transpose(3, 0, 1, 4, 2)
    zq = jnp.zeros_like(qs[:, 0])
    qbd = jnp.stack([jnp.concatenate([qs[:, 0], zq], axis=-1),
                     jnp.concatenate([zq, qs[:, 1]], axis=-1)], axis=1)
    qbd = qbd.reshape(NB, H * TS, KVH * HD)
    k_new = ktok[NP:].reshape(NB, TS, KVH * HD)
    v_new = vtok[NP:].reshape(NB, TS, KVH * HD)
    pad4 = jnp.zeros((NB, 8 - TS, KVH * HD), f32)
    kc = cache_k[0].reshape(NB, WIN, KVH * HD)
    vc = cache_v[0].reshape(NB, WIN, KVH * HD)
    sink_col = jnp.repeat(sinks[0], TS)[:, None]
    o_s = _attn_sample(qbd, kc, vc, jnp.concatenate([k_new, pad4], axis=1),
                       jnp.concatenate([v_new, pad4], axis=1), sink_col)
    o_s = o_s.reshape(NB, KVH, G, TS, KVH, HD)
    o_s = jnp.stack([o_s[:, 0, :, :, 0], o_s[:, 1, :, :, 1]], axis=1)
    o_s = o_s.transpose(0, 3, 1, 2, 4).reshape(NS, H * HD).astype(bf16)
    x = _proj(x, jnp.concatenate([o_p, o_s], axis=0), w_o[0].astype(bf16))
    x = _moe_layer(x, 0, *moe_w)

    new_k_prompt = ktok[NP - WIN:NP].reshape(1, 1, WIN, KVH, HD)
    new_v_prompt = vtok[NP - WIN:NP].reshape(1, 1, WIN, KVH, HD)
    new_k_sample = jnp.concatenate([kc[:, TS:], k_new], axis=1).reshape(1, NB, WIN, KVH, HD)
    new_v_sample = jnp.concatenate([vc[:, TS:], v_new], axis=1).reshape(1, NB, WIN, KVH, HD)

    st = state_conv[0]
    z = jnp.zeros((NB, 1, D), f32)
    p1 = jnp.concatenate([st[:, 1:2], z, z, z], axis=1).reshape(NS, D)
    p2 = jnp.concatenate([st[:, 0:1], st[:, 1:2], z, z], axis=1).reshape(NS, D)
    x, cu = _conv(x, norm_mix[1][None, :], w_in[0].astype(bf16), conv_w[0],
                  w_out[0].astype(bf16), p1, p2)
    x = _moe_layer(x, 1, *moe_w)

    new_conv_prompt = cu[T - 2:T].reshape(1, 1, 2, D)
    new_conv_sample = cu[T:].reshape(NB, TS, D)[:, TS - 2:][None]

    y_prompt = x[:NP].reshape(1, NP, D)
    y_sample = x[NP:].reshape(NB, TS, D)
    return (y_prompt, y_sample, new_k_prompt, new_v_prompt, new_conv_prompt,
            new_k_sample, new_v_sample, new_conv_sample)
```

```python
import jax
import jax.numpy as jnp
from jax import lax
from jax.experimental import pallas as pl
from jax.experimental.pallas import tpu as pltpu

D = 1024
NP = 16384
NB = 128
TS = 4
NS = NB * TS
N = NP + NS
PAST = 16384
H = 16
KVH = 2
G = H // KVH
HD = 64
HALF = HD // 2
QKV = (H + 2 * KVH) * HD
WIN = 128
THETA = 10000.0
NGRP = 4
EPG = 4
NE = NGRP * EPG
TOPK = 2
F = 512
EPS = 1e-6
SCALE = HD ** -0.5

T = 512
NT = N // T
TQ = 128
BB = 8
LANES = 128
VMEM_LIMIT = 50 * 1024 * 1024

TM = 256
NTM = N // TM
UNIT = 16
RC = TOPK * TM + NE * UNIT
UPT = RC // UNIT
XW = D + LANES
CH = 256
UPC = CH // UNIT
NCH = -(-(NTM * (UPT - 1) + NE * (UPC - 1)) // UPC) + 1
ZERO_UNIT_IN = UPT - 1
ZERO_UNIT_OUT = (NCH - 1) * UPC

f32 = jnp.float32
bf16 = jnp.bfloat16


def _params(*sem):
    return pltpu.CompilerParams(dimension_semantics=sem, vmem_limit_bytes=VMEM_LIMIT)


def _rms(x, g):
    ms = jnp.mean(x * x, axis=-1, keepdims=True)
    return x * lax.rsqrt(ms + EPS) * g


def _qkv_kernel(x_ref, g_ref, wT_ref, qn_ref, kn_ref, cos_ref, sin_ref,
                qT_ref, ktok_ref, vtok_ref, vT_ref):
    h = _rms(x_ref[...], g_ref[...]).astype(bf16)
    qkvT = lax.dot_general(wT_ref[...], h, (((1,), (1,)), ((), ())),
                           preferred_element_type=f32)
    cos = cos_ref[...]
    sin = sin_ref[...]

    def norm_rope(blk, gcol):
        ms = jnp.mean(blk * blk, axis=0, keepdims=True)
        y = blk * lax.rsqrt(ms + EPS) * gcol
        y1 = y[:HALF]
        y2 = y[HALF:]
        return y1 * cos - y2 * sin, y2 * cos + y1 * sin

    qn = qn_ref[...]
    for hd in range(H):
        o1, o2 = norm_rope(qkvT[hd * HD:(hd + 1) * HD], qn)
        qT_ref[hd * HD:hd * HD + HALF, :] = (o1 * SCALE).astype(bf16)
        qT_ref[hd * HD + HALF:(hd + 1) * HD, :] = (o2 * SCALE).astype(bf16)
    kn = kn_ref[...]
    ks = []
    for j in range(KVH):
        o1, o2 = norm_rope(qkvT[H * HD + j * HD:H * HD + (j + 1) * HD], kn)
        ks += [o1, o2]
    kT = jnp.concatenate(ks, axis=0)
    ktok_ref[...] = kT.T
    vT = qkvT[(H + KVH) * HD:]
    vtok_ref[...] = vT.T
    vT_ref[...] = vT.astype(bf16)


def _qkv(x, g, wT, qn, kn, cos, sin):
    return pl.pallas_call(
        _qkv_kernel,
        grid=(NT,),
        in_specs=[
            pl.BlockSpec((T, D), lambda i: (i, 0)),
            pl.BlockSpec((1, D), lambda i: (0, 0)),
            pl.BlockSpec((QKV, D), lambda i: (0, 0)),
            pl.BlockSpec((HD, 1), lambda i: (0, 0)),
            pl.BlockSpec((HD, 1), lambda i: (0, 0)),
            pl.BlockSpec((HALF, T), lambda i: (0, i)),
            pl.BlockSpec((HALF, T), lambda i: (0, i)),
        ],
        out_specs=[
            pl.BlockSpec((H * HD, T), lambda i: (0, i)),
            pl.BlockSpec((T, KVH * HD), lambda i: (i, 0)),
            pl.BlockSpec((T, KVH * HD), lambda i: (i, 0)),
            pl.BlockSpec((KVH * HD, T), lambda i: (0, i)),
        ],
        out_shape=[
            jax.ShapeDtypeStruct((H * HD, N), bf16),
            jax.ShapeDtypeStruct((N, KVH * HD), f32),
            jax.ShapeDtypeStruct((N, KVH * HD), f32),
            jax.ShapeDtypeStruct((KVH * HD, N), bf16),
        ],
        compiler_params=_params("parallel"),
        name="qkv",
    )(x, g, wT, qn, kn, cos, sin)


def _attn_prompt_kernel(qT_ref, kp_ref, kc_ref, vp_ref, vc_ref, sink_ref, o_ref):
    j = pl.program_id(0)
    kk = jnp.concatenate([kp_ref[...], kc_ref[...]], axis=0).astype(bf16)
    vv = jnp.concatenate([vp_ref[...], vc_ref[...]], axis=1)
    s_idx = lax.broadcasted_iota(jnp.int32, (2 * TQ, G * TQ), 0)
    t_idx = lax.broadcasted_iota(jnp.int32, (2 * TQ, G * TQ), 1) & (TQ - 1)
    dist = t_idx + TQ - s_idx
    valid = (dist >= 0) & (dist <= WIN) & ((s_idx >= TQ) | (j > 0))
    pieces = []
    for g in range(KVH):
        qg = jnp.concatenate(
            [qT_ref[(g * G + hh) * HD:(g * G + hh + 1) * HD, :] for hh in range(G)],
            axis=1)
        zeros = jnp.zeros_like(qg)
        rhs = jnp.concatenate([qg, zeros] if g == 0 else [zeros, qg], axis=0)
        sT = jnp.dot(kk, rhs, preferred_element_type=f32)
        sT = jnp.where(valid, sT, -jnp.inf)
        sink = sink_ref[g]
        m = jnp.maximum(jnp.max(sT, axis=0, keepdims=True), sink)
        p = jnp.exp(sT - m)
        l = jnp.sum(p, axis=0, keepdims=True) + jnp.exp(sink - m)
        oT = jnp.dot(vv[g * HD:(g + 1) * HD, :], p.astype(bf16),
                     preferred_element_type=f32)
        oT = oT / l
        pieces += [oT[:, hh * TQ:(hh + 1) * TQ] for hh in range(G)]
    oT_all = jnp.concatenate(pieces, axis=0)
    o_ref[...] = oT_all.T.astype(bf16)


def _attn_prompt(qT, ktok, vT, sink_rows):
    nq = NP // TQ
    return pl.pallas_call(
        _attn_prompt_kernel,
        grid=(nq,),
        in_specs=[
            pl.BlockSpec((H * HD, TQ), lambda j: (0, j)),
            pl.BlockSpec((TQ, KVH * HD), lambda j: (jnp.maximum(j - 1, 0), 0)),
            pl.BlockSpec((TQ, KVH * HD), lambda j: (j, 0)),
            pl.BlockSpec((KVH * HD, TQ), lambda j: (0, jnp.maximum(j - 1, 0))),
            pl.BlockSpec((KVH * HD, TQ), lambda j: (0, j)),
            pl.BlockSpec((KVH, 1, G * TQ), lambda j: (0, 0, 0)),
        ],
        out_specs=pl.BlockSpec((TQ, H * HD), lambda j: (j, 0)),
        out_shape=jax.ShapeDtypeStruct((NP, H * HD), bf16),
        compiler_params=_params("parallel"),
        name="attn_prompt",
    )(qT, ktok, ktok, vT, vT, sink_rows)


def _attn_sample_kernel(q_ref, kc_ref, vc_ref, kn_ref, vn_ref, sink_ref, o_ref):
    rows = H * TS
    t1 = lax.broadcasted_iota(jnp.int32, (rows, WIN), 0) & (TS - 1)
    s1 = lax.broadcasted_iota(jnp.int32, (rows, WIN), 1)
    valid1 = s1 >= t1
    t2 = lax.broadcasted_iota(jnp.int32, (rows, 8), 0) & (TS - 1)
    s2 = lax.broadcasted_iota(jnp.int32, (rows, 8), 1)
    valid2 = s2 <= t2
    sink = sink_ref[...]
    nt = (((1,), (1,)), ((), ()))
    for b in range(BB):
        q = q_ref[b]
        sc = lax.dot_general(q, kc_ref[b].astype(bf16), nt, preferred_element_type=f32)
        sn = lax.dot_general(q, kn_ref[b].astype(bf16), nt, preferred_element_type=f32)
        sc = jnp.where(valid1, sc, -jnp.inf)
        sn = jnp.where(valid2, sn, -jnp.inf)
        m = jnp.maximum(jnp.maximum(jnp.max(sc, axis=-1, keepdims=True),
                                    jnp.max(sn, axis=-1, keepdims=True)), sink)
        pc = jnp.exp(sc - m)
        pn = jnp.exp(sn - m)
        l = (jnp.sum(pc, axis=-1, keepdims=True) + jnp.sum(pn, axis=-1, keepdims=True)
             + jnp.exp(sink - m))
        o = (jnp.dot(pc.astype(bf16), vc_ref[b].astype(bf16), preferred_element_type=f32)
             + jnp.dot(pn.astype(bf16), vn_ref[b].astype(bf16), preferred_element_type=f32))
        o_ref[b] = o / l


def _attn_sample(qbd, kc, vc, kn, vn, sink_col):
    rows = H * TS
    return pl.pallas_call(
        _attn_sample_kernel,
        grid=(NB // BB,),
        in_specs=[
            pl.BlockSpec((BB, rows, KVH * HD), lambda i: (i, 0, 0)),
            pl.BlockSpec((BB, WIN, KVH * HD), lambda i: (i, 0, 0)),
            pl.BlockSpec((BB, WIN, KVH * HD), lambda i: (i, 0, 0)),
            pl.BlockSpec((BB, 8, KVH * HD), lambda i: (i, 0, 0)),
            pl.BlockSpec((BB, 8, KVH * HD), lambda i: (i, 0, 0)),
            pl.BlockSpec((rows, 1), lambda i: (0, 0)),
        ],
        out_specs=pl.BlockSpec((BB, rows, KVH * HD), lambda i: (i, 0, 0)),
        out_shape=jax.ShapeDtypeStruct((NB, rows, KVH * HD), f32),
        compiler_params=_params("parallel"),
        name="attn_sample",
    )(qbd, kc, vc, kn, vn, sink_col)


def _proj_kernel(x_ref, o_ref, w_ref, y_ref):
    y_ref[...] = x_ref[...] + jnp.dot(o_ref[...], w_ref[...], preferred_element_type=f32)


def _proj(x, o, w):
    return pl.pallas_call(
        _proj_kernel,
        grid=(NT,),
        in_specs=[
            pl.BlockSpec((T, D), lambda i: (i, 0)),
            pl.BlockSpec((T, D), lambda i: (i, 0)),
            pl.BlockSpec((D, D), lambda i: (0, 0)),
        ],
        out_specs=pl.BlockSpec((T, D), lambda i: (i, 0)),
        out_shape=jax.ShapeDtypeStruct((N, D), f32),
        compiler_params=_params("parallel"),
        name="proj",
    )(x, o, w)


def _dispatch_kernel(x_ref, g_ref, whi_ref, wlo_ref, b_ref, xc_ref, meta_ref, cnt_ref):
    hf = _rms(x_ref[...], g_ref[...])
    h_hi = hf.astype(bf16)
    h_lo = (hf - h_hi.astype(f32)).astype(bf16)
    whi = whi_ref[...]
    logits = (jnp.dot(h_hi, whi, preferred_element_type=f32)
              + jnp.dot(h_lo, whi, preferred_element_type=f32)
              + jnp.dot(h_hi, wlo_ref[...], preferred_element_type=f32)
              + b_ref[...])
    lane = lax.broadcasted_iota(jnp.int32, logits.shape, 1)
    lane_f = lane.astype(f32)
    big = float(LANES)
    gl = jnp.where(lane < NGRP, logits, -jnp.inf)
    gmax = jnp.max(gl, axis=-1, keepdims=True)
    gsel = jnp.min(jnp.where(gl == gmax, lane_f, big), axis=-1, keepdims=True)
    gsum = jnp.sum(jnp.exp(gl - gmax), axis=-1, keepdims=True)
    g_w = 1.0 / gsum
    lane_grp = ((lane - NGRP) >> 2).astype(f32)
    emask = (lane >= NGRP) & (lane < NGRP + NE) & (lane_grp == gsel)
    el = jnp.where(emask, logits, -jnp.inf)
    v1 = jnp.max(el, axis=-1, keepdims=True)
    i1 = jnp.min(jnp.where(el == v1, lane_f, big), axis=-1, keepdims=True)
    el2 = jnp.where(lane_f == i1, -jnp.inf, el)
    v2 = jnp.max(el2, axis=-1, keepdims=True)
    i2 = jnp.min(jnp.where(el2 == v2, lane_f, big), axis=-1, keepdims=True)
    e1 = jnp.exp(v2 - v1)
    den = 1.0 + e1
    w1 = (1.0 / den) * g_w
    w2 = (e1 / den) * g_w

    m1 = lane_f == i1
    m2 = lane_f == i2
    sel = jnp.where(m1 | m2, 1.0, 0.0)
    r_i = lax.broadcasted_iota(jnp.int32, (TM, TM), 0)
    c_i = lax.broadcasted_iota(jnp.int32, (TM, TM), 1)
    before = jnp.where(c_i < r_i, 1.0, 0.0).astype(bf16)
    ranks = jnp.dot(before, sel.astype(bf16), preferred_element_type=f32)
    counts = jnp.sum(sel, axis=0, keepdims=True)
    padded = jnp.floor((counts + (UNIT - 1.0)) * (1.0 / UNIT)) * UNIT
    k_i = lax.broadcasted_iota(jnp.int32, (LANES, LANES), 0)
    l_i = lax.broadcasted_iota(jnp.int32, (LANES, LANES), 1)
    lower_lanes = jnp.where(k_i < l_i, 1.0, 0.0).astype(bf16)
    seg = jnp.dot(jnp.broadcast_to(padded, (8, LANES)).astype(bf16), lower_lanes,
                  preferred_element_type=f32)[0:1]
    posall = seg + ranks
    pos1 = jnp.sum(jnp.where(m1, posall, 0.0), axis=-1, keepdims=True)
    pos2 = jnp.sum(jnp.where(m2, posall, 0.0), axis=-1, keepdims=True)
    meta = jnp.where(lane == 0, pos1, jnp.where(lane == 1, pos2, 0.0))
    meta_ref[...] = meta
    cnt_ref[...] = jnp.broadcast_to(counts, (8, LANES))

    metaT = meta.T
    rr = lax.broadcasted_iota(jnp.int32, (RC, TM), 0).astype(f32)
    p1 = rr == metaT[0:1, :]
    p2 = rr == metaT[1:2, :]
    xc_ref[:, :D] = jnp.dot(jnp.where(p1 | p2, 1.0, 0.0).astype(bf16), h_hi,
                            preferred_element_type=f32).astype(bf16)
    w1_hi = w1.astype(bf16).astype(f32)
    w2_hi = w2.astype(bf16).astype(f32)
    g1 = jnp.where(lane == 0, w1_hi, jnp.where(lane == 1, w1 - w1_hi, 0.0)).astype(bf16)
    g2 = jnp.where(lane == 0, w2_hi, jnp.where(lane == 1, w2 - w2_hi, 0.0)).astype(bf16)
    gc = (jnp.dot(jnp.where(p1, 1.0, 0.0).astype(bf16), g1, preferred_element_type=f32)
          + jnp.dot(jnp.where(p2, 1.0, 0.0).astype(bf16), g2, preferred_element_type=f32))
    xc_ref[:, D:] = gc.astype(bf16)


def _dispatch(x, g, whi, wlo, b):
    return pl.pallas_call(
        _dispatch_kernel,
        grid=(NTM,),
        in_specs=[
            pl.BlockSpec((TM, D), lambda i: (i, 0)),
            pl.BlockSpec((1, D), lambda i: (0, 0)),
            pl.BlockSpec((D, LANES), lambda i: (0, 0)),
            pl.BlockSpec((D, LANES), lambda i: (0, 0)),
            pl.BlockSpec((1, LANES), lambda i: (0, 0)),
        ],
        out_specs=[
            pl.BlockSpec((RC, XW), lambda i: (i, 0)),
            pl.BlockSpec((TM, LANES), lambda i: (i, 0)),
            pl.BlockSpec((8, LANES), lambda i: (i, 0)),
        ],
        out_shape=[
            jax.ShapeDtypeStruct((NTM * RC, XW), bf16),
            jax.ShapeDtypeStruct((N, LANES), f32),
            jax.ShapeDtypeStruct((NTM * 8, LANES), f32),
        ],
        compiler_params=_params("parallel"),
        name="dispatch",
    )(x, g, whi, wlo, b)


def _dispatch_tables(cnt):
    i32 = jnp.int32
    n = cnt.reshape(NTM, 8, LANES)[:, 0, NGRP:NGRP + NE].astype(i32)
    units = (n + UNIT - 1) // UNIT
    seg_end = jnp.cumsum(units, axis=1)
    seg_start = seg_end - units
    col_end = jnp.cumsum(units, axis=0)
    col_start = col_end - units
    chunks = (col_end[-1] + UPC - 1) // UPC
    ch_end = jnp.cumsum(chunks)
    ch_start = ch_end - chunks
    nused = ch_end[-1]
    c = jnp.arange(NCH, dtype=i32)
    eid = jnp.minimum(jnp.sum((ch_end[None, :] <= c[:, None]).astype(i32), axis=1), NE - 1)

    k = jnp.arange(NCH * UPC, dtype=i32)
    ek = eid[k // UPC]
    q = k - ch_start[ek] * UPC
    tile = jnp.sum((col_end.T[ek] <= q[:, None]).astype(i32), axis=1)
    valid = (tile < NTM) & (k // UPC < nused)
    tc = jnp.minimum(tile, NTM - 1)
    src = tc * UPT + seg_start[tc, ek] + (q - col_start[tc, ek])
    ffn_src = jnp.where(valid, src, ZERO_UNIT_IN)

    v = jnp.arange(UPT, dtype=i32)
    ev = jnp.sum((seg_end[:, None, :] <= v[None, :, None]).astype(i32), axis=2)
    valid_v = ev < NE
    ec = jnp.minimum(ev, NE - 1)
    slot = (ch_start[ec] * UPC + jnp.take_along_axis(col_start, ec, axis=1)
            + v[None, :] - jnp.take_along_axis(seg_start, ec, axis=1))
    comb_src = jnp.where(valid_v, slot, ZERO_UNIT_OUT).reshape(-1)
    return eid, nused.reshape(1), ffn_src, comb_src


def _unit_copies(src_ref, base, n_units, src_hbm, stage, slot, sem):
    out = []
    for j in range(n_units):
        row = pl.multiple_of(src_ref[base + j] * UNIT, UNIT)
        out.append(pltpu.make_async_copy(
            src_hbm.at[pl.ds(row, UNIT), :],
            stage.at[slot, pl.ds(j * UNIT, UNIT), :],
            sem.at[slot]))
    return out


def _ffn_kernel(eid_ref, nused_ref, src_ref, xc_hbm, wg_ref, wu_ref, wd_ref, o_ref,
                stage, sem, wg_b, wu_b, wd_b):
    c = pl.program_id(0)
    nused = nused_ref[0]
    slot = lax.rem(c, 2)

    @pl.when((c == 0) & (nused > 0))
    def _():
        for cp in _unit_copies(src_ref, 0, UPC, xc_hbm, stage, 0, sem):
            cp.start()

    @pl.when(c + 1 < nused)
    def _():
        for cp in _unit_copies(src_ref, (c + 1) * UPC, UPC, xc_hbm, stage, 1 - slot, sem):
            cp.start()

    @pl.when(c < nused)
    def _():
        for cp in _unit_copies(src_ref, c * UPC, UPC, xc_hbm, stage, slot, sem):
            cp.wait()

        @pl.when((c == 0) | (eid_ref[c] != eid_ref[jnp.maximum(c - 1, 0)]))
        def _():
            wg_b[...] = wg_ref[...].astype(bf16)
            wu_b[...] = wu_ref[...].astype(bf16)
            wd_b[...] = wd_ref[...].astype(bf16)

        xs = stage[slot]
        x = xs[:, :D]
        gb = xs[:, D:].astype(f32)
        gate = gb[:, 0:1] + gb[:, 1:2]
        a = jnp.dot(x, wg_b[...], preferred_element_type=f32)
        u = jnp.dot(x, wu_b[...], preferred_element_type=f32)
        act = (a * (1.0 / (1.0 + jnp.exp(-a)))) * u * gate
        o_ref[...] = jnp.dot(act.astype(bf16), wd_b[...],
                             preferred_element_type=f32).astype(bf16)

    @pl.when(c >= nused)
    def _():
        o_ref[...] = jnp.zeros((CH, D), bf16)


def _ffn(eid, nused, ffn_src, xc, wg, wu, wd):
    return pl.pallas_call(
        _ffn_kernel,
        grid_spec=pltpu.PrefetchScalarGridSpec(
            num_scalar_prefetch=3,
            grid=(NCH,),
            in_specs=[
                pl.BlockSpec(memory_space=pl.ANY),
                pl.BlockSpec((None, D, F), lambda c, eid, nu, src: (eid[c], 0, 0)),
                pl.BlockSpec((None, D, F), lambda c, eid, nu, src: (eid[c], 0, 0)),
                pl.BlockSpec((None, F, D), lambda c, eid, nu, src: (eid[c], 0, 0)),
            ],
            out_specs=pl.BlockSpec((CH, D), lambda c, eid, nu, src: (c, 0)),
            scratch_shapes=[
                pltpu.VMEM((2, CH, XW), bf16),
                pltpu.SemaphoreType.DMA((2,)),
                pltpu.VMEM((D, F), bf16),
                pltpu.VMEM((D, F), bf16),
                pltpu.VMEM((F, D), bf16),
            ],
        ),
        out_shape=jax.ShapeDtypeStruct((NCH * CH, D), bf16),
        compiler_params=_params("arbitrary"),
        name="ffn",
    )(eid, nused, ffn_src, xc, wg, wu, wd)


def _combine_kernel(src_ref, x_ref, meta_ref, o_hbm, y_ref, stage, sem):
    i = pl.program_id(0)
    slot = lax.rem(i, 2)

    @pl.when(i == 0)
    def _():
        for cp in _unit_copies(src_ref, 0, UPT, o_hbm, stage, 0, sem):
            cp.start()

    @pl.when(i + 1 < NTM)
    def _():
        for cp in _unit_copies(src_ref, (i + 1) * UPT, UPT, o_hbm, stage, 1 - slot, sem):
            cp.start()

    for cp in _unit_copies(src_ref, i * UPT, UPT, o_hbm, stage, slot, sem):
        cp.wait()

    meta = meta_ref[...]
    pos1 = meta[:, 0:1]
    pos2 = meta[:, 1:2]
    li = lax.broadcasted_iota(jnp.int32, (TM, RC), 1).astype(f32)
    pt = jnp.where((li == pos1) | (li == pos2), 1.0, 0.0).astype(bf16)
    y_ref[...] = x_ref[...] + jnp.dot(pt, stage[slot], preferred_element_type=f32)


def _combine(comb_src, x, meta, o_sorted):
    return pl.pallas_call(
        _combine_kernel,
        grid_spec=pltpu.PrefetchScalarGridSpec(
            num_scalar_prefetch=1,
            grid=(NTM,),
            in_specs=[
                pl.BlockSpec((TM, D), lambda i, src: (i, 0)),
                pl.BlockSpec((TM, LANES), lambda i, src: (i, 0)),
                pl.BlockSpec(memory_space=pl.ANY),
            ],
            out_specs=pl.BlockSpec((TM, D), lambda i, src: (i, 0)),
            scratch_shapes=[
                pltpu.VMEM((2, RC, D), bf16),
                pltpu.SemaphoreType.DMA((2,)),
            ],
        ),
        out_shape=jax.ShapeDtypeStruct((N, D), f32),
        compiler_params=_params("arbitrary"),
        name="combine",
    )(comb_src, x, meta, o_sorted)


def _conv_kernel(x_ref, g_ref, win_ref, cw_ref, wout_ref, p1_ref, p2_ref,
                 y_ref, cu_ref, pad_ref):
    i = pl.program_id(0)

    @pl.when(i == 0)
    def _():
        pad_ref[0:8, :] = jnp.zeros((8, D), f32)

    x = x_ref[...]
    h = _rms(x, g_ref[...]).astype(bf16)
    bcu = jnp.dot(h, win_ref[...], preferred_element_type=f32)
    b = bcu[:, :D]
    cu = bcu[:, D:2 * D] * bcu[:, 2 * D:]
    pad_ref[8:T + 8, :] = cu
    m1 = pad_ref[7:T + 7, :]
    m2 = pad_ref[6:T + 6, :]
    t = lax.broadcasted_iota(jnp.int32, (T, 1), 0) & (TS - 1)
    is_sample = i == NT - 1
    m1 = jnp.where(is_sample & (t == 0), p1_ref[...], m1)
    m2 = jnp.where(is_sample & (t < 2), p2_ref[...], m2)
    cw = cw_ref[...]
    conv = cw[0:1] * m2 + cw[1:2] * m1 + cw[2:3] * cu
    y = jnp.dot((b * conv).astype(bf16), wout_ref[...], preferred_element_type=f32)
    y_ref[...] = x + y
    cu_ref[...] = cu
    pad_ref[0:8, :] = cu[T - 8:, :]


def _conv(x, g, win, cw, wout, p1, p2):
    return pl.pallas_call(
        _conv_kernel,
        grid=(NT,),
        in_specs=[
            pl.BlockSpec((T, D), lambda i: (i, 0)),
            pl.BlockSpec((1, D), lambda i: (0, 0)),
            pl.BlockSpec((D, 3 * D), lambda i: (0, 0)),
            pl.BlockSpec((3, D), lambda i: (0, 0)),
            pl.BlockSpec((D, D), lambda i: (0, 0)),
            pl.BlockSpec((T, D), lambda i: (0, 0)),
            pl.BlockSpec((T, D), lambda i: (0, 0)),
        ],
        out_specs=[
            pl.BlockSpec((T, D), lambda i: (i, 0)),
            pl.BlockSpec((T, D), lambda i: (jnp.where(i == NT - 1, 1, 0), 0)),
        ],
        out_shape=[
            jax.ShapeDtypeStruct((N, D), f32),
            jax.ShapeDtypeStruct((2 * T, D), f32),
        ],
        scratch_shapes=[pltpu.VMEM((T + 8, D), f32)],
        compiler_params=_params("arbitrary"),
        name="conv",
    )(x, g, win, cw, wout, p1, p2)


def _rope_tables():
    inv_freq = THETA ** (-jnp.arange(HALF, dtype=f32) / HALF)
    pos = jnp.concatenate([
        jnp.arange(NP, dtype=jnp.int32),
        PAST + jnp.tile(jnp.arange(TS, dtype=jnp.int32), NB),
    ]).astype(f32)
    ang = inv_freq[:, None] * pos[None, :]
    return jnp.cos(ang), jnp.sin(ang)


def _router_weights(w_group, b_group, w_router, b_router):
    pad = LANES - NGRP - NE
    w = jnp.concatenate([w_group, w_router, jnp.zeros((D, pad), f32)], axis=1)
    b = jnp.concatenate([b_group, b_router, jnp.zeros((pad,), f32)])[None, :]
    whi = w.astype(bf16)
    wlo = (w - whi.astype(f32)).astype(bf16)
    return whi, wlo, b


def _moe_layer(x, i, norm_ffn, w_group, b_group, w_router, b_router, w_gate, w_up, w_down):
    whi, wlo, b = _router_weights(w_group[i], b_group[i], w_router[i], b_router[i])
    xc, meta, cnt = _dispatch(x, norm_ffn[i][None, :], whi, wlo, b)
    eid, nused, ffn_src, comb_src = _dispatch_tables(cnt)
    o_sorted = _ffn(eid, nused, ffn_src, xc, w_gate[i], w_up[i], w_down[i])
    return _combine(comb_src, x, meta, o_sorted)


def kernel(x_prompt, x_sample, cache_k, cache_v, state_conv, norm_mix, w_qkv, q_norm, k_norm,
           sinks, w_o, w_in, conv_w, w_out, norm_ffn, w_group, b_group, w_router, b_router,
           w_gate, w_up, w_down):
    x = jnp.concatenate([x_prompt.reshape(NP, D), x_sample.reshape(NS, D)], axis=0)
    moe_w = (norm_ffn, w_group, b_group, w_router, b_router, w_gate, w_up, w_down)

    cos, sin = _rope_tables()
    qT, ktok, vtok, vT = _qkv(x, norm_mix[0][None, :], w_qkv[0].T.astype(bf16),
                              q_norm[0][:, None], k_norm[0][:, None], cos, sin)
    sink_rows = jnp.repeat(sinks[0].reshape(KVH, G), TQ, axis=1)[:, None, :]
    o_p = _attn_prompt(qT, ktok, vT, sink_rows)

    qs = qT[:, NP:].reshape(KVH, G, HD, NB, TS).transpose(3, 0, 1, 4, 2)
    zq = jnp.zeros_like(qs[:, 0])
    qbd = jnp.stack([jnp.concatenate([qs[:, 0], zq], axis=-1),
                     jnp.concatenate([zq, qs[:, 1]], axis=-1)], axis=1)
    qbd = qbd.reshape(NB, H * TS, KVH * HD)
    k_new = ktok[NP:].reshape(NB, TS, KVH * HD)
    v_new = vtok[NP:].reshape(NB, TS, KVH * HD)
    pad4 = jnp.zeros((NB, 8 - TS, KVH * HD), f32)
    kc = cache_k[0].reshape(NB, WIN, KVH * HD)
    vc = cache_v[0].reshape(NB, WIN, KVH * HD)
    sink_col = jnp.repeat(sinks[0], TS)[:, None]
    o_s = _attn_sample(qbd, kc, vc, jnp.concatenate([k_new, pad4], axis=1),
                       jnp.concatenate([v_new, pad4], axis=1), sink_col)
    o_s = o_s.reshape(NB, KVH, G, TS, KVH, HD)
    o_s = jnp.stack([o_s[:, 0, :, :, 0], o_s[:, 1, :, :, 1]], axis=1)
    o_s = o_s.transpose(0, 3, 1, 2, 4).reshape(NS, H * HD).astype(bf16)
    x = _proj(x, jnp.concatenate([o_p, o_s], axis=0), w_o[0].astype(bf16))
    x = _moe_layer(x, 0, *moe_w)

    new_k_prompt = ktok[NP - WIN:NP].reshape(1, 1, WIN, KVH, HD)
    new_v_prompt = vtok[NP - WIN:NP].reshape(1, 1, WIN, KVH, HD)
    new_k_sample = jnp.concatenate([kc[:, TS:], k_new], axis=1).reshape(1, NB, WIN, KVH, HD)
    new_v_sample = jnp.concatenate([vc[:, TS:], v_new], axis=1).reshape(1, NB, WIN, KVH, HD)

    st = state_conv[0]
    z = jnp.zeros((NB, 1, D), f32)
    p1 = jnp.concatenate([st[:, 1:2], z, z, z], axis=1).reshape(NS, D)
    p2 = jnp.concatenate([st[:, 0:1], st[:, 1:2], z, z], axis=1).reshape(NS, D)
    x, cu = _conv(x, norm_mix[1][None, :], w_in[0].astype(bf16), conv_w[0],
                  w_out[0].astype(bf16), p1, p2)
    x = _moe_layer(x, 1, *moe_w)

    new_conv_prompt = cu[T - 2:T].reshape(1, 1, 2, D)
    new_conv_sample = cu[T:].reshape(NB, TS, D)[:, TS - 2:][None]

    y_prompt = x[:NP].reshape(1, NP, D)
    y_sample = x[NP:].reshape(NB, TS, D)
    return (y_prompt, y_sample, new_k_prompt, new_v_prompt, new_conv_prompt,
            new_k_sample, new_v_sample, new_conv_sample)
```

```python
import jax
import jax.numpy as jnp
from jax import lax
from jax.experimental import pallas as pl
from jax.experimental.pallas import tpu as pltpu

D = 1024
NP = 16384
NB = 128
TS = 4
NS = NB * TS
N = NP + NS
PAST = 16384
H = 16
KVH = 2
G = H // KVH
HD = 64
HALF = HD // 2
QKV = (H + 2 * KVH) * HD
WIN = 128
THETA = 10000.0
NGRP = 4
EPG = 4
NE = NGRP * EPG
TOPK = 2
F = 512
EPS = 1e-6
SCALE = HD ** -0.5

T = 512
NT = N // T
TQ = 128
BB = 8
LANES = 128
VMEM_LIMIT = 50 * 1024 * 1024

TM = 256
NTM = N // TM
UNIT = 16
RC = TOPK * TM + NE * UNIT
UPT = RC // UNIT
XW = D + LANES
CH = 256
UPC = CH // UNIT
NCH = -(-(NTM * (UPT - 1) + NE * (UPC - 1)) // UPC) + 1
ZERO_UNIT_IN = UPT - 1
ZERO_UNIT_OUT = (NCH - 1) * UPC

f32 = jnp.float32
bf16 = jnp.bfloat16


def _params(*sem):
    return pltpu.CompilerParams(dimension_semantics=sem, vmem_limit_bytes=VMEM_LIMIT)


def _rms(x, g):
    ms = jnp.mean(x * x, axis=-1, keepdims=True)
    return x * lax.rsqrt(ms + EPS) * g


def _pick(i, prompt_ref, sample_ref):
    return jnp.where(i == NT - 1, sample_ref[...], prompt_ref[...])


def _stream_specs():
    return [pl.BlockSpec((T, D), lambda i: (jnp.minimum(i, NT - 2), 0)),
            pl.BlockSpec((T, D), lambda i: (0, 0))]


def _qkv_kernel(xp_ref, xs_ref, g_ref, wT_ref, qn_ref, kn_ref, cos_ref, sin_ref,
                qT_ref, ktok_ref, vtok_ref, vT_ref):
    x = _pick(pl.program_id(0), xp_ref, xs_ref)
    h = _rms(x, g_ref[...]).astype(bf16)
    qkvT = lax.dot_general(wT_ref[...], h, (((1,), (1,)), ((), ())),
                           preferred_element_type=f32)
    cos = cos_ref[...]
    sin = sin_ref[...]

    def norm_rope(blk, gcol):
        ms = jnp.mean(blk * blk, axis=0, keepdims=True)
        y = blk * lax.rsqrt(ms + EPS) * gcol
        y1 = y[:HALF]
        y2 = y[HALF:]
        return y1 * cos - y2 * sin, y2 * cos + y1 * sin

    qn = qn_ref[...]
    for hd in range(H):
        o1, o2 = norm_rope(qkvT[hd * HD:(hd + 1) * HD], qn)
        qT_ref[hd * HD:hd * HD + HALF, :] = (o1 * SCALE).astype(bf16)
        qT_ref[hd * HD + HALF:(hd + 1) * HD, :] = (o2 * SCALE).astype(bf16)
    kn = kn_ref[...]
    ks = []
    for j in range(KVH):
        o1, o2 = norm_rope(qkvT[H * HD + j * HD:H * HD + (j + 1) * HD], kn)
        ks += [o1, o2]
    kT = jnp.concatenate(ks, axis=0)
    ktok_ref[...] = kT.T
    vT = qkvT[(H + KVH) * HD:]
    vtok_ref[...] = vT.T
    vT_ref[...] = vT.astype(bf16)


def _qkv(xp, xs, g, wT, qn, kn, cos, sin):
    return pl.pallas_call(
        _qkv_kernel,
        grid=(NT,),
        in_specs=_stream_specs() + [
            pl.BlockSpec((1, D), lambda i: (0, 0)),
            pl.BlockSpec((QKV, D), lambda i: (0, 0)),
            pl.BlockSpec((HD, 1), lambda i: (0, 0)),
            pl.BlockSpec((HD, 1), lambda i: (0, 0)),
            pl.BlockSpec((HALF, T), lambda i: (0, i)),
            pl.BlockSpec((HALF, T), lambda i: (0, i)),
        ],
        out_specs=[
            pl.BlockSpec((H * HD, T), lambda i: (0, i)),
            pl.BlockSpec((T, KVH * HD), lambda i: (i, 0)),
            pl.BlockSpec((T, KVH * HD), lambda i: (i, 0)),
            pl.BlockSpec((KVH * HD, T), lambda i: (0, i)),
        ],
        out_shape=[
            jax.ShapeDtypeStruct((H * HD, N), bf16),
            jax.ShapeDtypeStruct((N, KVH * HD), f32),
            jax.ShapeDtypeStruct((N, KVH * HD), f32),
            jax.ShapeDtypeStruct((KVH * HD, N), bf16),
        ],
        compiler_params=_params("parallel"),
        name="qkv",
    )(xp, xs, g, wT, qn, kn, cos, sin)


def _attn_prompt_kernel(qT_ref, kp_ref, kc_ref, vp_ref, vc_ref, sink_ref, o_ref):
    j = pl.program_id(0)
    kk = jnp.concatenate([kp_ref[...], kc_ref[...]], axis=0).astype(bf16)
    vv = jnp.concatenate([vp_ref[...], vc_ref[...]], axis=1)
    s_idx = lax.broadcasted_iota(jnp.int32, (2 * TQ, G * TQ), 0)
    t_idx = lax.broadcasted_iota(jnp.int32, (2 * TQ, G * TQ), 1) & (TQ - 1)
    dist = t_idx + TQ - s_idx
    valid = (dist >= 0) & (dist <= WIN) & ((s_idx >= TQ) | (j > 0))
    pieces = []
    for g in range(KVH):
        qg = jnp.concatenate(
            [qT_ref[(g * G + hh) * HD:(g * G + hh + 1) * HD, :] for hh in range(G)],
            axis=1)
        zeros = jnp.zeros_like(qg)
        rhs = jnp.concatenate([qg, zeros] if g == 0 else [zeros, qg], axis=0)
        sT = jnp.dot(kk, rhs, preferred_element_type=f32)
        sT = jnp.where(valid, sT, -jnp.inf)
        sink = sink_ref[g]
        m = jnp.maximum(jnp.max(sT, axis=0, keepdims=True), sink)
        p = jnp.exp(sT - m)
        l = jnp.sum(p, axis=0, keepdims=True) + jnp.exp(sink - m)
        oT = jnp.dot(vv[g * HD:(g + 1) * HD, :], p.astype(bf16),
                     preferred_element_type=f32)
        oT = oT / l
        pieces += [oT[:, hh * TQ:(hh + 1) * TQ] for hh in range(G)]
    oT_all = jnp.concatenate(pieces, axis=0)
    o_ref[...] = oT_all.T.astype(bf16)


def _attn_prompt(qT, ktok, vT, sink_rows):
    nq = NP // TQ
    return pl.pallas_call(
        _attn_prompt_kernel,
        grid=(nq,),
        in_specs=[
            pl.BlockSpec((H * HD, TQ), lambda j: (0, j)),
            pl.BlockSpec((TQ, KVH * HD), lambda j: (jnp.maximum(j - 1, 0), 0)),
            pl.BlockSpec((TQ, KVH * HD), lambda j: (j, 0)),
            pl.BlockSpec((KVH * HD, TQ), lambda j: (0, jnp.maximum(j - 1, 0))),
            pl.BlockSpec((KVH * HD, TQ), lambda j: (0, j)),
            pl.BlockSpec((KVH, 1, G * TQ), lambda j: (0, 0, 0)),
        ],
        out_specs=pl.BlockSpec((TQ, H * HD), lambda j: (j, 0)),
        out_shape=jax.ShapeDtypeStruct((NP, H * HD), bf16),
        compiler_params=_params("parallel"),
        name="attn_prompt",
    )(qT, ktok, ktok, vT, vT, sink_rows)


def _attn_sample_kernel(q_ref, kc_ref, vc_ref, kn_ref, vn_ref, sink_ref, o_ref):
    rows = H * TS
    t1 = lax.broadcasted_iota(jnp.int32, (rows, WIN), 0) & (TS - 1)
    s1 = lax.broadcasted_iota(jnp.int32, (rows, WIN), 1)
    valid1 = s1 >= t1
    t2 = lax.broadcasted_iota(jnp.int32, (rows, 8), 0) & (TS - 1)
    s2 = lax.broadcasted_iota(jnp.int32, (rows, 8), 1)
    valid2 = s2 <= t2
    sink = sink_ref[...]
    nt = (((1,), (1,)), ((), ()))
    for b in range(BB):
        q = q_ref[b]
        sc = lax.dot_general(q, kc_ref[b].astype(bf16), nt, preferred_element_type=f32)
        sn = lax.dot_general(q, kn_ref[b].astype(bf16), nt, preferred_element_type=f32)
        sc = jnp.where(valid1, sc, -jnp.inf)
        sn = jnp.where(valid2, sn, -jnp.inf)
        m = jnp.maximum(jnp.maximum(jnp.max(sc, axis=-1, keepdims=True),
                                    jnp.max(sn, axis=-1, keepdims=True)), sink)
        pc = jnp.exp(sc - m)
        pn = jnp.exp(sn - m)
        l = (jnp.sum(pc, axis=-1, keepdims=True) + jnp.sum(pn, axis=-1, keepdims=True)
             + jnp.exp(sink - m))
        o = (jnp.dot(pc.astype(bf16), vc_ref[b].astype(bf16), preferred_element_type=f32)
             + jnp.dot(pn.astype(bf16), vn_ref[b].astype(bf16), preferred_element_type=f32))
        o_ref[b] = o / l


def _attn_sample(qbd, kc, vc, kn, vn, sink_col):
    rows = H * TS
    return pl.pallas_call(
        _attn_sample_kernel,
        grid=(NB // BB,),
        in_specs=[
            pl.BlockSpec((BB, rows, KVH * HD), lambda i: (i, 0, 0)),
            pl.BlockSpec((BB, WIN, KVH * HD), lambda i: (i, 0, 0)),
            pl.BlockSpec((BB, WIN, KVH * HD), lambda i: (i, 0, 0)),
            pl.BlockSpec((BB, 8, KVH * HD), lambda i: (i, 0, 0)),
            pl.BlockSpec((BB, 8, KVH * HD), lambda i: (i, 0, 0)),
            pl.BlockSpec((rows, 1), lambda i: (0, 0)),
        ],
        out_specs=pl.BlockSpec((BB, rows, KVH * HD), lambda i: (i, 0, 0)),
        out_shape=jax.ShapeDtypeStruct((NB, rows, KVH * HD), f32),
        compiler_params=_params("parallel"),
        name="attn_sample",
    )(qbd, kc, vc, kn, vn, sink_col)


def _proj_kernel(xp_ref, xs_ref, op_ref, os_ref, w_ref, y_ref):
    i = pl.program_id(0)
    y_ref[...] = _pick(i, xp_ref, xs_ref) + jnp.dot(_pick(i, op_ref, os_ref), w_ref[...],
                                                    preferred_element_type=f32)


def _proj(xp, xs, o_p, o_s, w):
    return pl.pallas_call(
        _proj_kernel,
        grid=(NT,),
        in_specs=_stream_specs() + _stream_specs() + [
            pl.BlockSpec((D, D), lambda i: (0, 0)),
        ],
        out_specs=pl.BlockSpec((T, D), lambda i: (i, 0)),
        out_shape=jax.ShapeDtypeStruct((N, D), f32),
        compiler_params=_params("parallel"),
        name="proj",
    )(xp, xs, o_p, o_s, w)


def _dispatch_kernel(x_ref, g_ref, whi_ref, wlo_ref, b_ref, xc_ref, meta_ref, cnt_ref):
    hf = _rms(x_ref[...], g_ref[...])
    h_hi = hf.astype(bf16)
    h_lo = (hf - h_hi.astype(f32)).astype(bf16)
    whi = whi_ref[...]
    logits = (jnp.dot(h_hi, whi, preferred_element_type=f32)
              + jnp.dot(h_lo, whi, preferred_element_type=f32)
              + jnp.dot(h_hi, wlo_ref[...], preferred_element_type=f32)
              + b_ref[...])
    lane = lax.broadcasted_iota(jnp.int32, logits.shape, 1)
    lane_f = lane.astype(f32)
    big = float(LANES)
    gl = jnp.where(lane < NGRP, logits, -jnp.inf)
    gmax = jnp.max(gl, axis=-1, keepdims=True)
    gsel = jnp.min(jnp.where(gl == gmax, lane_f, big), axis=-1, keepdims=True)
    gsum = jnp.sum(jnp.exp(gl - gmax), axis=-1, keepdims=True)
    g_w = 1.0 / gsum
    lane_grp = ((lane - NGRP) >> 2).astype(f32)
    emask = (lane >= NGRP) & (lane < NGRP + NE) & (lane_grp == gsel)
    el = jnp.where(emask, logits, -jnp.inf)
    v1 = jnp.max(el, axis=-1, keepdims=True)
    i1 = jnp.min(jnp.where(el == v1, lane_f, big), axis=-1, keepdims=True)
    el2 = jnp.where(lane_f == i1, -jnp.inf, el)
    v2 = jnp.max(el2, axis=-1, keepdims=True)
    i2 = jnp.min(jnp.where(el2 == v2, lane_f, big), axis=-1, keepdims=True)
    e1 = jnp.exp(v2 - v1)
    den = 1.0 + e1
    w1 = (1.0 / den) * g_w
    w2 = (e1 / den) * g_w

    m1 = lane_f == i1
    m2 = lane_f == i2
    sel = jnp.where(m1 | m2, 1.0, 0.0)
    r_i = lax.broadcasted_iota(jnp.int32, (TM, TM), 0)
    c_i = lax.broadcasted_iota(jnp.int32, (TM, TM), 1)
    before = jnp.where(c_i < r_i, 1.0, 0.0).astype(bf16)
    ranks = jnp.dot(before, sel.astype(bf16), preferred_element_type=f32)
    counts = jnp.sum(sel, axis=0, keepdims=True)
    padded = jnp.floor((counts + (UNIT - 1.0)) * (1.0 / UNIT)) * UNIT
    k_i = lax.broadcasted_iota(jnp.int32, (LANES, LANES), 0)
    l_i = lax.broadcasted_iota(jnp.int32, (LANES, LANES), 1)
    lower_lanes = jnp.where(k_i < l_i, 1.0, 0.0).astype(bf16)
    seg = jnp.dot(jnp.broadcast_to(padded, (8, LANES)).astype(bf16), lower_lanes,
                  preferred_element_type=f32)[0:1]
    posall = seg + ranks
    pos1 = jnp.sum(jnp.where(m1, posall, 0.0), axis=-1, keepdims=True)
    pos2 = jnp.sum(jnp.where(m2, posall, 0.0), axis=-1, keepdims=True)
    meta = jnp.where(lane == 0, pos1, jnp.where(lane == 1, pos2, 0.0))
    meta_ref[...] = meta
    cnt_ref[...] = jnp.broadcast_to(counts, (8, LANES))

    metaT = meta.T
    rr = lax.broadcasted_iota(jnp.int32, (RC, TM), 0).astype(f32)
    p1 = rr == metaT[0:1, :]
    p2 = rr == metaT[1:2, :]
    xc_ref[:, :D] = jnp.dot(jnp.where(p1 | p2, 1.0, 0.0).astype(bf16), h_hi,
                            preferred_element_type=f32).astype(bf16)
    w1_hi = w1.astype(bf16).astype(f32)
    w2_hi = w2.astype(bf16).astype(f32)
    g1 = jnp.where(lane == 0, w1_hi, jnp.where(lane == 1, w1 - w1_hi, 0.0)).astype(bf16)
    g2 = jnp.where(lane == 0, w2_hi, jnp.where(lane == 1, w2 - w2_hi, 0.0)).astype(bf16)
    gc = (jnp.dot(jnp.where(p1, 1.0, 0.0).astype(bf16), g1, preferred_element_type=f32)
          + jnp.dot(jnp.where(p2, 1.0, 0.0).astype(bf16), g2, preferred_element_type=f32))
    xc_ref[:, D:] = gc.astype(bf16)


def _dispatch(x, g, whi, wlo, b):
    return pl.pallas_call(
        _dispatch_kernel,
        grid=(NTM,),
        in_specs=[
            pl.BlockSpec((TM, D), lambda i: (i, 0)),
            pl.BlockSpec((1, D), lambda i: (0, 0)),
            pl.BlockSpec((D, LANES), lambda i: (0, 0)),
            pl.BlockSpec((D, LANES), lambda i: (0, 0)),
            pl.BlockSpec((1, LANES), lambda i: (0, 0)),
        ],
        out_specs=[
            pl.BlockSpec((RC, XW), lambda i: (i, 0)),
            pl.BlockSpec((TM, LANES), lambda i: (i, 0)),
            pl.BlockSpec((8, LANES), lambda i: (i, 0)),
        ],
        out_shape=[
            jax.ShapeDtypeStruct((NTM * RC, XW), bf16),
            jax.ShapeDtypeStruct((N, LANES), f32),
            jax.ShapeDtypeStruct((NTM * 8, LANES), f32),
        ],
        compiler_params=_params("parallel"),
        name="dispatch",
    )(x, g, whi, wlo, b)


def _dispatch_tables(cnt):
    i32 = jnp.int32
    n = cnt.reshape(NTM, 8, LANES)[:, 0, NGRP:NGRP + NE].astype(i32)
    units = (n + UNIT - 1) // UNIT
    seg_end = jnp.cumsum(units, axis=1)
    seg_start = seg_end - units
    col_end = jnp.cumsum(units, axis=0)
    col_start = col_end - units
    chunks = (col_end[-1] + UPC - 1) // UPC
    ch_end = jnp.cumsum(chunks)
    ch_start = ch_end - chunks
    nused = ch_end[-1]
    c = jnp.arange(NCH, dtype=i32)
    eid = jnp.minimum(jnp.sum((ch_end[None, :] <= c[:, None]).astype(i32), axis=1), NE - 1)

    src0 = jnp.arange(NTM, dtype=i32)[:, None] * UPT + seg_start
    dst0 = ch_start[None, :] * UPC + col_start
    k = jnp.arange(NCH * UPC, dtype=i32)[:, None, None]
    inside = (k >= dst0[None]) & (k < (dst0 + units)[None])
    found = jnp.sum(inside.astype(i32), axis=(1, 2))
    shift = jnp.sum(jnp.where(inside, (src0 - dst0)[None], 0), axis=(1, 2))
    ffn_src = jnp.where(found > 0, k[:, 0, 0] + shift, ZERO_UNIT_IN)

    v = jnp.arange(UPT, dtype=i32)[None, :, None]
    inside_v = (v >= seg_start[:, None, :]) & (v < seg_end[:, None, :])
    found_v = jnp.sum(inside_v.astype(i32), axis=2)
    shift_v = jnp.sum(jnp.where(inside_v, (dst0 - seg_start)[:, None, :], 0), axis=2)
    comb_src = jnp.where(found_v > 0, v[:, :, 0] + shift_v, ZERO_UNIT_OUT).reshape(-1)
    return eid, nused.reshape(1), ffn_src, comb_src


def _unit_copies(src_ref, base, n_units, src_hbm, stage, slot, sem):
    out = []
    for j in range(n_units):
        row = pl.multiple_of(src_ref[base + j] * UNIT, UNIT)
        out.append(pltpu.make_async_copy(
            src_hbm.at[pl.ds(row, UNIT), :],
            stage.at[slot, pl.ds(j * UNIT, UNIT), :],
            sem.at[slot]))
    return out


def _ffn_kernel(eid_ref, nused_ref, src_ref, xc_hbm, wg_ref, wu_ref, wd_ref, o_ref,
                stage, sem, wg_b, wu_b, wd_b):
    c = pl.program_id(0)
    nused = nused_ref[0]
    slot = lax.rem(c, 2)

    @pl.when((c == 0) & (nused > 0))
    def _():
        for cp in _unit_copies(src_ref, 0, UPC, xc_hbm, stage, 0, sem):
            cp.start()

    @pl.when(c + 1 < nused)
    def _():
        for cp in _unit_copies(src_ref, (c + 1) * UPC, UPC, xc_hbm, stage, 1 - slot, sem):
            cp.start()

    @pl.when(c < nused)
    def _():
        for cp in _unit_copies(src_ref, c * UPC, UPC, xc_hbm, stage, slot, sem):
            cp.wait()

        @pl.when((c == 0) | (eid_ref[c] != eid_ref[jnp.maximum(c - 1, 0)]))
        def _():
            wg_b[...] = wg_ref[...].astype(bf16)
            wu_b[...] = wu_ref[...].astype(bf16)
            wd_b[...] = wd_ref[...].astype(bf16)

        xs = stage[slot]
        x = xs[:, :D]
        gb = xs[:, D:].astype(f32)
        gate = gb[:, 0:1] + gb[:, 1:2]
        a = jnp.dot(x, wg_b[...], preferred_element_type=f32)
        u = jnp.dot(x, wu_b[...], preferred_element_type=f32)
        act = (a * (1.0 / (1.0 + jnp.exp(-a)))) * u * gate
        o_ref[...] = jnp.dot(act.astype(bf16), wd_b[...],
                             preferred_element_type=f32).astype(bf16)

    @pl.when(c >= nused)
    def _():
        o_ref[...] = jnp.zeros((CH, D), bf16)


def _ffn(eid, nused, ffn_src, xc, wg, wu, wd):
    return pl.pallas_call(
        _ffn_kernel,
        grid_spec=pltpu.PrefetchScalarGridSpec(
            num_scalar_prefetch=3,
            grid=(NCH,),
            in_specs=[
                pl.BlockSpec(memory_space=pl.ANY),
                pl.BlockSpec((None, D, F), lambda c, eid, nu, src: (eid[c], 0, 0)),
                pl.BlockSpec((None, D, F), lambda c, eid, nu, src: (eid[c], 0, 0)),
                pl.BlockSpec((None, F, D), lambda c, eid, nu, src: (eid[c], 0, 0)),
            ],
            out_specs=pl.BlockSpec((CH, D), lambda c, eid, nu, src: (c, 0)),
            scratch_shapes=[
                pltpu.VMEM((2, CH, XW), bf16),
                pltpu.SemaphoreType.DMA((2,)),
                pltpu.VMEM((D, F), bf16),
                pltpu.VMEM((D, F), bf16),
                pltpu.VMEM((F, D), bf16),
            ],
        ),
        out_shape=jax.ShapeDtypeStruct((NCH * CH, D), bf16),
        compiler_params=_params("arbitrary"),
        name="ffn",
    )(eid, nused, ffn_src, xc, wg, wu, wd)


def _combine_kernel(src_ref, x_ref, meta_ref, o_hbm, y_refs, stage, sem):
    i = pl.program_id(0)
    slot = lax.rem(i, 2)

    @pl.when(i == 0)
    def _():
        for cp in _unit_copies(src_ref, 0, UPT, o_hbm, stage, 0, sem):
            cp.start()

    @pl.when(i + 1 < NTM)
    def _():
        for cp in _unit_copies(src_ref, (i + 1) * UPT, UPT, o_hbm, stage, 1 - slot, sem):
            cp.start()

    for cp in _unit_copies(src_ref, i * UPT, UPT, o_hbm, stage, slot, sem):
        cp.wait()

    meta = meta_ref[...]
    pos1 = meta[:, 0:1]
    pos2 = meta[:, 1:2]
    li = lax.broadcasted_iota(jnp.int32, (TM, RC), 1).astype(f32)
    pt = jnp.where((li == pos1) | (li == pos2), 1.0, 0.0).astype(bf16)
    y = x_ref[...] + jnp.dot(pt, stage[slot], preferred_element_type=f32)
    if len(y_refs) == 1:
        y_refs[0][...] = y
    else:
        @pl.when(i < NP // TM)
        def _():
            y_refs[0][...] = y

        @pl.when(i >= NP // TM)
        def _():
            y_refs[1][...] = y


def _combine_kernel_1(src_ref, x_ref, meta_ref, o_hbm, y_ref, stage, sem):
    _combine_kernel(src_ref, x_ref, meta_ref, o_hbm, (y_ref,), stage, sem)


def _combine_kernel_2(src_ref, x_ref, meta_ref, o_hbm, yp_ref, ys_ref, stage, sem):
    _combine_kernel(src_ref, x_ref, meta_ref, o_hbm, (yp_ref, ys_ref), stage, sem)


def _combine(comb_src, x, meta, o_sorted, split):
    npt = NP // TM
    if split:
        body = _combine_kernel_2
        out_specs = [pl.BlockSpec((TM, D), lambda i, src: (jnp.minimum(i, npt - 1), 0)),
                     pl.BlockSpec((TM, D), lambda i, src: (jnp.maximum(i - npt, 0), 0))]
        out_shape = [jax.ShapeDtypeStruct((NP, D), f32), jax.ShapeDtypeStruct((NS, D), f32)]
    else:
        body = _combine_kernel_1
        out_specs = pl.BlockSpec((TM, D), lambda i, src: (i, 0))
        out_shape = jax.ShapeDtypeStruct((N, D), f32)
    return pl.pallas_call(
        body,
        grid_spec=pltpu.PrefetchScalarGridSpec(
            num_scalar_prefetch=1,
            grid=(NTM,),
            in_specs=[
                pl.BlockSpec((TM, D), lambda i, src: (i, 0)),
                pl.BlockSpec((TM, LANES), lambda i, src: (i, 0)),
                pl.BlockSpec(memory_space=pl.ANY),
            ],
            out_specs=out_specs,
            scratch_shapes=[
                pltpu.VMEM((2, RC, D), bf16),
                pltpu.SemaphoreType.DMA((2,)),
            ],
        ),
        out_shape=out_shape,
        compiler_params=_params("arbitrary"),
        name="combine",
    )(comb_src, x, meta, o_sorted)


def _conv_kernel(x_ref, g_ref, win_ref, cw_ref, wout_ref, p1_ref, p2_ref,
                 y_ref, cu_ref, pad_ref):
    i = pl.program_id(0)

    @pl.when(i == 0)
    def _():
        pad_ref[0:8, :] = jnp.zeros((8, D), f32)

    x = x_ref[...]
    h = _rms(x, g_ref[...]).astype(bf16)
    bcu = jnp.dot(h, win_ref[...], preferred_element_type=f32)
    b = bcu[:, :D]
    cu = bcu[:, D:2 * D] * bcu[:, 2 * D:]
    pad_ref[8:T + 8, :] = cu
    m1 = pad_ref[7:T + 7, :]
    m2 = pad_ref[6:T + 6, :]
    t = lax.broadcasted_iota(jnp.int32, (T, 1), 0) & (TS - 1)
    is_sample = i == NT - 1
    m1 = jnp.where(is_sample & (t == 0), p1_ref[...], m1)
    m2 = jnp.where(is_sample & (t < 2), p2_ref[...], m2)
    cw = cw_ref[...]
    conv = cw[0:1] * m2 + cw[1:2] * m1 + cw[2:3] * cu
    y = jnp.dot((b * conv).astype(bf16), wout_ref[...], preferred_element_type=f32)
    y_ref[...] = x + y
    cu_ref[...] = cu
    pad_ref[0:8, :] = cu[T - 8:, :]


def _conv(x, g, win, cw, wout, p1, p2):
    return pl.pallas_call(
        _conv_kernel,
        grid=(NT,),
        in_specs=[
            pl.BlockSpec((T, D), lambda i: (i, 0)),
            pl.BlockSpec((1, D), lambda i: (0, 0)),
            pl.BlockSpec((D, 3 * D), lambda i: (0, 0)),
            pl.BlockSpec((3, D), lambda i: (0, 0)),
            pl.BlockSpec((D, D), lambda i: (0, 0)),
            pl.BlockSpec((T, D), lambda i: (0, 0)),
            pl.BlockSpec((T, D), lambda i: (0, 0)),
        ],
        out_specs=[
            pl.BlockSpec((T, D), lambda i: (i, 0)),
            pl.BlockSpec((T, D), lambda i: (jnp.where(i == NT - 1, 1, 0), 0)),
        ],
        out_shape=[
            jax.ShapeDtypeStruct((N, D), f32),
            jax.ShapeDtypeStruct((2 * T, D), f32),
        ],
        scratch_shapes=[pltpu.VMEM((T + 8, D), f32)],
        compiler_params=_params("arbitrary"),
        name="conv",
    )(x, g, win, cw, wout, p1, p2)


def _rope_tables():
    inv_freq = THETA ** (-jnp.arange(HALF, dtype=f32) / HALF)
    pos = jnp.concatenate([
        jnp.arange(NP, dtype=jnp.int32),
        PAST + jnp.tile(jnp.arange(TS, dtype=jnp.int32), NB),
    ]).astype(f32)
    ang = inv_freq[:, None] * pos[None, :]
    return jnp.cos(ang), jnp.sin(ang)


def _router_weights(w_group, b_group, w_router, b_router):
    pad = LANES - NGRP - NE
    w = jnp.concatenate([w_group, w_router, jnp.zeros((D, pad), f32)], axis=1)
    b = jnp.concatenate([b_group, b_router, jnp.zeros((pad,), f32)])[None, :]
    whi = w.astype(bf16)
    wlo = (w - whi.astype(f32)).astype(bf16)
    return whi, wlo, b


def _moe_layer(x, i, split, norm_ffn, w_group, b_group, w_router, b_router, w_gate, w_up, w_down):
    whi, wlo, b = _router_weights(w_group[i], b_group[i], w_router[i], b_router[i])
    xc, meta, cnt = _dispatch(x, norm_ffn[i][None, :], whi, wlo, b)
    eid, nused, ffn_src, comb_src = _dispatch_tables(cnt)
    o_sorted = _ffn(eid, nused, ffn_src, xc, w_gate[i], w_up[i], w_down[i])
    return _combine(comb_src, x, meta, o_sorted, split)


def kernel(x_prompt, x_sample, cache_k, cache_v, state_conv, norm_mix, w_qkv, q_norm, k_norm,
           sinks, w_o, w_in, conv_w, w_out, norm_ffn, w_group, b_group, w_router, b_router,
           w_gate, w_up, w_down):
    xp = x_prompt.reshape(NP, D)
    xs = x_sample.reshape(NS, D)
    moe_w = (norm_ffn, w_group, b_group, w_router, b_router, w_gate, w_up, w_down)

    cos, sin = _rope_tables()
    qT, ktok, vtok, vT = _qkv(xp, xs, norm_mix[0][None, :], w_qkv[0].T.astype(bf16),
                              q_norm[0][:, None], k_norm[0][:, None], cos, sin)
    sink_rows = jnp.repeat(sinks[0].reshape(KVH, G), TQ, axis=1)[:, None, :]
    o_p = _attn_prompt(qT, ktok, vT, sink_rows)

    qs = qT[:, NP:].reshape(KVH, G, HD, NB, TS).transpose(3, 0, 1, 4, 2)
    zq = jnp.zeros_like(qs[:, 0])
    qbd = jnp.stack([jnp.concatenate([qs[:, 0], zq], axis=-1),
                     jnp.concatenate([zq, qs[:, 1]], axis=-1)], axis=1)
    qbd = qbd.reshape(NB, H * TS, KVH * HD)
    k_new = ktok[NP:].reshape(NB, TS, KVH * HD)
    v_new = vtok[NP:].reshape(NB, TS, KVH * HD)
    pad4 = jnp.zeros((NB, 8 - TS, KVH * HD), f32)
    kc = cache_k[0].reshape(NB, WIN, KVH * HD)
    vc = cache_v[0].reshape(NB, WIN, KVH * HD)
    sink_col = jnp.repeat(sinks[0], TS)[:, None]
    o_s = _attn_sample(qbd, kc, vc, jnp.concatenate([k_new, pad4], axis=1),
                       jnp.concatenate([v_new, pad4], axis=1), sink_col)
    o_s = o_s.reshape(NB, KVH, G, TS, KVH, HD)
    o_s = jnp.stack([o_s[:, 0, :, :, 0], o_s[:, 1, :, :, 1]], axis=1)
    o_s = o_s.transpose(0, 3, 1, 2, 4).reshape(NS, H * HD).astype(bf16)
    x = _proj(xp, xs, o_p, o_s, w_o[0].astype(bf16))
    x = _moe_layer(x, 0, False, *moe_w)

    new_k_prompt = ktok[NP - WIN:NP].reshape(1, 1, WIN, KVH, HD)
    new_v_prompt = vtok[NP - WIN:NP].reshape(1, 1, WIN, KVH, HD)
    new_k_sample = jnp.concatenate([kc[:, TS:], k_new], axis=1).reshape(1, NB, WIN, KVH, HD)
    new_v_sample = jnp.concatenate([vc[:, TS:], v_new], axis=1).reshape(1, NB, WIN, KVH, HD)

    st = state_conv[0]
    z = jnp.zeros((NB, 1, D), f32)
    p1 = jnp.concatenate([st[:, 1:2], z, z, z], axis=1).reshape(NS, D)
    p2 = jnp.concatenate([st[:, 0:1], st[:, 1:2], z, z], axis=1).reshape(NS, D)
    x, cu = _conv(x, norm_mix[1][None, :], w_in[0].astype(bf16), conv_w[0],
                  w_out[0].astype(bf16), p1, p2)
    y_prompt, y_sample = _moe_layer(x, 1, True, *moe_w)

    new_conv_prompt = cu[T - 2:T].reshape(1, 1, 2, D)
    new_conv_sample = cu[T:].reshape(NB, TS, D)[:, TS - 2:][None]

    y_prompt = y_prompt.reshape(1, NP, D)
    y_sample = y_sample.reshape(NB, TS, D)
    return (y_prompt, y_sample, new_k_prompt, new_v_prompt, new_conv_prompt,
            new_k_sample, new_v_sample, new_conv_sample)
```

```python
import jax
import jax.numpy as jnp
from jax import lax
from jax.experimental import pallas as pl
from jax.experimental.pallas import tpu as pltpu

D = 1024
NP = 16384
NB = 128
TS = 4
NS = NB * TS
N = NP + NS
PAST = 16384
H = 16
KVH = 2
G = H // KVH
HD = 64
HALF = HD // 2
QKV = (H + 2 * KVH) * HD
WIN = 128
THETA = 10000.0
NGRP = 4
EPG = 4
NE = NGRP * EPG
TOPK = 2
F = 512
EPS = 1e-6
SCALE = HD ** -0.5

T = 512
NT = N // T
TQ = 128
BB = 8
LANES = 128
VMEM_LIMIT = 50 * 1024 * 1024

TM = 256
NTM = N // TM
UNIT = 16
RC = TOPK * TM + NE * UNIT
UPT = RC // UNIT
XW = D + LANES
CH = 256
UPC = CH // UNIT
NCH = -(-(NTM * (UPT - 1) + NE * (UPC - 1)) // UPC) + 1
ZERO_UNIT_IN = UPT - 1
ZERO_UNIT_OUT = (NCH - 1) * UPC
NSLOT = 3

f32 = jnp.float32
bf16 = jnp.bfloat16


def _params(*sem):
    return pltpu.CompilerParams(dimension_semantics=sem, vmem_limit_bytes=VMEM_LIMIT)


def _rms(x, g):
    ms = jnp.mean(x * x, axis=-1, keepdims=True)
    return x * lax.rsqrt(ms + EPS) * g


def _pick(i, prompt_ref, sample_ref):
    return jnp.where(i == NT - 1, sample_ref[...], prompt_ref[...])


def _stream_specs():
    return [pl.BlockSpec((T, D), lambda i: (jnp.minimum(i, NT - 2), 0)),
            pl.BlockSpec((T, D), lambda i: (0, 0))]


def _split(x):
    hi = x.astype(bf16)
    return hi, (x - hi.astype(f32)).astype(bf16)


def _qkv_kernel(xp_ref, xs_ref, g_ref, wT_ref, wTlo_ref, qn_ref, kn_ref, cos_ref, sin_ref,
                qT_ref, ktok_ref, vtok_ref, vT_ref, qs_ref, acc_ref):
    i = pl.program_id(0)
    hf = _rms(_pick(i, xp_ref, xs_ref), g_ref[...])
    h, h_lo = _split(hf)
    nt = (((1,), (1,)), ((), ()))
    acc_ref[...] = lax.dot_general(wT_ref[...], h, nt, preferred_element_type=f32)

    @pl.when(i == NT - 1)
    def _():
        acc_ref[...] += (lax.dot_general(wT_ref[...], h_lo, nt, preferred_element_type=f32)
                         + lax.dot_general(wTlo_ref[...], h, nt, preferred_element_type=f32))

    qkvT = acc_ref
    cos = cos_ref[...]
    sin = sin_ref[...]

    def norm_rope(blk, gcol):
        ms = jnp.mean(blk * blk, axis=0, keepdims=True)
        y = blk * lax.rsqrt(ms + EPS) * gcol
        y1 = y[:HALF]
        y2 = y[HALF:]
        return y1 * cos - y2 * sin, y2 * cos + y1 * sin

    qn = qn_ref[...]
    for hd in range(H):
        o1, o2 = norm_rope(qkvT[hd * HD:(hd + 1) * HD], qn)
        qT_ref[hd * HD:hd * HD + HALF, :] = (o1 * SCALE).astype(bf16)
        qT_ref[hd * HD + HALF:(hd + 1) * HD, :] = (o2 * SCALE).astype(bf16)
        qs_ref[hd * HD:hd * HD + HALF, :] = o1 * SCALE
        qs_ref[hd * HD + HALF:(hd + 1) * HD, :] = o2 * SCALE
    kn = kn_ref[...]
    ks = []
    for j in range(KVH):
        o1, o2 = norm_rope(qkvT[H * HD + j * HD:H * HD + (j + 1) * HD], kn)
        ks += [o1, o2]
    kT = jnp.concatenate(ks, axis=0)
    ktok_ref[...] = kT.T
    vT = qkvT[(H + KVH) * HD:]
    vtok_ref[...] = vT.T
    vT_ref[...] = vT.astype(bf16)


def _qkv(xp, xs, g, wT, wT_lo, qn, kn, cos, sin):
    return pl.pallas_call(
        _qkv_kernel,
        grid=(NT,),
        in_specs=_stream_specs() + [
            pl.BlockSpec((1, D), lambda i: (0, 0)),
            pl.BlockSpec((QKV, D), lambda i: (0, 0)),
            pl.BlockSpec((QKV, D), lambda i: (0, 0)),
            pl.BlockSpec((HD, 1), lambda i: (0, 0)),
            pl.BlockSpec((HD, 1), lambda i: (0, 0)),
            pl.BlockSpec((HALF, T), lambda i: (0, i)),
            pl.BlockSpec((HALF, T), lambda i: (0, i)),
        ],
        out_specs=[
            pl.BlockSpec((H * HD, T), lambda i: (0, i)),
            pl.BlockSpec((T, KVH * HD), lambda i: (i, 0)),
            pl.BlockSpec((T, KVH * HD), lambda i: (i, 0)),
            pl.BlockSpec((KVH * HD, T), lambda i: (0, i)),
            pl.BlockSpec((H * HD, T), lambda i: (0, 0)),
        ],
        out_shape=[
            jax.ShapeDtypeStruct((H * HD, N), bf16),
            jax.ShapeDtypeStruct((N, KVH * HD), f32),
            jax.ShapeDtypeStruct((N, KVH * HD), f32),
            jax.ShapeDtypeStruct((KVH * HD, N), bf16),
            jax.ShapeDtypeStruct((H * HD, T), f32),
        ],
        scratch_shapes=[pltpu.VMEM((QKV, T), f32)],
        compiler_params=_params("arbitrary"),
        name="qkv",
    )(xp, xs, g, wT, wT_lo, qn, kn, cos, sin)


def _attn_prompt_kernel(qT_ref, kp_ref, kc_ref, vp_ref, vc_ref, sink_ref, o_ref):
    j = pl.program_id(0)
    kk = jnp.concatenate([kp_ref[...], kc_ref[...]], axis=0).astype(bf16)
    vv = jnp.concatenate([vp_ref[...], vc_ref[...]], axis=1)
    s_idx = lax.broadcasted_iota(jnp.int32, (2 * TQ, G * TQ), 0)
    t_idx = lax.broadcasted_iota(jnp.int32, (2 * TQ, G * TQ), 1) & (TQ - 1)
    dist = t_idx + TQ - s_idx
    valid = (dist >= 0) & (dist <= WIN) & ((s_idx >= TQ) | (j > 0))
    pieces = []
    for g in range(KVH):
        qg = jnp.concatenate(
            [qT_ref[(g * G + hh) * HD:(g * G + hh + 1) * HD, :] for hh in range(G)],
            axis=1)
        zeros = jnp.zeros_like(qg)
        rhs = jnp.concatenate([qg, zeros] if g == 0 else [zeros, qg], axis=0)
        sT = jnp.dot(kk, rhs, preferred_element_type=f32)
        sT = jnp.where(valid, sT, -jnp.inf)
        sink = sink_ref[g]
        m = jnp.maximum(jnp.max(sT, axis=0, keepdims=True), sink)
        p = jnp.exp(sT - m)
        l = jnp.sum(p, axis=0, keepdims=True) + jnp.exp(sink - m)
        oT = jnp.dot(vv[g * HD:(g + 1) * HD, :], p.astype(bf16),
                     preferred_element_type=f32)
        oT = oT / l
        pieces += [oT[:, hh * TQ:(hh + 1) * TQ] for hh in range(G)]
    oT_all = jnp.concatenate(pieces, axis=0)
    o_ref[...] = oT_all.T.astype(bf16)


def _attn_prompt(qT, ktok, vT, sink_rows):
    nq = NP // TQ
    return pl.pallas_call(
        _attn_prompt_kernel,
        grid=(nq,),
        in_specs=[
            pl.BlockSpec((H * HD, TQ), lambda j: (0, j)),
            pl.BlockSpec((TQ, KVH * HD), lambda j: (jnp.maximum(j - 1, 0), 0)),
            pl.BlockSpec((TQ, KVH * HD), lambda j: (j, 0)),
            pl.BlockSpec((KVH * HD, TQ), lambda j: (0, jnp.maximum(j - 1, 0))),
            pl.BlockSpec((KVH * HD, TQ), lambda j: (0, j)),
            pl.BlockSpec((KVH, 1, G * TQ), lambda j: (0, 0, 0)),
        ],
        out_specs=pl.BlockSpec((TQ, H * HD), lambda j: (j, 0)),
        out_shape=jax.ShapeDtypeStruct((NP, H * HD), bf16),
        compiler_params=_params("parallel"),
        name="attn_prompt",
    )(qT, ktok, ktok, vT, vT, sink_rows)


def _attn_sample_kernel(q_ref, kc_ref, vc_ref, kn_ref, vn_ref, sink_ref, o_ref):
    rows = H * TS
    t1 = lax.broadcasted_iota(jnp.int32, (rows, WIN), 0) & (TS - 1)
    s1 = lax.broadcasted_iota(jnp.int32, (rows, WIN), 1)
    valid1 = s1 >= t1
    t2 = lax.broadcasted_iota(jnp.int32, (rows, 8), 0) & (TS - 1)
    s2 = lax.broadcasted_iota(jnp.int32, (rows, 8), 1)
    valid2 = s2 <= t2
    sink = sink_ref[...]
    nt = (((1,), (1,)), ((), ()))
    nn = (((1,), (0,)), ((), ()))

    def dot3(a, b, dims):
        a_hi, a_lo = _split(a)
        b_hi, b_lo = _split(b)
        return (lax.dot_general(a_hi, b_hi, dims, preferred_element_type=f32)
                + lax.dot_general(a_lo, b_hi, dims, preferred_element_type=f32)
                + lax.dot_general(a_hi, b_lo, dims, preferred_element_type=f32))

    for b in range(BB):
        q = q_ref[b]
        sc = dot3(q, kc_ref[b], nt)
        sn = dot3(q, kn_ref[b], nt)
        sc = jnp.where(valid1, sc, -jnp.inf)
        sn = jnp.where(valid2, sn, -jnp.inf)
        m = jnp.maximum(jnp.maximum(jnp.max(sc, axis=-1, keepdims=True),
                                    jnp.max(sn, axis=-1, keepdims=True)), sink)
        pc = jnp.exp(sc - m)
        pn = jnp.exp(sn - m)
        l = (jnp.sum(pc, axis=-1, keepdims=True) + jnp.sum(pn, axis=-1, keepdims=True)
             + jnp.exp(sink - m))
        o = dot3(pc, vc_ref[b], nn) + dot3(pn, vn_ref[b], nn)
        o_ref[b] = o / l


def _attn_sample(qbd, kc, vc, kn, vn, sink_col):
    rows = H * TS
    return pl.pallas_call(
        _attn_sample_kernel,
        grid=(NB // BB,),
        in_specs=[
            pl.BlockSpec((BB, rows, KVH * HD), lambda i: (i, 0, 0)),
            pl.BlockSpec((BB, WIN, KVH * HD), lambda i: (i, 0, 0)),
            pl.BlockSpec((BB, WIN, KVH * HD), lambda i: (i, 0, 0)),
            pl.BlockSpec((BB, 8, KVH * HD), lambda i: (i, 0, 0)),
            pl.BlockSpec((BB, 8, KVH * HD), lambda i: (i, 0, 0)),
            pl.BlockSpec((rows, 1), lambda i: (0, 0)),
        ],
        out_specs=pl.BlockSpec((BB, rows, KVH * HD), lambda i: (i, 0, 0)),
        out_shape=jax.ShapeDtypeStruct((NB, rows, KVH * HD), f32),
        compiler_params=_params("parallel"),
        name="attn_sample",
    )(qbd, kc, vc, kn, vn, sink_col)


def _proj_kernel(xp_ref, xs_ref, op_ref, os_ref, w_ref, wlo_ref, y_ref):
    i = pl.program_id(0)

    @pl.when(i < NT - 1)
    def _():
        y_ref[...] = xp_ref[...] + jnp.dot(op_ref[...], w_ref[...], preferred_element_type=f32)

    @pl.when(i == NT - 1)
    def _():
        o_hi, o_lo = _split(os_ref[...])
        w = w_ref[...]
        y_ref[...] = (xs_ref[...] + jnp.dot(o_hi, w, preferred_element_type=f32)
                      + jnp.dot(o_lo, w, preferred_element_type=f32)
                      + jnp.dot(o_hi, wlo_ref[...], preferred_element_type=f32))


def _proj(xp, xs, o_p, o_s, w, w_lo):
    return pl.pallas_call(
        _proj_kernel,
        grid=(NT,),
        in_specs=_stream_specs() + _stream_specs() + [
            pl.BlockSpec((D, D), lambda i: (0, 0)),
            pl.BlockSpec((D, D), lambda i: (0, 0)),
        ],
        out_specs=pl.BlockSpec((T, D), lambda i: (i, 0)),
        out_shape=jax.ShapeDtypeStruct((N, D), f32),
        compiler_params=_params("parallel"),
        name="proj",
    )(xp, xs, o_p, o_s, w, w_lo)


def _dispatch_kernel(x_ref, g_ref, whi_ref, wlo_ref, b_ref, xc_ref, meta_ref, cnt_ref):
    hf = _rms(x_ref[...], g_ref[...])
    h_hi = hf.astype(bf16)
    h_lo = (hf - h_hi.astype(f32)).astype(bf16)
    whi = whi_ref[...]
    logits = (jnp.dot(h_hi, whi, preferred_element_type=f32)
              + jnp.dot(h_lo, whi, preferred_element_type=f32)
              + jnp.dot(h_hi, wlo_ref[...], preferred_element_type=f32)
              + b_ref[...])
    lane = lax.broadcasted_iota(jnp.int32, logits.shape, 1)
    lane_f = lane.astype(f32)
    big = float(LANES)
    gl = jnp.where(lane < NGRP, logits, -jnp.inf)
    gmax = jnp.max(gl, axis=-1, keepdims=True)
    gsel = jnp.min(jnp.where(gl == gmax, lane_f, big), axis=-1, keepdims=True)
    gsum = jnp.sum(jnp.exp(gl - gmax), axis=-1, keepdims=True)
    g_w = 1.0 / gsum
    lane_grp = ((lane - NGRP) >> 2).astype(f32)
    emask = (lane >= NGRP) & (lane < NGRP + NE) & (lane_grp == gsel)
    el = jnp.where(emask, logits, -jnp.inf)
    v1 = jnp.max(el, axis=-1, keepdims=True)
    i1 = jnp.min(jnp.where(el == v1, lane_f, big), axis=-1, keepdims=True)
    el2 = jnp.where(lane_f == i1, -jnp.inf, el)
    v2 = jnp.max(el2, axis=-1, keepdims=True)
    i2 = jnp.min(jnp.where(el2 == v2, lane_f, big), axis=-1, keepdims=True)
    e1 = jnp.exp(v2 - v1)
    den = 1.0 + e1
    w1 = (1.0 / den) * g_w
    w2 = (e1 / den) * g_w

    m1 = lane_f == i1
    m2 = lane_f == i2
    sel = jnp.where(m1 | m2, 1.0, 0.0)
    r_i = lax.broadcasted_iota(jnp.int32, (TM, TM), 0)
    c_i = lax.broadcasted_iota(jnp.int32, (TM, TM), 1)
    before = jnp.where(c_i < r_i, 1.0, 0.0).astype(bf16)
    ranks = jnp.dot(before, sel.astype(bf16), preferred_element_type=f32)
    counts = jnp.sum(sel, axis=0, keepdims=True)
    padded = jnp.floor((counts + (UNIT - 1.0)) * (1.0 / UNIT)) * UNIT
    k_i = lax.broadcasted_iota(jnp.int32, (LANES, LANES), 0)
    l_i = lax.broadcasted_iota(jnp.int32, (LANES, LANES), 1)
    lower_lanes = jnp.where(k_i < l_i, 1.0, 0.0).astype(bf16)
    seg = jnp.dot(jnp.broadcast_to(padded, (8, LANES)).astype(bf16), lower_lanes,
                  preferred_element_type=f32)[0:1]
    posall = seg + ranks
    pos1 = jnp.sum(jnp.where(m1, posall, 0.0), axis=-1, keepdims=True)
    pos2 = jnp.sum(jnp.where(m2, posall, 0.0), axis=-1, keepdims=True)
    meta = jnp.where(lane == 0, pos1, jnp.where(lane == 1, pos2, 0.0))
    meta_ref[...] = meta
    cnt_ref[...] = jnp.broadcast_to(counts, (8, LANES))

    metaT = meta.T
    rr = lax.broadcasted_iota(jnp.int32, (RC, TM), 0).astype(f32)
    p1 = rr == metaT[0:1, :]
    p2 = rr == metaT[1:2, :]
    xc_ref[:, :D] = jnp.dot(jnp.where(p1 | p2, 1.0, 0.0).astype(bf16), h_hi,
                            preferred_element_type=f32).astype(bf16)
    w1_hi = w1.astype(bf16).astype(f32)
    w2_hi = w2.astype(bf16).astype(f32)
    g1 = jnp.where(lane == 0, w1_hi, jnp.where(lane == 1, w1 - w1_hi, 0.0)).astype(bf16)
    g2 = jnp.where(lane == 0, w2_hi, jnp.where(lane == 1, w2 - w2_hi, 0.0)).astype(bf16)
    gc = (jnp.dot(jnp.where(p1, 1.0, 0.0).astype(bf16), g1, preferred_element_type=f32)
          + jnp.dot(jnp.where(p2, 1.0, 0.0).astype(bf16), g2, preferred_element_type=f32))
    xc_ref[:, D:] = gc.astype(bf16)


def _dispatch(x, g, whi, wlo, b):
    return pl.pallas_call(
        _dispatch_kernel,
        grid=(NTM,),
        in_specs=[
            pl.BlockSpec((TM, D), lambda i: (i, 0)),
            pl.BlockSpec((1, D), lambda i: (0, 0)),
            pl.BlockSpec((D, LANES), lambda i: (0, 0)),
            pl.BlockSpec((D, LANES), lambda i: (0, 0)),
            pl.BlockSpec((1, LANES), lambda i: (0, 0)),
        ],
        out_specs=[
            pl.BlockSpec((RC, XW), lambda i: (i, 0)),
            pl.BlockSpec((TM, LANES), lambda i: (i, 0)),
            pl.BlockSpec((8, LANES), lambda i: (i, 0)),
        ],
        out_shape=[
            jax.ShapeDtypeStruct((NTM * RC, XW), bf16),
            jax.ShapeDtypeStruct((N, LANES), f32),
            jax.ShapeDtypeStruct((NTM * 8, LANES), f32),
        ],
        compiler_params=_params("parallel"),
        name="dispatch",
    )(x, g, whi, wlo, b)


def _dispatch_tables(cnt):
    i32 = jnp.int32
    n = cnt.reshape(NTM, 8, LANES)[:, 0, NGRP:NGRP + NE].astype(i32)
    units = (n + UNIT - 1) // UNIT
    seg_end = jnp.cumsum(units, axis=1)
    seg_start = seg_end - units
    col_end = jnp.cumsum(units, axis=0)
    col_start = col_end - units
    chunks = (col_end[-1] + UPC - 1) // UPC
    ch_end = jnp.cumsum(chunks)
    ch_start = ch_end - chunks
    nused = ch_end[-1]
    c = jnp.arange(NCH, dtype=i32)
    eid = jnp.minimum(jnp.sum((ch_end[None, :] <= c[:, None]).astype(i32), axis=1), NE - 1)

    src0 = jnp.arange(NTM, dtype=i32)[:, None] * UPT + seg_start
    dst0 = ch_start[None, :] * UPC + col_start
    k = jnp.arange(NCH * UPC, dtype=i32)[:, None, None]
    inside = (k >= dst0[None]) & (k < (dst0 + units)[None])
    found = jnp.sum(inside.astype(i32), axis=(1, 2))
    shift = jnp.sum(jnp.where(inside, (src0 - dst0)[None], 0), axis=(1, 2))
    ffn_src = jnp.where(found > 0, k[:, 0, 0] + shift, ZERO_UNIT_IN)

    v = jnp.arange(UPT, dtype=i32)[None, :, None]
    inside_v = (v >= seg_start[:, None, :]) & (v < seg_end[:, None, :])
    found_v = jnp.sum(inside_v.astype(i32), axis=2)
    shift_v = jnp.sum(jnp.where(inside_v, (dst0 - seg_start)[:, None, :], 0), axis=2)
    comb_src = jnp.where(found_v > 0, v[:, :, 0] + shift_v, ZERO_UNIT_OUT).reshape(-1)
    return eid, nused.reshape(1), ffn_src, comb_src


def _unit_copies(src_ref, base, n_units, src_hbm, stage, slot, sem):
    out = []
    for j in range(n_units):
        row = pl.multiple_of(src_ref[base + j] * UNIT, UNIT)
        out.append(pltpu.make_async_copy(
            src_hbm.at[pl.ds(row, UNIT), :],
            stage.at[slot, pl.ds(j * UNIT, UNIT), :],
            sem.at[slot]))
    return out


def _start_gathers(step, n_steps, n_units, src_ref, src_hbm, stage, sem):
    ahead = NSLOT - 1

    def start(s, slot):
        for cp in _unit_copies(src_ref, s * n_units, n_units, src_hbm, stage, slot, sem):
            cp.start()

    for s in range(ahead):
        @pl.when((step == 0) & (s < n_steps))
        def _():
            start(s, s)

    @pl.when(step + ahead < n_steps)
    def _():
        start(step + ahead, lax.rem(step + ahead, NSLOT))


def _wait_gather(step, n_units, src_ref, src_hbm, stage, sem):
    slot = lax.rem(step, NSLOT)
    for cp in _unit_copies(src_ref, step * n_units, n_units, src_hbm, stage, slot, sem):
        cp.wait()
    return slot


def _ffn_kernel(eid_ref, nused_ref, src_ref, xc_hbm, wg_ref, wu_ref, wd_ref, o_ref,
                stage, sem, wg_b, wu_b, wd_b):
    c = pl.program_id(0)
    nused = nused_ref[0]
    _start_gathers(c, nused, UPC, src_ref, xc_hbm, stage, sem)

    @pl.when(c < nused)
    def _():
        slot = _wait_gather(c, UPC, src_ref, xc_hbm, stage, sem)

        @pl.when((c == 0) | (eid_ref[c] != eid_ref[jnp.maximum(c - 1, 0)]))
        def _():
            wg_b[...] = wg_ref[...].astype(bf16)
            wu_b[...] = wu_ref[...].astype(bf16)
            wd_b[...] = wd_ref[...].astype(bf16)

        xs = stage[slot]
        x = xs[:, :D]
        gb = xs[:, D:].astype(f32)
        gate = gb[:, 0:1] + gb[:, 1:2]
        a = jnp.dot(x, wg_b[...], preferred_element_type=f32)
        u = jnp.dot(x, wu_b[...], preferred_element_type=f32)
        act = (a * (1.0 / (1.0 + jnp.exp(-a)))) * u * gate
        o_ref[...] = jnp.dot(act.astype(bf16), wd_b[...],
                             preferred_element_type=f32).astype(bf16)

    @pl.when(c >= nused)
    def _():
        o_ref[...] = jnp.zeros((CH, D), bf16)


def _ffn(layer, eid, nused, ffn_src, xc, wg, wu, wd):
    return pl.pallas_call(
        _ffn_kernel,
        grid_spec=pltpu.PrefetchScalarGridSpec(
            num_scalar_prefetch=3,
            grid=(NCH,),
            in_specs=[
                pl.BlockSpec(memory_space=pl.ANY),
                pl.BlockSpec((None, None, D, F), lambda c, eid, nu, src: (layer, eid[c], 0, 0)),
                pl.BlockSpec((None, None, D, F), lambda c, eid, nu, src: (layer, eid[c], 0, 0)),
                pl.BlockSpec((None, None, F, D), lambda c, eid, nu, src: (layer, eid[c], 0, 0)),
            ],
            out_specs=pl.BlockSpec((CH, D), lambda c, eid, nu, src: (c, 0)),
            scratch_shapes=[
                pltpu.VMEM((NSLOT, CH, XW), bf16),
                pltpu.SemaphoreType.DMA((NSLOT,)),
                pltpu.VMEM((D, F), bf16),
                pltpu.VMEM((D, F), bf16),
                pltpu.VMEM((F, D), bf16),
            ],
        ),
        out_shape=jax.ShapeDtypeStruct((NCH * CH, D), bf16),
        compiler_params=_params("arbitrary"),
        name="ffn",
    )(eid, nused, ffn_src, xc, wg, wu, wd)


def _combine_kernel(src_ref, x_ref, meta_ref, o_hbm, y_refs, stage, sem):
    i = pl.program_id(0)
    _start_gathers(i, NTM, UPT, src_ref, o_hbm, stage, sem)
    slot = _wait_gather(i, UPT, src_ref, o_hbm, stage, sem)

    meta = meta_ref[...]
    pos1 = meta[:, 0:1]
    pos2 = meta[:, 1:2]
    li = lax.broadcasted_iota(jnp.int32, (TM, RC), 1).astype(f32)
    pt = jnp.where((li == pos1) | (li == pos2), 1.0, 0.0).astype(bf16)
    y = x_ref[...] + jnp.dot(pt, stage[slot], preferred_element_type=f32)
    if len(y_refs) == 1:
        y_refs[0][...] = y
    else:
        @pl.when(i < NP // TM)
        def _():
            y_refs[0][...] = y

        @pl.when(i >= NP // TM)
        def _():
            y_refs[1][...] = y


def _combine_kernel_1(src_ref, x_ref, meta_ref, o_hbm, y_ref, stage, sem):
    _combine_kernel(src_ref, x_ref, meta_ref, o_hbm, (y_ref,), stage, sem)


def _combine_kernel_2(src_ref, x_ref, meta_ref, o_hbm, yp_ref, ys_ref, stage, sem):
    _combine_kernel(src_ref, x_ref, meta_ref, o_hbm, (yp_ref, ys_ref), stage, sem)


def _combine(comb_src, x, meta, o_sorted, split):
    npt = NP // TM
    if split:
        body = _combine_kernel_2
        out_specs = [pl.BlockSpec((TM, D), lambda i, src: (jnp.minimum(i, npt - 1), 0)),
                     pl.BlockSpec((TM, D), lambda i, src: (jnp.maximum(i - npt, 0), 0))]
        out_shape = [jax.ShapeDtypeStruct((NP, D), f32), jax.ShapeDtypeStruct((NS, D), f32)]
    else:
        body = _combine_kernel_1
        out_specs = pl.BlockSpec((TM, D), lambda i, src: (i, 0))
        out_shape = jax.ShapeDtypeStruct((N, D), f32)
    return pl.pallas_call(
        body,
        grid_spec=pltpu.PrefetchScalarGridSpec(
            num_scalar_prefetch=1,
            grid=(NTM,),
            in_specs=[
                pl.BlockSpec((TM, D), lambda i, src: (i, 0)),
                pl.BlockSpec((TM, LANES), lambda i, src: (i, 0)),
                pl.BlockSpec(memory_space=pl.ANY),
            ],
            out_specs=out_specs,
            scratch_shapes=[
                pltpu.VMEM((NSLOT, RC, D), bf16),
                pltpu.SemaphoreType.DMA((NSLOT,)),
            ],
        ),
        out_shape=out_shape,
        compiler_params=_params("arbitrary"),
        name="combine",
    )(comb_src, x, meta, o_sorted)


def _conv_kernel(x_ref, g_ref, win_ref, cw_ref, wout_ref, p1_ref, p2_ref,
                 y_ref, cu_ref, pad_ref):
    i = pl.program_id(0)

    @pl.when(i == 0)
    def _():
        pad_ref[0:8, :] = jnp.zeros((8, D), f32)

    x = x_ref[...]
    h = _rms(x, g_ref[...]).astype(bf16)
    bcu = jnp.dot(h, win_ref[...], preferred_element_type=f32)
    b = bcu[:, :D]
    cu = bcu[:, D:2 * D] * bcu[:, 2 * D:]
    pad_ref[8:T + 8, :] = cu
    m1 = pad_ref[7:T + 7, :]
    m2 = pad_ref[6:T + 6, :]
    t = lax.broadcasted_iota(jnp.int32, (T, 1), 0) & (TS - 1)
    is_sample = i == NT - 1
    m1 = jnp.where(is_sample & (t == 0), p1_ref[...], m1)
    m2 = jnp.where(is_sample & (t < 2), p2_ref[...], m2)
    cw = cw_ref[...]
    conv = cw[0:1] * m2 + cw[1:2] * m1 + cw[2:3] * cu
    y = jnp.dot((b * conv).astype(bf16), wout_ref[...], preferred_element_type=f32)
    y_ref[...] = x + y
    cu_ref[...] = cu
    pad_ref[0:8, :] = cu[T - 8:, :]


def _conv(x, g, win, cw, wout, p1, p2):
    return pl.pallas_call(
        _conv_kernel,
        grid=(NT,),
        in_specs=[
            pl.BlockSpec((T, D), lambda i: (i, 0)),
            pl.BlockSpec((1, D), lambda i: (0, 0)),
            pl.BlockSpec((D, 3 * D), lambda i: (0, 0)),
            pl.BlockSpec((3, D), lambda i: (0, 0)),
            pl.BlockSpec((D, D), lambda i: (0, 0)),
            pl.BlockSpec((T, D), lambda i: (0, 0)),
            pl.BlockSpec((T, D), lambda i: (0, 0)),
        ],
        out_specs=[
            pl.BlockSpec((T, D), lambda i: (i, 0)),
            pl.BlockSpec((T, D), lambda i: (jnp.where(i == NT - 1, 1, 0), 0)),
        ],
        out_shape=[
            jax.ShapeDtypeStruct((N, D), f32),
            jax.ShapeDtypeStruct((2 * T, D), f32),
        ],
        scratch_shapes=[pltpu.VMEM((T + 8, D), f32)],
        compiler_params=_params("arbitrary"),
        name="conv",
    )(x, g, win, cw, wout, p1, p2)


def _rope_tables():
    inv_freq = THETA ** (-jnp.arange(HALF, dtype=f32) / HALF)
    pos = jnp.concatenate([
        jnp.arange(NP, dtype=jnp.int32),
        PAST + jnp.tile(jnp.arange(TS, dtype=jnp.int32), NB),
    ]).astype(f32)
    ang = inv_freq[:, None] * pos[None, :]
    return jnp.cos(ang), jnp.sin(ang)


def _router_weights(w_group, b_group, w_router, b_router):
    pad = LANES - NGRP - NE
    w = jnp.concatenate([w_group, w_router, jnp.zeros((D, pad), f32)], axis=1)
    b = jnp.concatenate([b_group, b_router, jnp.zeros((pad,), f32)])[None, :]
    whi = w.astype(bf16)
    wlo = (w - whi.astype(f32)).astype(bf16)
    return whi, wlo, b


def _moe_layer(x, i, split, norm_ffn, w_group, b_group, w_router, b_router, w_gate, w_up, w_down):
    whi, wlo, b = _router_weights(w_group[i], b_group[i], w_router[i], b_router[i])
    xc, meta, cnt = _dispatch(x, norm_ffn[i][None, :], whi, wlo, b)
    eid, nused, ffn_src, comb_src = _dispatch_tables(cnt)
    o_sorted = _ffn(i, eid, nused, ffn_src, xc, w_gate, w_up, w_down)
    return _combine(comb_src, x, meta, o_sorted, split)


def kernel(x_prompt, x_sample, cache_k, cache_v, state_conv, norm_mix, w_qkv, q_norm, k_norm,
           sinks, w_o, w_in, conv_w, w_out, norm_ffn, w_group, b_group, w_router, b_router,
           w_gate, w_up, w_down):
    xp = x_prompt.reshape(NP, D)
    xs = x_sample.reshape(NS, D)
    moe_w = (norm_ffn, w_group, b_group, w_router, b_router, w_gate, w_up, w_down)

    cos, sin = _rope_tables()
    wT = w_qkv[0].T
    wT_hi = wT.astype(bf16)
    wT_lo = (wT - wT_hi.astype(f32)).astype(bf16)
    qT, ktok, vtok, vT, q_s = _qkv(xp, xs, norm_mix[0][None, :], wT_hi, wT_lo,
                                   q_norm[0][:, None], k_norm[0][:, None], cos, sin)
    sink_rows = jnp.repeat(sinks[0].reshape(KVH, G), TQ, axis=1)[:, None, :]
    o_p = _attn_prompt(qT, ktok, vT, sink_rows)

    qs = q_s.reshape(KVH, G, HD, NB, TS).transpose(3, 0, 1, 4, 2)
    zq = jnp.zeros_like(qs[:, 0])
    qbd = jnp.stack([jnp.concatenate([qs[:, 0], zq], axis=-1),
                     jnp.concatenate([zq, qs[:, 1]], axis=-1)], axis=1)
    qbd = qbd.reshape(NB, H * TS, KVH * HD)
    k_new = ktok[NP:].reshape(NB, TS, KVH * HD)
    v_new = vtok[NP:].reshape(NB, TS, KVH * HD)
    pad4 = jnp.zeros((NB, 8 - TS, KVH * HD), f32)
    kc = cache_k[0].reshape(NB, WIN, KVH * HD)
    vc = cache_v[0].reshape(NB, WIN, KVH * HD)
    sink_col = jnp.repeat(sinks[0], TS)[:, None]
    o_s = _attn_sample(qbd, kc, vc, jnp.concatenate([k_new, pad4], axis=1),
                       jnp.concatenate([v_new, pad4], axis=1), sink_col)
    o_s = o_s.reshape(NB, KVH, G, TS, KVH, HD)
    o_s = jnp.stack([o_s[:, 0, :, :, 0], o_s[:, 1, :, :, 1]], axis=1)
    o_s = o_s.transpose(0, 3, 1, 2, 4).reshape(NS, H * HD)
    wo_hi = w_o[0].astype(bf16)
    wo_lo = (w_o[0] - wo_hi.astype(f32)).astype(bf16)
    x = _proj(xp, xs, o_p, o_s, wo_hi, wo_lo)
    x = _moe_layer(x, 0, False, *moe_w)

    new_k_prompt = ktok[NP - WIN:NP].reshape(1, 1, WIN, KVH, HD)
    new_v_prompt = vtok[NP - WIN:NP].reshape(1, 1, WIN, KVH, HD)
    new_k_sample = jnp.concatenate([kc[:, TS:], k_new], axis=1).reshape(1, NB, WIN, KVH, HD)
    new_v_sample = jnp.concatenate([vc[:, TS:], v_new], axis=1).reshape(1, NB, WIN, KVH, HD)

    st = state_conv[0]
    z = jnp.zeros((NB, 1, D), f32)
    p1 = jnp.concatenate([st[:, 1:2], z, z, z], axis=1).reshape(NS, D)
    p2 = jnp.concatenate([st[:, 0:1], st[:, 1:2], z, z], axis=1).reshape(NS, D)
    x, cu = _conv(x, norm_mix[1][None, :], w_in[0].astype(bf16), conv_w[0],
                  w_out[0].astype(bf16), p1, p2)
    y_prompt, y_sample = _moe_layer(x, 1, True, *moe_w)

    new_conv_prompt = cu[T - 2:T].reshape(1, 1, 2, D)
    new_conv_sample = cu[T:].reshape(NB, TS, D)[:, TS - 2:][None]

    y_prompt = y_prompt.reshape(1, NP, D)
    y_sample = y_sample.reshape(NB, TS, D)
    return (y_prompt, y_sample, new_k_prompt, new_v_prompt, new_conv_prompt,
            new_k_sample, new_v_sample, new_conv_sample)
```

```python
import jax
import jax.numpy as jnp
from jax import lax
from jax.experimental import pallas as pl
from jax.experimental.pallas import tpu as pltpu

D = 1024
NP = 16384
NB = 128
TS = 4
NS = NB * TS
N = NP + NS
PAST = 16384
H = 16
KVH = 2
G = H // KVH
HD = 64
HALF = HD // 2
QKV = (H + 2 * KVH) * HD
WIN = 128
THETA = 10000.0
NGRP = 4
EPG = 4
NE = NGRP * EPG
TOPK = 2
F = 512
EPS = 1e-6
SCALE = HD ** -0.5

T = 512
NT = N // T
TQ = 128
BB = 8
LANES = 128
VMEM_LIMIT = 50 * 1024 * 1024

TM = 256
NTM = N // TM
UNIT = 16
RC = TOPK * TM + NE * UNIT
UPT = RC // UNIT
XW = D + LANES
CH = 512
CHB = 256
UPC = CH // UNIT
NCH = -(-(NTM * (UPT - 1) + NE * (UPC - 1)) // UPC) + 1
ZERO_UNIT_IN = UPT - 1
ZERO_UNIT_OUT = (NCH - 1) * UPC
NSLOT = 3

f32 = jnp.float32
bf16 = jnp.bfloat16


def _params(*sem):
    return pltpu.CompilerParams(dimension_semantics=sem, vmem_limit_bytes=VMEM_LIMIT)


def _rms(x, g):
    ms = jnp.mean(x * x, axis=-1, keepdims=True)
    return x * lax.rsqrt(ms + EPS) * g


def _pick(i, prompt_ref, sample_ref):
    return jnp.where(i == NT - 1, sample_ref[...], prompt_ref[...])


def _stream_specs():
    return [pl.BlockSpec((T, D), lambda i: (jnp.minimum(i, NT - 2), 0)),
            pl.BlockSpec((T, D), lambda i: (0, 0))]


def _split(x):
    hi = x.astype(bf16)
    return hi, (x - hi.astype(f32)).astype(bf16)


def _qkv_kernel(xp_ref, xs_ref, g_ref, wT_ref, wTlo_ref, qn_ref, kn_ref, cos_ref, sin_ref,
                qT_ref, ktok_ref, vtok_ref, vT_ref, qs_ref, acc_ref):
    i = pl.program_id(0)
    hf = _rms(_pick(i, xp_ref, xs_ref), g_ref[...])
    h, h_lo = _split(hf)
    nt = (((1,), (1,)), ((), ()))
    acc_ref[...] = lax.dot_general(wT_ref[...], h, nt, preferred_element_type=f32)

    @pl.when(i == NT - 1)
    def _():
        acc_ref[...] += (lax.dot_general(wT_ref[...], h_lo, nt, preferred_element_type=f32)
                         + lax.dot_general(wTlo_ref[...], h, nt, preferred_element_type=f32))

    qkvT = acc_ref
    cos = cos_ref[...]
    sin = sin_ref[...]

    def norm_rope(blk, gcol):
        ms = jnp.mean(blk * blk, axis=0, keepdims=True)
        y = blk * lax.rsqrt(ms + EPS) * gcol
        y1 = y[:HALF]
        y2 = y[HALF:]
        return y1 * cos - y2 * sin, y2 * cos + y1 * sin

    qn = qn_ref[...]
    for hd in range(H):
        o1, o2 = norm_rope(qkvT[hd * HD:(hd + 1) * HD], qn)
        qT_ref[hd * HD:hd * HD + HALF, :] = (o1 * SCALE).astype(bf16)
        qT_ref[hd * HD + HALF:(hd + 1) * HD, :] = (o2 * SCALE).astype(bf16)
        qs_ref[hd * HD:hd * HD + HALF, :] = o1 * SCALE
        qs_ref[hd * HD + HALF:(hd + 1) * HD, :] = o2 * SCALE
    kn = kn_ref[...]
    ks = []
    for j in range(KVH):
        o1, o2 = norm_rope(qkvT[H * HD + j * HD:H * HD + (j + 1) * HD], kn)
        ks += [o1, o2]
    kT = jnp.concatenate(ks, axis=0)
    ktok_ref[...] = kT.T
    vT = qkvT[(H + KVH) * HD:]
    vtok_ref[...] = vT.T
    vT_ref[...] = vT.astype(bf16)


def _qkv(xp, xs, g, wT, wT_lo, qn, kn, cos, sin):
    return pl.pallas_call(
        _qkv_kernel,
        grid=(NT,),
        in_specs=_stream_specs() + [
            pl.BlockSpec((1, D), lambda i: (0, 0)),
            pl.BlockSpec((QKV, D), lambda i: (0, 0)),
            pl.BlockSpec((QKV, D), lambda i: (0, 0)),
            pl.BlockSpec((HD, 1), lambda i: (0, 0)),
            pl.BlockSpec((HD, 1), lambda i: (0, 0)),
            pl.BlockSpec((HALF, T), lambda i: (0, i)),
            pl.BlockSpec((HALF, T), lambda i: (0, i)),
        ],
        out_specs=[
            pl.BlockSpec((H * HD, T), lambda i: (0, i)),
            pl.BlockSpec((T, KVH * HD), lambda i: (i, 0)),
            pl.BlockSpec((T, KVH * HD), lambda i: (i, 0)),
            pl.BlockSpec((KVH * HD, T), lambda i: (0, i)),
            pl.BlockSpec((H * HD, T), lambda i: (0, 0)),
        ],
        out_shape=[
            jax.ShapeDtypeStruct((H * HD, N), bf16),
            jax.ShapeDtypeStruct((N, KVH * HD), f32),
            jax.ShapeDtypeStruct((N, KVH * HD), f32),
            jax.ShapeDtypeStruct((KVH * HD, N), bf16),
            jax.ShapeDtypeStruct((H * HD, T), f32),
        ],
        scratch_shapes=[pltpu.VMEM((QKV, T), f32)],
        compiler_params=_params("arbitrary"),
        name="qkv",
    )(xp, xs, g, wT, wT_lo, qn, kn, cos, sin)


def _attn_prompt_kernel(qT_ref, kp_ref, kc_ref, vp_ref, vc_ref, sink_ref, o_ref):
    j = pl.program_id(0)
    kk = jnp.concatenate([kp_ref[...], kc_ref[...]], axis=0).astype(bf16)
    vv = jnp.concatenate([vp_ref[...], vc_ref[...]], axis=1)
    s_idx = lax.broadcasted_iota(jnp.int32, (2 * TQ, G * TQ), 0)
    t_idx = lax.broadcasted_iota(jnp.int32, (2 * TQ, G * TQ), 1) & (TQ - 1)
    dist = t_idx + TQ - s_idx
    valid = (dist >= 0) & (dist <= WIN) & ((s_idx >= TQ) | (j > 0))
    pieces = []
    for g in range(KVH):
        qg = jnp.concatenate(
            [qT_ref[(g * G + hh) * HD:(g * G + hh + 1) * HD, :] for hh in range(G)],
            axis=1)
        zeros = jnp.zeros_like(qg)
        rhs = jnp.concatenate([qg, zeros] if g == 0 else [zeros, qg], axis=0)
        sT = jnp.dot(kk, rhs, preferred_element_type=f32)
        sT = jnp.where(valid, sT, -jnp.inf)
        sink = sink_ref[g]
        m = jnp.maximum(jnp.max(sT, axis=0, keepdims=True), sink)
        p = jnp.exp(sT - m)
        l = jnp.sum(p, axis=0, keepdims=True) + jnp.exp(sink - m)
        oT = jnp.dot(vv[g * HD:(g + 1) * HD, :], p.astype(bf16),
                     preferred_element_type=f32)
        oT = oT / l
        pieces += [oT[:, hh * TQ:(hh + 1) * TQ] for hh in range(G)]
    oT_all = jnp.concatenate(pieces, axis=0)
    o_ref[...] = oT_all.T.astype(bf16)


def _attn_prompt(qT, ktok, vT, sink_rows):
    nq = NP // TQ
    return pl.pallas_call(
        _attn_prompt_kernel,
        grid=(nq,),
        in_specs=[
            pl.BlockSpec((H * HD, TQ), lambda j: (0, j)),
            pl.BlockSpec((TQ, KVH * HD), lambda j: (jnp.maximum(j - 1, 0), 0)),
            pl.BlockSpec((TQ, KVH * HD), lambda j: (j, 0)),
            pl.BlockSpec((KVH * HD, TQ), lambda j: (0, jnp.maximum(j - 1, 0))),
            pl.BlockSpec((KVH * HD, TQ), lambda j: (0, j)),
            pl.BlockSpec((KVH, 1, G * TQ), lambda j: (0, 0, 0)),
        ],
        out_specs=pl.BlockSpec((TQ, H * HD), lambda j: (j, 0)),
        out_shape=jax.ShapeDtypeStruct((NP, H * HD), bf16),
        compiler_params=_params("parallel"),
        name="attn_prompt",
    )(qT, ktok, ktok, vT, vT, sink_rows)


def _attn_sample_kernel(q_ref, kc_ref, vc_ref, kn_ref, vn_ref, sink_ref, o_ref):
    rows = H * TS
    t1 = lax.broadcasted_iota(jnp.int32, (rows, WIN), 0) & (TS - 1)
    s1 = lax.broadcasted_iota(jnp.int32, (rows, WIN), 1)
    valid1 = s1 >= t1
    t2 = lax.broadcasted_iota(jnp.int32, (rows, 8), 0) & (TS - 1)
    s2 = lax.broadcasted_iota(jnp.int32, (rows, 8), 1)
    valid2 = s2 <= t2
    sink = sink_ref[...]
    nt = (((1,), (1,)), ((), ()))
    nn = (((1,), (0,)), ((), ()))

    def dot3(a, b, dims):
        a_hi, a_lo = _split(a)
        b_hi, b_lo = _split(b)
        return (lax.dot_general(a_hi, b_hi, dims, preferred_element_type=f32)
                + lax.dot_general(a_lo, b_hi, dims, preferred_element_type=f32)
                + lax.dot_general(a_hi, b_lo, dims, preferred_element_type=f32))

    for b in range(BB):
        q = q_ref[b]
        sc = dot3(q, kc_ref[b], nt)
        sn = dot3(q, kn_ref[b], nt)
        sc = jnp.where(valid1, sc, -jnp.inf)
        sn = jnp.where(valid2, sn, -jnp.inf)
        m = jnp.maximum(jnp.maximum(jnp.max(sc, axis=-1, keepdims=True),
                                    jnp.max(sn, axis=-1, keepdims=True)), sink)
        pc = jnp.exp(sc - m)
        pn = jnp.exp(sn - m)
        l = (jnp.sum(pc, axis=-1, keepdims=True) + jnp.sum(pn, axis=-1, keepdims=True)
             + jnp.exp(sink - m))
        o = dot3(pc, vc_ref[b], nn) + dot3(pn, vn_ref[b], nn)
        o_ref[b] = o / l


def _attn_sample(qbd, kc, vc, kn, vn, sink_col):
    rows = H * TS
    return pl.pallas_call(
        _attn_sample_kernel,
        grid=(NB // BB,),
        in_specs=[
            pl.BlockSpec((BB, rows, KVH * HD), lambda i: (i, 0, 0)),
            pl.BlockSpec((BB, WIN, KVH * HD), lambda i: (i, 0, 0)),
            pl.BlockSpec((BB, WIN, KVH * HD), lambda i: (i, 0, 0)),
            pl.BlockSpec((BB, 8, KVH * HD), lambda i: (i, 0, 0)),
            pl.BlockSpec((BB, 8, KVH * HD), lambda i: (i, 0, 0)),
            pl.BlockSpec((rows, 1), lambda i: (0, 0)),
        ],
        out_specs=pl.BlockSpec((BB, rows, KVH * HD), lambda i: (i, 0, 0)),
        out_shape=jax.ShapeDtypeStruct((NB, rows, KVH * HD), f32),
        compiler_params=_params("parallel"),
        name="attn_sample",
    )(qbd, kc, vc, kn, vn, sink_col)


def _proj_kernel(xp_ref, xs_ref, op_ref, os_ref, w_ref, wlo_ref, y_ref):
    i = pl.program_id(0)

    @pl.when(i < NT - 1)
    def _():
        y_ref[...] = xp_ref[...] + jnp.dot(op_ref[...], w_ref[...], preferred_element_type=f32)

    @pl.when(i == NT - 1)
    def _():
        o_hi, o_lo = _split(os_ref[...])
        w = w_ref[...]
        y_ref[...] = (xs_ref[...] + jnp.dot(o_hi, w, preferred_element_type=f32)
                      + jnp.dot(o_lo, w, preferred_element_type=f32)
                      + jnp.dot(o_hi, wlo_ref[...], preferred_element_type=f32))


def _proj(xp, xs, o_p, o_s, w, w_lo):
    return pl.pallas_call(
        _proj_kernel,
        grid=(NT,),
        in_specs=_stream_specs() + _stream_specs() + [
            pl.BlockSpec((D, D), lambda i: (0, 0)),
            pl.BlockSpec((D, D), lambda i: (0, 0)),
        ],
        out_specs=pl.BlockSpec((T, D), lambda i: (i, 0)),
        out_shape=jax.ShapeDtypeStruct((N, D), f32),
        compiler_params=_params("parallel"),
        name="proj",
    )(xp, xs, o_p, o_s, w, w_lo)


def _dispatch_kernel(x_ref, g_ref, whi_ref, wlo_ref, b_ref, xc_ref, meta_ref, cnt_ref):
    hf = _rms(x_ref[...], g_ref[...])
    h_hi = hf.astype(bf16)
    h_lo = (hf - h_hi.astype(f32)).astype(bf16)
    whi = whi_ref[...]
    logits = (jnp.dot(h_hi, whi, preferred_element_type=f32)
              + jnp.dot(h_lo, whi, preferred_element_type=f32)
              + jnp.dot(h_hi, wlo_ref[...], preferred_element_type=f32)
              + b_ref[...])
    lane = lax.broadcasted_iota(jnp.int32, logits.shape, 1)
    lane_f = lane.astype(f32)
    big = float(LANES)
    gl = jnp.where(lane < NGRP, logits, -jnp.inf)
    gmax = jnp.max(gl, axis=-1, keepdims=True)
    gsel = jnp.min(jnp.where(gl == gmax, lane_f, big), axis=-1, keepdims=True)
    gsum = jnp.sum(jnp.exp(gl - gmax), axis=-1, keepdims=True)
    g_w = 1.0 / gsum
    lane_grp = ((lane - NGRP) >> 2).astype(f32)
    emask = (lane >= NGRP) & (lane < NGRP + NE) & (lane_grp == gsel)
    el = jnp.where(emask, logits, -jnp.inf)
    v1 = jnp.max(el, axis=-1, keepdims=True)
    i1 = jnp.min(jnp.where(el == v1, lane_f, big), axis=-1, keepdims=True)
    el2 = jnp.where(lane_f == i1, -jnp.inf, el)
    v2 = jnp.max(el2, axis=-1, keepdims=True)
    i2 = jnp.min(jnp.where(el2 == v2, lane_f, big), axis=-1, keepdims=True)
    e1 = jnp.exp(v2 - v1)
    den = 1.0 + e1
    w1 = (1.0 / den) * g_w
    w2 = (e1 / den) * g_w

    m1 = lane_f == i1
    m2 = lane_f == i2
    sel = jnp.where(m1 | m2, 1.0, 0.0)
    r_i = lax.broadcasted_iota(jnp.int32, (TM, TM), 0)
    c_i = lax.broadcasted_iota(jnp.int32, (TM, TM), 1)
    before = jnp.where(c_i < r_i, 1.0, 0.0).astype(bf16)
    ranks = jnp.dot(before, sel.astype(bf16), preferred_element_type=f32)
    counts = jnp.sum(sel, axis=0, keepdims=True)
    padded = jnp.floor((counts + (UNIT - 1.0)) * (1.0 / UNIT)) * UNIT
    k_i = lax.broadcasted_iota(jnp.int32, (LANES, LANES), 0)
    l_i = lax.broadcasted_iota(jnp.int32, (LANES, LANES), 1)
    lower_lanes = jnp.where(k_i < l_i, 1.0, 0.0).astype(bf16)
    seg = jnp.dot(jnp.broadcast_to(padded, (8, LANES)).astype(bf16), lower_lanes,
                  preferred_element_type=f32)[0:1]
    posall = seg + ranks
    pos1 = jnp.sum(jnp.where(m1, posall, 0.0), axis=-1, keepdims=True)
    pos2 = jnp.sum(jnp.where(m2, posall, 0.0), axis=-1, keepdims=True)
    meta = jnp.where(lane == 0, pos1, jnp.where(lane == 1, pos2, 0.0))
    meta_ref[...] = meta
    cnt_ref[...] = jnp.broadcast_to(counts, (8, LANES))

    metaT = meta.T
    rr = lax.broadcasted_iota(jnp.int32, (RC, TM), 0).astype(f32)
    p1 = rr == metaT[0:1, :]
    p2 = rr == metaT[1:2, :]
    xc_ref[:, :D] = jnp.dot(jnp.where(p1 | p2, 1.0, 0.0).astype(bf16), h_hi,
                            preferred_element_type=f32).astype(bf16)
    w1_hi = w1.astype(bf16).astype(f32)
    w2_hi = w2.astype(bf16).astype(f32)
    g1 = jnp.where(lane == 0, w1_hi, jnp.where(lane == 1, w1 - w1_hi, 0.0)).astype(bf16)
    g2 = jnp.where(lane == 0, w2_hi, jnp.where(lane == 1, w2 - w2_hi, 0.0)).astype(bf16)
    gc = (jnp.dot(jnp.where(p1, 1.0, 0.0).astype(bf16), g1, preferred_element_type=f32)
          + jnp.dot(jnp.where(p2, 1.0, 0.0).astype(bf16), g2, preferred_element_type=f32))
    xc_ref[:, D:] = gc.astype(bf16)


def _dispatch(x, g, whi, wlo, b):
    return pl.pallas_call(
        _dispatch_kernel,
        grid=(NTM,),
        in_specs=[
            pl.BlockSpec((TM, D), lambda i: (i, 0)),
            pl.BlockSpec((1, D), lambda i: (0, 0)),
            pl.BlockSpec((D, LANES), lambda i: (0, 0)),
            pl.BlockSpec((D, LANES), lambda i: (0, 0)),
            pl.BlockSpec((1, LANES), lambda i: (0, 0)),
        ],
        out_specs=[
            pl.BlockSpec((RC, XW), lambda i: (i, 0)),
            pl.BlockSpec((TM, LANES), lambda i: (i, 0)),
            pl.BlockSpec((8, LANES), lambda i: (i, 0)),
        ],
        out_shape=[
            jax.ShapeDtypeStruct((NTM * RC, XW), bf16),
            jax.ShapeDtypeStruct((N, LANES), f32),
            jax.ShapeDtypeStruct((NTM * 8, LANES), f32),
        ],
        compiler_params=_params("parallel"),
        name="dispatch",
    )(x, g, whi, wlo, b)


def _dispatch_tables(cnt):
    i32 = jnp.int32
    n = cnt.reshape(NTM, 8, LANES)[:, 0, NGRP:NGRP + NE].astype(i32)
    units = (n + UNIT - 1) // UNIT
    seg_end = jnp.cumsum(units, axis=1)
    seg_start = seg_end - units
    col_end = jnp.cumsum(units, axis=0)
    col_start = col_end - units
    chunks = (col_end[-1] + UPC - 1) // UPC
    ch_end = jnp.cumsum(chunks)
    ch_start = ch_end - chunks
    nused = ch_end[-1]
    c = jnp.arange(NCH, dtype=i32)
    eid = jnp.minimum(jnp.sum((ch_end[None, :] <= c[:, None]).astype(i32), axis=1), NE - 1)

    src0 = jnp.arange(NTM, dtype=i32)[:, None] * UPT + seg_start
    dst0 = ch_start[None, :] * UPC + col_start
    k = jnp.arange(NCH * UPC, dtype=i32)[:, None, None]
    inside = (k >= dst0[None]) & (k < (dst0 + units)[None])
    found = jnp.sum(inside.astype(i32), axis=(1, 2))
    shift = jnp.sum(jnp.where(inside, (src0 - dst0)[None], 0), axis=(1, 2))
    ffn_src = jnp.where(found > 0, k[:, 0, 0] + shift, ZERO_UNIT_IN)

    v = jnp.arange(UPT, dtype=i32)[None, :, None]
    inside_v = (v >= seg_start[:, None, :]) & (v < seg_end[:, None, :])
    found_v = jnp.sum(inside_v.astype(i32), axis=2)
    shift_v = jnp.sum(jnp.where(inside_v, (dst0 - seg_start)[:, None, :], 0), axis=2)
    comb_src = jnp.where(found_v > 0, v[:, :, 0] + shift_v, ZERO_UNIT_OUT).reshape(-1)
    return eid, nused.reshape(1), ffn_src, comb_src


def _unit_copies(src_ref, base, n_units, src_hbm, stage, slot, sem):
    out = []
    for j in range(n_units):
        row = pl.multiple_of(src_ref[base + j] * UNIT, UNIT)
        out.append(pltpu.make_async_copy(
            src_hbm.at[pl.ds(row, UNIT), :],
            stage.at[slot, pl.ds(j * UNIT, UNIT), :],
            sem.at[slot]))
    return out


def _start_gathers(step, n_steps, n_units, src_ref, src_hbm, stage, sem):
    ahead = NSLOT - 1

    def start(s, slot):
        for cp in _unit_copies(src_ref, s * n_units, n_units, src_hbm, stage, slot, sem):
            cp.start()

    for s in range(ahead):
        @pl.when((step == 0) & (s < n_steps))
        def _():
            start(s, s)

    @pl.when(step + ahead < n_steps)
    def _():
        start(step + ahead, lax.rem(step + ahead, NSLOT))


def _wait_gather(step, n_units, src_ref, src_hbm, stage, sem):
    slot = lax.rem(step, NSLOT)
    for cp in _unit_copies(src_ref, step * n_units, n_units, src_hbm, stage, slot, sem):
        cp.wait()
    return slot


def _ffn_kernel(eid_ref, nused_ref, src_ref, xc_hbm, wg_ref, wu_ref, wd_ref, o_ref,
                stage, sem, wg_b, wu_b, wd_b):
    c = pl.program_id(0)
    nused = nused_ref[0]
    _start_gathers(c, nused, UPC, src_ref, xc_hbm, stage, sem)

    @pl.when(c < nused)
    def _():
        slot = _wait_gather(c, UPC, src_ref, xc_hbm, stage, sem)

        @pl.when((c == 0) | (eid_ref[c] != eid_ref[jnp.maximum(c - 1, 0)]))
        def _():
            wg_b[...] = wg_ref[...].astype(bf16)
            wu_b[...] = wu_ref[...].astype(bf16)
            wd_b[...] = wd_ref[...].astype(bf16)

        for r in range(0, CH, CHB):
            xs = stage[slot, r:r + CHB, :]
            x = xs[:, :D]
            gb = xs[:, D:].astype(f32)
            gate = gb[:, 0:1] + gb[:, 1:2]
            a = jnp.dot(x, wg_b[...], preferred_element_type=f32)
            u = jnp.dot(x, wu_b[...], preferred_element_type=f32)
            act = (a * (1.0 / (1.0 + jnp.exp(-a)))) * u * gate
            o_ref[r:r + CHB, :] = jnp.dot(act.astype(bf16), wd_b[...],
                                          preferred_element_type=f32).astype(bf16)

    @pl.when(c >= nused)
    def _():
        o_ref[...] = jnp.zeros((CH, D), bf16)


def _ffn(layer, eid, nused, ffn_src, xc, wg, wu, wd):
    return pl.pallas_call(
        _ffn_kernel,
        grid_spec=pltpu.PrefetchScalarGridSpec(
            num_scalar_prefetch=3,
            grid=(NCH,),
            in_specs=[
                pl.BlockSpec(memory_space=pl.ANY),
                pl.BlockSpec((None, None, D, F), lambda c, eid, nu, src: (layer, eid[c], 0, 0)),
                pl.BlockSpec((None, None, D, F), lambda c, eid, nu, src: (layer, eid[c], 0, 0)),
                pl.BlockSpec((None, None, F, D), lambda c, eid, nu, src: (layer, eid[c], 0, 0)),
            ],
            out_specs=pl.BlockSpec((CH, D), lambda c, eid, nu, src: (c, 0)),
            scratch_shapes=[
                pltpu.VMEM((NSLOT, CH, XW), bf16),
                pltpu.SemaphoreType.DMA((NSLOT,)),
                pltpu.VMEM((D, F), bf16),
                pltpu.VMEM((D, F), bf16),
                pltpu.VMEM((F, D), bf16),
            ],
        ),
        out_shape=jax.ShapeDtypeStruct((NCH * CH, D), bf16),
        compiler_params=_params("arbitrary"),
        name="ffn",
    )(eid, nused, ffn_src, xc, wg, wu, wd)


def _combine_kernel(src_ref, x_ref, meta_ref, o_hbm, y_refs, stage, sem):
    i = pl.program_id(0)
    _start_gathers(i, NTM, UPT, src_ref, o_hbm, stage, sem)
    slot = _wait_gather(i, UPT, src_ref, o_hbm, stage, sem)

    meta = meta_ref[...]
    pos1 = meta[:, 0:1]
    pos2 = meta[:, 1:2]
    li = lax.broadcasted_iota(jnp.int32, (TM, RC), 1).astype(f32)
    pt = jnp.where((li == pos1) | (li == pos2), 1.0, 0.0).astype(bf16)
    y = x_ref[...] + jnp.dot(pt, stage[slot], preferred_element_type=f32)
    if len(y_refs) == 1:
        y_refs[0][...] = y
    else:
        @pl.when(i < NP // TM)
        def _():
            y_refs[0][...] = y

        @pl.when(i >= NP // TM)
        def _():
            y_refs[1][...] = y


def _combine_kernel_1(src_ref, x_ref, meta_ref, o_hbm, y_ref, stage, sem):
    _combine_kernel(src_ref, x_ref, meta_ref, o_hbm, (y_ref,), stage, sem)


def _combine_kernel_2(src_ref, x_ref, meta_ref, o_hbm, yp_ref, ys_ref, stage, sem):
    _combine_kernel(src_ref, x_ref, meta_ref, o_hbm, (yp_ref, ys_ref), stage, sem)


def _combine(comb_src, x, meta, o_sorted, split):
    npt = NP // TM
    if split:
        body = _combine_kernel_2
        out_specs = [pl.BlockSpec((TM, D), lambda i, src: (jnp.minimum(i, npt - 1), 0)),
                     pl.BlockSpec((TM, D), lambda i, src: (jnp.maximum(i - npt, 0), 0))]
        out_shape = [jax.ShapeDtypeStruct((NP, D), f32), jax.ShapeDtypeStruct((NS, D), f32)]
    else:
        body = _combine_kernel_1
        out_specs = pl.BlockSpec((TM, D), lambda i, src: (i, 0))
        out_shape = jax.ShapeDtypeStruct((N, D), f32)
    return pl.pallas_call(
        body,
        grid_spec=pltpu.PrefetchScalarGridSpec(
            num_scalar_prefetch=1,
            grid=(NTM,),
            in_specs=[
                pl.BlockSpec((TM, D), lambda i, src: (i, 0)),
                pl.BlockSpec((TM, LANES), lambda i, src: (i, 0)),
                pl.BlockSpec(memory_space=pl.ANY),
            ],
            out_specs=out_specs,
            scratch_shapes=[
                pltpu.VMEM((NSLOT, RC, D), bf16),
                pltpu.SemaphoreType.DMA((NSLOT,)),
            ],
        ),
        out_shape=out_shape,
        compiler_params=_params("arbitrary"),
        name="combine",
    )(comb_src, x, meta, o_sorted)


def _conv_kernel(x_ref, g_ref, win_ref, cw_ref, wout_ref, p1_ref, p2_ref,
                 y_ref, cu_ref, pad_ref):
    i = pl.program_id(0)

    @pl.when(i == 0)
    def _():
        pad_ref[0:8, :] = jnp.zeros((8, D), f32)

    x = x_ref[...]
    h = _rms(x, g_ref[...]).astype(bf16)
    bcu = jnp.dot(h, win_ref[...], preferred_element_type=f32)
    b = bcu[:, :D]
    cu = bcu[:, D:2 * D] * bcu[:, 2 * D:]
    pad_ref[8:T + 8, :] = cu
    m1 = pad_ref[7:T + 7, :]
    m2 = pad_ref[6:T + 6, :]
    t = lax.broadcasted_iota(jnp.int32, (T, 1), 0) & (TS - 1)
    is_sample = i == NT - 1
    m1 = jnp.where(is_sample & (t == 0), p1_ref[...], m1)
    m2 = jnp.where(is_sample & (t < 2), p2_ref[...], m2)
    cw = cw_ref[...]
    conv = cw[0:1] * m2 + cw[1:2] * m1 + cw[2:3] * cu
    y = jnp.dot((b * conv).astype(bf16), wout_ref[...], preferred_element_type=f32)
    y_ref[...] = x + y
    cu_ref[...] = cu
    pad_ref[0:8, :] = cu[T - 8:, :]


def _conv(x, g, win, cw, wout, p1, p2):
    return pl.pallas_call(
        _conv_kernel,
        grid=(NT,),
        in_specs=[
            pl.BlockSpec((T, D), lambda i: (i, 0)),
            pl.BlockSpec((1, D), lambda i: (0, 0)),
            pl.BlockSpec((D, 3 * D), lambda i: (0, 0)),
            pl.BlockSpec((3, D), lambda i: (0, 0)),
            pl.BlockSpec((D, D), lambda i: (0, 0)),
            pl.BlockSpec((T, D), lambda i: (0, 0)),
            pl.BlockSpec((T, D), lambda i: (0, 0)),
        ],
        out_specs=[
            pl.BlockSpec((T, D), lambda i: (i, 0)),
            pl.BlockSpec((T, D), lambda i: (jnp.where(i == NT - 1, 1, 0), 0)),
        ],
        out_shape=[
            jax.ShapeDtypeStruct((N, D), f32),
            jax.ShapeDtypeStruct((2 * T, D), f32),
        ],
        scratch_shapes=[pltpu.VMEM((T + 8, D), f32)],
        compiler_params=_params("arbitrary"),
        name="conv",
    )(x, g, win, cw, wout, p1, p2)


def _rope_tables():
    inv_freq = THETA ** (-jnp.arange(HALF, dtype=f32) / HALF)
    pos = jnp.concatenate([
        jnp.arange(NP, dtype=jnp.int32),
        PAST + jnp.tile(jnp.arange(TS, dtype=jnp.int32), NB),
    ]).astype(f32)
    ang = inv_freq[:, None] * pos[None, :]
    return jnp.cos(ang), jnp.sin(ang)


def _router_weights(w_group, b_group, w_router, b_router):
    pad = LANES - NGRP - NE
    w = jnp.concatenate([w_group, w_router, jnp.zeros((D, pad), f32)], axis=1)
    b = jnp.concatenate([b_group, b_router, jnp.zeros((pad,), f32)])[None, :]
    whi = w.astype(bf16)
    wlo = (w - whi.astype(f32)).astype(bf16)
    return whi, wlo, b


def _moe_layer(x, i, split, norm_ffn, w_group, b_group, w_router, b_router, w_gate, w_up, w_down):
    whi, wlo, b = _router_weights(w_group[i], b_group[i], w_router[i], b_router[i])
    xc, meta, cnt = _dispatch(x, norm_ffn[i][None, :], whi, wlo, b)
    eid, nused, ffn_src, comb_src = _dispatch_tables(cnt)
    o_sorted = _ffn(i, eid, nused, ffn_src, xc, w_gate, w_up, w_down)
    return _combine(comb_src, x, meta, o_sorted, split)


def kernel(x_prompt, x_sample, cache_k, cache_v, state_conv, norm_mix, w_qkv, q_norm, k_norm,
           sinks, w_o, w_in, conv_w, w_out, norm_ffn, w_group, b_group, w_router, b_router,
           w_gate, w_up, w_down):
    xp = x_prompt.reshape(NP, D)
    xs = x_sample.reshape(NS, D)
    moe_w = (norm_ffn, w_group, b_group, w_router, b_router, w_gate, w_up, w_down)

    cos, sin = _rope_tables()
    wT = w_qkv[0].T
    wT_hi = wT.astype(bf16)
    wT_lo = (wT - wT_hi.astype(f32)).astype(bf16)
    qT, ktok, vtok, vT, q_s = _qkv(xp, xs, norm_mix[0][None, :], wT_hi, wT_lo,
                                   q_norm[0][:, None], k_norm[0][:, None], cos, sin)
    sink_rows = jnp.repeat(sinks[0].reshape(KVH, G), TQ, axis=1)[:, None, :]
    o_p = _attn_prompt(qT, ktok, vT, sink_rows)

    qs = q_s.reshape(KVH, G, HD, NB, TS).transpose(3, 0, 1, 4, 2)
    zq = jnp.zeros_like(qs[:, 0])
    qbd = jnp.stack([jnp.concatenate([qs[:, 0], zq], axis=-1),
                     jnp.concatenate([zq, qs[:, 1]], axis=-1)], axis=1)
    qbd = qbd.reshape(NB, H * TS, KVH * HD)
    k_new = ktok[NP:].reshape(NB, TS, KVH * HD)
    v_new = vtok[NP:].reshape(NB, TS, KVH * HD)
    pad4 = jnp.zeros((NB, 8 - TS, KVH * HD), f32)
    kc = cache_k[0].reshape(NB, WIN, KVH * HD)
    vc = cache_v[0].reshape(NB, WIN, KVH * HD)
    sink_col = jnp.repeat(sinks[0], TS)[:, None]
    o_s = _attn_sample(qbd, kc, vc, jnp.concatenate([k_new, pad4], axis=1),
                       jnp.concatenate([v_new, pad4], axis=1), sink_col)
    o_s = o_s.reshape(NB, KVH, G, TS, KVH, HD)
    o_s = jnp.stack([o_s[:, 0, :, :, 0], o_s[:, 1, :, :, 1]], axis=1)
    o_s = o_s.transpose(0, 3, 1, 2, 4).reshape(NS, H * HD)
    wo_hi = w_o[0].astype(bf16)
    wo_lo = (w_o[0] - wo_hi.astype(f32)).astype(bf16)
    x = _proj(xp, xs, o_p, o_s, wo_hi, wo_lo)
    x = _moe_layer(x, 0, False, *moe_w)

    new_k_prompt = ktok[NP - WIN:NP].reshape(1, 1, WIN, KVH, HD)
    new_v_prompt = vtok[NP - WIN:NP].reshape(1, 1, WIN, KVH, HD)
    new_k_sample = jnp.concatenate([kc[:, TS:], k_new], axis=1).reshape(1, NB, WIN, KVH, HD)
    new_v_sample = jnp.concatenate([vc[:, TS:], v_new], axis=1).reshape(1, NB, WIN, KVH, HD)

    st = state_conv[0]
    z = jnp.zeros((NB, 1, D), f32)
    p1 = jnp.concatenate([st[:, 1:2], z, z, z], axis=1).reshape(NS, D)
    p2 = jnp.concatenate([st[:, 0:1], st[:, 1:2], z, z], axis=1).reshape(NS, D)
    x, cu = _conv(x, norm_mix[1][None, :], w_in[0].astype(bf16), conv_w[0],
                  w_out[0].astype(bf16), p1, p2)
    y_prompt, y_sample = _moe_layer(x, 1, True, *moe_w)

    new_conv_prompt = cu[T - 2:T].reshape(1, 1, 2, D)
    new_conv_sample = cu[T:].reshape(NB, TS, D)[:, TS - 2:][None]

    y_prompt = y_prompt.reshape(1, NP, D)
    y_sample = y_sample.reshape(NB, TS, D)
    return (y_prompt, y_sample, new_k_prompt, new_v_prompt, new_conv_prompt,
            new_k_sample, new_v_sample, new_conv_sample)
```

```python
import jax
import jax.numpy as jnp
import numpy as np
from jax import lax
from jax.experimental import pallas as pl
from jax.experimental.pallas import tpu as pltpu

D = 1024
NP = 16384
NB = 128
TS = 4
NS = NB * TS
N = NP + NS
PAST = 16384
H = 16
KVH = 2
G = H // KVH
HD = 64
HALF = HD // 2
QKV = (H + 2 * KVH) * HD
WIN = 128
THETA = 10000.0
NGRP = 4
EPG = 4
NE = NGRP * EPG
TOPK = 2
F = 512
EPS = 1e-6
SCALE = HD ** -0.5

T = 512
NT = N // T
TQ = 128
BB = 8
LANES = 128
VMEM_LIMIT = 50 * 1024 * 1024

TM = 256
NTM = N // TM
UNIT = 16
RC = TOPK * TM + NE * UNIT
UPT = RC // UNIT
XW = D + LANES
CH = 512
CHB = 256
UPC = CH // UNIT
NCH = -(-(NTM * (UPT - 1) + NE * (UPC - 1)) // UPC) + 1
ZERO_UNIT_IN = UPT - 1
ZERO_UNIT_OUT = (NCH - 1) * UPC
NSLOT = 3

f32 = jnp.float32
bf16 = jnp.bfloat16


def _params(*sem):
    return pltpu.CompilerParams(dimension_semantics=sem, vmem_limit_bytes=VMEM_LIMIT)


def _rms(x, g):
    ms = jnp.mean(x * x, axis=-1, keepdims=True)
    return x * lax.rsqrt(ms + EPS) * g


def _pick(i, prompt_ref, sample_ref):
    return jnp.where(i == NT - 1, sample_ref[...], prompt_ref[...])


def _stream_specs():
    return [pl.BlockSpec((T, D), lambda i: (jnp.minimum(i, NT - 2), 0)),
            pl.BlockSpec((T, D), lambda i: (0, 0))]


def _qkv_kernel(xp_ref, xs_ref, g_ref, wT_ref, qn_ref, kn_ref, cos_ref, sin_ref,
                qT_ref, ktok_ref, vtok_ref, vT_ref):
    h = _rms(_pick(pl.program_id(0), xp_ref, xs_ref), g_ref[...]).astype(bf16)
    qkvT = lax.dot_general(wT_ref[...], h, (((1,), (1,)), ((), ())),
                           preferred_element_type=f32)
    cos = cos_ref[...]
    sin = sin_ref[...]

    def norm_rope(blk, gcol):
        ms = jnp.mean(blk * blk, axis=0, keepdims=True)
        y = blk * lax.rsqrt(ms + EPS) * gcol
        y1 = y[:HALF]
        y2 = y[HALF:]
        return y1 * cos - y2 * sin, y2 * cos + y1 * sin

    qn = qn_ref[...]
    for hd in range(H):
        o1, o2 = norm_rope(qkvT[hd * HD:(hd + 1) * HD], qn)
        qT_ref[hd * HD:hd * HD + HALF, :] = (o1 * SCALE).astype(bf16)
        qT_ref[hd * HD + HALF:(hd + 1) * HD, :] = (o2 * SCALE).astype(bf16)
    kn = kn_ref[...]
    ks = []
    for j in range(KVH):
        o1, o2 = norm_rope(qkvT[H * HD + j * HD:H * HD + (j + 1) * HD], kn)
        ks += [o1, o2]
    kT = jnp.concatenate(ks, axis=0)
    ktok_ref[...] = kT.T
    vT = qkvT[(H + KVH) * HD:]
    vtok_ref[...] = vT.T
    vT_ref[...] = vT.astype(bf16)


def _qkv(xp, xs, g, wT, qn, kn, cos, sin):
    return pl.pallas_call(
        _qkv_kernel,
        grid=(NT,),
        in_specs=_stream_specs() + [
            pl.BlockSpec((1, D), lambda i: (0, 0)),
            pl.BlockSpec((QKV, D), lambda i: (0, 0)),
            pl.BlockSpec((HD, 1), lambda i: (0, 0)),
            pl.BlockSpec((HD, 1), lambda i: (0, 0)),
            pl.BlockSpec((HALF, T), lambda i: (0, i)),
            pl.BlockSpec((HALF, T), lambda i: (0, i)),
        ],
        out_specs=[
            pl.BlockSpec((H * HD, T), lambda i: (0, i)),
            pl.BlockSpec((T, KVH * HD), lambda i: (i, 0)),
            pl.BlockSpec((T, KVH * HD), lambda i: (i, 0)),
            pl.BlockSpec((KVH * HD, T), lambda i: (0, i)),
        ],
        out_shape=[
            jax.ShapeDtypeStruct((H * HD, N), bf16),
            jax.ShapeDtypeStruct((N, KVH * HD), f32),
            jax.ShapeDtypeStruct((N, KVH * HD), f32),
            jax.ShapeDtypeStruct((KVH * HD, N), bf16),
        ],
        compiler_params=_params("parallel"),
        name="qkv",
    )(xp, xs, g, wT, qn, kn, cos, sin)


def _attn_prompt_kernel(qT_ref, kp_ref, kc_ref, vp_ref, vc_ref, sink_ref, bias_ref, o_ref):
    kk = jnp.concatenate([kp_ref[...], kc_ref[...]], axis=0).astype(bf16)
    vv = jnp.concatenate([vp_ref[...], vc_ref[...]], axis=1)
    bias = jnp.concatenate([bias_ref[...]] * H, axis=1)
    qg = [jnp.concatenate([qT_ref[(g * G + hh) * HD:(g * G + hh + 1) * HD, :]
                           for hh in range(G)], axis=1) for g in range(KVH)]
    zeros = jnp.zeros_like(qg[0])
    rhs = jnp.concatenate([jnp.concatenate([qg[0], zeros], axis=1),
                           jnp.concatenate([zeros, qg[1]], axis=1)], axis=0)
    sT = jnp.dot(kk, rhs, preferred_element_type=f32) + bias
    sink = jnp.concatenate([sink_ref[0], sink_ref[1]], axis=1)
    m = jnp.maximum(jnp.max(sT, axis=0, keepdims=True), sink)
    p = jnp.exp(sT - m)
    l = jnp.sum(p, axis=0, keepdims=True) + jnp.exp(sink - m)
    p = (p * (1.0 / l)).astype(bf16)
    pieces = []
    for g in range(KVH):
        oT = jnp.dot(vv[g * HD:(g + 1) * HD, :], p[:, g * G * TQ:(g + 1) * G * TQ],
                     preferred_element_type=f32)
        pieces += [oT[:, hh * TQ:(hh + 1) * TQ] for hh in range(G)]
    oT_all = jnp.concatenate(pieces, axis=0)
    o_ref[...] = oT_all.T.astype(bf16)


def _band_bias():
    s = np.arange(2 * TQ)[:, None]
    t = np.arange(TQ)[None, :]
    dist = t + TQ - s
    band = (dist >= 0) & (dist <= WIN)
    first = band & (s >= TQ)
    return jnp.asarray(np.where(np.stack([first, band]), 0.0, -np.inf), f32)


def _attn_prompt(qT, ktok, vT, sink_rows):
    nq = NP // TQ
    return pl.pallas_call(
        _attn_prompt_kernel,
        grid=(nq,),
        in_specs=[
            pl.BlockSpec((H * HD, TQ), lambda j: (0, j)),
            pl.BlockSpec((TQ, KVH * HD), lambda j: (jnp.maximum(j - 1, 0), 0)),
            pl.BlockSpec((TQ, KVH * HD), lambda j: (j, 0)),
            pl.BlockSpec((KVH * HD, TQ), lambda j: (0, jnp.maximum(j - 1, 0))),
            pl.BlockSpec((KVH * HD, TQ), lambda j: (0, j)),
            pl.BlockSpec((KVH, 1, G * TQ), lambda j: (0, 0, 0)),
            pl.BlockSpec((None, 2 * TQ, TQ), lambda j: (jnp.minimum(j, 1), 0, 0)),
        ],
        out_specs=pl.BlockSpec((TQ, H * HD), lambda j: (j, 0)),
        out_shape=jax.ShapeDtypeStruct((NP, H * HD), bf16),
        compiler_params=_params("parallel"),
        name="attn_prompt",
    )(qT, ktok, ktok, vT, vT, sink_rows, _band_bias())


def _attn_sample_kernel(q_ref, kc_ref, vc_ref, kn_ref, vn_ref, sink_ref, o_ref):
    rows = H * TS
    t1 = lax.broadcasted_iota(jnp.int32, (rows, WIN), 0) & (TS - 1)
    s1 = lax.broadcasted_iota(jnp.int32, (rows, WIN), 1)
    valid1 = s1 >= t1
    t2 = lax.broadcasted_iota(jnp.int32, (rows, 8), 0) & (TS - 1)
    s2 = lax.broadcasted_iota(jnp.int32, (rows, 8), 1)
    valid2 = s2 <= t2
    sink = sink_ref[...]
    nt = (((1,), (1,)), ((), ()))
    for b in range(BB):
        q = q_ref[b]
        sc = lax.dot_general(q, kc_ref[b].astype(bf16), nt, preferred_element_type=f32)
        sn = lax.dot_general(q, kn_ref[b].astype(bf16), nt, preferred_element_type=f32)
        sc = jnp.where(valid1, sc, -jnp.inf)
        sn = jnp.where(valid2, sn, -jnp.inf)
        m = jnp.maximum(jnp.maximum(jnp.max(sc, axis=-1, keepdims=True),
                                    jnp.max(sn, axis=-1, keepdims=True)), sink)
        pc = jnp.exp(sc - m)
        pn = jnp.exp(sn - m)
        l = (jnp.sum(pc, axis=-1, keepdims=True) + jnp.sum(pn, axis=-1, keepdims=True)
             + jnp.exp(sink - m))
        r = 1.0 / l
        o_ref[b] = (jnp.dot((pc * r).astype(bf16), vc_ref[b].astype(bf16),
                            preferred_element_type=f32)
                    + jnp.dot((pn * r).astype(bf16), vn_ref[b].astype(bf16),
                              preferred_element_type=f32))


def _attn_sample(qbd, kc, vc, kn, vn, sink_col):
    rows = H * TS
    return pl.pallas_call(
        _attn_sample_kernel,
        grid=(NB // BB,),
        in_specs=[
            pl.BlockSpec((BB, rows, KVH * HD), lambda i: (i, 0, 0)),
            pl.BlockSpec((BB, WIN, KVH * HD), lambda i: (i, 0, 0)),
            pl.BlockSpec((BB, WIN, KVH * HD), lambda i: (i, 0, 0)),
            pl.BlockSpec((BB, 8, KVH * HD), lambda i: (i, 0, 0)),
            pl.BlockSpec((BB, 8, KVH * HD), lambda i: (i, 0, 0)),
            pl.BlockSpec((rows, 1), lambda i: (0, 0)),
        ],
        out_specs=pl.BlockSpec((BB, rows, KVH * HD), lambda i: (i, 0, 0)),
        out_shape=jax.ShapeDtypeStruct((NB, rows, KVH * HD), f32),
        compiler_params=_params("parallel"),
        name="attn_sample",
    )(qbd, kc, vc, kn, vn, sink_col)


def _proj_kernel(xp_ref, xs_ref, op_ref, os_ref, w_ref, y_ref):
    i = pl.program_id(0)
    y_ref[...] = _pick(i, xp_ref, xs_ref) + jnp.dot(_pick(i, op_ref, os_ref), w_ref[...],
                                                    preferred_element_type=f32)


def _proj(xp, xs, o_p, o_s, w):
    return pl.pallas_call(
        _proj_kernel,
        grid=(NT,),
        in_specs=_stream_specs() + _stream_specs() + [
            pl.BlockSpec((D, D), lambda i: (0, 0)),
        ],
        out_specs=pl.BlockSpec((T, D), lambda i: (i, 0)),
        out_shape=jax.ShapeDtypeStruct((N, D), f32),
        compiler_params=_params("parallel"),
        name="proj",
    )(xp, xs, o_p, o_s, w)


def _dispatch_kernel(x_ref, g_ref, w_ref, b_ref, xc_ref, meta_ref, cnt_ref):
    h_hi = _rms(x_ref[...], g_ref[...]).astype(bf16)
    logits = jnp.dot(h_hi, w_ref[...], preferred_element_type=f32) + b_ref[...]
    lane = lax.broadcasted_iota(jnp.int32, logits.shape, 1)
    lane_f = lane.astype(f32)
    big = float(LANES)
    gl = jnp.where(lane < NGRP, logits, -jnp.inf)
    gmax = jnp.max(gl, axis=-1, keepdims=True)
    gsel = jnp.min(jnp.where(gl == gmax, lane_f, big), axis=-1, keepdims=True)
    gsum = jnp.sum(jnp.exp(gl - gmax), axis=-1, keepdims=True)
    g_w = 1.0 / gsum
    lane_grp = ((lane - NGRP) >> 2).astype(f32)
    emask = (lane >= NGRP) & (lane < NGRP + NE) & (lane_grp == gsel)
    el = jnp.where(emask, logits, -jnp.inf)
    v1 = jnp.max(el, axis=-1, keepdims=True)
    i1 = jnp.min(jnp.where(el == v1, lane_f, big), axis=-1, keepdims=True)
    el2 = jnp.where(lane_f == i1, -jnp.inf, el)
    v2 = jnp.max(el2, axis=-1, keepdims=True)
    i2 = jnp.min(jnp.where(el2 == v2, lane_f, big), axis=-1, keepdims=True)
    e1 = jnp.exp(v2 - v1)
    den = 1.0 + e1
    w1 = (1.0 / den) * g_w
    w2 = (e1 / den) * g_w

    m1 = lane_f == i1
    m2 = lane_f == i2
    sel = jnp.where(m1 | m2, 1.0, 0.0)
    r_i = lax.broadcasted_iota(jnp.int32, (TM, TM), 0)
    c_i = lax.broadcasted_iota(jnp.int32, (TM, TM), 1)
    before = jnp.where(c_i < r_i, 1.0, 0.0).astype(bf16)
    ranks = jnp.dot(before, sel.astype(bf16), preferred_element_type=f32)
    counts = jnp.sum(sel, axis=0, keepdims=True)
    padded = jnp.floor((counts + (UNIT - 1.0)) * (1.0 / UNIT)) * UNIT
    k_i = lax.broadcasted_iota(jnp.int32, (LANES, LANES), 0)
    l_i = lax.broadcasted_iota(jnp.int32, (LANES, LANES), 1)
    lower_lanes = jnp.where(k_i < l_i, 1.0, 0.0).astype(bf16)
    seg = jnp.dot(jnp.broadcast_to(padded, (8, LANES)).astype(bf16), lower_lanes,
                  preferred_element_type=f32)[0:1]
    posall = seg + ranks
    pos1 = jnp.sum(jnp.where(m1, posall, 0.0), axis=-1, keepdims=True)
    pos2 = jnp.sum(jnp.where(m2, posall, 0.0), axis=-1, keepdims=True)
    meta = jnp.where(lane == 0, pos1, jnp.where(lane == 1, pos2, 0.0))
    meta_ref[...] = meta
    cnt_ref[...] = jnp.broadcast_to(counts, (8, LANES))

    metaT = meta.T
    rr = lax.broadcasted_iota(jnp.int32, (RC, TM), 0).astype(f32)
    p1 = rr == metaT[0:1, :]
    p2 = rr == metaT[1:2, :]
    w1_hi = w1.astype(bf16).astype(f32)
    w2_hi = w2.astype(bf16).astype(f32)
    cols = (w1_hi, w1 - w1_hi, w2_hi, w2 - w2_hi, i1)
    extra = jnp.zeros((TM, LANES), f32)
    for k, col in enumerate(cols):
        extra = jnp.where(lane == k, col, extra)
    h_aug = jnp.concatenate([h_hi, extra.astype(bf16)], axis=1)
    xc_ref[...] = jnp.dot(jnp.where(p1 | p2, 1.0, 0.0).astype(bf16), h_aug,
                          preferred_element_type=f32).astype(bf16)


def _dispatch(x, g, w, b):
    return pl.pallas_call(
        _dispatch_kernel,
        grid=(NTM,),
        in_specs=[
            pl.BlockSpec((TM, D), lambda i: (i, 0)),
            pl.BlockSpec((1, D), lambda i: (0, 0)),
            pl.BlockSpec((D, LANES), lambda i: (0, 0)),
            pl.BlockSpec((1, LANES), lambda i: (0, 0)),
        ],
        out_specs=[
            pl.BlockSpec((RC, XW), lambda i: (i, 0)),
            pl.BlockSpec((TM, LANES), lambda i: (i, 0)),
            pl.BlockSpec((8, LANES), lambda i: (i, 0)),
        ],
        out_shape=[
            jax.ShapeDtypeStruct((NTM * RC, XW), bf16),
            jax.ShapeDtypeStruct((N, LANES), f32),
            jax.ShapeDtypeStruct((NTM * 8, LANES), f32),
        ],
        compiler_params=_params("parallel"),
        name="dispatch",
    )(x, g, w, b)


def _dispatch_tables(cnt):
    i32 = jnp.int32
    n = cnt.reshape(NTM, 8, LANES)[:, 0, NGRP:NGRP + NE].astype(i32)
    units = (n + UNIT - 1) // UNIT
    seg_end = jnp.cumsum(units, axis=1)
    seg_start = seg_end - units
    col_end = jnp.cumsum(units, axis=0)
    col_start = col_end - units
    chunks = (col_end[-1] + UPC - 1) // UPC
    ch_end = jnp.cumsum(chunks)
    ch_start = ch_end - chunks
    nused = ch_end[-1]
    c = jnp.arange(NCH, dtype=i32)
    eid = jnp.minimum(jnp.sum((ch_end[None, :] <= c[:, None]).astype(i32), axis=1), NE - 1)

    src0 = jnp.arange(NTM, dtype=i32)[:, None] * UPT + seg_start
    dst0 = ch_start[None, :] * UPC + col_start
    k = jnp.arange(NCH * UPC, dtype=i32)[:, None, None]
    inside = (k >= dst0[None]) & (k < (dst0 + units)[None])
    found = jnp.sum(inside.astype(i32), axis=(1, 2))
    shift = jnp.sum(jnp.where(inside, (src0 - dst0)[None], 0), axis=(1, 2))
    ffn_src = jnp.where(found > 0, k[:, 0, 0] + shift, ZERO_UNIT_IN)

    v = jnp.arange(UPT, dtype=i32)[None, :, None]
    inside_v = (v >= seg_start[:, None, :]) & (v < seg_end[:, None, :])
    found_v = jnp.sum(inside_v.astype(i32), axis=2)
    shift_v = jnp.sum(jnp.where(inside_v, (dst0 - seg_start)[:, None, :], 0), axis=2)
    comb_src = jnp.where(found_v > 0, v[:, :, 0] + shift_v, ZERO_UNIT_OUT).reshape(-1)
    return eid, nused.reshape(1), ffn_src, comb_src


def _unit_copies(src_ref, base, n_units, src_hbm, stage, slot, sem):
    out = []
    for j in range(n_units):
        row = pl.multiple_of(src_ref[base + j] * UNIT, UNIT)
        out.append(pltpu.make_async_copy(
            src_hbm.at[pl.ds(row, UNIT), :],
            stage.at[slot, pl.ds(j * UNIT, UNIT), :],
            sem.at[slot]))
    return out


def _start_gathers(step, n_steps, n_units, src_ref, src_hbm, stage, sem):
    ahead = NSLOT - 1

    def start(s, slot):
        for cp in _unit_copies(src_ref, s * n_units, n_units, src_hbm, stage, slot, sem):
            cp.start()

    for s in range(ahead):
        @pl.when((step == 0) & (s < n_steps))
        def _():
            start(s, s)

    @pl.when(step + ahead < n_steps)
    def _():
        start(step + ahead, lax.rem(step + ahead, NSLOT))


def _wait_gather(step, n_units, src_ref, src_hbm, stage, sem):
    slot = lax.rem(step, NSLOT)
    for cp in _unit_copies(src_ref, step * n_units, n_units, src_hbm, stage, slot, sem):
        cp.wait()
    return slot


def _ffn_kernel(eid_ref, nused_ref, src_ref, xc_hbm, wg_ref, wu_ref, wd_ref, o_ref,
                stage, sem, wg_b, wu_b, wd_b):
    c = pl.program_id(0)
    nused = nused_ref[0]
    _start_gathers(c, nused, UPC, src_ref, xc_hbm, stage, sem)

    @pl.when(c < nused)
    def _():
        slot = _wait_gather(c, UPC, src_ref, xc_hbm, stage, sem)

        @pl.when((c == 0) | (eid_ref[c] != eid_ref[jnp.maximum(c - 1, 0)]))
        def _():
            wg_b[...] = wg_ref[...].astype(bf16)
            wu_b[...] = wu_ref[...].astype(bf16)
            wd_b[...] = wd_ref[...].astype(bf16)

        for r in range(0, CH, CHB):
            xs = stage[slot, r:r + CHB, :]
            x = xs[:, :D]
            gb = xs[:, D:].astype(f32)
            first = gb[:, 4:5] == (eid_ref[c] + NGRP).astype(f32)
            gate = jnp.where(first, gb[:, 0:1] + gb[:, 1:2], gb[:, 2:3] + gb[:, 3:4])
            a = jnp.dot(x, wg_b[...], preferred_element_type=f32)
            u = jnp.dot(x, wu_b[...], preferred_element_type=f32)
            act = (a * (1.0 / (1.0 + jnp.exp(-a)))) * u * gate
            o_ref[r:r + CHB, :] = jnp.dot(act.astype(bf16), wd_b[...],
                                          preferred_element_type=f32).astype(bf16)

    @pl.when(c >= nused)
    def _():
        o_ref[...] = jnp.zeros((CH, D), bf16)


def _ffn(layer, eid, nused, ffn_src, xc, wg, wu, wd):
    return pl.pallas_call(
        _ffn_kernel,
        grid_spec=pltpu.PrefetchScalarGridSpec(
            num_scalar_prefetch=3,
            grid=(NCH,),
            in_specs=[
                pl.BlockSpec(memory_space=pl.ANY),
                pl.BlockSpec((None, None, D, F), lambda c, eid, nu, src: (layer, eid[c], 0, 0)),
                pl.BlockSpec((None, None, D, F), lambda c, eid, nu, src: (layer, eid[c], 0, 0)),
                pl.BlockSpec((None, None, F, D), lambda c, eid, nu, src: (layer, eid[c], 0, 0)),
            ],
            out_specs=pl.BlockSpec((CH, D), lambda c, eid, nu, src: (c, 0)),
            scratch_shapes=[
                pltpu.VMEM((NSLOT, CH, XW), bf16),
                pltpu.SemaphoreType.DMA((NSLOT,)),
                pltpu.VMEM((D, F), bf16),
                pltpu.VMEM((D, F), bf16),
                pltpu.VMEM((F, D), bf16),
            ],
        ),
        out_shape=jax.ShapeDtypeStruct((NCH * CH, D), bf16),
        compiler_params=_params("arbitrary"),
        name="ffn",
    )(eid, nused, ffn_src, xc, wg, wu, wd)


def _combine_kernel(src_ref, x_ref, meta_ref, o_hbm, y_refs, stage, sem):
    i = pl.program_id(0)
    _start_gathers(i, NTM, UPT, src_ref, o_hbm, stage, sem)
    slot = _wait_gather(i, UPT, src_ref, o_hbm, stage, sem)

    meta = meta_ref[...]
    pos1 = meta[:, 0:1]
    pos2 = meta[:, 1:2]
    li = lax.broadcasted_iota(jnp.int32, (TM, RC), 1).astype(f32)
    pt = jnp.where((li == pos1) | (li == pos2), 1.0, 0.0).astype(bf16)
    y = x_ref[...] + jnp.dot(pt, stage[slot], preferred_element_type=f32)
    if len(y_refs) == 1:
        y_refs[0][...] = y
    else:
        @pl.when(i < NP // TM)
        def _():
            y_refs[0][...] = y

        @pl.when(i >= NP // TM)
        def _():
            y_refs[1][...] = y


def _combine_kernel_1(src_ref, x_ref, meta_ref, o_hbm, y_ref, stage, sem):
    _combine_kernel(src_ref, x_ref, meta_ref, o_hbm, (y_ref,), stage, sem)


def _combine_kernel_2(src_ref, x_ref, meta_ref, o_hbm, yp_ref, ys_ref, stage, sem):
    _combine_kernel(src_ref, x_ref, meta_ref, o_hbm, (yp_ref, ys_ref), stage, sem)


def _combine(comb_src, x, meta, o_sorted, split):
    npt = NP // TM
    if split:
        body = _combine_kernel_2
        out_specs = [pl.BlockSpec((TM, D), lambda i, src: (jnp.minimum(i, npt - 1), 0)),
                     pl.BlockSpec((TM, D), lambda i, src: (jnp.maximum(i - npt, 0), 0))]
        out_shape = [jax.ShapeDtypeStruct((NP, D), f32), jax.ShapeDtypeStruct((NS, D), f32)]
    else:
        body = _combine_kernel_1
        out_specs = pl.BlockSpec((TM, D), lambda i, src: (i, 0))
        out_shape = jax.ShapeDtypeStruct((N, D), f32)
    return pl.pallas_call(
        body,
        grid_spec=pltpu.PrefetchScalarGridSpec(
            num_scalar_prefetch=1,
            grid=(NTM,),
            in_specs=[
                pl.BlockSpec((TM, D), lambda i, src: (i, 0)),
                pl.BlockSpec((TM, LANES), lambda i, src: (i, 0)),
                pl.BlockSpec(memory_space=pl.ANY),
            ],
            out_specs=out_specs,
            scratch_shapes=[
                pltpu.VMEM((NSLOT, RC, D), bf16),
                pltpu.SemaphoreType.DMA((NSLOT,)),
            ],
        ),
        out_shape=out_shape,
        compiler_params=_params("arbitrary"),
        name="combine",
    )(comb_src, x, meta, o_sorted)


def _conv_kernel(x_ref, g_ref, win_ref, cw_ref, wout_ref, p1_ref, p2_ref,
                 y_ref, cu_ref, pad_ref):
    i = pl.program_id(0)

    @pl.when(i == 0)
    def _():
        pad_ref[0:8, :] = jnp.zeros((8, D), f32)

    x = x_ref[...]
    h = _rms(x, g_ref[...]).astype(bf16)
    bcu = jnp.dot(h, win_ref[...], preferred_element_type=f32)
    b = bcu[:, :D]
    cu = bcu[:, D:2 * D] * bcu[:, 2 * D:]
    pad_ref[8:T + 8, :] = cu
    m1 = pad_ref[7:T + 7, :]
    m2 = pad_ref[6:T + 6, :]
    t = lax.broadcasted_iota(jnp.int32, (T, 1), 0) & (TS - 1)
    is_sample = i == NT - 1
    m1 = jnp.where(is_sample & (t == 0), p1_ref[...], m1)
    m2 = jnp.where(is_sample & (t < 2), p2_ref[...], m2)
    cw = cw_ref[...]
    conv = cw[0:1] * m2 + cw[1:2] * m1 + cw[2:3] * cu
    y = jnp.dot((b * conv).astype(bf16), wout_ref[...], preferred_element_type=f32)
    y_ref[...] = x + y
    cu_ref[...] = cu
    pad_ref[0:8, :] = cu[T - 8:, :]


def _conv(x, g, win, cw, wout, p1, p2):
    return pl.pallas_call(
        _conv_kernel,
        grid=(NT,),
        in_specs=[
            pl.BlockSpec((T, D), lambda i: (i, 0)),
            pl.BlockSpec((1, D), lambda i: (0, 0)),
            pl.BlockSpec((D, 3 * D), lambda i: (0, 0)),
            pl.BlockSpec((3, D), lambda i: (0, 0)),
            pl.BlockSpec((D, D), lambda i: (0, 0)),
            pl.BlockSpec((T, D), lambda i: (0, 0)),
            pl.BlockSpec((T, D), lambda i: (0, 0)),
        ],
        out_specs=[
            pl.BlockSpec((T, D), lambda i: (i, 0)),
            pl.BlockSpec((T, D), lambda i: (jnp.where(i == NT - 1, 1, 0), 0)),
        ],
        out_shape=[
            jax.ShapeDtypeStruct((N, D), f32),
            jax.ShapeDtypeStruct((2 * T, D), f32),
        ],
        scratch_shapes=[pltpu.VMEM((T + 8, D), f32)],
        compiler_params=_params("arbitrary"),
        name="conv",
    )(x, g, win, cw, wout, p1, p2)


def _rope_tables():
    inv_freq = THETA ** (-jnp.arange(HALF, dtype=f32) / HALF)
    pos = jnp.concatenate([
        jnp.arange(NP, dtype=jnp.int32),
        PAST + jnp.tile(jnp.arange(TS, dtype=jnp.int32), NB),
    ]).astype(f32)
    ang = inv_freq[:, None] * pos[None, :]
    return jnp.cos(ang), jnp.sin(ang)


def _router_weights(w_group, b_group, w_router, b_router):
    pad = LANES - NGRP - NE
    w = jnp.concatenate([w_group, w_router, jnp.zeros((D, pad), f32)], axis=1)
    b = jnp.concatenate([b_group, b_router, jnp.zeros((pad,), f32)])[None, :]
    return w.astype(bf16), b


def _moe_layer(x, i, split, norm_ffn, w_group, b_group, w_router, b_router, w_gate, w_up, w_down):
    w, b = _router_weights(w_group[i], b_group[i], w_router[i], b_router[i])
    xc, meta, cnt = _dispatch(x, norm_ffn[i][None, :], w, b)
    eid, nused, ffn_src, comb_src = _dispatch_tables(cnt)
    o_sorted = _ffn(i, eid, nused, ffn_src, xc, w_gate, w_up, w_down)
    return _combine(comb_src, x, meta, o_sorted, split)


def kernel(x_prompt, x_sample, cache_k, cache_v, state_conv, norm_mix, w_qkv, q_norm, k_norm,
           sinks, w_o, w_in, conv_w, w_out, norm_ffn, w_group, b_group, w_router, b_router,
           w_gate, w_up, w_down):
    xp = x_prompt.reshape(NP, D)
    xs = x_sample.reshape(NS, D)
    moe_w = (norm_ffn, w_group, b_group, w_router, b_router, w_gate, w_up, w_down)

    cos, sin = _rope_tables()
    qT, ktok, vtok, vT = _qkv(xp, xs, norm_mix[0][None, :], w_qkv[0].T.astype(bf16),
                              q_norm[0][:, None], k_norm[0][:, None], cos, sin)
    sink_rows = jnp.repeat(sinks[0].reshape(KVH, G), TQ, axis=1)[:, None, :]
    o_p = _attn_prompt(qT, ktok, vT, sink_rows)

    qs = qT[:, NP:].reshape(KVH, G, HD, NB, TS).transpose(3, 0, 1, 4, 2)
    zq = jnp.zeros_like(qs[:, 0])
    qbd = jnp.stack([jnp.concatenate([qs[:, 0], zq], axis=-1),
                     jnp.concatenate([zq, qs[:, 1]], axis=-1)], axis=1)
    qbd = qbd.reshape(NB, H * TS, KVH * HD)
    k_new = ktok[NP:].reshape(NB, TS, KVH * HD)
    v_new = vtok[NP:].reshape(NB, TS, KVH * HD)
    pad4 = jnp.zeros((NB, 8 - TS, KVH * HD), f32)
    kc = cache_k[0].reshape(NB, WIN, KVH * HD)
    vc = cache_v[0].reshape(NB, WIN, KVH * HD)
    sink_col = jnp.repeat(sinks[0], TS)[:, None]
    o_s = _attn_sample(qbd, kc, vc, jnp.concatenate([k_new, pad4], axis=1),
                       jnp.concatenate([v_new, pad4], axis=1), sink_col)
    o_s = o_s.reshape(NB, KVH, G, TS, KVH, HD)
    o_s = jnp.stack([o_s[:, 0, :, :, 0], o_s[:, 1, :, :, 1]], axis=1)
    o_s = o_s.transpose(0, 3, 1, 2, 4).reshape(NS, H * HD).astype(bf16)
    x = _proj(xp, xs, o_p, o_s, w_o[0].astype(bf16))
    x = _moe_layer(x, 0, False, *moe_w)

    new_k_prompt = ktok[NP - WIN:NP].reshape(1, 1, WIN, KVH, HD)
    new_v_prompt = vtok[NP - WIN:NP].reshape(1, 1, WIN, KVH, HD)
    new_k_sample = jnp.concatenate([kc[:, TS:], k_new], axis=1).reshape(1, NB, WIN, KVH, HD)
    new_v_sample = jnp.concatenate([vc[:, TS:], v_new], axis=1).reshape(1, NB, WIN, KVH, HD)

    st = state_conv[0]
    z = jnp.zeros((NB, 1, D), f32)
    p1 = jnp.concatenate([st[:, 1:2], z, z, z], axis=1).reshape(NS, D)
    p2 = jnp.concatenate([st[:, 0:1], st[:, 1:2], z, z], axis=1).reshape(NS, D)
    x, cu = _conv(x, norm_mix[1][None, :], w_in[0].astype(bf16), conv_w[0],
                  w_out[0].astype(bf16), p1, p2)
    y_prompt, y_sample = _moe_layer(x, 1, True, *moe_w)

    new_conv_prompt = cu[T - 2:T].reshape(1, 1, 2, D)
    new_conv_sample = cu[T:].reshape(NB, TS, D)[:, TS - 2:][None]

    y_prompt = y_prompt.reshape(1, NP, D)
    y_sample = y_sample.reshape(NB, TS, D)
    return (y_prompt, y_sample, new_k_prompt, new_v_prompt, new_conv_prompt,
            new_k_sample, new_v_sample, new_conv_sample)
```

```python
import jax
import jax.numpy as jnp
import numpy as np
from jax import lax
from jax.experimental import pallas as pl
from jax.experimental.pallas import tpu as pltpu

D = 1024
NP = 16384
NB = 128
TS = 4
NS = NB * TS
N = NP + NS
PAST = 16384
H = 16
KVH = 2
G = H // KVH
HD = 64
HALF = HD // 2
QKV = (H + 2 * KVH) * HD
WIN = 128
THETA = 10000.0
NGRP = 4
EPG = 4
NE = NGRP * EPG
TOPK = 2
ROW_E = 8
F = 512
EPS = 1e-6
SCALE = HD ** -0.5

T = 512
NT = N // T
TQ = 128
BB = 8
LANES = 128
VMEM_LIMIT = 50 * 1024 * 1024

TM = 256
NTM = N // TM
UNIT = 16
RC = TOPK * TM + NE * UNIT
UPT = RC // UNIT
XW = D + LANES
CH = 512
CHB = 256
UPC = CH // UNIT
NCH = -(-(NTM * (UPT - 1) + NE * (UPC - 1)) // UPC) + 1
ZERO_UNIT_IN = UPT - 1
ZERO_UNIT_OUT = (NCH - 1) * UPC
NSLOT = 3

f32 = jnp.float32
bf16 = jnp.bfloat16


def _params(*sem):
    return pltpu.CompilerParams(dimension_semantics=sem, vmem_limit_bytes=VMEM_LIMIT)


def _rms(x, g):
    ms = jnp.mean(x * x, axis=-1, keepdims=True)
    return x * lax.rsqrt(ms + EPS) * g


def _pick(i, prompt_ref, sample_ref):
    return jnp.where(i == NT - 1, sample_ref[...], prompt_ref[...])


def _stream_specs():
    return [pl.BlockSpec((T, D), lambda i: (jnp.minimum(i, NT - 2), 0)),
            pl.BlockSpec((T, D), lambda i: (0, 0))]


def _qkv_kernel(xp_ref, xs_ref, g_ref, wT_ref, qn_ref, kn_ref, cos_ref, sin_ref,
                qT_ref, ktok_ref, vtok_ref, vT_ref):
    h = _rms(_pick(pl.program_id(0), xp_ref, xs_ref), g_ref[...]).astype(bf16)
    qkvT = lax.dot_general(wT_ref[...], h, (((1,), (1,)), ((), ())),
                           preferred_element_type=f32)
    cos = cos_ref[...]
    sin = sin_ref[...]

    def norm_rope(blk, gcol):
        ms = jnp.mean(blk * blk, axis=0, keepdims=True)
        y = blk * lax.rsqrt(ms + EPS) * gcol
        y1 = y[:HALF]
        y2 = y[HALF:]
        return y1 * cos - y2 * sin, y2 * cos + y1 * sin

    qn = qn_ref[...]
    for hd in range(H):
        o1, o2 = norm_rope(qkvT[hd * HD:(hd + 1) * HD], qn)
        qT_ref[hd * HD:hd * HD + HALF, :] = (o1 * SCALE).astype(bf16)
        qT_ref[hd * HD + HALF:(hd + 1) * HD, :] = (o2 * SCALE).astype(bf16)
    kn = kn_ref[...]
    ks = []
    for j in range(KVH):
        o1, o2 = norm_rope(qkvT[H * HD + j * HD:H * HD + (j + 1) * HD], kn)
        ks += [o1, o2]
    kT = jnp.concatenate(ks, axis=0)
    ktok_ref[...] = kT.T
    vT = qkvT[(H + KVH) * HD:]
    vtok_ref[...] = vT.T
    vT_ref[...] = vT.astype(bf16)


def _qkv(xp, xs, g, wT, qn, kn, cos, sin):
    return pl.pallas_call(
        _qkv_kernel,
        grid=(NT,),
        in_specs=_stream_specs() + [
            pl.BlockSpec((1, D), lambda i: (0, 0)),
            pl.BlockSpec((QKV, D), lambda i: (0, 0)),
            pl.BlockSpec((HD, 1), lambda i: (0, 0)),
            pl.BlockSpec((HD, 1), lambda i: (0, 0)),
            pl.BlockSpec((HALF, T), lambda i: (0, i)),
            pl.BlockSpec((HALF, T), lambda i: (0, i)),
        ],
        out_specs=[
            pl.BlockSpec((H * HD, T), lambda i: (0, i)),
            pl.BlockSpec((T, KVH * HD), lambda i: (i, 0)),
            pl.BlockSpec((T, KVH * HD), lambda i: (i, 0)),
            pl.BlockSpec((KVH * HD, T), lambda i: (0, i)),
        ],
        out_shape=[
            jax.ShapeDtypeStruct((H * HD, N), bf16),
            jax.ShapeDtypeStruct((N, KVH * HD), f32),
            jax.ShapeDtypeStruct((N, KVH * HD), f32),
            jax.ShapeDtypeStruct((KVH * HD, N), bf16),
        ],
        compiler_params=_params("parallel"),
        name="qkv",
    )(xp, xs, g, wT, qn, kn, cos, sin)


def _attn_prompt_kernel(qT_ref, kp_ref, kc_ref, vp_ref, vc_ref, sink_ref, bias_ref, o_ref):
    kk = jnp.concatenate([kp_ref[...], kc_ref[...]], axis=0).astype(bf16)
    vv = jnp.concatenate([vp_ref[...], vc_ref[...]], axis=1)
    bias = jnp.concatenate([bias_ref[...]] * H, axis=1)
    qg = [jnp.concatenate([qT_ref[(g * G + hh) * HD:(g * G + hh + 1) * HD, :]
                           for hh in range(G)], axis=1) for g in range(KVH)]
    zeros = jnp.zeros_like(qg[0])
    rhs = jnp.concatenate([jnp.concatenate([qg[0], zeros], axis=1),
                           jnp.concatenate([zeros, qg[1]], axis=1)], axis=0)
    sT = jnp.dot(kk, rhs, preferred_element_type=f32) + bias
    sink = jnp.concatenate([sink_ref[0], sink_ref[1]], axis=1)
    m = jnp.maximum(jnp.max(sT, axis=0, keepdims=True), sink)
    p = jnp.exp(sT - m)
    l = jnp.sum(p, axis=0, keepdims=True) + jnp.exp(sink - m)
    p = (p * (1.0 / l)).astype(bf16)
    pieces = []
    for g in range(KVH):
        oT = jnp.dot(vv[g * HD:(g + 1) * HD, :], p[:, g * G * TQ:(g + 1) * G * TQ],
                     preferred_element_type=f32)
        pieces += [oT[:, hh * TQ:(hh + 1) * TQ] for hh in range(G)]
    oT_all = jnp.concatenate(pieces, axis=0)
    o_ref[...] = oT_all.T.astype(bf16)


def _band_bias():
    s = np.arange(2 * TQ)[:, None]
    t = np.arange(TQ)[None, :]
    dist = t + TQ - s
    band = (dist >= 0) & (dist <= WIN)
    first = band & (s >= TQ)
    return jnp.asarray(np.where(np.stack([first, band]), 0.0, -np.inf), f32)


def _attn_prompt(qT, ktok, vT, sink_rows):
    nq = NP // TQ
    return pl.pallas_call(
        _attn_prompt_kernel,
        grid=(nq,),
        in_specs=[
            pl.BlockSpec((H * HD, TQ), lambda j: (0, j)),
            pl.BlockSpec((TQ, KVH * HD), lambda j: (jnp.maximum(j - 1, 0), 0)),
            pl.BlockSpec((TQ, KVH * HD), lambda j: (j, 0)),
            pl.BlockSpec((KVH * HD, TQ), lambda j: (0, jnp.maximum(j - 1, 0))),
            pl.BlockSpec((KVH * HD, TQ), lambda j: (0, j)),
            pl.BlockSpec((KVH, 1, G * TQ), lambda j: (0, 0, 0)),
            pl.BlockSpec((None, 2 * TQ, TQ), lambda j: (jnp.minimum(j, 1), 0, 0)),
        ],
        out_specs=pl.BlockSpec((TQ, H * HD), lambda j: (j, 0)),
        out_shape=jax.ShapeDtypeStruct((NP, H * HD), bf16),
        compiler_params=_params("parallel"),
        name="attn_prompt",
    )(qT, ktok, ktok, vT, vT, sink_rows, _band_bias())


def _attn_sample_kernel(q_ref, kc_ref, vc_ref, kn_ref, vn_ref, sink_ref,
                        o_ref, knew_ref, vnew_ref):
    rows = H * TS
    t1 = lax.broadcasted_iota(jnp.int32, (1, rows, WIN), 1) & (TS - 1)
    s1 = lax.broadcasted_iota(jnp.int32, (1, rows, WIN), 2)
    valid1 = s1 >= t1
    t2 = lax.broadcasted_iota(jnp.int32, (1, rows, 8), 1) & (TS - 1)
    s2 = lax.broadcasted_iota(jnp.int32, (1, rows, 8), 2)
    valid2 = s2 <= t2
    sink = sink_ref[...][None]
    q = q_ref[...]
    kc = kc_ref[...]
    vc = vc_ref[...]
    kn = kn_ref[...]
    vn = vn_ref[...]
    sc = jnp.einsum('bqd,bkd->bqk', q, kc.astype(bf16), preferred_element_type=f32)
    sn = jnp.einsum('bqd,bkd->bqk', q, kn.astype(bf16), preferred_element_type=f32)
    sc = jnp.where(valid1, sc, -jnp.inf)
    sn = jnp.where(valid2, sn, -jnp.inf)
    m = jnp.maximum(jnp.maximum(jnp.max(sc, axis=-1, keepdims=True),
                                jnp.max(sn, axis=-1, keepdims=True)), sink)
    pc = jnp.exp(sc - m)
    pn = jnp.exp(sn - m)
    l = (jnp.sum(pc, axis=-1, keepdims=True) + jnp.sum(pn, axis=-1, keepdims=True)
         + jnp.exp(sink - m))
    r = 1.0 / l
    o_ref[...] = (jnp.einsum('bqk,bkd->bqd', (pc * r).astype(bf16), vc.astype(bf16),
                             preferred_element_type=f32)
                  + jnp.einsum('bqk,bkd->bqd', (pn * r).astype(bf16), vn.astype(bf16),
                               preferred_element_type=f32))
    knew_ref[:, :WIN - TS, :] = kc[:, TS:, :]
    knew_ref[:, WIN - TS:, :] = kn[:, :TS, :]
    vnew_ref[:, :WIN - TS, :] = vc[:, TS:, :]
    vnew_ref[:, WIN - TS:, :] = vn[:, :TS, :]


def _attn_sample(qbd, kc, vc, kn, vn, sink_col):
    rows = H * TS
    cache_spec = pl.BlockSpec((BB, WIN, KVH * HD), lambda i: (i, 0, 0))
    new_spec = pl.BlockSpec((BB, 8, KVH * HD), lambda i: (i, 0, 0))
    cache_shape = jax.ShapeDtypeStruct((NB, WIN, KVH * HD), f32)
    return pl.pallas_call(
        _attn_sample_kernel,
        grid=(NB // BB,),
        in_specs=[
            pl.BlockSpec((BB, rows, KVH * HD), lambda i: (i, 0, 0)),
            cache_spec, cache_spec, new_spec, new_spec,
            pl.BlockSpec((rows, 1), lambda i: (0, 0)),
        ],
        out_specs=[pl.BlockSpec((BB, rows, KVH * HD), lambda i: (i, 0, 0)),
                   cache_spec, cache_spec],
        out_shape=[jax.ShapeDtypeStruct((NB, rows, KVH * HD), f32), cache_shape, cache_shape],
        compiler_params=_params("parallel"),
        name="attn_sample",
    )(qbd, kc, vc, kn, vn, sink_col)


def _proj_kernel(xp_ref, xs_ref, op_ref, os_ref, w_ref, y_ref):
    i = pl.program_id(0)
    y_ref[...] = _pick(i, xp_ref, xs_ref) + jnp.dot(_pick(i, op_ref, os_ref), w_ref[...],
                                                    preferred_element_type=f32)


def _proj(xp, xs, o_p, o_s, w):
    return pl.pallas_call(
        _proj_kernel,
        grid=(NT,),
        in_specs=_stream_specs() + _stream_specs() + [
            pl.BlockSpec((D, D), lambda i: (0, 0)),
        ],
        out_specs=pl.BlockSpec((T, D), lambda i: (i, 0)),
        out_shape=jax.ShapeDtypeStruct((N, D), f32),
        compiler_params=_params("parallel"),
        name="proj",
    )(xp, xs, o_p, o_s, w)


def _dispatch_kernel(x_ref, g_ref, w_ref, b_ref, upper_ref, xc_ref, meta_ref, cnt_ref):
    h_hi = _rms(x_ref[...], g_ref[...]).astype(bf16)
    logits = jnp.dot(h_hi, w_ref[...], preferred_element_type=f32) + b_ref[...]
    lt = logits.T
    inf = jnp.inf
    row8 = lax.broadcasted_iota(jnp.int32, (8, TM), 0).astype(f32)
    gl = jnp.where(row8 < NGRP, lt[0:8], -inf)
    gmax = jnp.max(gl, axis=0, keepdims=True)
    gsel = jnp.min(jnp.where(gl == gmax, row8, 8.0), axis=0, keepdims=True)
    g_w = 1.0 / jnp.sum(jnp.exp(gl - gmax), axis=0, keepdims=True)
    row = lax.broadcasted_iota(jnp.int32, (NE, TM), 0)
    row_f = row.astype(f32)
    el = jnp.where((row >> 2).astype(f32) == gsel, lt[ROW_E:ROW_E + NE], -inf)
    v1 = jnp.max(el, axis=0, keepdims=True)
    i1 = jnp.min(jnp.where(el == v1, row_f, float(NE)), axis=0, keepdims=True)
    el2 = jnp.where(row_f == i1, -inf, el)
    v2 = jnp.max(el2, axis=0, keepdims=True)
    i2 = jnp.min(jnp.where(el2 == v2, row_f, float(NE)), axis=0, keepdims=True)
    e1 = jnp.exp(v2 - v1)
    den = 1.0 + e1
    w1 = (1.0 / den) * g_w
    w2 = (e1 / den) * g_w

    m1 = row_f == i1
    m2 = row_f == i2
    sel = jnp.where(m1 | m2, 1.0, 0.0)
    ranks = jnp.dot(sel.astype(bf16), upper_ref[...],
                    preferred_element_type=f32)
    counts = jnp.sum(sel, axis=1, keepdims=True)
    padded = jnp.floor((counts + (UNIT - 1.0)) * (1.0 / UNIT)) * UNIT
    e_i = lax.broadcasted_iota(jnp.int32, (NE, NE), 0)
    f_i = lax.broadcasted_iota(jnp.int32, (NE, NE), 1)
    below = jnp.where(f_i < e_i, 1.0, 0.0).astype(bf16)
    seg = jnp.dot(below, jnp.broadcast_to(padded, (NE, LANES)).astype(bf16),
                  preferred_element_type=f32)[:, 0:1]
    posall = seg + ranks
    pos1 = jnp.sum(jnp.where(m1, posall, 0.0), axis=0, keepdims=True)
    pos2 = jnp.sum(jnp.where(m2, posall, 0.0), axis=0, keepdims=True)
    cnt_ref[...] = jnp.broadcast_to(counts, (NE, LANES))

    w1_hi = w1.astype(bf16).astype(f32)
    w2_hi = w2.astype(bf16).astype(f32)
    slab = jnp.zeros((8, TM), f32)
    for k, r in enumerate((pos1, pos2, w1_hi, w1 - w1_hi, w2_hi, w2 - w2_hi, i1)):
        slab = jnp.where(row8 == k, r, slab)
    meta = jnp.concatenate([slab, jnp.zeros((LANES - 8, TM), f32)], axis=0).T
    meta_ref[...] = meta

    rr = lax.broadcasted_iota(jnp.int32, (RC, TM), 0).astype(f32)
    onehot = jnp.where((rr == pos1) | (rr == pos2), 1.0, 0.0).astype(bf16)
    h_aug = jnp.concatenate([h_hi, meta.astype(bf16)], axis=1)
    xc_ref[...] = jnp.dot(onehot, h_aug, preferred_element_type=f32).astype(bf16)


def _dispatch(x, g, w, b):
    upper = jnp.asarray(np.triu(np.ones((TM, TM), np.float32), 1), bf16)
    return pl.pallas_call(
        _dispatch_kernel,
        grid=(NTM,),
        in_specs=[
            pl.BlockSpec((TM, D), lambda i: (i, 0)),
            pl.BlockSpec((1, D), lambda i: (0, 0)),
            pl.BlockSpec((D, LANES), lambda i: (0, 0)),
            pl.BlockSpec((1, LANES), lambda i: (0, 0)),
            pl.BlockSpec((TM, TM), lambda i: (0, 0)),
        ],
        out_specs=[
            pl.BlockSpec((RC, XW), lambda i: (i, 0)),
            pl.BlockSpec((TM, LANES), lambda i: (i, 0)),
            pl.BlockSpec((NE, LANES), lambda i: (i, 0)),
        ],
        out_shape=[
            jax.ShapeDtypeStruct((NTM * RC, XW), bf16),
            jax.ShapeDtypeStruct((N, LANES), f32),
            jax.ShapeDtypeStruct((NTM * NE, LANES), f32),
        ],
        compiler_params=_params("parallel"),
        name="dispatch",
    )(x, g, w, b, upper)


def _dispatch_tables(cnt):
    i32 = jnp.int32
    n = cnt.reshape(NTM, NE, LANES)[:, :, 0].astype(i32)
    units = (n + UNIT - 1) // UNIT
    seg_end = jnp.cumsum(units, axis=1)
    seg_start = seg_end - units
    col_end = jnp.cumsum(units, axis=0)
    col_start = col_end - units
    chunks = (col_end[-1] + UPC - 1) // UPC
    ch_end = jnp.cumsum(chunks)
    ch_start = ch_end - chunks
    nused = ch_end[-1]
    c = jnp.arange(NCH, dtype=i32)
    eid = jnp.minimum(jnp.sum((ch_end[None, :] <= c[:, None]).astype(i32), axis=1), NE - 1)

    src0 = jnp.arange(NTM, dtype=i32)[:, None] * UPT + seg_start
    dst0 = ch_start[None, :] * UPC + col_start
    k = jnp.arange(NCH * UPC, dtype=i32)[:, None, None]
    inside = (k >= dst0[None]) & (k < (dst0 + units)[None])
    found = jnp.sum(inside.astype(i32), axis=(1, 2))
    shift = jnp.sum(jnp.where(inside, (src0 - dst0)[None], 0), axis=(1, 2))
    ffn_src = jnp.where(found > 0, k[:, 0, 0] + shift, ZERO_UNIT_IN)

    v = jnp.arange(UPT, dtype=i32)[None, :, None]
    inside_v = (v >= seg_start[:, None, :]) & (v < seg_end[:, None, :])
    found_v = jnp.sum(inside_v.astype(i32), axis=2)
    shift_v = jnp.sum(jnp.where(inside_v, (dst0 - seg_start)[:, None, :], 0), axis=2)
    comb_src = jnp.where(found_v > 0, v[:, :, 0] + shift_v, ZERO_UNIT_OUT).reshape(-1)
    return eid, nused.reshape(1), ffn_src, comb_src


def _unit_copies(src_ref, base, n_units, src_hbm, stage, slot, sem):
    out = []
    for j in range(n_units):
        row = pl.multiple_of(src_ref[base + j] * UNIT, UNIT)
        out.append(pltpu.make_async_copy(
            src_hbm.at[pl.ds(row, UNIT), :],
            stage.at[slot, pl.ds(j * UNIT, UNIT), :],
            sem.at[slot]))
    return out


def _start_gathers(step, n_steps, n_units, src_ref, src_hbm, stage, sem):
    ahead = NSLOT - 1

    def start(s, slot):
        for cp in _unit_copies(src_ref, s * n_units, n_units, src_hbm, stage, slot, sem):
            cp.start()

    for s in range(ahead):
        @pl.when((step == 0) & (s < n_steps))
        def _():
            start(s, s)

    @pl.when(step + ahead < n_steps)
    def _():
        start(step + ahead, lax.rem(step + ahead, NSLOT))


def _wait_gather(step, n_units, src_ref, src_hbm, stage, sem):
    slot = lax.rem(step, NSLOT)
    for cp in _unit_copies(src_ref, step * n_units, n_units, src_hbm, stage, slot, sem):
        cp.wait()
    return slot


def _ffn_kernel(eid_ref, nused_ref, src_ref, xc_hbm, wg_ref, wu_ref, wd_ref, o_ref,
                stage, sem, wg_b, wu_b, wd_b):
    c = pl.program_id(0)
    nused = nused_ref[0]
    _start_gathers(c, nused, UPC, src_ref, xc_hbm, stage, sem)

    @pl.when(c < nused)
    def _():
        slot = _wait_gather(c, UPC, src_ref, xc_hbm, stage, sem)

        @pl.when((c == 0) | (eid_ref[c] != eid_ref[jnp.maximum(c - 1, 0)]))
        def _():
            wg_b[...] = wg_ref[...].astype(bf16)
            wu_b[...] = wu_ref[...].astype(bf16)
            wd_b[...] = wd_ref[...].astype(bf16)

        for r in range(0, CH, CHB):
            xs = stage[slot, r:r + CHB, :]
            x = xs[:, :D]
            gb = xs[:, D:].astype(f32)
            first = gb[:, 6:7] == eid_ref[c].astype(f32)
            gate = jnp.where(first, gb[:, 2:3] + gb[:, 3:4], gb[:, 4:5] + gb[:, 5:6])
            a = jnp.dot(x, wg_b[...], preferred_element_type=f32)
            u = jnp.dot(x, wu_b[...], preferred_element_type=f32)
            act = (a * (1.0 / (1.0 + jnp.exp(-a)))) * u * gate
            o_ref[r:r + CHB, :] = jnp.dot(act.astype(bf16), wd_b[...],
                                          preferred_element_type=f32).astype(bf16)

    @pl.when(c >= nused)
    def _():
        o_ref[...] = jnp.zeros((CH, D), bf16)


def _ffn(layer, eid, nused, ffn_src, xc, wg, wu, wd):
    return pl.pallas_call(
        _ffn_kernel,
        grid_spec=pltpu.PrefetchScalarGridSpec(
            num_scalar_prefetch=3,
            grid=(NCH,),
            in_specs=[
                pl.BlockSpec(memory_space=pl.ANY),
                pl.BlockSpec((None, None, D, F), lambda c, eid, nu, src: (layer, eid[c], 0, 0)),
                pl.BlockSpec((None, None, D, F), lambda c, eid, nu, src: (layer, eid[c], 0, 0)),
                pl.BlockSpec((None, None, F, D), lambda c, eid, nu, src: (layer, eid[c], 0, 0)),
            ],
            out_specs=pl.BlockSpec((CH, D), lambda c, eid, nu, src: (c, 0)),
            scratch_shapes=[
                pltpu.VMEM((NSLOT, CH, XW), bf16),
                pltpu.SemaphoreType.DMA((NSLOT,)),
                pltpu.VMEM((D, F), bf16),
                pltpu.VMEM((D, F), bf16),
                pltpu.VMEM((F, D), bf16),
            ],
        ),
        out_shape=jax.ShapeDtypeStruct((NCH * CH, D), bf16),
        compiler_params=_params("arbitrary"),
        name="ffn",
    )(eid, nused, ffn_src, xc, wg, wu, wd)


def _combine_kernel(src_ref, x_ref, meta_ref, o_hbm, y_refs, stage, sem):
    i = pl.program_id(0)
    _start_gathers(i, NTM, UPT, src_ref, o_hbm, stage, sem)
    slot = _wait_gather(i, UPT, src_ref, o_hbm, stage, sem)

    meta = meta_ref[...]
    pos1 = meta[:, 0:1]
    pos2 = meta[:, 1:2]
    li = lax.broadcasted_iota(jnp.int32, (TM, RC), 1).astype(f32)
    pt = jnp.where((li == pos1) | (li == pos2), 1.0, 0.0).astype(bf16)
    y = x_ref[...] + jnp.dot(pt, stage[slot], preferred_element_type=f32)
    if len(y_refs) == 1:
        y_refs[0][...] = y
    else:
        @pl.when(i < NP // TM)
        def _():
            y_refs[0][...] = y

        @pl.when(i >= NP // TM)
        def _():
            y_refs[1][...] = y


def _combine_kernel_1(src_ref, x_ref, meta_ref, o_hbm, y_ref, stage, sem):
    _combine_kernel(src_ref, x_ref, meta_ref, o_hbm, (y_ref,), stage, sem)


def _combine_kernel_2(src_ref, x_ref, meta_ref, o_hbm, yp_ref, ys_ref, stage, sem):
    _combine_kernel(src_ref, x_ref, meta_ref, o_hbm, (yp_ref, ys_ref), stage, sem)


def _combine(comb_src, x, meta, o_sorted, split):
    npt = NP // TM
    if split:
        body = _combine_kernel_2
        out_specs = [pl.BlockSpec((TM, D), lambda i, src: (jnp.minimum(i, npt - 1), 0)),
                     pl.BlockSpec((TM, D), lambda i, src: (jnp.maximum(i - npt, 0), 0))]
        out_shape = [jax.ShapeDtypeStruct((NP, D), f32), jax.ShapeDtypeStruct((NS, D), f32)]
    else:
        body = _combine_kernel_1
        out_specs = pl.BlockSpec((TM, D), lambda i, src: (i, 0))
        out_shape = jax.ShapeDtypeStruct((N, D), f32)
    return pl.pallas_call(
        body,
        grid_spec=pltpu.PrefetchScalarGridSpec(
            num_scalar_prefetch=1,
            grid=(NTM,),
            in_specs=[
                pl.BlockSpec((TM, D), lambda i, src: (i, 0)),
                pl.BlockSpec((TM, LANES), lambda i, src: (i, 0)),
                pl.BlockSpec(memory_space=pl.ANY),
            ],
            out_specs=out_specs,
            scratch_shapes=[
                pltpu.VMEM((NSLOT, RC, D), bf16),
                pltpu.SemaphoreType.DMA((NSLOT,)),
            ],
        ),
        out_shape=out_shape,
        compiler_params=_params("arbitrary"),
        name="combine",
    )(comb_src, x, meta, o_sorted)


def _conv_kernel(x_ref, g_ref, win_ref, cw_ref, wout_ref, p1_ref, p2_ref,
                 y_ref, cu_ref, pad_ref):
    i = pl.program_id(0)

    @pl.when(i == 0)
    def _():
        pad_ref[0:8, :] = jnp.zeros((8, D), f32)

    x = x_ref[...]
    h = _rms(x, g_ref[...]).astype(bf16)
    bcu = jnp.dot(h, win_ref[...], preferred_element_type=f32)
    b = bcu[:, :D]
    cu = bcu[:, D:2 * D] * bcu[:, 2 * D:]
    pad_ref[8:T + 8, :] = cu
    m1 = pad_ref[7:T + 7, :]
    m2 = pad_ref[6:T + 6, :]
    t = lax.broadcasted_iota(jnp.int32, (T, 1), 0) & (TS - 1)
    is_sample = i == NT - 1
    m1 = jnp.where(is_sample & (t == 0), p1_ref[...], m1)
    m2 = jnp.where(is_sample & (t < 2), p2_ref[...], m2)
    cw = cw_ref[...]
    conv = cw[0:1] * m2 + cw[1:2] * m1 + cw[2:3] * cu
    y = jnp.dot((b * conv).astype(bf16), wout_ref[...], preferred_element_type=f32)
    y_ref[...] = x + y
    cu_ref[...] = cu
    pad_ref[0:8, :] = cu[T - 8:, :]


def _conv(x, g, win, cw, wout, p1, p2):
    return pl.pallas_call(
        _conv_kernel,
        grid=(NT,),
        in_specs=[
            pl.BlockSpec((T, D), lambda i: (i, 0)),
            pl.BlockSpec((1, D), lambda i: (0, 0)),
            pl.BlockSpec((D, 3 * D), lambda i: (0, 0)),
            pl.BlockSpec((3, D), lambda i: (0, 0)),
            pl.BlockSpec((D, D), lambda i: (0, 0)),
            pl.BlockSpec((T, D), lambda i: (0, 0)),
            pl.BlockSpec((T, D), lambda i: (0, 0)),
        ],
        out_specs=[
            pl.BlockSpec((T, D), lambda i: (i, 0)),
            pl.BlockSpec((T, D), lambda i: (jnp.where(i == NT - 1, 1, 0), 0)),
        ],
        out_shape=[
            jax.ShapeDtypeStruct((N, D), f32),
            jax.ShapeDtypeStruct((2 * T, D), f32),
        ],
        scratch_shapes=[pltpu.VMEM((T + 8, D), f32)],
        compiler_params=_params("arbitrary"),
        name="conv",
    )(x, g, win, cw, wout, p1, p2)


def _rope_tables():
    inv_freq = THETA ** (-jnp.arange(HALF, dtype=f32) / HALF)
    pos = jnp.concatenate([
        jnp.arange(NP, dtype=jnp.int32),
        PAST + jnp.tile(jnp.arange(TS, dtype=jnp.int32), NB),
    ]).astype(f32)
    ang = inv_freq[:, None] * pos[None, :]
    return jnp.cos(ang), jnp.sin(ang)


def _router_weights(w_group, b_group, w_router, b_router):
    gap = ROW_E - NGRP
    pad = LANES - ROW_E - NE
    w = jnp.concatenate([w_group, jnp.zeros((D, gap), f32), w_router,
                         jnp.zeros((D, pad), f32)], axis=1)
    b = jnp.concatenate([b_group, jnp.zeros((gap,), f32), b_router,
                         jnp.zeros((pad,), f32)])[None, :]
    return w.astype(bf16), b


def _moe_layer(x, i, split, norm_ffn, w_group, b_group, w_router, b_router, w_gate, w_up, w_down):
    w, b = _router_weights(w_group[i], b_group[i], w_router[i], b_router[i])
    xc, meta, cnt = _dispatch(x, norm_ffn[i][None, :], w, b)
    eid, nused, ffn_src, comb_src = _dispatch_tables(cnt)
    o_sorted = _ffn(i, eid, nused, ffn_src, xc, w_gate, w_up, w_down)
    return _combine(comb_src, x, meta, o_sorted, split)


def kernel(x_prompt, x_sample, cache_k, cache_v, state_conv, norm_mix, w_qkv, q_norm, k_norm,
           sinks, w_o, w_in, conv_w, w_out, norm_ffn, w_group, b_group, w_router, b_router,
           w_gate, w_up, w_down):
    xp = x_prompt.reshape(NP, D)
    xs = x_sample.reshape(NS, D)
    moe_w = (norm_ffn, w_group, b_group, w_router, b_router, w_gate, w_up, w_down)

    cos, sin = _rope_tables()
    qT, ktok, vtok, vT = _qkv(xp, xs, norm_mix[0][None, :], w_qkv[0].T.astype(bf16),
                              q_norm[0][:, None], k_norm[0][:, None], cos, sin)
    sink_rows = jnp.repeat(sinks[0].reshape(KVH, G), TQ, axis=1)[:, None, :]
    o_p = _attn_prompt(qT, ktok, vT, sink_rows)

    qs = qT[:, NP:].reshape(KVH, G, HD, NB, TS).transpose(3, 0, 1, 4, 2)
    zq = jnp.zeros_like(qs[:, 0])
    qbd = jnp.stack([jnp.concatenate([qs[:, 0], zq], axis=-1),
                     jnp.concatenate([zq, qs[:, 1]], axis=-1)], axis=1)
    qbd = qbd.reshape(NB, H * TS, KVH * HD)
    k_new = ktok[NP:].reshape(NB, TS, KVH * HD)
    v_new = vtok[NP:].reshape(NB, TS, KVH * HD)
    pad4 = jnp.zeros((NB, 8 - TS, KVH * HD), f32)
    kc = cache_k[0].reshape(NB, WIN, KVH * HD)
    vc = cache_v[0].reshape(NB, WIN, KVH * HD)
    sink_col = jnp.repeat(sinks[0], TS)[:, None]
    o_s, kc_new, vc_new = _attn_sample(qbd, kc, vc, jnp.concatenate([k_new, pad4], axis=1),
                                       jnp.concatenate([v_new, pad4], axis=1), sink_col)
    o_s = o_s.reshape(NB, KVH, G, TS, KVH, HD)
    o_s = jnp.stack([o_s[:, 0, :, :, 0], o_s[:, 1, :, :, 1]], axis=1)
    o_s = o_s.transpose(0, 3, 1, 2, 4).reshape(NS, H * HD).astype(bf16)
    x = _proj(xp, xs, o_p, o_s, w_o[0].astype(bf16))
    x = _moe_layer(x, 0, False, *moe_w)

    new_k_prompt = ktok[NP - WIN:NP].reshape(1, 1, WIN, KVH, HD)
    new_v_prompt = vtok[NP - WIN:NP].reshape(1, 1, WIN, KVH, HD)
    new_k_sample = kc_new.reshape(1, NB, WIN, KVH, HD)
    new_v_sample = vc_new.reshape(1, NB, WIN, KVH, HD)

    st = state_conv[0]
    z = jnp.zeros((NB, 1, D), f32)
    p1 = jnp.concatenate([st[:, 1:2], z, z, z], axis=1).reshape(NS, D)
    p2 = jnp.concatenate([st[:, 0:1], st[:, 1:2], z, z], axis=1).reshape(NS, D)
    x, cu = _conv(x, norm_mix[1][None, :], w_in[0].astype(bf16), conv_w[0],
                  w_out[0].astype(bf16), p1, p2)
    y_prompt, y_sample = _moe_layer(x, 1, True, *moe_w)

    new_conv_prompt = cu[T - 2:T].reshape(1, 1, 2, D)
    new_conv_sample = cu[T:].reshape(NB, TS, D)[:, TS - 2:][None]

    y_prompt = y_prompt.reshape(1, NP, D)
    y_sample = y_sample.reshape(NB, TS, D)
    return (y_prompt, y_sample, new_k_prompt, new_v_prompt, new_conv_prompt,
            new_k_sample, new_v_sample, new_conv_sample)
```

```python
import jax
import jax.numpy as jnp
import numpy as np
from jax import lax
from jax.experimental import pallas as pl
from jax.experimental.pallas import tpu as pltpu

D = 1024
NP = 16384
NB = 128
TS = 4
NS = NB * TS
N = NP + NS
PAST = 16384
H = 16
KVH = 2
G = H // KVH
HD = 64
HALF = HD // 2
QKV = (H + 2 * KVH) * HD
WIN = 128
THETA = 10000.0
NGRP = 4
EPG = 4
NE = NGRP * EPG
TOPK = 2
ROW_E = 8
F = 512
EPS = 1e-6
SCALE = HD ** -0.5

T = 512
NT = N // T
TQ = 128
NQ = NP // TQ
BB = 8
LANES = 128
VMEM_LIMIT = 50 * 1024 * 1024

TM = 256
NTM = N // TM
UNIT = 16
RC = TOPK * TM + NE * UNIT
UPT = RC // UNIT
XW = D + LANES
CH = 512
CHB = 256
UPC = CH // UNIT
NSLOT = 3
NCH = -(-(NTM * (UPT - 1) + NE * (UPC - 1)) // UPC) + NSLOT - 1
ZERO_UNIT_IN = UPT - 1
ZERO_UNIT_OUT = (NCH - 1) * UPC

f32 = jnp.float32
bf16 = jnp.bfloat16


def _params(*sem):
    return pltpu.CompilerParams(dimension_semantics=sem, vmem_limit_bytes=VMEM_LIMIT)


def _rms(x, g):
    ms = jnp.mean(x * x, axis=-1, keepdims=True)
    return x * lax.rsqrt(ms + EPS) * g


def _pick(i, prompt_ref, sample_ref):
    return jnp.where(i == NT - 1, sample_ref[...], prompt_ref[...])


def _stream_specs():
    return [pl.BlockSpec((T, D), lambda i: (jnp.minimum(i, NT - 2), 0)),
            pl.BlockSpec((T, D), lambda i: (0, 0))]


def _qkv_kernel(xp_ref, xs_ref, g_ref, wT_ref, qn_ref, kn_ref, cos_ref, sin_ref,
                qT_ref, ktok_ref, vtok_ref, vT_ref):
    h = _rms(_pick(pl.program_id(0), xp_ref, xs_ref), g_ref[...]).astype(bf16)
    qkvT = lax.dot_general(wT_ref[...], h, (((1,), (1,)), ((), ())),
                           preferred_element_type=f32)
    cos = cos_ref[...]
    sin = sin_ref[...]

    def norm_rope(blk, gcol):
        ms = jnp.mean(blk * blk, axis=0, keepdims=True)
        y = blk * lax.rsqrt(ms + EPS) * gcol
        y1 = y[:HALF]
        y2 = y[HALF:]
        return y1 * cos - y2 * sin, y2 * cos + y1 * sin

    qn = qn_ref[...]
    for hd in range(H):
        o1, o2 = norm_rope(qkvT[hd * HD:(hd + 1) * HD], qn)
        qT_ref[hd * HD:hd * HD + HALF, :] = (o1 * SCALE).astype(bf16)
        qT_ref[hd * HD + HALF:(hd + 1) * HD, :] = (o2 * SCALE).astype(bf16)
    kn = kn_ref[...]
    ks = []
    for j in range(KVH):
        o1, o2 = norm_rope(qkvT[H * HD + j * HD:H * HD + (j + 1) * HD], kn)
        ks += [o1, o2]
    kT = jnp.concatenate(ks, axis=0)
    ktok_ref[...] = kT.T
    vT = qkvT[(H + KVH) * HD:]
    vtok_ref[...] = vT.T
    vT_ref[...] = vT.astype(bf16)


def _qkv(xp, xs, g, wT, qn, kn, cos, sin):
    return pl.pallas_call(
        _qkv_kernel,
        grid=(NT,),
        in_specs=_stream_specs() + [
            pl.BlockSpec((1, D), lambda i: (0, 0)),
            pl.BlockSpec((QKV, D), lambda i: (0, 0)),
            pl.BlockSpec((HD, 1), lambda i: (0, 0)),
            pl.BlockSpec((HD, 1), lambda i: (0, 0)),
            pl.BlockSpec((HALF, T), lambda i: (0, i)),
            pl.BlockSpec((HALF, T), lambda i: (0, i)),
        ],
        out_specs=[
            pl.BlockSpec((H * HD, T), lambda i: (0, i)),
            pl.BlockSpec((T, KVH * HD), lambda i: (i, 0)),
            pl.BlockSpec((T, KVH * HD), lambda i: (i, 0)),
            pl.BlockSpec((KVH * HD, T), lambda i: (0, i)),
        ],
        out_shape=[
            jax.ShapeDtypeStruct((H * HD, N), bf16),
            jax.ShapeDtypeStruct((N, KVH * HD), f32),
            jax.ShapeDtypeStruct((N, KVH * HD), f32),
            jax.ShapeDtypeStruct((KVH * HD, N), bf16),
        ],
        compiler_params=_params("parallel"),
        name="qkv",
    )(xp, xs, g, wT, qn, kn, cos, sin)


def _attn_prompt_kernel(qT_ref, kp_ref, kc_ref, vp_ref, vc_ref, sink_ref, bias_ref,
                        xp_ref, xs_ref, os_ref, wo_ref, y_ref):
    j = pl.program_id(0)

    @pl.when(j >= NQ)
    def _():
        y_ref[...] = xs_ref[...] + jnp.dot(os_ref[...], wo_ref[...],
                                           preferred_element_type=f32)

    @pl.when(j < NQ)
    def _():
        _attend_block(qT_ref, kp_ref, kc_ref, vp_ref, vc_ref, sink_ref, bias_ref,
                      xp_ref, wo_ref, y_ref)


def _attend_block(qT_ref, kp_ref, kc_ref, vp_ref, vc_ref, sink_ref, bias_ref,
                  x_ref, wo_ref, y_ref):
    kk = jnp.concatenate([kp_ref[...], kc_ref[...]], axis=0).astype(bf16)
    vv = jnp.concatenate([vp_ref[...], vc_ref[...]], axis=1)
    bias = jnp.concatenate([bias_ref[...]] * H, axis=1)
    qg = [jnp.concatenate([qT_ref[(g * G + hh) * HD:(g * G + hh + 1) * HD, :]
                           for hh in range(G)], axis=1) for g in range(KVH)]
    zeros = jnp.zeros_like(qg[0])
    rhs = jnp.concatenate([jnp.concatenate([qg[0], zeros], axis=1),
                           jnp.concatenate([zeros, qg[1]], axis=1)], axis=0)
    sT = jnp.dot(kk, rhs, preferred_element_type=f32) + bias
    sink = jnp.concatenate([sink_ref[0], sink_ref[1]], axis=1)
    m = jnp.maximum(jnp.max(sT, axis=0, keepdims=True), sink)
    p = jnp.exp(sT - m)
    l = jnp.sum(p, axis=0, keepdims=True) + jnp.exp(sink - m)
    p = (p * (1.0 / l)).astype(bf16)
    pieces = []
    for g in range(KVH):
        oT = jnp.dot(vv[g * HD:(g + 1) * HD, :], p[:, g * G * TQ:(g + 1) * G * TQ],
                     preferred_element_type=f32)
        pieces += [oT[:, hh * TQ:(hh + 1) * TQ] for hh in range(G)]
    oT_all = jnp.concatenate(pieces, axis=0)
    y_ref[...] = x_ref[...] + jnp.dot(oT_all.T.astype(bf16), wo_ref[...],
                                      preferred_element_type=f32)


def _band_bias():
    s = np.arange(2 * TQ)[:, None]
    t = np.arange(TQ)[None, :]
    dist = t + TQ - s
    band = (dist >= 0) & (dist <= WIN)
    first = band & (s >= TQ)
    return jnp.asarray(np.where(np.stack([first, band]), 0.0, -np.inf), f32)


def _attn_prompt(qT, ktok, vT, sink_rows, xp, xs, o_s, wo):
    def cur(j):
        return jnp.minimum(j, NQ - 1)

    def prev(j):
        return jnp.maximum(cur(j) - 1, 0)

    def sample(j):
        return jnp.maximum(j - NQ, 0)

    return pl.pallas_call(
        _attn_prompt_kernel,
        grid=(N // TQ,),
        in_specs=[
            pl.BlockSpec((H * HD, TQ), lambda j: (0, cur(j))),
            pl.BlockSpec((TQ, KVH * HD), lambda j: (prev(j), 0)),
            pl.BlockSpec((TQ, KVH * HD), lambda j: (cur(j), 0)),
            pl.BlockSpec((KVH * HD, TQ), lambda j: (0, prev(j))),
            pl.BlockSpec((KVH * HD, TQ), lambda j: (0, cur(j))),
            pl.BlockSpec((KVH, 1, G * TQ), lambda j: (0, 0, 0)),
            pl.BlockSpec((None, 2 * TQ, TQ), lambda j: (jnp.minimum(j, 1), 0, 0)),
            pl.BlockSpec((TQ, D), lambda j: (cur(j), 0)),
            pl.BlockSpec((TQ, D), lambda j: (sample(j), 0)),
            pl.BlockSpec((TQ, H * HD), lambda j: (sample(j), 0)),
            pl.BlockSpec((D, D), lambda j: (0, 0)),
        ],
        out_specs=pl.BlockSpec((TQ, D), lambda j: (j, 0)),
        out_shape=jax.ShapeDtypeStruct((N, D), f32),
        compiler_params=_params("parallel"),
        name="attn_prompt",
    )(qT, ktok, ktok, vT, vT, sink_rows, _band_bias(), xp, xs, o_s, wo)


def _attn_sample_kernel(q_ref, kc_ref, vc_ref, kn_ref, vn_ref, sink_ref,
                        o_ref, knew_ref, vnew_ref):
    rows = H * TS
    t1 = lax.broadcasted_iota(jnp.int32, (1, rows, WIN), 1) & (TS - 1)
    s1 = lax.broadcasted_iota(jnp.int32, (1, rows, WIN), 2)
    valid1 = s1 >= t1
    t2 = lax.broadcasted_iota(jnp.int32, (1, rows, 8), 1) & (TS - 1)
    s2 = lax.broadcasted_iota(jnp.int32, (1, rows, 8), 2)
    valid2 = s2 <= t2
    sink = sink_ref[...][None]
    q = q_ref[...]
    kc = kc_ref[...]
    vc = vc_ref[...]
    kn = kn_ref[...]
    vn = vn_ref[...]
    sc = jnp.einsum('bqd,bkd->bqk', q, kc.astype(bf16), preferred_element_type=f32)
    sn = jnp.einsum('bqd,bkd->bqk', q, kn.astype(bf16), preferred_element_type=f32)
    sc = jnp.where(valid1, sc, -jnp.inf)
    sn = jnp.where(valid2, sn, -jnp.inf)
    m = jnp.maximum(jnp.maximum(jnp.max(sc, axis=-1, keepdims=True),
                                jnp.max(sn, axis=-1, keepdims=True)), sink)
    pc = jnp.exp(sc - m)
    pn = jnp.exp(sn - m)
    l = (jnp.sum(pc, axis=-1, keepdims=True) + jnp.sum(pn, axis=-1, keepdims=True)
         + jnp.exp(sink - m))
    r = 1.0 / l
    o_ref[...] = (jnp.einsum('bqk,bkd->bqd', (pc * r).astype(bf16), vc.astype(bf16),
                             preferred_element_type=f32)
                  + jnp.einsum('bqk,bkd->bqd', (pn * r).astype(bf16), vn.astype(bf16),
                               preferred_element_type=f32))
    knew_ref[:, :WIN - TS, :] = kc[:, TS:, :]
    knew_ref[:, WIN - TS:, :] = kn[:, :TS, :]
    vnew_ref[:, :WIN - TS, :] = vc[:, TS:, :]
    vnew_ref[:, WIN - TS:, :] = vn[:, :TS, :]


def _attn_sample(qbd, kc, vc, kn, vn, sink_col):
    rows = H * TS
    cache_spec = pl.BlockSpec((BB, WIN, KVH * HD), lambda i: (i, 0, 0))
    new_spec = pl.BlockSpec((BB, 8, KVH * HD), lambda i: (i, 0, 0))
    cache_shape = jax.ShapeDtypeStruct((NB, WIN, KVH * HD), f32)
    return pl.pallas_call(
        _attn_sample_kernel,
        grid=(NB // BB,),
        in_specs=[
            pl.BlockSpec((BB, rows, KVH * HD), lambda i: (i, 0, 0)),
            cache_spec, cache_spec, new_spec, new_spec,
            pl.BlockSpec((rows, 1), lambda i: (0, 0)),
        ],
        out_specs=[pl.BlockSpec((BB, rows, KVH * HD), lambda i: (i, 0, 0)),
                   cache_spec, cache_spec],
        out_shape=[jax.ShapeDtypeStruct((NB, rows, KVH * HD), f32), cache_shape, cache_shape],
        compiler_params=_params("parallel"),
        name="attn_sample",
    )(qbd, kc, vc, kn, vn, sink_col)


def _dispatch_kernel(x_ref, g_ref, w_ref, b_ref, upper_ref, xc_ref, meta_ref, cnt_ref):
    h_hi = _rms(x_ref[...], g_ref[...]).astype(bf16)
    logits = jnp.dot(h_hi, w_ref[...], preferred_element_type=f32) + b_ref[...]
    lt = logits.T
    inf = jnp.inf
    row8 = lax.broadcasted_iota(jnp.int32, (8, TM), 0).astype(f32)
    gl = jnp.where(row8 < NGRP, lt[0:8], -inf)
    gmax = jnp.max(gl, axis=0, keepdims=True)
    gsel = jnp.min(jnp.where(gl == gmax, row8, 8.0), axis=0, keepdims=True)
    g_w = 1.0 / jnp.sum(jnp.exp(gl - gmax), axis=0, keepdims=True)
    row = lax.broadcasted_iota(jnp.int32, (NE, TM), 0)
    row_f = row.astype(f32)
    el = jnp.where((row >> 2).astype(f32) == gsel, lt[ROW_E:ROW_E + NE], -inf)
    v1 = jnp.max(el, axis=0, keepdims=True)
    i1 = jnp.min(jnp.where(el == v1, row_f, float(NE)), axis=0, keepdims=True)
    el2 = jnp.where(row_f == i1, -inf, el)
    v2 = jnp.max(el2, axis=0, keepdims=True)
    i2 = jnp.min(jnp.where(el2 == v2, row_f, float(NE)), axis=0, keepdims=True)
    e1 = jnp.exp(v2 - v1)
    den = 1.0 + e1
    w1 = (1.0 / den) * g_w
    w2 = (e1 / den) * g_w

    m1 = row_f == i1
    m2 = row_f == i2
    sel = jnp.where(m1 | m2, 1.0, 0.0)
    ranks = jnp.dot(sel.astype(bf16), upper_ref[...],
                    preferred_element_type=f32)
    counts = jnp.sum(sel, axis=1, keepdims=True)
    padded = jnp.floor((counts + (UNIT - 1.0)) * (1.0 / UNIT)) * UNIT
    e_i = lax.broadcasted_iota(jnp.int32, (NE, NE), 0)
    f_i = lax.broadcasted_iota(jnp.int32, (NE, NE), 1)
    below = jnp.where(f_i < e_i, 1.0, 0.0).astype(bf16)
    seg = jnp.dot(below, jnp.broadcast_to(padded, (NE, LANES)).astype(bf16),
                  preferred_element_type=f32)[:, 0:1]
    posall = seg + ranks
    pos1 = jnp.sum(jnp.where(m1, posall, 0.0), axis=0, keepdims=True)
    pos2 = jnp.sum(jnp.where(m2, posall, 0.0), axis=0, keepdims=True)
    cnt_ref[...] = jnp.broadcast_to(counts, (NE, LANES))

    w1_hi = w1.astype(bf16).astype(f32)
    w2_hi = w2.astype(bf16).astype(f32)
    slab = jnp.zeros((8, TM), f32)
    for k, r in enumerate((pos1, pos2, w1_hi, w1 - w1_hi, w2_hi, w2 - w2_hi, i1)):
        slab = jnp.where(row8 == k, r, slab)
    meta = jnp.concatenate([slab, jnp.zeros((LANES - 8, TM), f32)], axis=0).T
    meta_ref[...] = meta

    rr = lax.broadcasted_iota(jnp.int32, (RC, TM), 0).astype(f32)
    onehot = jnp.where((rr == pos1) | (rr == pos2), 1.0, 0.0).astype(bf16)
    h_aug = jnp.concatenate([h_hi, meta.astype(bf16)], axis=1)
    xc_ref[...] = jnp.dot(onehot, h_aug, preferred_element_type=f32).astype(bf16)


def _dispatch(x, g, w, b):
    upper = jnp.asarray(np.triu(np.ones((TM, TM), np.float32), 1), bf16)
    return pl.pallas_call(
        _dispatch_kernel,
        grid=(NTM,),
        in_specs=[
            pl.BlockSpec((TM, D), lambda i: (i, 0)),
            pl.BlockSpec((1, D), lambda i: (0, 0)),
            pl.BlockSpec((D, LANES), lambda i: (0, 0)),
            pl.BlockSpec((1, LANES), lambda i: (0, 0)),
            pl.BlockSpec((TM, TM), lambda i: (0, 0)),
        ],
        out_specs=[
            pl.BlockSpec((RC, XW), lambda i: (i, 0)),
            pl.BlockSpec((TM, LANES), lambda i: (i, 0)),
            pl.BlockSpec((NE, LANES), lambda i: (i, 0)),
        ],
        out_shape=[
            jax.ShapeDtypeStruct((NTM * RC, XW), bf16),
            jax.ShapeDtypeStruct((N, LANES), f32),
            jax.ShapeDtypeStruct((NTM * NE, LANES), f32),
        ],
        compiler_params=_params("parallel"),
        name="dispatch",
    )(x, g, w, b, upper)


def _dispatch_tables(cnt):
    i32 = jnp.int32
    n = cnt.reshape(NTM, NE, LANES)[:, :, 0].astype(i32)
    units = (n + UNIT - 1) // UNIT
    seg_end = jnp.cumsum(units, axis=1)
    seg_start = seg_end - units
    col_end = jnp.cumsum(units, axis=0)
    col_start = col_end - units
    chunks = (col_end[-1] + UPC - 1) // UPC
    ch_end = jnp.cumsum(chunks)
    ch_start = ch_end - chunks
    nused = ch_end[-1]
    c = jnp.arange(NCH, dtype=i32)
    eid = jnp.minimum(jnp.sum((ch_end[None, :] <= c[:, None]).astype(i32), axis=1), NE - 1)

    src0 = jnp.arange(NTM, dtype=i32)[:, None] * UPT + seg_start
    dst0 = ch_start[None, :] * UPC + col_start
    k = jnp.arange((NCH + NSLOT - 1) * UPC, dtype=i32)[:, None, None]
    inside = (k >= dst0[None]) & (k < (dst0 + units)[None])
    found = jnp.sum(inside.astype(i32), axis=(1, 2))
    shift = jnp.sum(jnp.where(inside, (src0 - dst0)[None], 0), axis=(1, 2))
    ffn_src = jnp.where(found > 0, k[:, 0, 0] + shift, ZERO_UNIT_IN)

    v = jnp.arange(UPT, dtype=i32)[None, :, None]
    inside_v = (v >= seg_start[:, None, :]) & (v < seg_end[:, None, :])
    found_v = jnp.sum(inside_v.astype(i32), axis=2)
    shift_v = jnp.sum(jnp.where(inside_v, (dst0 - seg_start)[:, None, :], 0), axis=2)
    comb_src = jnp.where(found_v > 0, v[:, :, 0] + shift_v, ZERO_UNIT_OUT).reshape(-1)
    return eid, nused.reshape(1), ffn_src, comb_src


def _unit_copies(src_ref, base, n_units, src_hbm, stage, slot, sem):
    out = []
    for j in range(n_units):
        row = pl.multiple_of(src_ref[base + j] * UNIT, UNIT)
        out.append(pltpu.make_async_copy(
            src_hbm.at[pl.ds(row, UNIT), :],
            stage.at[slot, pl.ds(j * UNIT, UNIT), :],
            sem.at[slot]))
    return out


def _start_gathers(step, n_steps, n_units, src_ref, src_hbm, stage, sem):
    ahead = NSLOT - 1

    def start(s, slot):
        for cp in _unit_copies(src_ref, s * n_units, n_units, src_hbm, stage, slot, sem):
            cp.start()

    for s in range(ahead):
        @pl.when((step == 0) & (s < n_steps))
        def _():
            start(s, s)

    @pl.when(step + ahead < n_steps)
    def _():
        start(step + ahead, lax.rem(step + ahead, NSLOT))


def _wait_gather(step, n_units, src_ref, src_hbm, stage, sem):
    slot = lax.rem(step, NSLOT)
    for cp in _unit_copies(src_ref, step * n_units, n_units, src_hbm, stage, slot, sem):
        cp.wait()
    return slot


def _ffn_kernel(eid_ref, nused_ref, src_ref, xc_hbm, wg_ref, wu_ref, wd_ref, o_ref,
                stage, sem, wg_b, wu_b, wd_b):
    c = pl.program_id(0)
    nused = nused_ref[0]
    ahead = NSLOT - 1

    def start(s):
        for cp in _unit_copies(src_ref, s * UPC, UPC, xc_hbm, stage, lax.rem(s, NSLOT), sem):
            cp.start()

    @pl.when(c == 0)
    def _():
        for s in range(ahead):
            start(s)

    @pl.when(c < nused)
    def _():
        slot = _wait_gather(c, UPC, src_ref, xc_hbm, stage, sem)

        @pl.when((c == 0) | (eid_ref[c] != eid_ref[jnp.maximum(c - 1, 0)]))
        def _():
            wg_b[...] = wg_ref[...].astype(bf16)
            wu_b[...] = wu_ref[...].astype(bf16)
            wd_b[...] = wd_ref[...].astype(bf16)

        for r in range(0, CH, CHB):
            xs = stage[slot, r:r + CHB, :]
            x = xs[:, :D]
            gb = xs[:, D:].astype(f32)
            first = gb[:, 6:7] == eid_ref[c].astype(f32)
            gate = jnp.where(first, gb[:, 2:3] + gb[:, 3:4], gb[:, 4:5] + gb[:, 5:6])
            a = jnp.dot(x, wg_b[...], preferred_element_type=f32)
            u = jnp.dot(x, wu_b[...], preferred_element_type=f32)
            act = (a * (1.0 / (1.0 + jnp.exp(-a)))) * u * gate
            o_ref[r:r + CHB, :] = jnp.dot(act.astype(bf16), wd_b[...],
                                          preferred_element_type=f32).astype(bf16)
        start(c + ahead)

    @pl.when(c >= nused)
    def _():
        @pl.when(c < nused + ahead)
        def _():
            _wait_gather(c, UPC, src_ref, xc_hbm, stage, sem)

        o_ref[...] = jnp.zeros((CH, D), bf16)


def _ffn(layer, eid, nused, ffn_src, xc, wg, wu, wd):
    return pl.pallas_call(
        _ffn_kernel,
        grid_spec=pltpu.PrefetchScalarGridSpec(
            num_scalar_prefetch=3,
            grid=(NCH,),
            in_specs=[
                pl.BlockSpec(memory_space=pl.ANY),
                pl.BlockSpec((None, None, D, F), lambda c, eid, nu, src: (layer, eid[c], 0, 0)),
                pl.BlockSpec((None, None, D, F), lambda c, eid, nu, src: (layer, eid[c], 0, 0)),
                pl.BlockSpec((None, None, F, D), lambda c, eid, nu, src: (layer, eid[c], 0, 0)),
            ],
            out_specs=pl.BlockSpec((CH, D), lambda c, eid, nu, src: (c, 0)),
            scratch_shapes=[
                pltpu.VMEM((NSLOT, CH, XW), bf16),
                pltpu.SemaphoreType.DMA((NSLOT,)),
                pltpu.VMEM((D, F), bf16),
                pltpu.VMEM((D, F), bf16),
                pltpu.VMEM((F, D), bf16),
            ],
        ),
        out_shape=jax.ShapeDtypeStruct((NCH * CH, D), bf16),
        compiler_params=_params("arbitrary"),
        name="ffn",
    )(eid, nused, ffn_src, xc, wg, wu, wd)


def _combine_kernel(src_ref, x_ref, meta_ref, o_hbm, y_refs, stage, sem):
    i = pl.program_id(0)
    _start_gathers(i, NTM, UPT, src_ref, o_hbm, stage, sem)
    slot = _wait_gather(i, UPT, src_ref, o_hbm, stage, sem)

    meta = meta_ref[...]
    pos1 = meta[:, 0:1]
    pos2 = meta[:, 1:2]
    li = lax.broadcasted_iota(jnp.int32, (TM, RC), 1).astype(f32)
    pt = jnp.where((li == pos1) | (li == pos2), 1.0, 0.0).astype(bf16)
    y = x_ref[...] + jnp.dot(pt, stage[slot], preferred_element_type=f32)
    if len(y_refs) == 1:
        y_refs[0][...] = y
    else:
        @pl.when(i < NP // TM)
        def _():
            y_refs[0][...] = y

        @pl.when(i >= NP // TM)
        def _():
            y_refs[1][...] = y


def _combine_kernel_1(src_ref, x_ref, meta_ref, o_hbm, y_ref, stage, sem):
    _combine_kernel(src_ref, x_ref, meta_ref, o_hbm, (y_ref,), stage, sem)


def _combine_kernel_2(src_ref, x_ref, meta_ref, o_hbm, yp_ref, ys_ref, stage, sem):
    _combine_kernel(src_ref, x_ref, meta_ref, o_hbm, (yp_ref, ys_ref), stage, sem)


def _combine(comb_src, x, meta, o_sorted, split):
    npt = NP // TM
    if split:
        body = _combine_kernel_2
        out_specs = [pl.BlockSpec((TM, D), lambda i, src: (jnp.minimum(i, npt - 1), 0)),
                     pl.BlockSpec((TM, D), lambda i, src: (jnp.maximum(i - npt, 0), 0))]
        out_shape = [jax.ShapeDtypeStruct((NP, D), f32), jax.ShapeDtypeStruct((NS, D), f32)]
    else:
        body = _combine_kernel_1
        out_specs = pl.BlockSpec((TM, D), lambda i, src: (i, 0))
        out_shape = jax.ShapeDtypeStruct((N, D), f32)
    return pl.pallas_call(
        body,
        grid_spec=pltpu.PrefetchScalarGridSpec(
            num_scalar_prefetch=1,
            grid=(NTM,),
            in_specs=[
                pl.BlockSpec((TM, D), lambda i, src: (i, 0)),
                pl.BlockSpec((TM, LANES), lambda i, src: (i, 0)),
                pl.BlockSpec(memory_space=pl.ANY),
            ],
            out_specs=out_specs,
            scratch_shapes=[
                pltpu.VMEM((NSLOT, RC, D), bf16),
                pltpu.SemaphoreType.DMA((NSLOT,)),
            ],
        ),
        out_shape=out_shape,
        compiler_params=_params("arbitrary"),
        name="combine",
    )(comb_src, x, meta, o_sorted)


def _conv_kernel(x_ref, g_ref, win_ref, cw_ref, wout_ref, p1_ref, p2_ref,
                 y_ref, cu_ref, pad_ref):
    i = pl.program_id(0)

    @pl.when(i == 0)
    def _():
        pad_ref[0:8, :] = jnp.zeros((8, D), f32)

    blocks = [(r, r + T // 2) for r in (0, T // 2)]
    gates = []
    for r0, r1 in blocks:
        h = _rms(x_ref[r0:r1, :], g_ref[...]).astype(bf16)
        bcu = jnp.dot(h, win_ref[...], preferred_element_type=f32)
        gates.append(bcu[:, :D])
        cu = bcu[:, D:2 * D] * bcu[:, 2 * D:]
        pad_ref[8 + r0:8 + r1, :] = cu
        cu_ref[r0:r1, :] = cu
    t = lax.broadcasted_iota(jnp.int32, (T // 2, 1), 0) & (TS - 1)
    is_sample = i == NT - 1
    cw = cw_ref[...]
    for (r0, r1), b in zip(blocks, gates):
        m1 = jnp.where(is_sample & (t == 0), p1_ref[r0:r1, :], pad_ref[7 + r0:7 + r1, :])
        m2 = jnp.where(is_sample & (t < 2), p2_ref[r0:r1, :], pad_ref[6 + r0:6 + r1, :])
        conv = cw[0:1] * m2 + cw[1:2] * m1 + cw[2:3] * pad_ref[8 + r0:8 + r1, :]
        y = jnp.dot((b * conv).astype(bf16), wout_ref[...], preferred_element_type=f32)
        y_ref[r0:r1, :] = x_ref[r0:r1, :] + y
    pad_ref[0:8, :] = pad_ref[T:T + 8, :]


def _conv(x, g, win, cw, wout, p1, p2):
    return pl.pallas_call(
        _conv_kernel,
        grid=(NT,),
        in_specs=[
            pl.BlockSpec((T, D), lambda i: (i, 0)),
            pl.BlockSpec((1, D), lambda i: (0, 0)),
            pl.BlockSpec((D, 3 * D), lambda i: (0, 0)),
            pl.BlockSpec((3, D), lambda i: (0, 0)),
            pl.BlockSpec((D, D), lambda i: (0, 0)),
            pl.BlockSpec((T, D), lambda i: (0, 0)),
            pl.BlockSpec((T, D), lambda i: (0, 0)),
        ],
        out_specs=[
            pl.BlockSpec((T, D), lambda i: (i, 0)),
            pl.BlockSpec((T, D), lambda i: (jnp.where(i == NT - 1, 1, 0), 0)),
        ],
        out_shape=[
            jax.ShapeDtypeStruct((N, D), f32),
            jax.ShapeDtypeStruct((2 * T, D), f32),
        ],
        scratch_shapes=[pltpu.VMEM((T + 8, D), f32)],
        compiler_params=_params("arbitrary"),
        name="conv",
    )(x, g, win, cw, wout, p1, p2)


def _rope_tables():
    inv_freq = THETA ** (-jnp.arange(HALF, dtype=f32) / HALF)
    pos = jnp.concatenate([
        jnp.arange(NP, dtype=jnp.int32),
        PAST + jnp.tile(jnp.arange(TS, dtype=jnp.int32), NB),
    ]).astype(f32)
    ang = inv_freq[:, None] * pos[None, :]
    return jnp.cos(ang), jnp.sin(ang)


def _router_weights(w_group, b_group, w_router, b_router):
    gap = ROW_E - NGRP
    pad = LANES - ROW_E - NE
    w = jnp.concatenate([w_group, jnp.zeros((D, gap), f32), w_router,
                         jnp.zeros((D, pad), f32)], axis=1)
    b = jnp.concatenate([b_group, jnp.zeros((gap,), f32), b_router,
                         jnp.zeros((pad,), f32)])[None, :]
    return w.astype(bf16), b


def _moe_layer(x, i, split, norm_ffn, w_group, b_group, w_router, b_router, w_gate, w_up, w_down):
    w, b = _router_weights(w_group[i], b_group[i], w_router[i], b_router[i])
    xc, meta, cnt = _dispatch(x, norm_ffn[i][None, :], w, b)
    eid, nused, ffn_src, comb_src = _dispatch_tables(cnt)
    o_sorted = _ffn(i, eid, nused, ffn_src, xc, w_gate, w_up, w_down)
    return _combine(comb_src, x, meta, o_sorted, split)


def kernel(x_prompt, x_sample, cache_k, cache_v, state_conv, norm_mix, w_qkv, q_norm, k_norm,
           sinks, w_o, w_in, conv_w, w_out, norm_ffn, w_group, b_group, w_router, b_router,
           w_gate, w_up, w_down):
    xp = x_prompt.reshape(NP, D)
    xs = x_sample.reshape(NS, D)
    moe_w = (norm_ffn, w_group, b_group, w_router, b_router, w_gate, w_up, w_down)

    cos, sin = _rope_tables()
    qT, ktok, vtok, vT = _qkv(xp, xs, norm_mix[0][None, :], w_qkv[0].T.astype(bf16),
                              q_norm[0][:, None], k_norm[0][:, None], cos, sin)
    sink_rows = jnp.repeat(sinks[0].reshape(KVH, G), TQ, axis=1)[:, None, :]

    qs = qT[:, NP:].reshape(KVH, G, HD, NB, TS).transpose(3, 0, 1, 4, 2)
    zq = jnp.zeros_like(qs[:, 0])
    qbd = jnp.stack([jnp.concatenate([qs[:, 0], zq], axis=-1),
                     jnp.concatenate([zq, qs[:, 1]], axis=-1)], axis=1)
    qbd = qbd.reshape(NB, H * TS, KVH * HD)
    k_new = ktok[NP:].reshape(NB, TS, KVH * HD)
    v_new = vtok[NP:].reshape(NB, TS, KVH * HD)
    pad4 = jnp.zeros((NB, 8 - TS, KVH * HD), f32)
    kc = cache_k[0].reshape(NB, WIN, KVH * HD)
    vc = cache_v[0].reshape(NB, WIN, KVH * HD)
    sink_col = jnp.repeat(sinks[0], TS)[:, None]
    o_s, kc_new, vc_new = _attn_sample(qbd, kc, vc, jnp.concatenate([k_new, pad4], axis=1),
                                       jnp.concatenate([v_new, pad4], axis=1), sink_col)
    o_s = o_s.reshape(NB, KVH, G, TS, KVH, HD)
    o_s = jnp.stack([o_s[:, 0, :, :, 0], o_s[:, 1, :, :, 1]], axis=1)
    o_s = o_s.transpose(0, 3, 1, 2, 4).reshape(NS, H * HD).astype(bf16)
    x = _attn_prompt(qT, ktok, vT, sink_rows, xp, xs, o_s, w_o[0].astype(bf16))
    x = _moe_layer(x, 0, False, *moe_w)

    new_k_prompt = ktok[NP - WIN:NP].reshape(1, 1, WIN, KVH, HD)
    new_v_prompt = vtok[NP - WIN:NP].reshape(1, 1, WIN, KVH, HD)
    new_k_sample = kc_new.reshape(1, NB, WIN, KVH, HD)
    new_v_sample = vc_new.reshape(1, NB, WIN, KVH, HD)

    st = state_conv[0]
    z = jnp.zeros((NB, 1, D), f32)
    p1 = jnp.concatenate([st[:, 1:2], z, z, z], axis=1).reshape(NS, D)
    p2 = jnp.concatenate([st[:, 0:1], st[:, 1:2], z, z], axis=1).reshape(NS, D)
    x, cu = _conv(x, norm_mix[1][None, :], w_in[0].astype(bf16), conv_w[0],
                  w_out[0].astype(bf16), p1, p2)
    y_prompt, y_sample = _moe_layer(x, 1, True, *moe_w)

    new_conv_prompt = cu[T - 2:T].reshape(1, 1, 2, D)
    new_conv_sample = cu[T:].reshape(NB, TS, D)[:, TS - 2:][None]

    y_prompt = y_prompt.reshape(1, NP, D)
    y_sample = y_sample.reshape(NB, TS, D)
    return (y_prompt, y_sample, new_k_prompt, new_v_prompt, new_conv_prompt,
            new_k_sample, new_v_sample, new_conv_sample)
```

```python
import jax
import jax.numpy as jnp
import numpy as np
from jax import lax
from jax.experimental import pallas as pl
from jax.experimental.pallas import tpu as pltpu

D = 1024
NP = 16384
NB = 128
TS = 4
NS = NB * TS
N = NP + NS
PAST = 16384
H = 16
KVH = 2
G = H // KVH
HD = 64
HALF = HD // 2
QKV = (H + 2 * KVH) * HD
WIN = 128
THETA = 10000.0
NGRP = 4
EPG = 4
NE = NGRP * EPG
TOPK = 2
ROW_E = 8
F = 512
EPS = 1e-6
SCALE = HD ** -0.5

T = 512
NT = N // T
TQ = 128
NQ = NP // TQ
QB = 4
BB = 8
LANES = 128
VMEM_LIMIT = 50 * 1024 * 1024

TM = 256
NTM = N // TM
DT = 3
UNIT = 16
RC = TOPK * TM + NE * UNIT
UPT = RC // UNIT
XW = D + LANES
CH = 512
CHB = 256
UPC = CH // UNIT
NSLOT = 3
NCH = -(-(NTM * (UPT - 1) + NE * (UPC - 1)) // UPC) + NSLOT - 1
ZERO_UNIT_IN = UPT - 1
ZERO_UNIT_OUT = (NCH - 1) * UPC

f32 = jnp.float32
bf16 = jnp.bfloat16


def _params(*sem):
    return pltpu.CompilerParams(dimension_semantics=sem, vmem_limit_bytes=VMEM_LIMIT)


def _rms(x, g):
    ms = jnp.mean(x * x, axis=-1, keepdims=True)
    return x * lax.rsqrt(ms + EPS) * g


def _pick(i, prompt_ref, sample_ref):
    return jnp.where(i == NT - 1, sample_ref[...], prompt_ref[...])


def _stream_specs():
    return [pl.BlockSpec((T, D), lambda i: (jnp.minimum(i, NT - 2), 0)),
            pl.BlockSpec((T, D), lambda i: (0, 0))]


def _qkv_kernel(xp_ref, xs_ref, g_ref, wT_ref, qn_ref, kn_ref, cos_ref, sin_ref,
                qT_ref, ktok_ref, vtok_ref, vT_ref):
    h = _rms(_pick(pl.program_id(0), xp_ref, xs_ref), g_ref[...]).astype(bf16)
    qkvT = lax.dot_general(wT_ref[...], h, (((1,), (1,)), ((), ())),
                           preferred_element_type=f32)
    cos = cos_ref[...]
    sin = sin_ref[...]

    def norm_rope(blk, gcol):
        ms = jnp.mean(blk * blk, axis=0, keepdims=True)
        y = blk * lax.rsqrt(ms + EPS) * gcol
        y1 = y[:HALF]
        y2 = y[HALF:]
        return y1 * cos - y2 * sin, y2 * cos + y1 * sin

    qn = qn_ref[...]
    for hd in range(H):
        o1, o2 = norm_rope(qkvT[hd * HD:(hd + 1) * HD], qn)
        qT_ref[hd * HD:hd * HD + HALF, :] = (o1 * SCALE).astype(bf16)
        qT_ref[hd * HD + HALF:(hd + 1) * HD, :] = (o2 * SCALE).astype(bf16)
    kn = kn_ref[...]
    ks = []
    for j in range(KVH):
        o1, o2 = norm_rope(qkvT[H * HD + j * HD:H * HD + (j + 1) * HD], kn)
        ks += [o1, o2]
    kT = jnp.concatenate(ks, axis=0)
    ktok_ref[...] = kT.T
    vT = qkvT[(H + KVH) * HD:]
    vtok_ref[...] = vT.T
    vT_ref[...] = vT.astype(bf16)


def _qkv(xp, xs, g, wT, qn, kn, cos, sin):
    return pl.pallas_call(
        _qkv_kernel,
        grid=(NT,),
        in_specs=_stream_specs() + [
            pl.BlockSpec((1, D), lambda i: (0, 0)),
            pl.BlockSpec((QKV, D), lambda i: (0, 0)),
            pl.BlockSpec((HD, 1), lambda i: (0, 0)),
            pl.BlockSpec((HD, 1), lambda i: (0, 0)),
            pl.BlockSpec((HALF, T), lambda i: (0, i)),
            pl.BlockSpec((HALF, T), lambda i: (0, i)),
        ],
        out_specs=[
            pl.BlockSpec((H * HD, T), lambda i: (0, i)),
            pl.BlockSpec((T, KVH * HD), lambda i: (i, 0)),
            pl.BlockSpec((T, KVH * HD), lambda i: (i, 0)),
            pl.BlockSpec((KVH * HD, T), lambda i: (0, i)),
        ],
        out_shape=[
            jax.ShapeDtypeStruct((H * HD, N), bf16),
            jax.ShapeDtypeStruct((N, KVH * HD), f32),
            jax.ShapeDtypeStruct((N, KVH * HD), f32),
            jax.ShapeDtypeStruct((KVH * HD, N), bf16),
        ],
        compiler_params=_params("parallel"),
        name="qkv",
    )(xp, xs, g, wT, qn, kn, cos, sin)


def _attn_prompt_kernel(qT_ref, kp_ref, kc_ref, vp_ref, vc_ref, sink_ref, bias0_ref, bias_ref,
                        xp_ref, xs_ref, os_ref, wo_ref, y_ref):
    j = pl.program_id(0)

    @pl.when(j >= NQ // QB)
    def _():
        y_ref[...] = xs_ref[...] + jnp.dot(os_ref[...], wo_ref[...],
                                           preferred_element_type=f32)

    @pl.when(j < NQ // QB)
    def _():
        k_all = jnp.concatenate([kp_ref[...], kc_ref[...]], axis=0)
        v_all = jnp.concatenate([vp_ref[...], vc_ref[...]], axis=1)
        blocks = []
        for k in range(QB):
            rows = slice(k * TQ, (k + 1) * TQ)
            blocks.append(_attend_block(
                qT_ref.at[:, rows], k_all[k * TQ:(k + 2) * TQ], v_all[:, k * TQ:(k + 2) * TQ],
                sink_ref, bias0_ref if k == 0 else bias_ref, xp_ref.at[rows], wo_ref,
                y_ref.at[rows]))
        for _ in range(3):
            for blk in blocks:
                next(blk, None)


def _attend_block(qT_ref, kk, vv, sink_ref, bias_ref, x_ref, wo_ref, y_ref):
    kk = kk.astype(bf16)
    bias = jnp.concatenate([bias_ref[...]] * H, axis=1)
    qg = [jnp.concatenate([qT_ref[(g * G + hh) * HD:(g * G + hh + 1) * HD, :]
                           for hh in range(G)], axis=1) for g in range(KVH)]
    zeros = jnp.zeros_like(qg[0])
    rhs = jnp.concatenate([jnp.concatenate([qg[0], zeros], axis=1),
                           jnp.concatenate([zeros, qg[1]], axis=1)], axis=0)
    sT = jnp.dot(kk, rhs, preferred_element_type=f32) + bias
    yield
    sink = jnp.concatenate([sink_ref[0], sink_ref[1]], axis=1)
    m = jnp.maximum(jnp.max(sT, axis=0, keepdims=True), sink)
    p = jnp.exp(sT - m)
    l = jnp.sum(p, axis=0, keepdims=True) + jnp.exp(sink - m)
    p = (p * (1.0 / l)).astype(bf16)
    yield
    pieces = []
    for g in range(KVH):
        oT = jnp.dot(vv[g * HD:(g + 1) * HD, :], p[:, g * G * TQ:(g + 1) * G * TQ],
                     preferred_element_type=f32)
        pieces += [oT[:, hh * TQ:(hh + 1) * TQ] for hh in range(G)]
    oT_all = jnp.concatenate(pieces, axis=0)
    y_ref[...] = x_ref[...] + jnp.dot(oT_all.T.astype(bf16), wo_ref[...],
                                      preferred_element_type=f32)


def _band_bias():
    s = np.arange(2 * TQ)[:, None]
    t = np.arange(TQ)[None, :]
    dist = t + TQ - s
    band = (dist >= 0) & (dist <= WIN)
    first = band & (s >= TQ)
    return jnp.asarray(np.where(np.stack([first, band]), 0.0, -np.inf), f32)


def _attn_prompt(qT, ktok, vT, sink_rows, xp, xs, o_s, wo):
    steps = NQ // QB
    rows = QB * TQ

    def cur(j):
        return jnp.minimum(j, steps - 1)

    def prev(j):
        return jnp.maximum(cur(j) * QB - 1, 0)

    def sample(j):
        return jnp.maximum(j - steps, 0)

    bias = _band_bias()
    return pl.pallas_call(
        _attn_prompt_kernel,
        grid=(N // rows,),
        in_specs=[
            pl.BlockSpec((H * HD, rows), lambda j: (0, cur(j))),
            pl.BlockSpec((TQ, KVH * HD), lambda j: (prev(j), 0)),
            pl.BlockSpec((rows, KVH * HD), lambda j: (cur(j), 0)),
            pl.BlockSpec((KVH * HD, TQ), lambda j: (0, prev(j))),
            pl.BlockSpec((KVH * HD, rows), lambda j: (0, cur(j))),
            pl.BlockSpec((KVH, 1, G * TQ), lambda j: (0, 0, 0)),
            pl.BlockSpec((None, 2 * TQ, TQ), lambda j: (jnp.minimum(j, 1), 0, 0)),
            pl.BlockSpec((None, 2 * TQ, TQ), lambda j: (1, 0, 0)),
            pl.BlockSpec((rows, D), lambda j: (cur(j), 0)),
            pl.BlockSpec((rows, D), lambda j: (sample(j), 0)),
            pl.BlockSpec((rows, H * HD), lambda j: (sample(j), 0)),
            pl.BlockSpec((D, D), lambda j: (0, 0)),
        ],
        out_specs=pl.BlockSpec((rows, D), lambda j: (j, 0)),
        out_shape=jax.ShapeDtypeStruct((N, D), f32),
        compiler_params=_params("parallel"),
        name="attn_prompt",
    )(qT, ktok, ktok, vT, vT, sink_rows, bias, bias, xp, xs, o_s, wo)


def _attn_sample_kernel(q_ref, kc_ref, vc_ref, kn_ref, vn_ref, sink_ref,
                        o_ref, knew_ref, vnew_ref):
    rows = H * TS
    t1 = lax.broadcasted_iota(jnp.int32, (1, rows, WIN), 1) & (TS - 1)
    s1 = lax.broadcasted_iota(jnp.int32, (1, rows, WIN), 2)
    valid1 = s1 >= t1
    t2 = lax.broadcasted_iota(jnp.int32, (1, rows, 8), 1) & (TS - 1)
    s2 = lax.broadcasted_iota(jnp.int32, (1, rows, 8), 2)
    valid2 = s2 <= t2
    sink = sink_ref[...][None]
    q = q_ref[...]
    kc = kc_ref[...]
    vc = vc_ref[...]
    kn = kn_ref[...]
    vn = vn_ref[...]
    sc = jnp.einsum('bqd,bkd->bqk', q, kc.astype(bf16), preferred_element_type=f32)
    sn = jnp.einsum('bqd,bkd->bqk', q, kn.astype(bf16), preferred_element_type=f32)
    sc = jnp.where(valid1, sc, -jnp.inf)
    sn = jnp.where(valid2, sn, -jnp.inf)
    m = jnp.maximum(jnp.maximum(jnp.max(sc, axis=-1, keepdims=True),
                                jnp.max(sn, axis=-1, keepdims=True)), sink)
    pc = jnp.exp(sc - m)
    pn = jnp.exp(sn - m)
    l = (jnp.sum(pc, axis=-1, keepdims=True) + jnp.sum(pn, axis=-1, keepdims=True)
         + jnp.exp(sink - m))
    r = 1.0 / l
    o_ref[...] = (jnp.einsum('bqk,bkd->bqd', (pc * r).astype(bf16), vc.astype(bf16),
                             preferred_element_type=f32)
                  + jnp.einsum('bqk,bkd->bqd', (pn * r).astype(bf16), vn.astype(bf16),
                               preferred_element_type=f32))
    knew_ref[:, :WIN - TS, :] = kc[:, TS:, :]
    knew_ref[:, WIN - TS:, :] = kn[:, :TS, :]
    vnew_ref[:, :WIN - TS, :] = vc[:, TS:, :]
    vnew_ref[:, WIN - TS:, :] = vn[:, :TS, :]


def _attn_sample(qbd, kc, vc, kn, vn, sink_col):
    rows = H * TS
    cache_spec = pl.BlockSpec((BB, WIN, KVH * HD), lambda i: (i, 0, 0))
    new_spec = pl.BlockSpec((BB, 8, KVH * HD), lambda i: (i, 0, 0))
    cache_shape = jax.ShapeDtypeStruct((NB, WIN, KVH * HD), f32)
    return pl.pallas_call(
        _attn_sample_kernel,
        grid=(NB // BB,),
        in_specs=[
            pl.BlockSpec((BB, rows, KVH * HD), lambda i: (i, 0, 0)),
            cache_spec, cache_spec, new_spec, new_spec,
            pl.BlockSpec((rows, 1), lambda i: (0, 0)),
        ],
        out_specs=[pl.BlockSpec((BB, rows, KVH * HD), lambda i: (i, 0, 0)),
                   cache_spec, cache_spec],
        out_shape=[jax.ShapeDtypeStruct((NB, rows, KVH * HD), f32), cache_shape, cache_shape],
        compiler_params=_params("parallel"),
        name="attn_sample",
    )(qbd, kc, vc, kn, vn, sink_col)


def _dispatch_kernel(x_ref, g_ref, w_ref, b_ref, upper_ref, xc_ref, meta_ref, cnt_ref):
    tiles = [_dispatch_tile(x_ref.at[k * TM:(k + 1) * TM], g_ref, w_ref, b_ref, upper_ref,
                            xc_ref.at[k * RC:(k + 1) * RC], meta_ref.at[k * TM:(k + 1) * TM],
                            cnt_ref.at[k * NE:(k + 1) * NE]) for k in range(DT)]
    for _ in range(3):
        for t in tiles:
            next(t, None)


def _dispatch_tile(x_ref, g_ref, w_ref, b_ref, upper_ref, xc_ref, meta_ref, cnt_ref):
    h_hi = _rms(x_ref[...], g_ref[...]).astype(bf16)
    logits = jnp.dot(h_hi, w_ref[...], preferred_element_type=f32) + b_ref[...]
    yield
    lt = logits.T
    inf = jnp.inf
    row8 = lax.broadcasted_iota(jnp.int32, (8, TM), 0).astype(f32)
    gl = jnp.where(row8 < NGRP, lt[0:8], -inf)
    gmax = jnp.max(gl, axis=0, keepdims=True)
    gsel = jnp.min(jnp.where(gl == gmax, row8, 8.0), axis=0, keepdims=True)
    g_w = 1.0 / jnp.sum(jnp.exp(gl - gmax), axis=0, keepdims=True)
    row = lax.broadcasted_iota(jnp.int32, (NE, TM), 0)
    row_f = row.astype(f32)
    el = jnp.where((row >> 2).astype(f32) == gsel, lt[ROW_E:ROW_E + NE], -inf)
    v1 = jnp.max(el, axis=0, keepdims=True)
    i1 = jnp.min(jnp.where(el == v1, row_f, float(NE)), axis=0, keepdims=True)
    el2 = jnp.where(row_f == i1, -inf, el)
    v2 = jnp.max(el2, axis=0, keepdims=True)
    i2 = jnp.min(jnp.where(el2 == v2, row_f, float(NE)), axis=0, keepdims=True)
    e1 = jnp.exp(v2 - v1)
    den = 1.0 + e1
    w1 = (1.0 / den) * g_w
    w2 = (e1 / den) * g_w

    m1 = row_f == i1
    m2 = row_f == i2
    sel = jnp.where(m1 | m2, 1.0, 0.0)
    ranks = jnp.dot(sel.astype(bf16), upper_ref[...],
                    preferred_element_type=f32)
    counts = jnp.sum(sel, axis=1, keepdims=True)
    padded = jnp.floor((counts + (UNIT - 1.0)) * (1.0 / UNIT)) * UNIT
    e_i = lax.broadcasted_iota(jnp.int32, (NE, NE), 0)
    f_i = lax.broadcasted_iota(jnp.int32, (NE, NE), 1)
    below = jnp.where(f_i < e_i, 1.0, 0.0).astype(bf16)
    seg = jnp.dot(below, jnp.broadcast_to(padded, (NE, LANES)).astype(bf16),
                  preferred_element_type=f32)[:, 0:1]
    posall = seg + ranks
    pos1 = jnp.sum(jnp.where(m1, posall, 0.0), axis=0, keepdims=True)
    pos2 = jnp.sum(jnp.where(m2, posall, 0.0), axis=0, keepdims=True)
    cnt_ref[...] = jnp.broadcast_to(counts, (NE, LANES))

    w1_hi = w1.astype(bf16).astype(f32)
    w2_hi = w2.astype(bf16).astype(f32)
    slab = jnp.zeros((8, TM), f32)
    for k, r in enumerate((pos1, pos2, w1_hi, w1 - w1_hi, w2_hi, w2 - w2_hi, i1)):
        slab = jnp.where(row8 == k, r, slab)
    meta = jnp.concatenate([slab, jnp.zeros((LANES - 8, TM), f32)], axis=0).T
    meta_ref[...] = meta
    yield

    rr = lax.broadcasted_iota(jnp.int32, (RC, TM), 0).astype(f32)
    onehot = jnp.where((rr == pos1) | (rr == pos2), 1.0, 0.0).astype(bf16)
    h_aug = jnp.concatenate([h_hi, meta.astype(bf16)], axis=1)
    xc_ref[...] = jnp.dot(onehot, h_aug, preferred_element_type=f32).astype(bf16)


def _dispatch(x, g, w, b):
    upper = jnp.asarray(np.triu(np.ones((TM, TM), np.float32), 1), bf16)
    return pl.pallas_call(
        _dispatch_kernel,
        grid=(NTM // DT,),
        in_specs=[
            pl.BlockSpec((DT * TM, D), lambda i: (i, 0)),
            pl.BlockSpec((1, D), lambda i: (0, 0)),
            pl.BlockSpec((D, LANES), lambda i: (0, 0)),
            pl.BlockSpec((1, LANES), lambda i: (0, 0)),
            pl.BlockSpec((TM, TM), lambda i: (0, 0)),
        ],
        out_specs=[
            pl.BlockSpec((DT * RC, XW), lambda i: (i, 0)),
            pl.BlockSpec((DT * TM, LANES), lambda i: (i, 0)),
            pl.BlockSpec((DT * NE, LANES), lambda i: (i, 0)),
        ],
        out_shape=[
            jax.ShapeDtypeStruct((NTM * RC, XW), bf16),
            jax.ShapeDtypeStruct((N, LANES), f32),
            jax.ShapeDtypeStruct((NTM * NE, LANES), f32),
        ],
        compiler_params=_params("parallel"),
        name="dispatch",
    )(x, g, w, b, upper)


def _dispatch_tables(cnt):
    i32 = jnp.int32
    n = cnt.reshape(NTM, NE, LANES)[:, :, 0].astype(i32)
    units = (n + UNIT - 1) // UNIT
    seg_end = jnp.cumsum(units, axis=1)
    seg_start = seg_end - units
    col_end = jnp.cumsum(units, axis=0)
    col_start = col_end - units
    chunks = (col_end[-1] + UPC - 1) // UPC
    ch_end = jnp.cumsum(chunks)
    ch_start = ch_end - chunks
    nused = ch_end[-1]
    c = jnp.arange(NCH, dtype=i32)
    eid = jnp.minimum(jnp.sum((ch_end[None, :] <= c[:, None]).astype(i32), axis=1), NE - 1)

    src0 = jnp.arange(NTM, dtype=i32)[:, None] * UPT + seg_start
    dst0 = ch_start[None, :] * UPC + col_start
    k = jnp.arange((NCH + NSLOT - 1) * UPC, dtype=i32)[:, None, None]
    inside = (k >= dst0[None]) & (k < (dst0 + units)[None])
    found = jnp.sum(inside.astype(i32), axis=(1, 2))
    shift = jnp.sum(jnp.where(inside, (src0 - dst0)[None], 0), axis=(1, 2))
    ffn_src = jnp.where(found > 0, k[:, 0, 0] + shift, ZERO_UNIT_IN)

    v = jnp.arange(UPT, dtype=i32)[None, :, None]
    inside_v = (v >= seg_start[:, None, :]) & (v < seg_end[:, None, :])
    found_v = jnp.sum(inside_v.astype(i32), axis=2)
    shift_v = jnp.sum(jnp.where(inside_v, (dst0 - seg_start)[:, None, :], 0), axis=2)
    comb_src = jnp.where(found_v > 0, v[:, :, 0] + shift_v, ZERO_UNIT_OUT).reshape(-1)
    return eid, nused.reshape(1), ffn_src, comb_src


def _unit_copies(src_ref, base, n_units, src_hbm, stage, slot, sem):
    out = []
    for j in range(n_units):
        row = pl.multiple_of(src_ref[base + j] * UNIT, UNIT)
        out.append(pltpu.make_async_copy(
            src_hbm.at[pl.ds(row, UNIT), :],
            stage.at[slot, pl.ds(j * UNIT, UNIT), :],
            sem.at[slot]))
    return out


def _start_gathers(step, n_steps, n_units, src_ref, src_hbm, stage, sem):
    ahead = NSLOT - 1

    def start(s, slot):
        for cp in _unit_copies(src_ref, s * n_units, n_units, src_hbm, stage, slot, sem):
            cp.start()

    for s in range(ahead):
        @pl.when((step == 0) & (s < n_steps))
        def _():
            start(s, s)

    @pl.when(step + ahead < n_steps)
    def _():
        start(step + ahead, lax.rem(step + ahead, NSLOT))


def _wait_gather(step, n_units, src_ref, src_hbm, stage, sem):
    slot = lax.rem(step, NSLOT)
    for cp in _unit_copies(src_ref, step * n_units, n_units, src_hbm, stage, slot, sem):
        cp.wait()
    return slot


def _ffn_kernel(eid_ref, nused_ref, src_ref, xc_hbm, wg_ref, wu_ref, wd_ref, o_ref,
                stage, sem, wg_b, wu_b, wd_b):
    c = pl.program_id(0)
    nused = nused_ref[0]
    ahead = NSLOT - 1

    def start(s):
        for cp in _unit_copies(src_ref, s * UPC, UPC, xc_hbm, stage, lax.rem(s, NSLOT), sem):
            cp.start()

    @pl.when(c == 0)
    def _():
        for s in range(ahead):
            start(s)

    @pl.when(c < nused)
    def _():
        slot = _wait_gather(c, UPC, src_ref, xc_hbm, stage, sem)

        @pl.when((c == 0) | (eid_ref[c] != eid_ref[jnp.maximum(c - 1, 0)]))
        def _():
            wg_b[...] = wg_ref[...].astype(bf16)
            wu_b[...] = wu_ref[...].astype(bf16)
            wd_b[...] = wd_ref[...].astype(bf16)

        for r in range(0, CH, CHB):
            xs = stage[slot, r:r + CHB, :]
            x = xs[:, :D]
            gb = xs[:, D:].astype(f32)
            first = gb[:, 6:7] == eid_ref[c].astype(f32)
            gate = jnp.where(first, gb[:, 2:3] + gb[:, 3:4], gb[:, 4:5] + gb[:, 5:6])
            a = jnp.dot(x, wg_b[...], preferred_element_type=f32)
            u = jnp.dot(x, wu_b[...], preferred_element_type=f32)
            act = (a * (1.0 / (1.0 + jnp.exp(-a)))) * u * gate
            o_ref[r:r + CHB, :] = jnp.dot(act.astype(bf16), wd_b[...],
                                          preferred_element_type=f32).astype(bf16)
        start(c + ahead)

    @pl.when(c >= nused)
    def _():
        @pl.when(c < nused + ahead)
        def _():
            _wait_gather(c, UPC, src_ref, xc_hbm, stage, sem)

        o_ref[...] = jnp.zeros((CH, D), bf16)


def _ffn(layer, eid, nused, ffn_src, xc, wg, wu, wd):
    return pl.pallas_call(
        _ffn_kernel,
        grid_spec=pltpu.PrefetchScalarGridSpec(
            num_scalar_prefetch=3,
            grid=(NCH,),
            in_specs=[
                pl.BlockSpec(memory_space=pl.ANY),
                pl.BlockSpec((None, None, D, F), lambda c, eid, nu, src: (layer, eid[c], 0, 0)),
                pl.BlockSpec((None, None, D, F), lambda c, eid, nu, src: (layer, eid[c], 0, 0)),
                pl.BlockSpec((None, None, F, D), lambda c, eid, nu, src: (layer, eid[c], 0, 0)),
            ],
            out_specs=pl.BlockSpec((CH, D), lambda c, eid, nu, src: (c, 0)),
            scratch_shapes=[
                pltpu.VMEM((NSLOT, CH, XW), bf16),
                pltpu.SemaphoreType.DMA((NSLOT,)),
                pltpu.VMEM((D, F), bf16),
                pltpu.VMEM((D, F), bf16),
                pltpu.VMEM((F, D), bf16),
            ],
        ),
        out_shape=jax.ShapeDtypeStruct((NCH * CH, D), bf16),
        compiler_params=_params("arbitrary"),
        name="ffn",
    )(eid, nused, ffn_src, xc, wg, wu, wd)


def _combine_kernel(src_ref, x_ref, meta_ref, o_hbm, y_refs, stage, sem):
    i = pl.program_id(0)
    _start_gathers(i, NTM, UPT, src_ref, o_hbm, stage, sem)
    slot = _wait_gather(i, UPT, src_ref, o_hbm, stage, sem)

    meta = meta_ref[...]
    pos1 = meta[:, 0:1]
    pos2 = meta[:, 1:2]
    li = lax.broadcasted_iota(jnp.int32, (TM, RC), 1).astype(f32)
    pt = jnp.where((li == pos1) | (li == pos2), 1.0, 0.0).astype(bf16)
    y = x_ref[...] + jnp.dot(pt, stage[slot], preferred_element_type=f32)
    if len(y_refs) == 1:
        y_refs[0][...] = y
    else:
        @pl.when(i < NP // TM)
        def _():
            y_refs[0][...] = y

        @pl.when(i >= NP // TM)
        def _():
            y_refs[1][...] = y


def _combine_kernel_1(src_ref, x_ref, meta_ref, o_hbm, y_ref, stage, sem):
    _combine_kernel(src_ref, x_ref, meta_ref, o_hbm, (y_ref,), stage, sem)


def _combine_kernel_2(src_ref, x_ref, meta_ref, o_hbm, yp_ref, ys_ref, stage, sem):
    _combine_kernel(src_ref, x_ref, meta_ref, o_hbm, (yp_ref, ys_ref), stage, sem)


def _combine(comb_src, x, meta, o_sorted, split):
    npt = NP // TM
    if split:
        body = _combine_kernel_2
        out_specs = [pl.BlockSpec((TM, D), lambda i, src: (jnp.minimum(i, npt - 1), 0)),
                     pl.BlockSpec((TM, D), lambda i, src: (jnp.maximum(i - npt, 0), 0))]
        out_shape = [jax.ShapeDtypeStruct((NP, D), f32), jax.ShapeDtypeStruct((NS, D), f32)]
    else:
        body = _combine_kernel_1
        out_specs = pl.BlockSpec((TM, D), lambda i, src: (i, 0))
        out_shape = jax.ShapeDtypeStruct((N, D), f32)
    return pl.pallas_call(
        body,
        grid_spec=pltpu.PrefetchScalarGridSpec(
            num_scalar_prefetch=1,
            grid=(NTM,),
            in_specs=[
                pl.BlockSpec((TM, D), lambda i, src: (i, 0)),
                pl.BlockSpec((TM, LANES), lambda i, src: (i, 0)),
                pl.BlockSpec(memory_space=pl.ANY),
            ],
            out_specs=out_specs,
            scratch_shapes=[
                pltpu.VMEM((NSLOT, RC, D), bf16),
                pltpu.SemaphoreType.DMA((NSLOT,)),
            ],
        ),
        out_shape=out_shape,
        compiler_params=_params("arbitrary"),
        name="combine",
    )(comb_src, x, meta, o_sorted)


def _conv_kernel(x_ref, g_ref, win_ref, cw_ref, wout_ref, p1_ref, p2_ref,
                 y_ref, cu_ref, pad_ref):
    i = pl.program_id(0)

    @pl.when(i == 0)
    def _():
        pad_ref[0:8, :] = jnp.zeros((8, D), f32)

    blocks = [(r, r + T // 2) for r in (0, T // 2)]
    gates = []
    for r0, r1 in blocks:
        h = _rms(x_ref[r0:r1, :], g_ref[...]).astype(bf16)
        bcu = jnp.dot(h, win_ref[...], preferred_element_type=f32)
        gates.append(bcu[:, :D])
        cu = bcu[:, D:2 * D] * bcu[:, 2 * D:]
        pad_ref[8 + r0:8 + r1, :] = cu
        cu_ref[r0:r1, :] = cu
    t = lax.broadcasted_iota(jnp.int32, (T // 2, 1), 0) & (TS - 1)
    is_sample = i == NT - 1
    cw = cw_ref[...]
    for (r0, r1), b in zip(blocks, gates):
        m1 = jnp.where(is_sample & (t == 0), p1_ref[r0:r1, :], pad_ref[7 + r0:7 + r1, :])
        m2 = jnp.where(is_sample & (t < 2), p2_ref[r0:r1, :], pad_ref[6 + r0:6 + r1, :])
        conv = cw[0:1] * m2 + cw[1:2] * m1 + cw[2:3] * pad_ref[8 + r0:8 + r1, :]
        y = jnp.dot((b * conv).astype(bf16), wout_ref[...], preferred_element_type=f32)
        y_ref[r0:r1, :] = x_ref[r0:r1, :] + y
    pad_ref[0:8, :] = pad_ref[T:T + 8, :]


def _conv(x, g, win, cw, wout, p1, p2):
    return pl.pallas_call(
        _conv_kernel,
        grid=(NT,),
        in_specs=[
            pl.BlockSpec((T, D), lambda i: (i, 0)),
            pl.BlockSpec((1, D), lambda i: (0, 0)),
            pl.BlockSpec((D, 3 * D), lambda i: (0, 0)),
            pl.BlockSpec((3, D), lambda i: (0, 0)),
            pl.BlockSpec((D, D), lambda i: (0, 0)),
            pl.BlockSpec((T, D), lambda i: (0, 0)),
            pl.BlockSpec((T, D), lambda i: (0, 0)),
        ],
        out_specs=[
            pl.BlockSpec((T, D), lambda i: (i, 0)),
            pl.BlockSpec((T, D), lambda i: (jnp.where(i == NT - 1, 1, 0), 0)),
        ],
        out_shape=[
            jax.ShapeDtypeStruct((N, D), f32),
            jax.ShapeDtypeStruct((2 * T, D), f32),
        ],
        scratch_shapes=[pltpu.VMEM((T + 8, D), f32)],
        compiler_params=_params("arbitrary"),
        name="conv",
    )(x, g, win, cw, wout, p1, p2)


def _rope_tables():
    inv_freq = THETA ** (-jnp.arange(HALF, dtype=f32) / HALF)
    pos = jnp.concatenate([
        jnp.arange(NP, dtype=jnp.int32),
        PAST + jnp.tile(jnp.arange(TS, dtype=jnp.int32), NB),
    ]).astype(f32)
    ang = inv_freq[:, None] * pos[None, :]
    return jnp.cos(ang), jnp.sin(ang)


def _router_weights(w_group, b_group, w_router, b_router):
    gap = ROW_E - NGRP
    pad = LANES - ROW_E - NE
    w = jnp.concatenate([w_group, jnp.zeros((D, gap), f32), w_router,
                         jnp.zeros((D, pad), f32)], axis=1)
    b = jnp.concatenate([b_group, jnp.zeros((gap,), f32), b_router,
                         jnp.zeros((pad,), f32)])[None, :]
    return w.astype(bf16), b


def _moe_layer(x, i, split, norm_ffn, w_group, b_group, w_router, b_router, w_gate, w_up, w_down):
    w, b = _router_weights(w_group[i], b_group[i], w_router[i], b_router[i])
    xc, meta, cnt = _dispatch(x, norm_ffn[i][None, :], w, b)
    eid, nused, ffn_src, comb_src = _dispatch_tables(cnt)
    o_sorted = _ffn(i, eid, nused, ffn_src, xc, w_gate, w_up, w_down)
    return _combine(comb_src, x, meta, o_sorted, split)


def kernel(x_prompt, x_sample, cache_k, cache_v, state_conv, norm_mix, w_qkv, q_norm, k_norm,
           sinks, w_o, w_in, conv_w, w_out, norm_ffn, w_group, b_group, w_router, b_router,
           w_gate, w_up, w_down):
    xp = x_prompt.reshape(NP, D)
    xs = x_sample.reshape(NS, D)
    moe_w = (norm_ffn, w_group, b_group, w_router, b_router, w_gate, w_up, w_down)

    cos, sin = _rope_tables()
    qT, ktok, vtok, vT = _qkv(xp, xs, norm_mix[0][None, :], w_qkv[0].T.astype(bf16),
                              q_norm[0][:, None], k_norm[0][:, None], cos, sin)
    sink_rows = jnp.repeat(sinks[0].reshape(KVH, G), TQ, axis=1)[:, None, :]

    qs = qT[:, NP:].reshape(KVH, G, HD, NB, TS).transpose(3, 0, 1, 4, 2)
    zq = jnp.zeros_like(qs[:, 0])
    qbd = jnp.stack([jnp.concatenate([qs[:, 0], zq], axis=-1),
                     jnp.concatenate([zq, qs[:, 1]], axis=-1)], axis=1)
    qbd = qbd.reshape(NB, H * TS, KVH * HD)
    k_new = ktok[NP:].reshape(NB, TS, KVH * HD)
    v_new = vtok[NP:].reshape(NB, TS, KVH * HD)
    pad4 = jnp.zeros((NB, 8 - TS, KVH * HD), f32)
    kc = cache_k[0].reshape(NB, WIN, KVH * HD)
    vc = cache_v[0].reshape(NB, WIN, KVH * HD)
    sink_col = jnp.repeat(sinks[0], TS)[:, None]
    o_s, kc_new, vc_new = _attn_sample(qbd, kc, vc, jnp.concatenate([k_new, pad4], axis=1),
                                       jnp.concatenate([v_new, pad4], axis=1), sink_col)
    o_s = o_s.reshape(NB, KVH, G, TS, KVH, HD)
    o_s = jnp.stack([o_s[:, 0, :, :, 0], o_s[:, 1, :, :, 1]], axis=1)
    o_s = o_s.transpose(0, 3, 1, 2, 4).reshape(NS, H * HD).astype(bf16)
    x = _attn_prompt(qT, ktok, vT, sink_rows, xp, xs, o_s, w_o[0].astype(bf16))
    x = _moe_layer(x, 0, False, *moe_w)

    new_k_prompt = ktok[NP - WIN:NP].reshape(1, 1, WIN, KVH, HD)
    new_v_prompt = vtok[NP - WIN:NP].reshape(1, 1, WIN, KVH, HD)
    new_k_sample = kc_new.reshape(1, NB, WIN, KVH, HD)
    new_v_sample = vc_new.reshape(1, NB, WIN, KVH, HD)

    st = state_conv[0]
    z = jnp.zeros((NB, 1, D), f32)
    p1 = jnp.concatenate([st[:, 1:2], z, z, z], axis=1).reshape(NS, D)
    p2 = jnp.concatenate([st[:, 0:1], st[:, 1:2], z, z], axis=1).reshape(NS, D)
    x, cu = _conv(x, norm_mix[1][None, :], w_in[0].astype(bf16), conv_w[0],
                  w_out[0].astype(bf16), p1, p2)
    y_prompt, y_sample = _moe_layer(x, 1, True, *moe_w)

    new_conv_prompt = cu[T - 2:T].reshape(1, 1, 2, D)
    new_conv_sample = cu[T:].reshape(NB, TS, D)[:, TS - 2:][None]

    y_prompt = y_prompt.reshape(1, NP, D)
    y_sample = y_sample.reshape(NB, TS, D)
    return (y_prompt, y_sample, new_k_prompt, new_v_prompt, new_conv_prompt,
            new_k_sample, new_v_sample, new_conv_sample)
```

```python
import jax
import jax.numpy as jnp
import numpy as np
from jax import lax
from jax.experimental import pallas as pl
from jax.experimental.pallas import tpu as pltpu

D = 1024
NP = 16384
NB = 128
TS = 4
NS = NB * TS
N = NP + NS
PAST = 16384
H = 16
KVH = 2
G = H // KVH
HD = 64
HALF = HD // 2
QKV = (H + 2 * KVH) * HD
WIN = 128
THETA = 10000.0
NGRP = 4
EPG = 4
NE = NGRP * EPG
TOPK = 2
ROW_E = 8
F = 512
EPS = 1e-6
SCALE = HD ** -0.5

T = 512
NT = N // T
TQ = 128
NQ = NP // TQ
QB = 4
BB = 8
LANES = 128
VMEM_LIMIT = 50 * 1024 * 1024

TM = 256
NTM = N // TM
DT = 3
UNIT = 16
RC = TOPK * TM + NE * UNIT
UPT = RC // UNIT
XW = D + LANES
CH = 512
CHB = 256
UPC = CH // UNIT
NSLOT = 3
NCH = -(-(NTM * (UPT - 1) + NE * (UPC - 1)) // UPC) + NSLOT - 1
ZERO_UNIT_IN = UPT - 1
ZERO_UNIT_OUT = (NCH - 1) * UPC

f32 = jnp.float32
bf16 = jnp.bfloat16


def _params(*sem):
    return pltpu.CompilerParams(dimension_semantics=sem, vmem_limit_bytes=VMEM_LIMIT)


def _rms(x, g):
    ms = jnp.mean(x * x, axis=-1, keepdims=True)
    return x * lax.rsqrt(ms + EPS) * g


def _pick(i, prompt_ref, sample_ref):
    return jnp.where(i == NT - 1, sample_ref[...], prompt_ref[...])


def _stream_specs():
    return [pl.BlockSpec((T, D), lambda i: (jnp.minimum(i, NT - 2), 0)),
            pl.BlockSpec((T, D), lambda i: (0, 0))]


def _qkv_kernel(xp_ref, xs_ref, g_ref, wT_ref, qn_ref, kn_ref, cos_ref, sin_ref,
                qT_ref, ktok_ref, vtok_ref, vT_ref):
    h = _rms(_pick(pl.program_id(0), xp_ref, xs_ref), g_ref[...]).astype(bf16)
    qkvT = lax.dot_general(wT_ref[...], h, (((1,), (1,)), ((), ())),
                           preferred_element_type=f32)
    cos = cos_ref[...]
    sin = sin_ref[...]

    def norm_rope(blk, gcol):
        ms = jnp.mean(blk * blk, axis=0, keepdims=True)
        y = blk * lax.rsqrt(ms + EPS) * gcol
        y1 = y[:HALF]
        y2 = y[HALF:]
        return y1 * cos - y2 * sin, y2 * cos + y1 * sin

    qn = qn_ref[...]
    for hd in range(H):
        o1, o2 = norm_rope(qkvT[hd * HD:(hd + 1) * HD], qn)
        qT_ref[hd * HD:hd * HD + HALF, :] = (o1 * SCALE).astype(bf16)
        qT_ref[hd * HD + HALF:(hd + 1) * HD, :] = (o2 * SCALE).astype(bf16)
    kn = kn_ref[...]
    ks = []
    for j in range(KVH):
        o1, o2 = norm_rope(qkvT[H * HD + j * HD:H * HD + (j + 1) * HD], kn)
        ks += [o1, o2]
    kT = jnp.concatenate(ks, axis=0)
    ktok_ref[...] = kT.T
    vT = qkvT[(H + KVH) * HD:]
    vtok_ref[...] = vT.T
    vT_ref[...] = vT.astype(bf16)


def _qkv(xp, xs, g, wT, qn, kn, cos, sin):
    return pl.pallas_call(
        _qkv_kernel,
        grid=(NT,),
        in_specs=_stream_specs() + [
            pl.BlockSpec((1, D), lambda i: (0, 0)),
            pl.BlockSpec((QKV, D), lambda i: (0, 0)),
            pl.BlockSpec((HD, 1), lambda i: (0, 0)),
            pl.BlockSpec((HD, 1), lambda i: (0, 0)),
            pl.BlockSpec((HALF, T), lambda i: (0, i)),
            pl.BlockSpec((HALF, T), lambda i: (0, i)),
        ],
        out_specs=[
            pl.BlockSpec((H * HD, T), lambda i: (0, i)),
            pl.BlockSpec((T, KVH * HD), lambda i: (i, 0)),
            pl.BlockSpec((T, KVH * HD), lambda i: (i, 0)),
            pl.BlockSpec((KVH * HD, T), lambda i: (0, i)),
        ],
        out_shape=[
            jax.ShapeDtypeStruct((H * HD, N), bf16),
            jax.ShapeDtypeStruct((N, KVH * HD), f32),
            jax.ShapeDtypeStruct((N, KVH * HD), f32),
            jax.ShapeDtypeStruct((KVH * HD, N), bf16),
        ],
        compiler_params=_params("parallel"),
        name="qkv",
    )(xp, xs, g, wT, qn, kn, cos, sin)


def _attn_prompt_kernel(qT_ref, kp_ref, kc_ref, vp_ref, vc_ref, sink_ref, bias0_ref, bias_ref,
                        xp_ref, xs_ref, os_ref, wo_ref, y_ref):
    j = pl.program_id(0)

    @pl.when(j >= NQ // QB)
    def _():
        y_ref[...] = xs_ref[...] + jnp.dot(os_ref[...], wo_ref[...],
                                           preferred_element_type=f32)

    @pl.when(j < NQ // QB)
    def _():
        k_all = jnp.concatenate([kp_ref[...], kc_ref[...]], axis=0)
        v_all = jnp.concatenate([vp_ref[...], vc_ref[...]], axis=1)
        blocks = []
        for k in range(QB):
            rows = slice(k * TQ, (k + 1) * TQ)
            blocks.append(_attend_block(
                qT_ref.at[:, rows], k_all[k * TQ:(k + 2) * TQ], v_all[:, k * TQ:(k + 2) * TQ],
                sink_ref, bias0_ref if k == 0 else bias_ref, xp_ref.at[rows], wo_ref,
                y_ref.at[rows]))
        for _ in range(3):
            for blk in blocks:
                next(blk, None)


def _attend_block(qT_ref, kk, vv, sink_ref, bias_ref, x_ref, wo_ref, y_ref):
    kk = kk.astype(bf16)
    bias = jnp.concatenate([bias_ref[...]] * H, axis=1)
    qg = [jnp.concatenate([qT_ref[(g * G + hh) * HD:(g * G + hh + 1) * HD, :]
                           for hh in range(G)], axis=1) for g in range(KVH)]
    zeros = jnp.zeros_like(qg[0])
    rhs = jnp.concatenate([jnp.concatenate([qg[0], zeros], axis=1),
                           jnp.concatenate([zeros, qg[1]], axis=1)], axis=0)
    sT = jnp.dot(kk, rhs, preferred_element_type=f32) + bias
    yield
    sink = jnp.concatenate([sink_ref[0], sink_ref[1]], axis=1)
    m = jnp.maximum(jnp.max(sT, axis=0, keepdims=True), sink)
    p = jnp.exp(sT - m)
    l = jnp.sum(p, axis=0, keepdims=True) + jnp.exp(sink - m)
    p = (p * (1.0 / l)).astype(bf16)
    yield
    pieces = []
    for g in range(KVH):
        oT = jnp.dot(vv[g * HD:(g + 1) * HD, :], p[:, g * G * TQ:(g + 1) * G * TQ],
                     preferred_element_type=f32)
        pieces += [oT[:, hh * TQ:(hh + 1) * TQ] for hh in range(G)]
    oT_all = jnp.concatenate(pieces, axis=0)
    y_ref[...] = x_ref[...] + jnp.dot(oT_all.T.astype(bf16), wo_ref[...],
                                      preferred_element_type=f32)


def _band_bias():
    s = np.arange(2 * TQ)[:, None]
    t = np.arange(TQ)[None, :]
    dist = t + TQ - s
    band = (dist >= 0) & (dist <= WIN)
    first = band & (s >= TQ)
    return jnp.asarray(np.where(np.stack([first, band]), 0.0, -np.inf), f32)


def _attn_prompt(qT, ktok, vT, sink_rows, xp, xs, o_s, wo):
    steps = NQ // QB
    rows = QB * TQ

    def cur(j):
        return jnp.minimum(j, steps - 1)

    def prev(j):
        return jnp.maximum(cur(j) * QB - 1, 0)

    def sample(j):
        return jnp.maximum(j - steps, 0)

    bias = _band_bias()
    return pl.pallas_call(
        _attn_prompt_kernel,
        grid=(N // rows,),
        in_specs=[
            pl.BlockSpec((H * HD, rows), lambda j: (0, cur(j))),
            pl.BlockSpec((TQ, KVH * HD), lambda j: (prev(j), 0)),
            pl.BlockSpec((rows, KVH * HD), lambda j: (cur(j), 0)),
            pl.BlockSpec((KVH * HD, TQ), lambda j: (0, prev(j))),
            pl.BlockSpec((KVH * HD, rows), lambda j: (0, cur(j))),
            pl.BlockSpec((KVH, 1, G * TQ), lambda j: (0, 0, 0)),
            pl.BlockSpec((None, 2 * TQ, TQ), lambda j: (jnp.minimum(j, 1), 0, 0)),
            pl.BlockSpec((None, 2 * TQ, TQ), lambda j: (1, 0, 0)),
            pl.BlockSpec((rows, D), lambda j: (cur(j), 0)),
            pl.BlockSpec((rows, D), lambda j: (sample(j), 0)),
            pl.BlockSpec((rows, H * HD), lambda j: (sample(j), 0)),
            pl.BlockSpec((D, D), lambda j: (0, 0)),
        ],
        out_specs=pl.BlockSpec((rows, D), lambda j: (j, 0)),
        out_shape=jax.ShapeDtypeStruct((N, D), f32),
        compiler_params=_params("parallel"),
        name="attn_prompt",
    )(qT, ktok, ktok, vT, vT, sink_rows, bias, bias, xp, xs, o_s, wo)


def _attn_sample_kernel(q_ref, kc_ref, vc_ref, kn_ref, vn_ref, sink_ref,
                        o_ref, knew_ref, vnew_ref):
    rows = H * TS
    t1 = lax.broadcasted_iota(jnp.int32, (1, rows, WIN), 1) & (TS - 1)
    s1 = lax.broadcasted_iota(jnp.int32, (1, rows, WIN), 2)
    valid1 = s1 >= t1
    t2 = lax.broadcasted_iota(jnp.int32, (1, rows, 8), 1) & (TS - 1)
    s2 = lax.broadcasted_iota(jnp.int32, (1, rows, 8), 2)
    valid2 = s2 <= t2
    sink = sink_ref[...][None]
    q = q_ref[...]
    kc = kc_ref[...]
    vc = vc_ref[...]
    kn = kn_ref[...]
    vn = vn_ref[...]
    sc = jnp.einsum('bqd,bkd->bqk', q, kc.astype(bf16), preferred_element_type=f32)
    sn = jnp.einsum('bqd,bkd->bqk', q, kn.astype(bf16), preferred_element_type=f32)
    sc = jnp.where(valid1, sc, -jnp.inf)
    sn = jnp.where(valid2, sn, -jnp.inf)
    m = jnp.maximum(jnp.maximum(jnp.max(sc, axis=-1, keepdims=True),
                                jnp.max(sn, axis=-1, keepdims=True)), sink)
    pc = jnp.exp(sc - m)
    pn = jnp.exp(sn - m)
    l = (jnp.sum(pc, axis=-1, keepdims=True) + jnp.sum(pn, axis=-1, keepdims=True)
         + jnp.exp(sink - m))
    r = 1.0 / l
    o_ref[...] = (jnp.einsum('bqk,bkd->bqd', (pc * r).astype(bf16), vc.astype(bf16),
                             preferred_element_type=f32)
                  + jnp.einsum('bqk,bkd->bqd', (pn * r).astype(bf16), vn.astype(bf16),
                               preferred_element_type=f32))
    knew_ref[:, :WIN - TS, :] = kc[:, TS:, :]
    knew_ref[:, WIN - TS:, :] = kn[:, :TS, :]
    vnew_ref[:, :WIN - TS, :] = vc[:, TS:, :]
    vnew_ref[:, WIN - TS:, :] = vn[:, :TS, :]


def _attn_sample(qbd, kc, vc, kn, vn, sink_col):
    rows = H * TS
    cache_spec = pl.BlockSpec((BB, WIN, KVH * HD), lambda i: (i, 0, 0))
    new_spec = pl.BlockSpec((BB, 8, KVH * HD), lambda i: (i, 0, 0))
    cache_shape = jax.ShapeDtypeStruct((NB, WIN, KVH * HD), f32)
    return pl.pallas_call(
        _attn_sample_kernel,
        grid=(NB // BB,),
        in_specs=[
            pl.BlockSpec((BB, rows, KVH * HD), lambda i: (i, 0, 0)),
            cache_spec, cache_spec, new_spec, new_spec,
            pl.BlockSpec((rows, 1), lambda i: (0, 0)),
        ],
        out_specs=[pl.BlockSpec((BB, rows, KVH * HD), lambda i: (i, 0, 0)),
                   cache_spec, cache_spec],
        out_shape=[jax.ShapeDtypeStruct((NB, rows, KVH * HD), f32), cache_shape, cache_shape],
        compiler_params=_params("parallel"),
        name="attn_sample",
    )(qbd, kc, vc, kn, vn, sink_col)


def _dispatch_kernel(x_ref, g_ref, w_ref, b_ref, upper_ref, xc_ref, meta_ref, cnt_ref):
    tiles = [_dispatch_tile(x_ref.at[k * TM:(k + 1) * TM], g_ref, w_ref, b_ref, upper_ref,
                            xc_ref.at[k * RC:(k + 1) * RC], meta_ref.at[k * TM:(k + 1) * TM],
                            cnt_ref.at[k * NE:(k + 1) * NE]) for k in range(DT)]
    for _ in range(3):
        for t in tiles:
            next(t, None)


def _dispatch_tile(x_ref, g_ref, w_ref, b_ref, upper_ref, xc_ref, meta_ref, cnt_ref):
    h_hi = _rms(x_ref[...], g_ref[...]).astype(bf16)
    logits = jnp.dot(h_hi, w_ref[...], preferred_element_type=f32) + b_ref[...]
    yield
    lt = logits.T
    inf = jnp.inf
    row8 = lax.broadcasted_iota(jnp.int32, (8, TM), 0).astype(f32)
    gl = jnp.where(row8 < NGRP, lt[0:8], -inf)
    gmax = jnp.max(gl, axis=0, keepdims=True)
    gsel = jnp.min(jnp.where(gl == gmax, row8, 8.0), axis=0, keepdims=True)
    g_w = 1.0 / jnp.sum(jnp.exp(gl - gmax), axis=0, keepdims=True)
    row = lax.broadcasted_iota(jnp.int32, (NE, TM), 0)
    row_f = row.astype(f32)
    el = jnp.where((row >> 2).astype(f32) == gsel, lt[ROW_E:ROW_E + NE], -inf)
    v1 = jnp.max(el, axis=0, keepdims=True)
    i1 = jnp.min(jnp.where(el == v1, row_f, float(NE)), axis=0, keepdims=True)
    el2 = jnp.where(row_f == i1, -inf, el)
    v2 = jnp.max(el2, axis=0, keepdims=True)
    i2 = jnp.min(jnp.where(el2 == v2, row_f, float(NE)), axis=0, keepdims=True)
    e1 = jnp.exp(v2 - v1)
    den = 1.0 + e1
    w1 = (1.0 / den) * g_w
    w2 = (e1 / den) * g_w

    m1 = row_f == i1
    m2 = row_f == i2
    sel = jnp.where(m1 | m2, 1.0, 0.0)
    ranks = jnp.dot(sel.astype(bf16), upper_ref[...],
                    preferred_element_type=f32)
    counts = jnp.sum(sel, axis=1, keepdims=True)
    padded = jnp.floor((counts + (UNIT - 1.0)) * (1.0 / UNIT)) * UNIT
    e_i = lax.broadcasted_iota(jnp.int32, (NE, NE), 0)
    f_i = lax.broadcasted_iota(jnp.int32, (NE, NE), 1)
    below = jnp.where(f_i < e_i, 1.0, 0.0).astype(bf16)
    seg = jnp.dot(below, jnp.broadcast_to(padded, (NE, LANES)).astype(bf16),
                  preferred_element_type=f32)[:, 0:1]
    posall = seg + ranks
    pos1 = jnp.sum(jnp.where(m1, posall, 0.0), axis=0, keepdims=True)
    pos2 = jnp.sum(jnp.where(m2, posall, 0.0), axis=0, keepdims=True)
    cnt_ref[...] = jnp.broadcast_to(counts, (NE, LANES))

    w1_hi = w1.astype(bf16).astype(f32)
    w2_hi = w2.astype(bf16).astype(f32)
    slab = jnp.zeros((8, TM), f32)
    for k, r in enumerate((pos1, pos2, w1_hi, w1 - w1_hi, w2_hi, w2 - w2_hi, i1)):
        slab = jnp.where(row8 == k, r, slab)
    meta = jnp.concatenate([slab, jnp.zeros((LANES - 8, TM), f32)], axis=0).T
    meta_ref[...] = meta
    yield

    rr = lax.broadcasted_iota(jnp.int32, (RC, TM), 0).astype(f32)
    onehot = jnp.where((rr == pos1) | (rr == pos2), 1.0, 0.0).astype(bf16)
    h_aug = jnp.concatenate([h_hi, meta.astype(bf16)], axis=1)
    xc_ref[...] = jnp.dot(onehot, h_aug, preferred_element_type=f32).astype(bf16)


def _dispatch(x, g, w, b):
    upper = jnp.asarray(np.triu(np.ones((TM, TM), np.float32), 1), bf16)
    return pl.pallas_call(
        _dispatch_kernel,
        grid=(NTM // DT,),
        in_specs=[
            pl.BlockSpec((DT * TM, D), lambda i: (i, 0)),
            pl.BlockSpec((1, D), lambda i: (0, 0)),
            pl.BlockSpec((D, LANES), lambda i: (0, 0)),
            pl.BlockSpec((1, LANES), lambda i: (0, 0)),
            pl.BlockSpec((TM, TM), lambda i: (0, 0)),
        ],
        out_specs=[
            pl.BlockSpec((DT * RC, XW), lambda i: (i, 0)),
            pl.BlockSpec((DT * TM, LANES), lambda i: (i, 0)),
            pl.BlockSpec((DT * NE, LANES), lambda i: (i, 0)),
        ],
        out_shape=[
            jax.ShapeDtypeStruct((NTM * RC, XW), bf16),
            jax.ShapeDtypeStruct((N, LANES), f32),
            jax.ShapeDtypeStruct((NTM * NE, LANES), f32),
        ],
        compiler_params=_params("parallel"),
        name="dispatch",
    )(x, g, w, b, upper)


def _dispatch_tables(cnt):
    i32 = jnp.int32
    n = cnt.reshape(NTM, NE, LANES)[:, :, 0].astype(i32)
    units = (n + UNIT - 1) // UNIT
    seg_end = jnp.cumsum(units, axis=1)
    seg_start = seg_end - units
    col_end = jnp.cumsum(units, axis=0)
    col_start = col_end - units
    chunks = (col_end[-1] + UPC - 1) // UPC
    ch_end = jnp.cumsum(chunks)
    ch_start = ch_end - chunks
    nused = ch_end[-1]
    c = jnp.arange(NCH, dtype=i32)
    eid = jnp.minimum(jnp.sum((ch_end[None, :] <= c[:, None]).astype(i32), axis=1), NE - 1)

    src0 = jnp.arange(NTM, dtype=i32)[:, None] * UPT + seg_start
    dst0 = ch_start[None, :] * UPC + col_start
    k = jnp.arange((NCH + NSLOT - 1) * UPC, dtype=i32)[:, None, None]
    inside = (k >= dst0[None]) & (k < (dst0 + units)[None])
    found = jnp.sum(inside.astype(i32), axis=(1, 2))
    shift = jnp.sum(jnp.where(inside, (src0 - dst0)[None], 0), axis=(1, 2))
    ffn_src = jnp.where(found > 0, k[:, 0, 0] + shift, ZERO_UNIT_IN)

    v = jnp.arange(UPT, dtype=i32)[None, :, None]
    inside_v = (v >= seg_start[:, None, :]) & (v < seg_end[:, None, :])
    found_v = jnp.sum(inside_v.astype(i32), axis=2)
    shift_v = jnp.sum(jnp.where(inside_v, (dst0 - seg_start)[:, None, :], 0), axis=2)
    comb_src = jnp.where(found_v > 0, v[:, :, 0] + shift_v, ZERO_UNIT_OUT).reshape(-1)
    return eid, nused.reshape(1), ffn_src, comb_src


def _unit_copies(src_ref, base, n_units, src_hbm, stage, slot, sem):
    out = []
    for j in range(n_units):
        row = pl.multiple_of(src_ref[base + j] * UNIT, UNIT)
        out.append(pltpu.make_async_copy(
            src_hbm.at[pl.ds(row, UNIT), :],
            stage.at[slot, pl.ds(j * UNIT, UNIT), :],
            sem.at[slot]))
    return out


def _start_gathers(step, n_steps, n_units, src_ref, src_hbm, stage, sem):
    ahead = NSLOT - 1

    def start(s, slot):
        for cp in _unit_copies(src_ref, s * n_units, n_units, src_hbm, stage, slot, sem):
            cp.start()

    for s in range(ahead):
        @pl.when((step == 0) & (s < n_steps))
        def _():
            start(s, s)

    @pl.when(step + ahead < n_steps)
    def _():
        start(step + ahead, lax.rem(step + ahead, NSLOT))


def _wait_gather(step, n_units, src_ref, src_hbm, stage, sem):
    slot = lax.rem(step, NSLOT)
    for cp in _unit_copies(src_ref, step * n_units, n_units, src_hbm, stage, slot, sem):
        cp.wait()
    return slot


def _ffn_kernel(eid_ref, nused_ref, src_ref, xc_hbm, wg_ref, wu_ref, wd_ref, o_ref,
                stage, sem, wg_b, wu_b, wd_b):
    c = pl.program_id(0)
    nused = nused_ref[0]
    ahead = NSLOT - 1

    def start(s):
        for cp in _unit_copies(src_ref, s * UPC, UPC, xc_hbm, stage, lax.rem(s, NSLOT), sem):
            cp.start()

    @pl.when(c == 0)
    def _():
        for s in range(ahead):
            start(s)

    @pl.when(c < nused)
    def _():
        slot = _wait_gather(c, UPC, src_ref, xc_hbm, stage, sem)

        @pl.when((c == 0) | (eid_ref[c] != eid_ref[jnp.maximum(c - 1, 0)]))
        def _():
            wg_b[...] = wg_ref[...].astype(bf16)
            wu_b[...] = wu_ref[...].astype(bf16)
            wd_b[...] = wd_ref[...].astype(bf16)

        def block(r):
            xs = stage[slot, r:r + CHB, :]
            x = xs[:, :D]
            a = jnp.dot(x, wg_b[...], preferred_element_type=f32)
            u = jnp.dot(x, wu_b[...], preferred_element_type=f32)
            yield
            gb = xs[:, D:].astype(f32)
            first = gb[:, 6:7] == eid_ref[c].astype(f32)
            gate = jnp.where(first, gb[:, 2:3] + gb[:, 3:4], gb[:, 4:5] + gb[:, 5:6])
            act = (a * (1.0 / (1.0 + jnp.exp(-a)))) * u * gate
            o_ref[r:r + CHB, :] = jnp.dot(act.astype(bf16), wd_b[...],
                                          preferred_element_type=f32).astype(bf16)

        blocks = [block(r) for r in range(0, CH, CHB)]
        for _ in range(2):
            for blk in blocks:
                next(blk, None)
        start(c + ahead)

    @pl.when(c >= nused)
    def _():
        @pl.when(c < nused + ahead)
        def _():
            _wait_gather(c, UPC, src_ref, xc_hbm, stage, sem)

        o_ref[...] = jnp.zeros((CH, D), bf16)


def _ffn(layer, eid, nused, ffn_src, xc, wg, wu, wd):
    return pl.pallas_call(
        _ffn_kernel,
        grid_spec=pltpu.PrefetchScalarGridSpec(
            num_scalar_prefetch=3,
            grid=(NCH,),
            in_specs=[
                pl.BlockSpec(memory_space=pl.ANY),
                pl.BlockSpec((None, None, D, F), lambda c, eid, nu, src: (layer, eid[c], 0, 0)),
                pl.BlockSpec((None, None, D, F), lambda c, eid, nu, src: (layer, eid[c], 0, 0)),
                pl.BlockSpec((None, None, F, D), lambda c, eid, nu, src: (layer, eid[c], 0, 0)),
            ],
            out_specs=pl.BlockSpec((CH, D), lambda c, eid, nu, src: (c, 0)),
            scratch_shapes=[
                pltpu.VMEM((NSLOT, CH, XW), bf16),
                pltpu.SemaphoreType.DMA((NSLOT,)),
                pltpu.VMEM((D, F), bf16),
                pltpu.VMEM((D, F), bf16),
                pltpu.VMEM((F, D), bf16),
            ],
        ),
        out_shape=jax.ShapeDtypeStruct((NCH * CH, D), bf16),
        compiler_params=_params("arbitrary"),
        name="ffn",
    )(eid, nused, ffn_src, xc, wg, wu, wd)


def _combine_tile(x, meta, rows):
    li = lax.broadcasted_iota(jnp.int32, (TM, RC), 1).astype(f32)
    pt = jnp.where((li == meta[:, 0:1]) | (li == meta[:, 1:2]), 1.0, 0.0).astype(bf16)
    return x + jnp.dot(pt, rows, preferred_element_type=f32)


def _combine_kernel(src_ref, x_ref, meta_ref, o_hbm, yp_ref, ys_ref, stage, sem):
    i = pl.program_id(0)
    _start_gathers(i, NTM, UPT, src_ref, o_hbm, stage, sem)
    slot = _wait_gather(i, UPT, src_ref, o_hbm, stage, sem)
    y = _combine_tile(x_ref[...], meta_ref[...], stage[slot])

    @pl.when(i < NP // TM)
    def _():
        yp_ref[...] = y

    @pl.when(i >= NP // TM)
    def _():
        ys_ref[...] = y


def _combine(comb_src, x, meta, o_sorted):
    npt = NP // TM
    out_specs = [pl.BlockSpec((TM, D), lambda i, src: (jnp.minimum(i, npt - 1), 0)),
                 pl.BlockSpec((TM, D), lambda i, src: (jnp.maximum(i - npt, 0), 0))]
    out_shape = [jax.ShapeDtypeStruct((NP, D), f32), jax.ShapeDtypeStruct((NS, D), f32)]
    return pl.pallas_call(
        _combine_kernel,
        grid_spec=pltpu.PrefetchScalarGridSpec(
            num_scalar_prefetch=1,
            grid=(NTM,),
            in_specs=[
                pl.BlockSpec((TM, D), lambda i, src: (i, 0)),
                pl.BlockSpec((TM, LANES), lambda i, src: (i, 0)),
                pl.BlockSpec(memory_space=pl.ANY),
            ],
            out_specs=out_specs,
            scratch_shapes=[
                pltpu.VMEM((NSLOT, RC, D), bf16),
                pltpu.SemaphoreType.DMA((NSLOT,)),
            ],
        ),
        out_shape=out_shape,
        compiler_params=_params("arbitrary"),
        name="combine",
    )(comb_src, x, meta, o_sorted)


def _conv_kernel(src_ref, x1_ref, meta_ref, o_hbm, g_ref, win_ref, cw_ref, wout_ref,
                 p1_ref, p2_ref, y_ref, cu_ref, pad_ref, x_ref, stage, sem):
    i = pl.program_id(0)

    @pl.when(i == 0)
    def _():
        pad_ref[0:8, :] = jnp.zeros((8, D), f32)

    tiles = T // TM
    ahead = NSLOT - 1

    def start(s):
        for cp in _unit_copies(src_ref, s * tiles * UPT, tiles * UPT, o_hbm, stage,
                               lax.rem(s, NSLOT), sem):
            cp.start()

    @pl.when(i == 0)
    def _():
        for s in range(ahead):
            start(s)

    slot = _wait_gather(i, tiles * UPT, src_ref, o_hbm, stage, sem)
    for k in range(tiles):
        rows = slice(k * TM, (k + 1) * TM)
        x_ref[rows, :] = _combine_tile(x1_ref[rows, :], meta_ref[rows, :],
                                       stage[slot, k * RC:(k + 1) * RC, :])

    blocks = [(r, r + T // 2) for r in (0, T // 2)]
    gates = []
    for r0, r1 in blocks:
        h = _rms(x_ref[r0:r1, :], g_ref[...]).astype(bf16)
        bcu = jnp.dot(h, win_ref[...], preferred_element_type=f32)
        gates.append(bcu[:, :D])
        cu = bcu[:, D:2 * D] * bcu[:, 2 * D:]
        pad_ref[8 + r0:8 + r1, :] = cu
        cu_ref[r0:r1, :] = cu
    t = lax.broadcasted_iota(jnp.int32, (T // 2, 1), 0) & (TS - 1)
    is_sample = i == NT - 1
    cw = cw_ref[...]
    for (r0, r1), b in zip(blocks, gates):
        m1 = jnp.where(is_sample & (t == 0), p1_ref[r0:r1, :], pad_ref[7 + r0:7 + r1, :])
        m2 = jnp.where(is_sample & (t < 2), p2_ref[r0:r1, :], pad_ref[6 + r0:6 + r1, :])
        conv = cw[0:1] * m2 + cw[1:2] * m1 + cw[2:3] * pad_ref[8 + r0:8 + r1, :]
        y = jnp.dot((b * conv).astype(bf16), wout_ref[...], preferred_element_type=f32)
        y_ref[r0:r1, :] = x_ref[r0:r1, :] + y
    pad_ref[0:8, :] = pad_ref[T:T + 8, :]
    start(i + ahead)

    @pl.when(i == NT - 1)
    def _():
        for s in range(ahead):
            _wait_gather(i + 1 + s, tiles * UPT, src_ref, o_hbm, stage, sem)


def _conv(comb_src, x1, meta, o_sorted, g, win, cw, wout, p1, p2):
    const = dict(pipeline_mode=pl.Buffered(1))
    comb_src = jnp.concatenate([
        comb_src, jnp.full(((NSLOT - 1) * (T // TM) * UPT,), ZERO_UNIT_OUT, jnp.int32)])
    return pl.pallas_call(
        _conv_kernel,
        grid_spec=pltpu.PrefetchScalarGridSpec(
            num_scalar_prefetch=1,
            grid=(NT,),
            in_specs=[
                pl.BlockSpec((T, D), lambda i, src: (i, 0)),
                pl.BlockSpec((T, LANES), lambda i, src: (i, 0)),
                pl.BlockSpec(memory_space=pl.ANY),
                pl.BlockSpec((1, D), lambda i, src: (0, 0)),
                pl.BlockSpec((D, 3 * D), lambda i, src: (0, 0), **const),
                pl.BlockSpec((3, D), lambda i, src: (0, 0)),
                pl.BlockSpec((D, D), lambda i, src: (0, 0), **const),
                pl.BlockSpec((T, D), lambda i, src: (0, 0), **const),
                pl.BlockSpec((T, D), lambda i, src: (0, 0), **const),
            ],
            out_specs=[
                pl.BlockSpec((T, D), lambda i, src: (i, 0)),
                pl.BlockSpec((T, D), lambda i, src: (jnp.where(i == NT - 1, 1, 0), 0)),
            ],
            scratch_shapes=[
                pltpu.VMEM((T + 8, D), f32),
                pltpu.VMEM((T, D), f32),
                pltpu.VMEM((NSLOT, (T // TM) * RC, D), bf16),
                pltpu.SemaphoreType.DMA((NSLOT,)),
            ],
        ),
        out_shape=[
            jax.ShapeDtypeStruct((N, D), f32),
            jax.ShapeDtypeStruct((2 * T, D), f32),
        ],
        compiler_params=_params("arbitrary"),
        name="conv",
    )(comb_src, x1, meta, o_sorted, g, win, cw, wout, p1, p2)


def _rope_tables():
    inv_freq = THETA ** (-jnp.arange(HALF, dtype=f32) / HALF)
    pos = jnp.concatenate([
        jnp.arange(NP, dtype=jnp.int32),
        PAST + jnp.tile(jnp.arange(TS, dtype=jnp.int32), NB),
    ]).astype(f32)
    ang = inv_freq[:, None] * pos[None, :]
    return jnp.cos(ang), jnp.sin(ang)


def _router_weights(w_group, b_group, w_router, b_router):
    gap = ROW_E - NGRP
    pad = LANES - ROW_E - NE
    w = jnp.concatenate([w_group, jnp.zeros((D, gap), f32), w_router,
                         jnp.zeros((D, pad), f32)], axis=1)
    b = jnp.concatenate([b_group, jnp.zeros((gap,), f32), b_router,
                         jnp.zeros((pad,), f32)])[None, :]
    return w.astype(bf16), b


def _moe_experts(x, i, norm_ffn, w_group, b_group, w_router, b_router, w_gate, w_up, w_down):
    w, b = _router_weights(w_group[i], b_group[i], w_router[i], b_router[i])
    xc, meta, cnt = _dispatch(x, norm_ffn[i][None, :], w, b)
    eid, nused, ffn_src, comb_src = _dispatch_tables(cnt)
    o_sorted = _ffn(i, eid, nused, ffn_src, xc, w_gate, w_up, w_down)
    return comb_src, meta, o_sorted


def kernel(x_prompt, x_sample, cache_k, cache_v, state_conv, norm_mix, w_qkv, q_norm, k_norm,
           sinks, w_o, w_in, conv_w, w_out, norm_ffn, w_group, b_group, w_router, b_router,
           w_gate, w_up, w_down):
    xp = x_prompt.reshape(NP, D)
    xs = x_sample.reshape(NS, D)
    moe_w = (norm_ffn, w_group, b_group, w_router, b_router, w_gate, w_up, w_down)

    cos, sin = _rope_tables()
    qT, ktok, vtok, vT = _qkv(xp, xs, norm_mix[0][None, :], w_qkv[0].T.astype(bf16),
                              q_norm[0][:, None], k_norm[0][:, None], cos, sin)
    sink_rows = jnp.repeat(sinks[0].reshape(KVH, G), TQ, axis=1)[:, None, :]

    qs = qT[:, NP:].reshape(KVH, G, HD, NB, TS).transpose(3, 0, 1, 4, 2)
    zq = jnp.zeros_like(qs[:, 0])
    qbd = jnp.stack([jnp.concatenate([qs[:, 0], zq], axis=-1),
                     jnp.concatenate([zq, qs[:, 1]], axis=-1)], axis=1)
    qbd = qbd.reshape(NB, H * TS, KVH * HD)
    k_new = ktok[NP:].reshape(NB, TS, KVH * HD)
    v_new = vtok[NP:].reshape(NB, TS, KVH * HD)
    pad4 = jnp.zeros((NB, 8 - TS, KVH * HD), f32)
    kc = cache_k[0].reshape(NB, WIN, KVH * HD)
    vc = cache_v[0].reshape(NB, WIN, KVH * HD)
    sink_col = jnp.repeat(sinks[0], TS)[:, None]
    o_s, kc_new, vc_new = _attn_sample(qbd, kc, vc, jnp.concatenate([k_new, pad4], axis=1),
                                       jnp.concatenate([v_new, pad4], axis=1), sink_col)
    o_s = o_s.reshape(NB, KVH, G, TS, KVH, HD)
    o_s = jnp.stack([o_s[:, 0, :, :, 0], o_s[:, 1, :, :, 1]], axis=1)
    o_s = o_s.transpose(0, 3, 1, 2, 4).reshape(NS, H * HD).astype(bf16)
    x = _attn_prompt(qT, ktok, vT, sink_rows, xp, xs, o_s, w_o[0].astype(bf16))
    comb_src, meta, o_sorted = _moe_experts(x, 0, *moe_w)

    new_k_prompt = ktok[NP - WIN:NP].reshape(1, 1, WIN, KVH, HD)
    new_v_prompt = vtok[NP - WIN:NP].reshape(1, 1, WIN, KVH, HD)
    new_k_sample = kc_new.reshape(1, NB, WIN, KVH, HD)
    new_v_sample = vc_new.reshape(1, NB, WIN, KVH, HD)

    st = state_conv[0]
    z = jnp.zeros((NB, 1, D), f32)
    p1 = jnp.concatenate([st[:, 1:2], z, z, z], axis=1).reshape(NS, D)
    p2 = jnp.concatenate([st[:, 0:1], st[:, 1:2], z, z], axis=1).reshape(NS, D)
    x, cu = _conv(comb_src, x, meta, o_sorted, norm_mix[1][None, :], w_in[0].astype(bf16),
                  conv_w[0], w_out[0].astype(bf16), p1, p2)
    comb_src, meta, o_sorted = _moe_experts(x, 1, *moe_w)
    y_prompt, y_sample = _combine(comb_src, x, meta, o_sorted)

    new_conv_prompt = cu[T - 2:T].reshape(1, 1, 2, D)
    new_conv_sample = cu[T:].reshape(NB, TS, D)[:, TS - 2:][None]

    y_prompt = y_prompt.reshape(1, NP, D)
    y_sample = y_sample.reshape(NB, TS, D)
    return (y_prompt, y_sample, new_k_prompt, new_v_prompt, new_conv_prompt,
            new_k_sample, new_v_sample, new_conv_sample)
```

```python
import jax
import jax.numpy as jnp
import numpy as np
from jax import lax
from jax.experimental import pallas as pl
from jax.experimental.pallas import tpu as pltpu

D = 1024
NP = 16384
NB = 128
TS = 4
NS = NB * TS
N = NP + NS
PAST = 16384
H = 16
KVH = 2
G = H // KVH
HD = 64
HALF = HD // 2
QKV = (H + 2 * KVH) * HD
WIN = 128
THETA = 10000.0
NGRP = 4
EPG = 4
NE = NGRP * EPG
TOPK = 2
ROW_E = 8
F = 512
EPS = 1e-6
SCALE = HD ** -0.5

T = 512
NT = N // T
TQ = 128
NQ = NP // TQ
QB = 4
BB = 8
LANES = 128
VMEM_LIMIT = 50 * 1024 * 1024

TM = 256
NTM = N // TM
DT = 3
UNIT = 16
RC = TOPK * TM + NE * UNIT
UPT = RC // UNIT
XW = D + LANES
CH = 512
CHB = 256
UPC = CH // UNIT
NSLOT = 3
NCH = -(-(NTM * (UPT - 1) + NE * (UPC - 1)) // UPC) + NSLOT - 1
ZERO_UNIT_IN = UPT - 1
ZERO_UNIT_OUT = (NCH - 1) * UPC

f32 = jnp.float32
bf16 = jnp.bfloat16


def _params(*sem):
    return pltpu.CompilerParams(dimension_semantics=sem, vmem_limit_bytes=VMEM_LIMIT)


def _rms(x, g):
    ms = jnp.mean(x * x, axis=-1, keepdims=True)
    return x * lax.rsqrt(ms + EPS) * g


def _pick(i, prompt_ref, sample_ref):
    return jnp.where(i == NT - 1, sample_ref[...], prompt_ref[...])


def _stream_specs():
    return [pl.BlockSpec((T, D), lambda i: (jnp.minimum(i, NT - 2), 0)),
            pl.BlockSpec((T, D), lambda i: (0, 0))]


def _qkv_kernel(xp_ref, xs_ref, g_ref, wT_ref, qn_ref, kn_ref, cos_ref, sin_ref,
                qT_ref, ktok_ref, vtok_ref, vT_ref):
    h = _rms(_pick(pl.program_id(0), xp_ref, xs_ref), g_ref[...]).astype(bf16)
    qkvT = lax.dot_general(wT_ref[...], h, (((1,), (1,)), ((), ())),
                           preferred_element_type=f32)
    cos = cos_ref[...]
    sin = sin_ref[...]

    def norm_rope(blk, gcol):
        ms = jnp.mean(blk * blk, axis=0, keepdims=True)
        y = blk * lax.rsqrt(ms + EPS) * gcol
        y1 = y[:HALF]
        y2 = y[HALF:]
        return y1 * cos - y2 * sin, y2 * cos + y1 * sin

    qn = qn_ref[...]
    for hd in range(H):
        o1, o2 = norm_rope(qkvT[hd * HD:(hd + 1) * HD], qn)
        qT_ref[hd * HD:hd * HD + HALF, :] = (o1 * SCALE).astype(bf16)
        qT_ref[hd * HD + HALF:(hd + 1) * HD, :] = (o2 * SCALE).astype(bf16)
    kn = kn_ref[...]
    ks = []
    for j in range(KVH):
        o1, o2 = norm_rope(qkvT[H * HD + j * HD:H * HD + (j + 1) * HD], kn)
        ks += [o1, o2]
    kT = jnp.concatenate(ks, axis=0)
    ktok_ref[...] = kT.T
    vT = qkvT[(H + KVH) * HD:]
    vtok_ref[...] = vT.T
    vT_ref[...] = vT.astype(bf16)


def _qkv(xp, xs, g, wT, qn, kn, cos, sin):
    return pl.pallas_call(
        _qkv_kernel,
        grid=(NT,),
        in_specs=_stream_specs() + [
            pl.BlockSpec((1, D), lambda i: (0, 0)),
            pl.BlockSpec((QKV, D), lambda i: (0, 0)),
            pl.BlockSpec((HD, 1), lambda i: (0, 0)),
            pl.BlockSpec((HD, 1), lambda i: (0, 0)),
            pl.BlockSpec((HALF, T), lambda i: (0, i)),
            pl.BlockSpec((HALF, T), lambda i: (0, i)),
        ],
        out_specs=[
            pl.BlockSpec((None, H * HD, T), lambda i: (i, 0, 0)),
            pl.BlockSpec((T, KVH * HD), lambda i: (i, 0)),
            pl.BlockSpec((T, KVH * HD), lambda i: (i, 0)),
            pl.BlockSpec((None, KVH * HD, T), lambda i: (i, 0, 0)),
        ],
        out_shape=[
            jax.ShapeDtypeStruct((NT, H * HD, T), bf16),
            jax.ShapeDtypeStruct((N, KVH * HD), f32),
            jax.ShapeDtypeStruct((N, KVH * HD), f32),
            jax.ShapeDtypeStruct((NT, KVH * HD, T), bf16),
        ],
        compiler_params=_params("parallel"),
        name="qkv",
    )(xp, xs, g, wT, qn, kn, cos, sin)


def _attn_prompt_kernel(qT_ref, kp_ref, kc_ref, vp_ref, vc_ref, sink_ref, bias0_ref, bias_ref,
                        xp_ref, xs_ref, os_ref, wo_ref, y_ref):
    j = pl.program_id(0)

    @pl.when(j >= NQ // QB)
    def _():
        y_ref[...] = xs_ref[...] + jnp.dot(os_ref[...], wo_ref[...],
                                           preferred_element_type=f32)

    @pl.when(j < NQ // QB)
    def _():
        k_all = jnp.concatenate([kp_ref[...], kc_ref[...]], axis=0)
        v_all = jnp.concatenate([vp_ref[...], vc_ref[...]], axis=1)
        blocks = []
        for k in range(QB):
            rows = slice(k * TQ, (k + 1) * TQ)
            blocks.append(_attend_block(
                qT_ref.at[:, rows], k_all[k * TQ:(k + 2) * TQ], v_all[:, k * TQ:(k + 2) * TQ],
                sink_ref, bias0_ref if k == 0 else bias_ref, xp_ref.at[rows], wo_ref,
                y_ref.at[rows]))
        for _ in range(3):
            for blk in blocks:
                next(blk, None)


def _attend_block(qT_ref, kk, vv, sink_ref, bias_ref, x_ref, wo_ref, y_ref):
    kk = kk.astype(bf16)
    bias = jnp.concatenate([bias_ref[...]] * H, axis=1)
    qg = [jnp.concatenate([qT_ref[(g * G + hh) * HD:(g * G + hh + 1) * HD, :]
                           for hh in range(G)], axis=1) for g in range(KVH)]
    zeros = jnp.zeros_like(qg[0])
    rhs = jnp.concatenate([jnp.concatenate([qg[0], zeros], axis=1),
                           jnp.concatenate([zeros, qg[1]], axis=1)], axis=0)
    sT = jnp.dot(kk, rhs, preferred_element_type=f32) + bias
    yield
    sink = jnp.concatenate([sink_ref[0], sink_ref[1]], axis=1)
    m = jnp.maximum(jnp.max(sT, axis=0, keepdims=True), sink)
    p = jnp.exp(sT - m)
    l = jnp.sum(p, axis=0, keepdims=True) + jnp.exp(sink - m)
    p = (p * (1.0 / l)).astype(bf16)
    yield
    pieces = []
    for g in range(KVH):
        oT = jnp.dot(vv[g * HD:(g + 1) * HD, :], p[:, g * G * TQ:(g + 1) * G * TQ],
                     preferred_element_type=f32)
        pieces += [oT[:, hh * TQ:(hh + 1) * TQ] for hh in range(G)]
    oT_all = jnp.concatenate(pieces, axis=0)
    y_ref[...] = x_ref[...] + jnp.dot(oT_all.T.astype(bf16), wo_ref[...],
                                      preferred_element_type=f32)


def _band_bias():
    s = np.arange(2 * TQ)[:, None]
    t = np.arange(TQ)[None, :]
    dist = t + TQ - s
    band = (dist >= 0) & (dist <= WIN)
    first = band & (s >= TQ)
    return jnp.asarray(np.where(np.stack([first, band]), 0.0, -np.inf), f32)


def _attn_prompt(qT, ktok, vT, sink_rows, xp, xs, o_s, wo):
    steps = NQ // QB
    rows = QB * TQ
    assert rows == T

    def cur(j):
        return jnp.minimum(j, steps - 1)

    def prev(j):
        return jnp.maximum(cur(j) * QB - 1, 0)

    def prev_tile(j):
        return jnp.maximum(cur(j) - 1, 0)

    def sample(j):
        return jnp.maximum(j - steps, 0)

    bias = _band_bias()
    return pl.pallas_call(
        _attn_prompt_kernel,
        grid=(N // rows,),
        in_specs=[
            pl.BlockSpec((None, H * HD, rows), lambda j: (cur(j), 0, 0)),
            pl.BlockSpec((TQ, KVH * HD), lambda j: (prev(j), 0)),
            pl.BlockSpec((rows, KVH * HD), lambda j: (cur(j), 0)),
            pl.BlockSpec((None, KVH * HD, TQ), lambda j: (prev_tile(j), 0, QB - 1)),
            pl.BlockSpec((None, KVH * HD, rows), lambda j: (cur(j), 0, 0)),
            pl.BlockSpec((KVH, 1, G * TQ), lambda j: (0, 0, 0)),
            pl.BlockSpec((None, 2 * TQ, TQ), lambda j: (jnp.minimum(j, 1), 0, 0)),
            pl.BlockSpec((None, 2 * TQ, TQ), lambda j: (1, 0, 0)),
            pl.BlockSpec((rows, D), lambda j: (cur(j), 0)),
            pl.BlockSpec((rows, D), lambda j: (sample(j), 0)),
            pl.BlockSpec((rows, H * HD), lambda j: (sample(j), 0)),
            pl.BlockSpec((D, D), lambda j: (0, 0)),
        ],
        out_specs=pl.BlockSpec((rows, D), lambda j: (j, 0)),
        out_shape=jax.ShapeDtypeStruct((N, D), f32),
        compiler_params=_params("parallel"),
        name="attn_prompt",
    )(qT, ktok, ktok, vT, vT, sink_rows, bias, bias, xp, xs, o_s, wo)


def _attn_sample_kernel(q_ref, kc_ref, vc_ref, kn_ref, vn_ref, sink_ref,
                        o_ref, knew_ref, vnew_ref):
    rows = H * TS
    t1 = lax.broadcasted_iota(jnp.int32, (1, rows, WIN), 1) & (TS - 1)
    s1 = lax.broadcasted_iota(jnp.int32, (1, rows, WIN), 2)
    valid1 = s1 >= t1
    t2 = lax.broadcasted_iota(jnp.int32, (1, rows, 8), 1) & (TS - 1)
    s2 = lax.broadcasted_iota(jnp.int32, (1, rows, 8), 2)
    valid2 = s2 <= t2
    sink = sink_ref[...][None]
    q = q_ref[...]
    kc = kc_ref[...]
    vc = vc_ref[...]
    kn = kn_ref[...]
    vn = vn_ref[...]
    sc = jnp.einsum('bqd,bkd->bqk', q, kc.astype(bf16), preferred_element_type=f32)
    sn = jnp.einsum('bqd,bkd->bqk', q, kn.astype(bf16), preferred_element_type=f32)
    sc = jnp.where(valid1, sc, -jnp.inf)
    sn = jnp.where(valid2, sn, -jnp.inf)
    m = jnp.maximum(jnp.maximum(jnp.max(sc, axis=-1, keepdims=True),
                                jnp.max(sn, axis=-1, keepdims=True)), sink)
    pc = jnp.exp(sc - m)
    pn = jnp.exp(sn - m)
    l = (jnp.sum(pc, axis=-1, keepdims=True) + jnp.sum(pn, axis=-1, keepdims=True)
         + jnp.exp(sink - m))
    r = 1.0 / l
    o_ref[...] = (jnp.einsum('bqk,bkd->bqd', (pc * r).astype(bf16), vc.astype(bf16),
                             preferred_element_type=f32)
                  + jnp.einsum('bqk,bkd->bqd', (pn * r).astype(bf16), vn.astype(bf16),
                               preferred_element_type=f32))
    knew_ref[:, :WIN - TS, :] = kc[:, TS:, :]
    knew_ref[:, WIN - TS:, :] = kn[:, :TS, :]
    vnew_ref[:, :WIN - TS, :] = vc[:, TS:, :]
    vnew_ref[:, WIN - TS:, :] = vn[:, :TS, :]


def _attn_sample(qbd, kc, vc, kn, vn, sink_col):
    rows = H * TS
    cache_spec = pl.BlockSpec((BB, WIN, KVH * HD), lambda i: (i, 0, 0))
    new_spec = pl.BlockSpec((BB, 8, KVH * HD), lambda i: (i, 0, 0))
    cache_shape = jax.ShapeDtypeStruct((NB, WIN, KVH * HD), f32)
    return pl.pallas_call(
        _attn_sample_kernel,
        grid=(NB // BB,),
        in_specs=[
            pl.BlockSpec((BB, rows, KVH * HD), lambda i: (i, 0, 0)),
            cache_spec, cache_spec, new_spec, new_spec,
            pl.BlockSpec((rows, 1), lambda i: (0, 0)),
        ],
        out_specs=[pl.BlockSpec((BB, rows, KVH * HD), lambda i: (i, 0, 0)),
                   cache_spec, cache_spec],
        out_shape=[jax.ShapeDtypeStruct((NB, rows, KVH * HD), f32), cache_shape, cache_shape],
        compiler_params=_params("parallel"),
        name="attn_sample",
    )(qbd, kc, vc, kn, vn, sink_col)


def _dispatch_kernel(x_ref, g_ref, w_ref, b_ref, upper_ref, xc_ref, meta_ref, cnt_ref):
    tiles = [_dispatch_tile(x_ref.at[k * TM:(k + 1) * TM], g_ref, w_ref, b_ref, upper_ref,
                            xc_ref.at[k * RC:(k + 1) * RC], meta_ref.at[k * TM:(k + 1) * TM],
                            cnt_ref.at[k * NE:(k + 1) * NE]) for k in range(DT)]
    for _ in range(3):
        for t in tiles:
            next(t, None)


def _dispatch_tile(x_ref, g_ref, w_ref, b_ref, upper_ref, xc_ref, meta_ref, cnt_ref):
    h_hi = _rms(x_ref[...], g_ref[...]).astype(bf16)
    logits = jnp.dot(h_hi, w_ref[...], preferred_element_type=f32) + b_ref[...]
    yield
    lt = logits.T
    inf = jnp.inf
    row8 = lax.broadcasted_iota(jnp.int32, (8, TM), 0).astype(f32)
    gl = jnp.where(row8 < NGRP, lt[0:8], -inf)
    gmax = jnp.max(gl, axis=0, keepdims=True)
    gsel = jnp.min(jnp.where(gl == gmax, row8, 8.0), axis=0, keepdims=True)
    g_w = 1.0 / jnp.sum(jnp.exp(gl - gmax), axis=0, keepdims=True)
    row = lax.broadcasted_iota(jnp.int32, (NE, TM), 0)
    row_f = row.astype(f32)
    el = jnp.where((row >> 2).astype(f32) == gsel, lt[ROW_E:ROW_E + NE], -inf)
    v1 = jnp.max(el, axis=0, keepdims=True)
    i1 = jnp.min(jnp.where(el == v1, row_f, float(NE)), axis=0, keepdims=True)
    el2 = jnp.where(row_f == i1, -inf, el)
    v2 = jnp.max(el2, axis=0, keepdims=True)
    i2 = jnp.min(jnp.where(el2 == v2, row_f, float(NE)), axis=0, keepdims=True)
    e1 = jnp.exp(v2 - v1)
    den = 1.0 + e1
    w1 = (1.0 / den) * g_w
    w2 = (e1 / den) * g_w

    m1 = row_f == i1
    m2 = row_f == i2
    sel = jnp.where(m1 | m2, 1.0, 0.0)
    ranks = jnp.dot(sel.astype(bf16), upper_ref[...],
                    preferred_element_type=f32)
    counts = jnp.sum(sel, axis=1, keepdims=True)
    padded = jnp.floor((counts + (UNIT - 1.0)) * (1.0 / UNIT)) * UNIT
    e_i = lax.broadcasted_iota(jnp.int32, (NE, NE), 0)
    f_i = lax.broadcasted_iota(jnp.int32, (NE, NE), 1)
    below = jnp.where(f_i < e_i, 1.0, 0.0).astype(bf16)
    seg = jnp.dot(below, jnp.broadcast_to(padded, (NE, LANES)).astype(bf16),
                  preferred_element_type=f32)[:, 0:1]
    posall = seg + ranks
    pos1 = jnp.sum(jnp.where(m1, posall, 0.0), axis=0, keepdims=True)
    pos2 = jnp.sum(jnp.where(m2, posall, 0.0), axis=0, keepdims=True)
    cnt_ref[...] = jnp.broadcast_to(counts, (NE, LANES))

    w1_hi = w1.astype(bf16).astype(f32)
    w2_hi = w2.astype(bf16).astype(f32)
    slab = jnp.zeros((8, TM), f32)
    for k, r in enumerate((pos1, pos2, w1_hi, w1 - w1_hi, w2_hi, w2 - w2_hi, i1)):
        slab = jnp.where(row8 == k, r, slab)
    meta = jnp.concatenate([slab, jnp.zeros((LANES - 8, TM), f32)], axis=0).T
    meta_ref[...] = meta
    yield

    rr = lax.broadcasted_iota(jnp.int32, (RC, TM), 0).astype(f32)
    onehot = jnp.where((rr == pos1) | (rr == pos2), 1.0, 0.0).astype(bf16)
    h_aug = jnp.concatenate([h_hi, meta.astype(bf16)], axis=1)
    xc_ref[...] = jnp.dot(onehot, h_aug, preferred_element_type=f32).astype(bf16)


def _dispatch(x, g, w, b):
    upper = jnp.asarray(np.triu(np.ones((TM, TM), np.float32), 1), bf16)
    return pl.pallas_call(
        _dispatch_kernel,
        grid=(NTM // DT,),
        in_specs=[
            pl.BlockSpec((DT * TM, D), lambda i: (i, 0)),
            pl.BlockSpec((1, D), lambda i: (0, 0)),
            pl.BlockSpec((D, LANES), lambda i: (0, 0)),
            pl.BlockSpec((1, LANES), lambda i: (0, 0)),
            pl.BlockSpec((TM, TM), lambda i: (0, 0)),
        ],
        out_specs=[
            pl.BlockSpec((DT * RC, XW), lambda i: (i, 0)),
            pl.BlockSpec((DT * TM, LANES), lambda i: (i, 0)),
            pl.BlockSpec((DT * NE, LANES), lambda i: (i, 0)),
        ],
        out_shape=[
            jax.ShapeDtypeStruct((NTM * RC, XW), bf16),
            jax.ShapeDtypeStruct((N, LANES), f32),
            jax.ShapeDtypeStruct((NTM * NE, LANES), f32),
        ],
        compiler_params=_params("parallel"),
        name="dispatch",
    )(x, g, w, b, upper)


def _dispatch_tables(cnt):
    i32 = jnp.int32
    n = cnt.reshape(NTM, NE, LANES)[:, :, 0].astype(i32)
    units = (n + UNIT - 1) // UNIT
    seg_end = jnp.cumsum(units, axis=1)
    seg_start = seg_end - units
    col_end = jnp.cumsum(units, axis=0)
    col_start = col_end - units
    chunks = (col_end[-1] + UPC - 1) // UPC
    ch_end = jnp.cumsum(chunks)
    ch_start = ch_end - chunks
    nused = ch_end[-1]
    c = jnp.arange(NCH, dtype=i32)
    eid = jnp.minimum(jnp.sum((ch_end[None, :] <= c[:, None]).astype(i32), axis=1), NE - 1)

    src0 = jnp.arange(NTM, dtype=i32)[:, None] * UPT + seg_start
    dst0 = ch_start[None, :] * UPC + col_start
    k = jnp.arange((NCH + NSLOT - 1) * UPC, dtype=i32)[:, None, None]
    inside = (k >= dst0[None]) & (k < (dst0 + units)[None])
    found = jnp.sum(inside.astype(i32), axis=(1, 2))
    shift = jnp.sum(jnp.where(inside, (src0 - dst0)[None], 0), axis=(1, 2))
    ffn_src = jnp.where(found > 0, k[:, 0, 0] + shift, ZERO_UNIT_IN)

    v = jnp.arange(UPT, dtype=i32)[None, :, None]
    inside_v = (v >= seg_start[:, None, :]) & (v < seg_end[:, None, :])
    found_v = jnp.sum(inside_v.astype(i32), axis=2)
    shift_v = jnp.sum(jnp.where(inside_v, (dst0 - seg_start)[:, None, :], 0), axis=2)
    comb_src = jnp.where(found_v > 0, v[:, :, 0] + shift_v, ZERO_UNIT_OUT).reshape(-1)
    return eid, nused.reshape(1), ffn_src, comb_src


def _unit_copies(src_ref, base, n_units, src_hbm, stage, slot, sem):
    out = []
    for j in range(n_units):
        row = pl.multiple_of(src_ref[base + j] * UNIT, UNIT)
        out.append(pltpu.make_async_copy(
            src_hbm.at[pl.ds(row, UNIT), :],
            stage.at[slot, pl.ds(j * UNIT, UNIT), :],
            sem.at[slot]))
    return out


def _start_gathers(step, n_steps, n_units, src_ref, src_hbm, stage, sem):
    ahead = NSLOT - 1

    def start(s, slot):
        for cp in _unit_copies(src_ref, s * n_units, n_units, src_hbm, stage, slot, sem):
            cp.start()

    for s in range(ahead):
        @pl.when((step == 0) & (s < n_steps))
        def _():
            start(s, s)

    @pl.when(step + ahead < n_steps)
    def _():
        start(step + ahead, lax.rem(step + ahead, NSLOT))


def _wait_gather(step, n_units, src_ref, src_hbm, stage, sem):
    slot = lax.rem(step, NSLOT)
    for cp in _unit_copies(src_ref, step * n_units, n_units, src_hbm, stage, slot, sem):
        cp.wait()
    return slot


def _ffn_kernel(eid_ref, nused_ref, src_ref, xc_hbm, wg_ref, wu_ref, wd_ref, o_ref,
                stage, sem, wg_b, wu_b, wd_b):
    c = pl.program_id(0)
    nused = nused_ref[0]
    ahead = NSLOT - 1

    def start(s):
        for cp in _unit_copies(src_ref, s * UPC, UPC, xc_hbm, stage, lax.rem(s, NSLOT), sem):
            cp.start()

    @pl.when(c == 0)
    def _():
        for s in range(ahead):
            start(s)

    @pl.when(c < nused)
    def _():
        slot = _wait_gather(c, UPC, src_ref, xc_hbm, stage, sem)

        @pl.when((c == 0) | (eid_ref[c] != eid_ref[jnp.maximum(c - 1, 0)]))
        def _():
            wg_b[...] = wg_ref[...].astype(bf16)
            wu_b[...] = wu_ref[...].astype(bf16)
            wd_b[...] = wd_ref[...].astype(bf16)

        def block(r):
            xs = stage[slot, r:r + CHB, :]
            x = xs[:, :D]
            a = jnp.dot(x, wg_b[...], preferred_element_type=f32)
            u = jnp.dot(x, wu_b[...], preferred_element_type=f32)
            yield
            gb = xs[:, D:].astype(f32)
            first = gb[:, 6:7] == eid_ref[c].astype(f32)
            gate = jnp.where(first, gb[:, 2:3] + gb[:, 3:4], gb[:, 4:5] + gb[:, 5:6])
            act = (a * (1.0 / (1.0 + jnp.exp(-a)))) * u * gate
            o_ref[r:r + CHB, :] = jnp.dot(act.astype(bf16), wd_b[...],
                                          preferred_element_type=f32).astype(bf16)

        blocks = [block(r) for r in range(0, CH, CHB)]
        for _ in range(2):
            for blk in blocks:
                next(blk, None)
        start(c + ahead)

    @pl.when(c >= nused)
    def _():
        @pl.when(c < nused + ahead)
        def _():
            _wait_gather(c, UPC, src_ref, xc_hbm, stage, sem)

        o_ref[...] = jnp.zeros((CH, D), bf16)


def _ffn(layer, eid, nused, ffn_src, xc, wg, wu, wd):
    return pl.pallas_call(
        _ffn_kernel,
        grid_spec=pltpu.PrefetchScalarGridSpec(
            num_scalar_prefetch=3,
            grid=(NCH,),
            in_specs=[
                pl.BlockSpec(memory_space=pl.ANY),
                pl.BlockSpec((None, None, D, F), lambda c, eid, nu, src: (layer, eid[c], 0, 0)),
                pl.BlockSpec((None, None, D, F), lambda c, eid, nu, src: (layer, eid[c], 0, 0)),
                pl.BlockSpec((None, None, F, D), lambda c, eid, nu, src: (layer, eid[c], 0, 0)),
            ],
            out_specs=pl.BlockSpec((CH, D), lambda c, eid, nu, src: (c, 0)),
            scratch_shapes=[
                pltpu.VMEM((NSLOT, CH, XW), bf16),
                pltpu.SemaphoreType.DMA((NSLOT,)),
                pltpu.VMEM((D, F), bf16),
                pltpu.VMEM((D, F), bf16),
                pltpu.VMEM((F, D), bf16),
            ],
        ),
        out_shape=jax.ShapeDtypeStruct((NCH * CH, D), bf16),
        compiler_params=_params("arbitrary"),
        name="ffn",
    )(eid, nused, ffn_src, xc, wg, wu, wd)


def _combine_tile(x, meta, rows):
    li = lax.broadcasted_iota(jnp.int32, (TM, RC), 1).astype(f32)
    pt = jnp.where((li == meta[:, 0:1]) | (li == meta[:, 1:2]), 1.0, 0.0).astype(bf16)
    return x + jnp.dot(pt, rows, preferred_element_type=f32)


def _combine_kernel(src_ref, x_ref, meta_ref, o_hbm, yp_ref, ys_ref, stage, sem):
    i = pl.program_id(0)
    tiles = T // TM
    _start_gathers(i, NT, tiles * UPT, src_ref, o_hbm, stage, sem)
    slot = _wait_gather(i, tiles * UPT, src_ref, o_hbm, stage, sem)
    ys = [_combine_tile(x_ref[k * TM:(k + 1) * TM, :], meta_ref[k * TM:(k + 1) * TM, :],
                        stage[slot, k * RC:(k + 1) * RC, :]) for k in range(tiles)]
    y = jnp.concatenate(ys, axis=0)

    @pl.when(i < NT - 1)
    def _():
        yp_ref[...] = y

    @pl.when(i == NT - 1)
    def _():
        ys_ref[...] = y


def _combine(comb_src, x, meta, o_sorted):
    out_specs = [pl.BlockSpec((T, D), lambda i, src: (jnp.minimum(i, NT - 2), 0)),
                 pl.BlockSpec((T, D), lambda i, src: (0, 0))]
    out_shape = [jax.ShapeDtypeStruct((NP, D), f32), jax.ShapeDtypeStruct((NS, D), f32)]
    return pl.pallas_call(
        _combine_kernel,
        grid_spec=pltpu.PrefetchScalarGridSpec(
            num_scalar_prefetch=1,
            grid=(NT,),
            in_specs=[
                pl.BlockSpec((T, D), lambda i, src: (i, 0)),
                pl.BlockSpec((T, LANES), lambda i, src: (i, 0)),
                pl.BlockSpec(memory_space=pl.ANY),
            ],
            out_specs=out_specs,
            scratch_shapes=[
                pltpu.VMEM((NSLOT, (T // TM) * RC, D), bf16),
                pltpu.SemaphoreType.DMA((NSLOT,)),
            ],
        ),
        out_shape=out_shape,
        compiler_params=_params("arbitrary"),
        name="combine",
    )(comb_src, x, meta, o_sorted)


def _conv_kernel(src_ref, x1_ref, meta_ref, o_hbm, g_ref, win_ref, cw_ref, wout_ref,
                 p1_ref, p2_ref, y_ref, cu_ref, pad_ref, x_ref, stage, sem):
    i = pl.program_id(0)

    @pl.when(i == 0)
    def _():
        pad_ref[0:8, :] = jnp.zeros((8, D), f32)

    tiles = T // TM
    ahead = NSLOT - 1

    def start(s):
        for cp in _unit_copies(src_ref, s * tiles * UPT, tiles * UPT, o_hbm, stage,
                               lax.rem(s, NSLOT), sem):
            cp.start()

    @pl.when(i == 0)
    def _():
        for s in range(ahead):
            start(s)

    slot = _wait_gather(i, tiles * UPT, src_ref, o_hbm, stage, sem)
    for k in range(tiles):
        rows = slice(k * TM, (k + 1) * TM)
        x_ref[rows, :] = _combine_tile(x1_ref[rows, :], meta_ref[rows, :],
                                       stage[slot, k * RC:(k + 1) * RC, :])

    blocks = [(r, r + T // 2) for r in (0, T // 2)]
    gates = []
    for r0, r1 in blocks:
        h = _rms(x_ref[r0:r1, :], g_ref[...]).astype(bf16)
        bcu = jnp.dot(h, win_ref[...], preferred_element_type=f32)
        gates.append(bcu[:, :D])
        cu = bcu[:, D:2 * D] * bcu[:, 2 * D:]
        pad_ref[8 + r0:8 + r1, :] = cu
        cu_ref[r0:r1, :] = cu
    t = lax.broadcasted_iota(jnp.int32, (T // 2, 1), 0) & (TS - 1)
    is_sample = i == NT - 1
    cw = cw_ref[...]
    for (r0, r1), b in zip(blocks, gates):
        m1 = jnp.where(is_sample & (t == 0), p1_ref[r0:r1, :], pad_ref[7 + r0:7 + r1, :])
        m2 = jnp.where(is_sample & (t < 2), p2_ref[r0:r1, :], pad_ref[6 + r0:6 + r1, :])
        conv = cw[0:1] * m2 + cw[1:2] * m1 + cw[2:3] * pad_ref[8 + r0:8 + r1, :]
        y = jnp.dot((b * conv).astype(bf16), wout_ref[...], preferred_element_type=f32)
        y_ref[r0:r1, :] = x_ref[r0:r1, :] + y
    pad_ref[0:8, :] = pad_ref[T:T + 8, :]
    start(i + ahead)

    @pl.when(i == NT - 1)
    def _():
        for s in range(ahead):
            _wait_gather(i + 1 + s, tiles * UPT, src_ref, o_hbm, stage, sem)


def _conv(comb_src, x1, meta, o_sorted, g, win, cw, wout, p1, p2):
    const = dict(pipeline_mode=pl.Buffered(1))
    comb_src = jnp.concatenate([
        comb_src, jnp.full(((NSLOT - 1) * (T // TM) * UPT,), ZERO_UNIT_OUT, jnp.int32)])
    return pl.pallas_call(
        _conv_kernel,
        grid_spec=pltpu.PrefetchScalarGridSpec(
            num_scalar_prefetch=1,
            grid=(NT,),
            in_specs=[
                pl.BlockSpec((T, D), lambda i, src: (i, 0)),
                pl.BlockSpec((T, LANES), lambda i, src: (i, 0)),
                pl.BlockSpec(memory_space=pl.ANY),
                pl.BlockSpec((1, D), lambda i, src: (0, 0)),
                pl.BlockSpec((D, 3 * D), lambda i, src: (0, 0), **const),
                pl.BlockSpec((3, D), lambda i, src: (0, 0)),
                pl.BlockSpec((D, D), lambda i, src: (0, 0), **const),
                pl.BlockSpec((T, D), lambda i, src: (0, 0), **const),
                pl.BlockSpec((T, D), lambda i, src: (0, 0), **const),
            ],
            out_specs=[
                pl.BlockSpec((T, D), lambda i, src: (i, 0)),
                pl.BlockSpec((T, D), lambda i, src: (jnp.where(i == NT - 1, 1, 0), 0)),
            ],
            scratch_shapes=[
                pltpu.VMEM((T + 8, D), f32),
                pltpu.VMEM((T, D), f32),
                pltpu.VMEM((NSLOT, (T // TM) * RC, D), bf16),
                pltpu.SemaphoreType.DMA((NSLOT,)),
            ],
        ),
        out_shape=[
            jax.ShapeDtypeStruct((N, D), f32),
            jax.ShapeDtypeStruct((2 * T, D), f32),
        ],
        compiler_params=_params("arbitrary"),
        name="conv",
    )(comb_src, x1, meta, o_sorted, g, win, cw, wout, p1, p2)


def _rope_tables():
    inv_freq = THETA ** (-jnp.arange(HALF, dtype=f32) / HALF)
    pos = jnp.concatenate([
        jnp.arange(NP, dtype=jnp.int32),
        PAST + jnp.tile(jnp.arange(TS, dtype=jnp.int32), NB),
    ]).astype(f32)
    ang = inv_freq[:, None] * pos[None, :]
    return jnp.cos(ang), jnp.sin(ang)


def _router_weights(w_group, b_group, w_router, b_router):
    gap = ROW_E - NGRP
    pad = LANES - ROW_E - NE
    w = jnp.concatenate([w_group, jnp.zeros((D, gap), f32), w_router,
                         jnp.zeros((D, pad), f32)], axis=1)
    b = jnp.concatenate([b_group, jnp.zeros((gap,), f32), b_router,
                         jnp.zeros((pad,), f32)])[None, :]
    return w.astype(bf16), b


def _moe_experts(x, i, norm_ffn, w_group, b_group, w_router, b_router, w_gate, w_up, w_down):
    w, b = _router_weights(w_group[i], b_group[i], w_router[i], b_router[i])
    xc, meta, cnt = _dispatch(x, norm_ffn[i][None, :], w, b)
    eid, nused, ffn_src, comb_src = _dispatch_tables(cnt)
    o_sorted = _ffn(i, eid, nused, ffn_src, xc, w_gate, w_up, w_down)
    return comb_src, meta, o_sorted


def kernel(x_prompt, x_sample, cache_k, cache_v, state_conv, norm_mix, w_qkv, q_norm, k_norm,
           sinks, w_o, w_in, conv_w, w_out, norm_ffn, w_group, b_group, w_router, b_router,
           w_gate, w_up, w_down):
    xp = x_prompt.reshape(NP, D)
    xs = x_sample.reshape(NS, D)
    moe_w = (norm_ffn, w_group, b_group, w_router, b_router, w_gate, w_up, w_down)

    cos, sin = _rope_tables()
    qT, ktok, vtok, vT = _qkv(xp, xs, norm_mix[0][None, :], w_qkv[0].T.astype(bf16),
                              q_norm[0][:, None], k_norm[0][:, None], cos, sin)
    sink_rows = jnp.repeat(sinks[0].reshape(KVH, G), TQ, axis=1)[:, None, :]

    qs = qT[NT - 1].reshape(KVH, G, HD, NB, TS).transpose(3, 0, 1, 4, 2)
    zq = jnp.zeros_like(qs[:, 0])
    qbd = jnp.stack([jnp.concatenate([qs[:, 0], zq], axis=-1),
                     jnp.concatenate([zq, qs[:, 1]], axis=-1)], axis=1)
    qbd = qbd.reshape(NB, H * TS, KVH * HD)
    k_new = ktok[NP:].reshape(NB, TS, KVH * HD)
    v_new = vtok[NP:].reshape(NB, TS, KVH * HD)
    pad4 = jnp.zeros((NB, 8 - TS, KVH * HD), f32)
    kc = cache_k[0].reshape(NB, WIN, KVH * HD)
    vc = cache_v[0].reshape(NB, WIN, KVH * HD)
    sink_col = jnp.repeat(sinks[0], TS)[:, None]
    o_s, kc_new, vc_new = _attn_sample(qbd, kc, vc, jnp.concatenate([k_new, pad4], axis=1),
                                       jnp.concatenate([v_new, pad4], axis=1), sink_col)
    o_s = o_s.reshape(NB, KVH, G, TS, KVH, HD)
    o_s = jnp.stack([o_s[:, 0, :, :, 0], o_s[:, 1, :, :, 1]], axis=1)
    o_s = o_s.transpose(0, 3, 1, 2, 4).reshape(NS, H * HD).astype(bf16)
    x = _attn_prompt(qT, ktok, vT, sink_rows, xp, xs, o_s, w_o[0].astype(bf16))
    comb_src, meta, o_sorted = _moe_experts(x, 0, *moe_w)

    new_k_prompt = ktok[NP - WIN:NP].reshape(1, 1, WIN, KVH, HD)
    new_v_prompt = vtok[NP - WIN:NP].reshape(1, 1, WIN, KVH, HD)
    new_k_sample = kc_new.reshape(1, NB, WIN, KVH, HD)
    new_v_sample = vc_new.reshape(1, NB, WIN, KVH, HD)

    st = state_conv[0]
    z = jnp.zeros((NB, 1, D), f32)
    p1 = jnp.concatenate([st[:, 1:2], z, z, z], axis=1).reshape(NS, D)
    p2 = jnp.concatenate([st[:, 0:1], st[:, 1:2], z, z], axis=1).reshape(NS, D)
    x, cu = _conv(comb_src, x, meta, o_sorted, norm_mix[1][None, :], w_in[0].astype(bf16),
                  conv_w[0], w_out[0].astype(bf16), p1, p2)
    comb_src, meta, o_sorted = _moe_experts(x, 1, *moe_w)
    y_prompt, y_sample = _combine(comb_src, x, meta, o_sorted)

    new_conv_prompt = cu[T - 2:T].reshape(1, 1, 2, D)
    new_conv_sample = cu[T:].reshape(NB, TS, D)[:, TS - 2:][None]

    y_prompt = y_prompt.reshape(1, NP, D)
    y_sample = y_sample.reshape(NB, TS, D)
    return (y_prompt, y_sample, new_k_prompt, new_v_prompt, new_conv_prompt,
            new_k_sample, new_v_sample, new_conv_sample)
```

```python
import jax
import jax.numpy as jnp
import numpy as np
from jax import lax
from jax.experimental import pallas as pl
from jax.experimental.pallas import tpu as pltpu

D = 1024
NP = 16384
NB = 128
TS = 4
NS = NB * TS
N = NP + NS
PAST = 16384
H = 16
KVH = 2
G = H // KVH
HD = 64
HALF = HD // 2
QKV = (H + 2 * KVH) * HD
WIN = 128
THETA = 10000.0
NGRP = 4
EPG = 4
NE = NGRP * EPG
TOPK = 2
ROW_E = 8
F = 512
EPS = 1e-6
SCALE = HD ** -0.5

T = 512
NT = N // T
TQ = 128
NQ = NP // TQ
QB = 4
BB = 16
LANES = 128
V7X_VMEM_BYTES = 64 * 1024 * 1024
VMEM_LIMIT = V7X_VMEM_BYTES - 14 * 1024 * 1024

TM = 256
NTM = N // TM
DT = 6
UNIT = 16
RC = TOPK * TM + NE * UNIT
UPT = RC // UNIT
XW = D + LANES
CH = 512
CHB = 512
UPC = CH // UNIT
NSLOT = 3
NCH = -(-(NTM * (UPT - 1) + NE * (UPC - 1)) // UPC) + NSLOT - 1
ZERO_UNIT_IN = UPT - 1
ZERO_UNIT_OUT = (NCH - 1) * UPC

f32 = jnp.float32
bf16 = jnp.bfloat16


def _params(*sem):
    return pltpu.CompilerParams(dimension_semantics=sem, vmem_limit_bytes=VMEM_LIMIT)


def _rms(x, g):
    ms = jnp.mean(x * x, axis=-1, keepdims=True)
    return x * lax.rsqrt(ms + EPS) * g


def _pick(i, prompt_ref, sample_ref):
    return jnp.where(i == NT - 1, sample_ref[...], prompt_ref[...])


def _stream_specs():
    return [pl.BlockSpec((T, D), lambda i: (jnp.minimum(i, NT - 2), 0)),
            pl.BlockSpec((T, D), lambda i: (0, 0))]


def _qkv_kernel(xp_ref, xs_ref, g_ref, wT_ref, qn_ref, kn_ref, cos_ref, sin_ref,
                qT_ref, ktok_ref, vtok_ref, vT_ref):
    h = _rms(_pick(pl.program_id(0), xp_ref, xs_ref), g_ref[...]).astype(bf16)
    qkvT = lax.dot_general(wT_ref[...], h, (((1,), (1,)), ((), ())),
                           preferred_element_type=f32)
    cos = cos_ref[...]
    sin = sin_ref[...]

    def norm_rope(blk, gcol):
        ms = jnp.mean(blk * blk, axis=0, keepdims=True)
        y = blk * lax.rsqrt(ms + EPS) * gcol
        y1 = y[:HALF]
        y2 = y[HALF:]
        return y1 * cos - y2 * sin, y2 * cos + y1 * sin

    qn = qn_ref[...]
    for hd in range(H):
        o1, o2 = norm_rope(qkvT[hd * HD:(hd + 1) * HD], qn)
        qT_ref[hd * HD:hd * HD + HALF, :] = (o1 * SCALE).astype(bf16)
        qT_ref[hd * HD + HALF:(hd + 1) * HD, :] = (o2 * SCALE).astype(bf16)
    kn = kn_ref[...]
    ks = []
    for j in range(KVH):
        o1, o2 = norm_rope(qkvT[H * HD + j * HD:H * HD + (j + 1) * HD], kn)
        ks += [o1, o2]
    kT = jnp.concatenate(ks, axis=0)
    ktok_ref[...] = kT.T
    vT = qkvT[(H + KVH) * HD:]
    vtok_ref[...] = vT.T
    vT_ref[...] = vT.astype(bf16)


def _qkv(xp, xs, g, wT, qn, kn, cos, sin):
    return pl.pallas_call(
        _qkv_kernel,
        grid=(NT,),
        in_specs=_stream_specs() + [
            pl.BlockSpec((1, D), lambda i: (0, 0)),
            pl.BlockSpec((QKV, D), lambda i: (0, 0)),
            pl.BlockSpec((HD, 1), lambda i: (0, 0)),
            pl.BlockSpec((HD, 1), lambda i: (0, 0)),
            pl.BlockSpec((HALF, T), lambda i: (0, i)),
            pl.BlockSpec((HALF, T), lambda i: (0, i)),
        ],
        out_specs=[
            pl.BlockSpec((None, H * HD, T), lambda i: (i, 0, 0)),
            pl.BlockSpec((T, KVH * HD), lambda i: (i, 0)),
            pl.BlockSpec((T, KVH * HD), lambda i: (i, 0)),
            pl.BlockSpec((None, KVH * HD, T), lambda i: (i, 0, 0)),
        ],
        out_shape=[
            jax.ShapeDtypeStruct((NT, H * HD, T), bf16),
            jax.ShapeDtypeStruct((N, KVH * HD), f32),
            jax.ShapeDtypeStruct((N, KVH * HD), f32),
            jax.ShapeDtypeStruct((NT, KVH * HD, T), bf16),
        ],
        compiler_params=_params("parallel"),
        name="qkv",
    )(xp, xs, g, wT, qn, kn, cos, sin)


def _attn_prompt_kernel(qT_ref, kp_ref, kc_ref, vp_ref, vc_ref, sink_ref, bias0_ref, bias_ref,
                        xp_ref, xs_ref, os_ref, wo_ref, y_ref):
    j = pl.program_id(0)

    @pl.when(j >= NQ // QB)
    def _():
        y_ref[...] = xs_ref[...] + jnp.dot(os_ref[...], wo_ref[...],
                                           preferred_element_type=f32)

    @pl.when(j < NQ // QB)
    def _():
        k_all = jnp.concatenate([kp_ref[...], kc_ref[...]], axis=0)
        v_all = jnp.concatenate([vp_ref[...], vc_ref[...]], axis=1)
        blocks = []
        for k in range(QB):
            rows = slice(k * TQ, (k + 1) * TQ)
            blocks.append(_attend_block(
                qT_ref.at[:, rows], k_all[k * TQ:(k + 2) * TQ], v_all[:, k * TQ:(k + 2) * TQ],
                sink_ref, bias0_ref if k == 0 else bias_ref, xp_ref.at[rows], wo_ref,
                y_ref.at[rows]))
        for _ in range(3):
            for blk in blocks:
                next(blk, None)


def _attend_block(qT_ref, kk, vv, sink_ref, bias_ref, x_ref, wo_ref, y_ref):
    kk = kk.astype(bf16)
    bias = jnp.concatenate([bias_ref[...]] * H, axis=1)
    qg = [jnp.concatenate([qT_ref[(g * G + hh) * HD:(g * G + hh + 1) * HD, :]
                           for hh in range(G)], axis=1) for g in range(KVH)]
    zeros = jnp.zeros_like(qg[0])
    rhs = jnp.concatenate([jnp.concatenate([qg[0], zeros], axis=1),
                           jnp.concatenate([zeros, qg[1]], axis=1)], axis=0)
    sT = jnp.dot(kk, rhs, preferred_element_type=f32) + bias
    yield
    sink = jnp.concatenate([sink_ref[0], sink_ref[1]], axis=1)
    m = jnp.maximum(jnp.max(sT, axis=0, keepdims=True), sink)
    p = jnp.exp(sT - m)
    l = jnp.sum(p, axis=0, keepdims=True) + jnp.exp(sink - m)
    p = (p * (1.0 / l)).astype(bf16)
    yield
    pieces = []
    for g in range(KVH):
        oT = jnp.dot(vv[g * HD:(g + 1) * HD, :], p[:, g * G * TQ:(g + 1) * G * TQ],
                     preferred_element_type=f32)
        pieces += [oT[:, hh * TQ:(hh + 1) * TQ] for hh in range(G)]
    oT_all = jnp.concatenate(pieces, axis=0)
    y_ref[...] = x_ref[...] + jnp.dot(oT_all.T.astype(bf16), wo_ref[...],
                                      preferred_element_type=f32)


def _band_bias():
    s = np.arange(2 * TQ)[:, None]
    t = np.arange(TQ)[None, :]
    dist = t + TQ - s
    band = (dist >= 0) & (dist <= WIN)
    first = band & (s >= TQ)
    return jnp.asarray(np.where(np.stack([first, band]), 0.0, -np.inf), f32)


def _attn_prompt(qT, ktok, vT, sink_rows, xp, xs, o_s, wo):
    steps = NQ // QB
    rows = QB * TQ
    assert rows == T

    def cur(j):
        return jnp.minimum(j, steps - 1)

    def prev(j):
        return jnp.maximum(cur(j) * QB - 1, 0)

    def prev_tile(j):
        return jnp.maximum(cur(j) - 1, 0)

    def sample(j):
        return jnp.maximum(j - steps, 0)

    bias = _band_bias()
    return pl.pallas_call(
        _attn_prompt_kernel,
        grid=(N // rows,),
        in_specs=[
            pl.BlockSpec((None, H * HD, rows), lambda j: (cur(j), 0, 0)),
            pl.BlockSpec((TQ, KVH * HD), lambda j: (prev(j), 0)),
            pl.BlockSpec((rows, KVH * HD), lambda j: (cur(j), 0)),
            pl.BlockSpec((None, KVH * HD, TQ), lambda j: (prev_tile(j), 0, QB - 1)),
            pl.BlockSpec((None, KVH * HD, rows), lambda j: (cur(j), 0, 0)),
            pl.BlockSpec((KVH, 1, G * TQ), lambda j: (0, 0, 0)),
            pl.BlockSpec((None, 2 * TQ, TQ), lambda j: (jnp.minimum(j, 1), 0, 0)),
            pl.BlockSpec((None, 2 * TQ, TQ), lambda j: (1, 0, 0)),
            pl.BlockSpec((rows, D), lambda j: (cur(j), 0)),
            pl.BlockSpec((rows, D), lambda j: (sample(j), 0)),
            pl.BlockSpec((rows, H * HD), lambda j: (sample(j), 0)),
            pl.BlockSpec((D, D), lambda j: (0, 0)),
        ],
        out_specs=pl.BlockSpec((rows, D), lambda j: (j, 0)),
        out_shape=jax.ShapeDtypeStruct((N, D), f32),
        compiler_params=_params("parallel"),
        name="attn_prompt",
    )(qT, ktok, ktok, vT, vT, sink_rows, bias, bias, xp, xs, o_s, wo)


def _attn_sample_kernel(q_ref, kc_ref, vc_ref, kn_ref, vn_ref, sink_ref,
                        o_ref, knew_ref, vnew_ref):
    rows = H * TS
    t1 = lax.broadcasted_iota(jnp.int32, (1, rows, WIN), 1) & (TS - 1)
    s1 = lax.broadcasted_iota(jnp.int32, (1, rows, WIN), 2)
    valid1 = s1 >= t1
    t2 = lax.broadcasted_iota(jnp.int32, (1, rows, 8), 1) & (TS - 1)
    s2 = lax.broadcasted_iota(jnp.int32, (1, rows, 8), 2)
    valid2 = s2 <= t2
    sink = sink_ref[...][None]
    q = q_ref[...]
    kc = kc_ref[...]
    vc = vc_ref[...]
    kn = kn_ref[...]
    vn = vn_ref[...]
    sc = jnp.einsum('bqd,bkd->bqk', q, kc.astype(bf16), preferred_element_type=f32)
    sn = jnp.einsum('bqd,bkd->bqk', q, kn.astype(bf16), preferred_element_type=f32)
    sc = jnp.where(valid1, sc, -jnp.inf)
    sn = jnp.where(valid2, sn, -jnp.inf)
    m = jnp.maximum(jnp.maximum(jnp.max(sc, axis=-1, keepdims=True),
                                jnp.max(sn, axis=-1, keepdims=True)), sink)
    pc = jnp.exp(sc - m)
    pn = jnp.exp(sn - m)
    l = (jnp.sum(pc, axis=-1, keepdims=True) + jnp.sum(pn, axis=-1, keepdims=True)
         + jnp.exp(sink - m))
    r = 1.0 / l
    o_ref[...] = (jnp.einsum('bqk,bkd->bqd', (pc * r).astype(bf16), vc.astype(bf16),
                             preferred_element_type=f32)
                  + jnp.einsum('bqk,bkd->bqd', (pn * r).astype(bf16), vn.astype(bf16),
                               preferred_element_type=f32))
    knew_ref[:, :WIN - TS, :] = kc[:, TS:, :]
    knew_ref[:, WIN - TS:, :] = kn[:, :TS, :]
    vnew_ref[:, :WIN - TS, :] = vc[:, TS:, :]
    vnew_ref[:, WIN - TS:, :] = vn[:, :TS, :]


def _attn_sample(qbd, kc, vc, kn, vn, sink_col):
    rows = H * TS
    cache_spec = pl.BlockSpec((BB, WIN, KVH * HD), lambda i: (i, 0, 0))
    new_spec = pl.BlockSpec((BB, 8, KVH * HD), lambda i: (i, 0, 0))
    cache_shape = jax.ShapeDtypeStruct((NB, WIN, KVH * HD), f32)
    return pl.pallas_call(
        _attn_sample_kernel,
        grid=(NB // BB,),
        in_specs=[
            pl.BlockSpec((BB, rows, KVH * HD), lambda i: (i, 0, 0)),
            cache_spec, cache_spec, new_spec, new_spec,
            pl.BlockSpec((rows, 1), lambda i: (0, 0)),
        ],
        out_specs=[pl.BlockSpec((BB, rows, KVH * HD), lambda i: (i, 0, 0)),
                   cache_spec, cache_spec],
        out_shape=[jax.ShapeDtypeStruct((NB, rows, KVH * HD), f32), cache_shape, cache_shape],
        compiler_params=_params("parallel"),
        name="attn_sample",
    )(qbd, kc, vc, kn, vn, sink_col)


def _dispatch_kernel(x_ref, g_ref, w_ref, b_ref, upper_ref, xc_ref, meta_ref, cnt_ref):
    tiles = [_dispatch_tile(x_ref.at[k * TM:(k + 1) * TM], g_ref, w_ref, b_ref, upper_ref,
                            xc_ref.at[k * RC:(k + 1) * RC], meta_ref.at[k * TM:(k + 1) * TM],
                            cnt_ref.at[k * NE:(k + 1) * NE]) for k in range(DT)]
    for _ in range(3):
        for t in tiles:
            next(t, None)


def _dispatch_tile(x_ref, g_ref, w_ref, b_ref, upper_ref, xc_ref, meta_ref, cnt_ref):
    h_hi = _rms(x_ref[...], g_ref[...]).astype(bf16)
    logits = jnp.dot(h_hi, w_ref[...], preferred_element_type=f32) + b_ref[...]
    yield
    lt = logits.T
    inf = jnp.inf
    row8 = lax.broadcasted_iota(jnp.int32, (8, TM), 0).astype(f32)
    gl = jnp.where(row8 < NGRP, lt[0:8], -inf)
    gmax = jnp.max(gl, axis=0, keepdims=True)
    gsel = jnp.min(jnp.where(gl == gmax, row8, 8.0), axis=0, keepdims=True)
    g_w = 1.0 / jnp.sum(jnp.exp(gl - gmax), axis=0, keepdims=True)
    row = lax.broadcasted_iota(jnp.int32, (NE, TM), 0)
    row_f = row.astype(f32)
    el = jnp.where((row >> 2).astype(f32) == gsel, lt[ROW_E:ROW_E + NE], -inf)
    v1 = jnp.max(el, axis=0, keepdims=True)
    i1 = jnp.min(jnp.where(el == v1, row_f, float(NE)), axis=0, keepdims=True)
    el2 = jnp.where(row_f == i1, -inf, el)
    v2 = jnp.max(el2, axis=0, keepdims=True)
    i2 = jnp.min(jnp.where(el2 == v2, row_f, float(NE)), axis=0, keepdims=True)
    e1 = jnp.exp(v2 - v1)
    den = 1.0 + e1
    w1 = (1.0 / den) * g_w
    w2 = (e1 / den) * g_w

    m1 = row_f == i1
    m2 = row_f == i2
    sel = jnp.where(m1 | m2, 1.0, 0.0)
    ranks = jnp.dot(sel.astype(bf16), upper_ref[...],
                    preferred_element_type=f32)
    counts = jnp.sum(sel, axis=1, keepdims=True)
    padded = jnp.floor((counts + (UNIT - 1.0)) * (1.0 / UNIT)) * UNIT
    e_i = lax.broadcasted_iota(jnp.int32, (NE, NE), 0)
    f_i = lax.broadcasted_iota(jnp.int32, (NE, NE), 1)
    below = jnp.where(f_i < e_i, 1.0, 0.0).astype(bf16)
    seg = jnp.dot(below, jnp.broadcast_to(padded, (NE, LANES)).astype(bf16),
                  preferred_element_type=f32)[:, 0:1]
    posall = seg + ranks
    pos1 = jnp.sum(jnp.where(m1, posall, 0.0), axis=0, keepdims=True)
    pos2 = jnp.sum(jnp.where(m2, posall, 0.0), axis=0, keepdims=True)
    cnt_ref[...] = jnp.broadcast_to(counts, (NE, LANES))

    w1_hi = w1.astype(bf16).astype(f32)
    w2_hi = w2.astype(bf16).astype(f32)
    slab = jnp.zeros((8, TM), f32)
    for k, r in enumerate((pos1, pos2, w1_hi, w1 - w1_hi, w2_hi, w2 - w2_hi, i1)):
        slab = jnp.where(row8 == k, r, slab)
    meta = jnp.concatenate([slab, jnp.zeros((LANES - 8, TM), f32)], axis=0).T
    meta_ref[...] = meta
    yield

    rr = lax.broadcasted_iota(jnp.int32, (RC, TM), 0).astype(f32)
    onehot = jnp.where((rr == pos1) | (rr == pos2), 1.0, 0.0).astype(bf16)
    h_aug = jnp.concatenate([h_hi, meta.astype(bf16)], axis=1)
    xc_ref[...] = jnp.dot(onehot, h_aug, preferred_element_type=f32).astype(bf16)


def _dispatch(x, g, w, b):
    upper = jnp.asarray(np.triu(np.ones((TM, TM), np.float32), 1), bf16)
    return pl.pallas_call(
        _dispatch_kernel,
        grid=(NTM // DT,),
        in_specs=[
            pl.BlockSpec((DT * TM, D), lambda i: (i, 0)),
            pl.BlockSpec((1, D), lambda i: (0, 0)),
            pl.BlockSpec((D, LANES), lambda i: (0, 0)),
            pl.BlockSpec((1, LANES), lambda i: (0, 0)),
            pl.BlockSpec((TM, TM), lambda i: (0, 0)),
        ],
        out_specs=[
            pl.BlockSpec((DT * RC, XW), lambda i: (i, 0)),
            pl.BlockSpec((DT * TM, LANES), lambda i: (i, 0)),
            pl.BlockSpec((DT * NE, LANES), lambda i: (i, 0)),
        ],
        out_shape=[
            jax.ShapeDtypeStruct((NTM * RC, XW), bf16),
            jax.ShapeDtypeStruct((N, LANES), f32),
            jax.ShapeDtypeStruct((NTM * NE, LANES), f32),
        ],
        compiler_params=_params("parallel"),
        name="dispatch",
    )(x, g, w, b, upper)


def _dispatch_tables(cnt):
    i32 = jnp.int32
    n = cnt.reshape(NTM, NE, LANES)[:, :, 0].astype(i32)
    units = (n + UNIT - 1) // UNIT
    seg_end = jnp.cumsum(units, axis=1)
    seg_start = seg_end - units
    col_end = jnp.cumsum(units, axis=0)
    col_start = col_end - units
    chunks = (col_end[-1] + UPC - 1) // UPC
    ch_end = jnp.cumsum(chunks)
    ch_start = ch_end - chunks
    nused = ch_end[-1]
    c = jnp.arange(NCH, dtype=i32)
    eid = jnp.minimum(jnp.sum((ch_end[None, :] <= c[:, None]).astype(i32), axis=1), NE - 1)

    src0 = jnp.arange(NTM, dtype=i32)[:, None] * UPT + seg_start
    dst0 = ch_start[None, :] * UPC + col_start
    k = jnp.arange((NCH + NSLOT - 1) * UPC, dtype=i32)[:, None, None]
    inside = (k >= dst0[None]) & (k < (dst0 + units)[None])
    found = jnp.sum(inside.astype(i32), axis=(1, 2))
    shift = jnp.sum(jnp.where(inside, (src0 - dst0)[None], 0), axis=(1, 2))
    ffn_src = jnp.where(found > 0, k[:, 0, 0] + shift, ZERO_UNIT_IN)

    v = jnp.arange(UPT, dtype=i32)[None, :, None]
    inside_v = (v >= seg_start[:, None, :]) & (v < seg_end[:, None, :])
    found_v = jnp.sum(inside_v.astype(i32), axis=2)
    shift_v = jnp.sum(jnp.where(inside_v, (dst0 - seg_start)[:, None, :], 0), axis=2)
    comb_src = jnp.where(found_v > 0, v[:, :, 0] + shift_v, ZERO_UNIT_OUT).reshape(-1)
    return eid, nused.reshape(1), ffn_src, comb_src


def _unit_copies(src_ref, base, n_units, src_hbm, stage, slot, sem):
    out = []
    for j in range(n_units):
        row = pl.multiple_of(src_ref[base + j] * UNIT, UNIT)
        out.append(pltpu.make_async_copy(
            src_hbm.at[pl.ds(row, UNIT), :],
            stage.at[slot, pl.ds(j * UNIT, UNIT), :],
            sem.at[slot]))
    return out


def _start_gathers(step, n_steps, n_units, src_ref, src_hbm, stage, sem):
    ahead = NSLOT - 1

    def start(s, slot):
        for cp in _unit_copies(src_ref, s * n_units, n_units, src_hbm, stage, slot, sem):
            cp.start()

    for s in range(ahead):
        @pl.when((step == 0) & (s < n_steps))
        def _():
            start(s, s)

    @pl.when(step + ahead < n_steps)
    def _():
        start(step + ahead, lax.rem(step + ahead, NSLOT))


def _wait_gather(step, n_units, src_ref, src_hbm, stage, sem):
    slot = lax.rem(step, NSLOT)
    for cp in _unit_copies(src_ref, step * n_units, n_units, src_hbm, stage, slot, sem):
        cp.wait()
    return slot


def _ffn_kernel(eid_ref, nused_ref, src_ref, xc_hbm, wg_ref, wu_ref, wd_ref, o_ref,
                stage, sem, wg_b, wu_b, wd_b):
    c = pl.program_id(0)
    nused = nused_ref[0]
    ahead = NSLOT - 1

    def start(s):
        for cp in _unit_copies(src_ref, s * UPC, UPC, xc_hbm, stage, lax.rem(s, NSLOT), sem):
            cp.start()

    @pl.when(c == 0)
    def _():
        for s in range(ahead):
            start(s)

    @pl.when(c < nused)
    def _():
        slot = _wait_gather(c, UPC, src_ref, xc_hbm, stage, sem)

        @pl.when((c == 0) | (eid_ref[c] != eid_ref[jnp.maximum(c - 1, 0)]))
        def _():
            wg_b[...] = wg_ref[...].astype(bf16)
            wu_b[...] = wu_ref[...].astype(bf16)
            wd_b[...] = wd_ref[...].astype(bf16)

        def block(r):
            xs = stage[slot, r:r + CHB, :]
            x = xs[:, :D]
            a = jnp.dot(x, wg_b[...], preferred_element_type=f32)
            u = jnp.dot(x, wu_b[...], preferred_element_type=f32)
            yield
            gb = xs[:, D:].astype(f32)
            first = gb[:, 6:7] == eid_ref[c].astype(f32)
            gate = jnp.where(first, gb[:, 2:3] + gb[:, 3:4], gb[:, 4:5] + gb[:, 5:6])
            act = (a * (1.0 / (1.0 + jnp.exp(-a)))) * u * gate
            o_ref[r:r + CHB, :] = jnp.dot(act.astype(bf16), wd_b[...],
                                          preferred_element_type=f32).astype(bf16)

        blocks = [block(r) for r in range(0, CH, CHB)]
        for _ in range(2):
            for blk in blocks:
                next(blk, None)
        start(c + ahead)

    @pl.when(c >= nused)
    def _():
        @pl.when(c < nused + ahead)
        def _():
            _wait_gather(c, UPC, src_ref, xc_hbm, stage, sem)

        o_ref[...] = jnp.zeros((CH, D), bf16)


def _ffn(layer, eid, nused, ffn_src, xc, wg, wu, wd):
    return pl.pallas_call(
        _ffn_kernel,
        grid_spec=pltpu.PrefetchScalarGridSpec(
            num_scalar_prefetch=3,
            grid=(NCH,),
            in_specs=[
                pl.BlockSpec(memory_space=pl.ANY),
                pl.BlockSpec((None, None, D, F), lambda c, eid, nu, src: (layer, eid[c], 0, 0)),
                pl.BlockSpec((None, None, D, F), lambda c, eid, nu, src: (layer, eid[c], 0, 0)),
                pl.BlockSpec((None, None, F, D), lambda c, eid, nu, src: (layer, eid[c], 0, 0)),
            ],
            out_specs=pl.BlockSpec((CH, D), lambda c, eid, nu, src: (c, 0)),
            scratch_shapes=[
                pltpu.VMEM((NSLOT, CH, XW), bf16),
                pltpu.SemaphoreType.DMA((NSLOT,)),
                pltpu.VMEM((D, F), bf16),
                pltpu.VMEM((D, F), bf16),
                pltpu.VMEM((F, D), bf16),
            ],
        ),
        out_shape=jax.ShapeDtypeStruct((NCH * CH, D), bf16),
        compiler_params=_params("arbitrary"),
        name="ffn",
    )(eid, nused, ffn_src, xc, wg, wu, wd)


def _combine_tile(x, meta, rows):
    li = lax.broadcasted_iota(jnp.int32, (TM, RC), 1).astype(f32)
    pt = jnp.where((li == meta[:, 0:1]) | (li == meta[:, 1:2]), 1.0, 0.0).astype(bf16)
    return x + jnp.dot(pt, rows, preferred_element_type=f32)


def _combine_kernel(src_ref, x_ref, meta_ref, o_hbm, yp_ref, ys_ref, stage, sem):
    i = pl.program_id(0)
    tiles = T // TM
    _start_gathers(i, NT, tiles * UPT, src_ref, o_hbm, stage, sem)
    slot = _wait_gather(i, tiles * UPT, src_ref, o_hbm, stage, sem)
    ys = [_combine_tile(x_ref[k * TM:(k + 1) * TM, :], meta_ref[k * TM:(k + 1) * TM, :],
                        stage[slot, k * RC:(k + 1) * RC, :]) for k in range(tiles)]
    y = jnp.concatenate(ys, axis=0)

    @pl.when(i < NT - 1)
    def _():
        yp_ref[...] = y

    @pl.when(i == NT - 1)
    def _():
        ys_ref[...] = y


def _combine(comb_src, x, meta, o_sorted):
    out_specs = [pl.BlockSpec((T, D), lambda i, src: (jnp.minimum(i, NT - 2), 0)),
                 pl.BlockSpec((T, D), lambda i, src: (0, 0))]
    out_shape = [jax.ShapeDtypeStruct((NP, D), f32), jax.ShapeDtypeStruct((NS, D), f32)]
    return pl.pallas_call(
        _combine_kernel,
        grid_spec=pltpu.PrefetchScalarGridSpec(
            num_scalar_prefetch=1,
            grid=(NT,),
            in_specs=[
                pl.BlockSpec((T, D), lambda i, src: (i, 0)),
                pl.BlockSpec((T, LANES), lambda i, src: (i, 0)),
                pl.BlockSpec(memory_space=pl.ANY),
            ],
            out_specs=out_specs,
            scratch_shapes=[
                pltpu.VMEM((NSLOT, (T // TM) * RC, D), bf16),
                pltpu.SemaphoreType.DMA((NSLOT,)),
            ],
        ),
        out_shape=out_shape,
        compiler_params=_params("arbitrary"),
        name="combine",
    )(comb_src, x, meta, o_sorted)


def _conv_kernel(src_ref, x1_ref, meta_ref, o_hbm, g_ref, win_ref, cw_ref, wout_ref,
                 p1_ref, p2_ref, y_ref, cu_ref, pad_ref, x_ref, stage, sem):
    i = pl.program_id(0)

    @pl.when(i == 0)
    def _():
        pad_ref[0:8, :] = jnp.zeros((8, D), f32)

    tiles = T // TM
    ahead = NSLOT - 1

    def start(s):
        for cp in _unit_copies(src_ref, s * tiles * UPT, tiles * UPT, o_hbm, stage,
                               lax.rem(s, NSLOT), sem):
            cp.start()

    @pl.when(i == 0)
    def _():
        for s in range(ahead):
            start(s)

    slot = _wait_gather(i, tiles * UPT, src_ref, o_hbm, stage, sem)

    for k in range(tiles):
        rows = slice(k * TM, (k + 1) * TM)
        x_ref[rows, :] = _combine_tile(x1_ref[rows, :], meta_ref[rows, :],
                                       stage[slot, k * RC:(k + 1) * RC, :])

    blocks = [(r, r + T // 2) for r in (0, T // 2)]
    gates = []
    for r0, r1 in blocks:
        h = _rms(x_ref[r0:r1, :], g_ref[...]).astype(bf16)
        bcu = jnp.dot(h, win_ref[...], preferred_element_type=f32)
        gates.append(bcu[:, :D])
        cu = bcu[:, D:2 * D] * bcu[:, 2 * D:]
        pad_ref[8 + r0:8 + r1, :] = cu
        cu_ref[r0:r1, :] = cu
    t = lax.broadcasted_iota(jnp.int32, (T // 2, 1), 0) & (TS - 1)
    is_sample = i == NT - 1
    cw = cw_ref[...]
    for (r0, r1), b in zip(blocks, gates):
        m1 = jnp.where(is_sample & (t == 0), p1_ref[r0:r1, :], pad_ref[7 + r0:7 + r1, :])
        m2 = jnp.where(is_sample & (t < 2), p2_ref[r0:r1, :], pad_ref[6 + r0:6 + r1, :])
        conv = cw[0:1] * m2 + cw[1:2] * m1 + cw[2:3] * pad_ref[8 + r0:8 + r1, :]
        y = jnp.dot((b * conv).astype(bf16), wout_ref[...], preferred_element_type=f32)
        y_ref[r0:r1, :] = x_ref[r0:r1, :] + y
    pad_ref[0:8, :] = pad_ref[T:T + 8, :]
    start(i + ahead)

    @pl.when(i == NT - 1)
    def _():
        for s in range(ahead):
            _wait_gather(i + 1 + s, tiles * UPT, src_ref, o_hbm, stage, sem)


def _conv(comb_src, x1, meta, o_sorted, g, win, cw, wout, p1, p2):
    const = dict(pipeline_mode=pl.Buffered(1))
    comb_src = jnp.concatenate([
        comb_src, jnp.full(((NSLOT - 1) * (T // TM) * UPT,), ZERO_UNIT_OUT, jnp.int32)])
    return pl.pallas_call(
        _conv_kernel,
        grid_spec=pltpu.PrefetchScalarGridSpec(
            num_scalar_prefetch=1,
            grid=(NT,),
            in_specs=[
                pl.BlockSpec((T, D), lambda i, src: (i, 0)),
                pl.BlockSpec((T, LANES), lambda i, src: (i, 0)),
                pl.BlockSpec(memory_space=pl.ANY),
                pl.BlockSpec((1, D), lambda i, src: (0, 0)),
                pl.BlockSpec((D, 3 * D), lambda i, src: (0, 0), **const),
                pl.BlockSpec((3, D), lambda i, src: (0, 0)),
                pl.BlockSpec((D, D), lambda i, src: (0, 0), **const),
                pl.BlockSpec((T, D), lambda i, src: (0, 0), **const),
                pl.BlockSpec((T, D), lambda i, src: (0, 0), **const),
            ],
            out_specs=[
                pl.BlockSpec((T, D), lambda i, src: (i, 0)),
                pl.BlockSpec((T, D), lambda i, src: (jnp.where(i == NT - 1, 1, 0), 0)),
            ],
            scratch_shapes=[
                pltpu.VMEM((T + 8, D), f32),
                pltpu.VMEM((T, D), f32),
                pltpu.VMEM((NSLOT, (T // TM) * RC, D), bf16),
                pltpu.SemaphoreType.DMA((NSLOT,)),
            ],
        ),
        out_shape=[
            jax.ShapeDtypeStruct((N, D), f32),
            jax.ShapeDtypeStruct((2 * T, D), f32),
        ],
        compiler_params=_params("arbitrary"),
        name="conv",
    )(comb_src, x1, meta, o_sorted, g, win, cw, wout, p1, p2)


def _rope_tables():
    inv_freq = THETA ** (-jnp.arange(HALF, dtype=f32) / HALF)
    pos = jnp.concatenate([
        jnp.arange(NP, dtype=jnp.int32),
        PAST + jnp.tile(jnp.arange(TS, dtype=jnp.int32), NB),
    ]).astype(f32)
    ang = inv_freq[:, None] * pos[None, :]
    return jnp.cos(ang), jnp.sin(ang)


def _router_weights(w_group, b_group, w_router, b_router):
    gap = ROW_E - NGRP
    pad = LANES - ROW_E - NE
    w = jnp.concatenate([w_group, jnp.zeros((D, gap), f32), w_router,
                         jnp.zeros((D, pad), f32)], axis=1)
    b = jnp.concatenate([b_group, jnp.zeros((gap,), f32), b_router,
                         jnp.zeros((pad,), f32)])[None, :]
    return w.astype(bf16), b


def _moe_experts(x, i, norm_ffn, w_group, b_group, w_router, b_router, w_gate, w_up, w_down):
    w, b = _router_weights(w_group[i], b_group[i], w_router[i], b_router[i])
    xc, meta, cnt = _dispatch(x, norm_ffn[i][None, :], w, b)
    eid, nused, ffn_src, comb_src = _dispatch_tables(cnt)
    o_sorted = _ffn(i, eid, nused, ffn_src, xc, w_gate, w_up, w_down)
    return comb_src, meta, o_sorted


def kernel(x_prompt, x_sample, cache_k, cache_v, state_conv, norm_mix, w_qkv, q_norm, k_norm,
           sinks, w_o, w_in, conv_w, w_out, norm_ffn, w_group, b_group, w_router, b_router,
           w_gate, w_up, w_down):
    xp = x_prompt.reshape(NP, D)
    xs = x_sample.reshape(NS, D)
    moe_w = (norm_ffn, w_group, b_group, w_router, b_router, w_gate, w_up, w_down)

    cos, sin = _rope_tables()
    qT, ktok, vtok, vT = _qkv(xp, xs, norm_mix[0][None, :], w_qkv[0].T.astype(bf16),
                              q_norm[0][:, None], k_norm[0][:, None], cos, sin)
    sink_rows = jnp.repeat(sinks[0].reshape(KVH, G), TQ, axis=1)[:, None, :]

    qs = qT[NT - 1].reshape(KVH, G, HD, NB, TS).transpose(3, 0, 1, 4, 2)
    zq = jnp.zeros_like(qs[:, 0])
    qbd = jnp.stack([jnp.concatenate([qs[:, 0], zq], axis=-1),
                     jnp.concatenate([zq, qs[:, 1]], axis=-1)], axis=1)
    qbd = qbd.reshape(NB, H * TS, KVH * HD)
    k_new = ktok[NP:].reshape(NB, TS, KVH * HD)
    v_new = vtok[NP:].reshape(NB, TS, KVH * HD)
    pad4 = jnp.zeros((NB, 8 - TS, KVH * HD), f32)
    kc = cache_k[0].reshape(NB, WIN, KVH * HD)
    vc = cache_v[0].reshape(NB, WIN, KVH * HD)
    sink_col = jnp.repeat(sinks[0], TS)[:, None]
    o_s, kc_new, vc_new = _attn_sample(qbd, kc, vc, jnp.concatenate([k_new, pad4], axis=1),
                                       jnp.concatenate([v_new, pad4], axis=1), sink_col)
    o_s = o_s.reshape(NB, KVH, G, TS, KVH, HD)
    o_s = jnp.stack([o_s[:, 0, :, :, 0], o_s[:, 1, :, :, 1]], axis=1)
    o_s = o_s.transpose(0, 3, 1, 2, 4).reshape(NS, H * HD).astype(bf16)
    x = _attn_prompt(qT, ktok, vT, sink_rows, xp, xs, o_s, w_o[0].astype(bf16))
    comb_src, meta, o_sorted = _moe_experts(x, 0, *moe_w)

    new_k_prompt = ktok[NP - WIN:NP].reshape(1, 1, WIN, KVH, HD)
    new_v_prompt = vtok[NP - WIN:NP].reshape(1, 1, WIN, KVH, HD)
    new_k_sample = kc_new.reshape(1, NB, WIN, KVH, HD)
    new_v_sample = vc_new.reshape(1, NB, WIN, KVH, HD)

    st = state_conv[0]
    z = jnp.zeros((NB, 1, D), f32)
    p1 = jnp.concatenate([st[:, 1:2], z, z, z], axis=1).reshape(NS, D)
    p2 = jnp.concatenate([st[:, 0:1], st[:, 1:2], z, z], axis=1).reshape(NS, D)
    x, cu = _conv(comb_src, x, meta, o_sorted, norm_mix[1][None, :], w_in[0].astype(bf16),
                  conv_w[0], w_out[0].astype(bf16), p1, p2)
    comb_src, meta, o_sorted = _moe_experts(x, 1, *moe_w)
    y_prompt, y_sample = _combine(comb_src, x, meta, o_sorted)

    new_conv_prompt = cu[T - 2:T].reshape(1, 1, 2, D)
    new_conv_sample = cu[T:].reshape(NB, TS, D)[:, TS - 2:][None]

    y_prompt = y_prompt.reshape(1, NP, D)
    y_sample = y_sample.reshape(NB, TS, D)
    return (y_prompt, y_sample, new_k_prompt, new_v_prompt, new_conv_prompt,
            new_k_sample, new_v_sample, new_conv_sample)
```

```python
import functools

import jax
import jax.numpy as jnp
import numpy as np
from jax import lax
from jax.experimental import pallas as pl
from jax.experimental.pallas import tpu as pltpu

D = 1024
NP = 16384
NB = 128
TS = 4
NS = NB * TS
N = NP + NS
PAST = 16384
H = 16
KVH = 2
G = H // KVH
HD = 64
HALF = HD // 2
QKV = (H + 2 * KVH) * HD
WIN = 128
THETA = 10000.0
NGRP = 4
EPG = 4
NE = NGRP * EPG
TOPK = 2
ROW_E = 8
F = 512
EPS = 1e-6
SCALE = HD ** -0.5

T = 512
NT = N // T
TQ = 128
NQ = NP // TQ
QB = 4
BB = 16
LANES = 128
V7X_VMEM_BYTES = 64 * 1024 * 1024
VMEM_LIMIT = V7X_VMEM_BYTES - 14 * 1024 * 1024

TM = 256
NTM = N // TM
DT = 6
UNIT = 16
RC = TOPK * TM + NE * UNIT
UPT = RC // UNIT
XW = D + LANES
CH = 512
CHB = 256
UPC = CH // UNIT
NSLOT = 3
NCH = -(-(NTM * (UPT - 1) + NE * (UPC - 1)) // UPC) + NSLOT - 1
ZERO_UNIT_IN = UPT - 1
ZERO_UNIT_OUT = (NCH - 1) * UPC

f32 = jnp.float32
bf16 = jnp.bfloat16


def _params(*sem):
    return pltpu.CompilerParams(dimension_semantics=sem, vmem_limit_bytes=VMEM_LIMIT)


def _rms(x, g):
    ms = jnp.mean(x * x, axis=-1, keepdims=True)
    return x * lax.rsqrt(ms + EPS) * g


def _pick(i, prompt_ref, sample_ref):
    return jnp.where(i == NT - 1, sample_ref[...], prompt_ref[...])


def _stream_specs():
    return [pl.BlockSpec((T, D), lambda i: (jnp.minimum(i, NT - 2), 0)),
            pl.BlockSpec((T, D), lambda i: (0, 0))]


def _split(x):
    hi = x.astype(bf16)
    return hi, (x - hi.astype(f32)).astype(bf16)


def _qkv_kernel(xp_ref, xs_ref, g_ref, wT_ref, wTlo_ref, qn_ref, kn_ref, cos_ref, sin_ref,
                qT_ref, ktok_ref, vtok_ref, vT_ref, qs_ref, acc_ref):
    i = pl.program_id(0)
    hf = _rms(_pick(i, xp_ref, xs_ref), g_ref[...])
    h, h_lo = _split(hf)
    nt = (((1,), (1,)), ((), ()))
    acc_ref[...] = lax.dot_general(wT_ref[...], h, nt, preferred_element_type=f32)

    @pl.when(i == NT - 1)
    def _():
        acc_ref[...] += (lax.dot_general(wT_ref[...], h_lo, nt, preferred_element_type=f32)
                         + lax.dot_general(wTlo_ref[...], h, nt, preferred_element_type=f32))

    qkvT = acc_ref
    cos = cos_ref[...]
    sin = sin_ref[...]

    def norm_rope(blk, gcol):
        ms = jnp.mean(blk * blk, axis=0, keepdims=True)
        y = blk * lax.rsqrt(ms + EPS) * gcol
        y1 = y[:HALF]
        y2 = y[HALF:]
        return y1 * cos - y2 * sin, y2 * cos + y1 * sin

    qn = qn_ref[...]
    for hd in range(H):
        o1, o2 = norm_rope(qkvT[hd * HD:(hd + 1) * HD], qn)
        qT_ref[hd * HD:hd * HD + HALF, :] = (o1 * SCALE).astype(bf16)
        qT_ref[hd * HD + HALF:(hd + 1) * HD, :] = (o2 * SCALE).astype(bf16)
        qs_ref[hd * HD:hd * HD + HALF, :] = o1 * SCALE
        qs_ref[hd * HD + HALF:(hd + 1) * HD, :] = o2 * SCALE
    kn = kn_ref[...]
    ks = []
    for j in range(KVH):
        o1, o2 = norm_rope(qkvT[H * HD + j * HD:H * HD + (j + 1) * HD], kn)
        ks += [o1, o2]
    kT = jnp.concatenate(ks, axis=0)
    ktok_ref[...] = kT.T
    vT = qkvT[(H + KVH) * HD:]
    vtok_ref[...] = vT.T
    vT_ref[...] = vT.astype(bf16)


def _qkv(xp, xs, g, wT, wT_lo, qn, kn, cos, sin):
    return pl.pallas_call(
        _qkv_kernel,
        grid=(NT,),
        in_specs=_stream_specs() + [
            pl.BlockSpec((1, D), lambda i: (0, 0)),
            pl.BlockSpec((QKV, D), lambda i: (0, 0)),
            pl.BlockSpec((QKV, D), lambda i: (0, 0)),
            pl.BlockSpec((HD, 1), lambda i: (0, 0)),
            pl.BlockSpec((HD, 1), lambda i: (0, 0)),
            pl.BlockSpec((HALF, T), lambda i: (0, i)),
            pl.BlockSpec((HALF, T), lambda i: (0, i)),
        ],
        out_specs=[
            pl.BlockSpec((None, H * HD, T), lambda i: (i, 0, 0)),
            pl.BlockSpec((T, KVH * HD), lambda i: (i, 0)),
            pl.BlockSpec((T, KVH * HD), lambda i: (i, 0)),
            pl.BlockSpec((None, KVH * HD, T), lambda i: (i, 0, 0)),
            pl.BlockSpec((H * HD, T), lambda i: (0, 0)),
        ],
        out_shape=[
            jax.ShapeDtypeStruct((NT, H * HD, T), bf16),
            jax.ShapeDtypeStruct((N, KVH * HD), f32),
            jax.ShapeDtypeStruct((N, KVH * HD), f32),
            jax.ShapeDtypeStruct((NT, KVH * HD, T), bf16),
            jax.ShapeDtypeStruct((H * HD, T), f32),
        ],
        scratch_shapes=[pltpu.VMEM((QKV, T), f32)],
        compiler_params=_params("arbitrary"),
        name="qkv",
    )(xp, xs, g, wT, wT_lo, qn, kn, cos, sin)


def _attn_prompt_kernel(qT_ref, kp_ref, kc_ref, vp_ref, vc_ref, sink_ref, bias0_ref, bias_ref,
                        xp_ref, xs_ref, os_ref, wo_ref, wolo_ref, y_ref):
    j = pl.program_id(0)

    @pl.when(j >= NQ // QB)
    def _():
        o_hi, o_lo = _split(os_ref[...])
        wo = wo_ref[...]
        y_ref[...] = (xs_ref[...] + jnp.dot(o_hi, wo, preferred_element_type=f32)
                      + jnp.dot(o_lo, wo, preferred_element_type=f32)
                      + jnp.dot(o_hi, wolo_ref[...], preferred_element_type=f32))

    @pl.when(j < NQ // QB)
    def _():
        k_all = jnp.concatenate([kp_ref[...], kc_ref[...]], axis=0)
        v_all = jnp.concatenate([vp_ref[...], vc_ref[...]], axis=1)
        blocks = []
        for k in range(QB):
            rows = slice(k * TQ, (k + 1) * TQ)
            blocks.append(_attend_block(
                qT_ref.at[:, rows], k_all[k * TQ:(k + 2) * TQ], v_all[:, k * TQ:(k + 2) * TQ],
                sink_ref, bias0_ref if k == 0 else bias_ref, xp_ref.at[rows], wo_ref,
                y_ref.at[rows]))
        for _ in range(3):
            for blk in blocks:
                next(blk, None)


def _attend_block(qT_ref, kk, vv, sink_ref, bias_ref, x_ref, wo_ref, y_ref):
    kk = kk.astype(bf16)
    bias = jnp.concatenate([bias_ref[...]] * H, axis=1)
    qg = [jnp.concatenate([qT_ref[(g * G + hh) * HD:(g * G + hh + 1) * HD, :]
                           for hh in range(G)], axis=1) for g in range(KVH)]
    zeros = jnp.zeros_like(qg[0])
    rhs = jnp.concatenate([jnp.concatenate([qg[0], zeros], axis=1),
                           jnp.concatenate([zeros, qg[1]], axis=1)], axis=0)
    sT = jnp.dot(kk, rhs, preferred_element_type=f32) + bias
    yield
    sink = jnp.concatenate([sink_ref[0], sink_ref[1]], axis=1)
    m = jnp.maximum(jnp.max(sT, axis=0, keepdims=True), sink)
    p = jnp.exp(sT - m)
    l = jnp.sum(p, axis=0, keepdims=True) + jnp.exp(sink - m)
    p = (p * (1.0 / l)).astype(bf16)
    yield
    pieces = []
    for g in range(KVH):
        oT = jnp.dot(vv[g * HD:(g + 1) * HD, :], p[:, g * G * TQ:(g + 1) * G * TQ],
                     preferred_element_type=f32)
        pieces += [oT[:, hh * TQ:(hh + 1) * TQ] for hh in range(G)]
    oT_all = jnp.concatenate(pieces, axis=0)
    y_ref[...] = x_ref[...] + jnp.dot(oT_all.T.astype(bf16), wo_ref[...],
                                      preferred_element_type=f32)


def _band_bias():
    s = np.arange(2 * TQ)[:, None]
    t = np.arange(TQ)[None, :]
    dist = t + TQ - s
    band = (dist >= 0) & (dist <= WIN)
    first = band & (s >= TQ)
    return jnp.asarray(np.where(np.stack([first, band]), 0.0, -np.inf), f32)


def _attn_prompt(qT, ktok, vT, sink_rows, xp, xs, o_s, wo, wo_lo):
    steps = NQ // QB
    rows = QB * TQ
    assert rows == T

    def cur(j):
        return jnp.minimum(j, steps - 1)

    def prev(j):
        return jnp.maximum(cur(j) * QB - 1, 0)

    def prev_tile(j):
        return jnp.maximum(cur(j) - 1, 0)

    def sample(j):
        return jnp.maximum(j - steps, 0)

    bias = _band_bias()
    return pl.pallas_call(
        _attn_prompt_kernel,
        grid=(N // rows,),
        in_specs=[
            pl.BlockSpec((None, H * HD, rows), lambda j: (cur(j), 0, 0)),
            pl.BlockSpec((TQ, KVH * HD), lambda j: (prev(j), 0)),
            pl.BlockSpec((rows, KVH * HD), lambda j: (cur(j), 0)),
            pl.BlockSpec((None, KVH * HD, TQ), lambda j: (prev_tile(j), 0, QB - 1)),
            pl.BlockSpec((None, KVH * HD, rows), lambda j: (cur(j), 0, 0)),
            pl.BlockSpec((KVH, 1, G * TQ), lambda j: (0, 0, 0)),
            pl.BlockSpec((None, 2 * TQ, TQ), lambda j: (jnp.minimum(j, 1), 0, 0)),
            pl.BlockSpec((None, 2 * TQ, TQ), lambda j: (1, 0, 0)),
            pl.BlockSpec((rows, D), lambda j: (cur(j), 0)),
            pl.BlockSpec((rows, D), lambda j: (sample(j), 0)),
            pl.BlockSpec((rows, H * HD), lambda j: (sample(j), 0)),
            pl.BlockSpec((D, D), lambda j: (0, 0)),
            pl.BlockSpec((D, D), lambda j: (0, 0)),
        ],
        out_specs=pl.BlockSpec((rows, D), lambda j: (j, 0)),
        out_shape=jax.ShapeDtypeStruct((N, D), f32),
        compiler_params=_params("parallel"),
        name="attn_prompt",
    )(qT, ktok, ktok, vT, vT, sink_rows, bias, bias, xp, xs, o_s, wo, wo_lo)


def _attn_sample_kernel(q_ref, kc_ref, vc_ref, kn_ref, vn_ref, sink_ref,
                        o_ref, knew_ref, vnew_ref):
    rows = H * TS
    t1 = lax.broadcasted_iota(jnp.int32, (1, rows, WIN), 1) & (TS - 1)
    s1 = lax.broadcasted_iota(jnp.int32, (1, rows, WIN), 2)
    valid1 = s1 >= t1
    t2 = lax.broadcasted_iota(jnp.int32, (1, rows, 8), 1) & (TS - 1)
    s2 = lax.broadcasted_iota(jnp.int32, (1, rows, 8), 2)
    valid2 = s2 <= t2
    sink = sink_ref[...][None]
    q = q_ref[...]
    kc = kc_ref[...]
    vc = vc_ref[...]
    kn = kn_ref[...]
    vn = vn_ref[...]
    def dot3(eq, a, b):
        a_hi, a_lo = _split(a)
        b_hi, b_lo = _split(b)
        return (jnp.einsum(eq, a_hi, b_hi, preferred_element_type=f32)
                + jnp.einsum(eq, a_lo, b_hi, preferred_element_type=f32)
                + jnp.einsum(eq, a_hi, b_lo, preferred_element_type=f32))

    sc = dot3('bqd,bkd->bqk', q, kc)
    sn = dot3('bqd,bkd->bqk', q, kn)
    sc = jnp.where(valid1, sc, -jnp.inf)
    sn = jnp.where(valid2, sn, -jnp.inf)
    m = jnp.maximum(jnp.maximum(jnp.max(sc, axis=-1, keepdims=True),
                                jnp.max(sn, axis=-1, keepdims=True)), sink)
    pc = jnp.exp(sc - m)
    pn = jnp.exp(sn - m)
    l = (jnp.sum(pc, axis=-1, keepdims=True) + jnp.sum(pn, axis=-1, keepdims=True)
         + jnp.exp(sink - m))
    o_ref[...] = (dot3('bqk,bkd->bqd', pc, vc) + dot3('bqk,bkd->bqd', pn, vn)) / l
    knew_ref[:, :WIN - TS, :] = kc[:, TS:, :]
    knew_ref[:, WIN - TS:, :] = kn[:, :TS, :]
    vnew_ref[:, :WIN - TS, :] = vc[:, TS:, :]
    vnew_ref[:, WIN - TS:, :] = vn[:, :TS, :]


def _attn_sample(qbd, kc, vc, kn, vn, sink_col):
    rows = H * TS
    cache_spec = pl.BlockSpec((BB, WIN, KVH * HD), lambda i: (i, 0, 0))
    new_spec = pl.BlockSpec((BB, 8, KVH * HD), lambda i: (i, 0, 0))
    cache_shape = jax.ShapeDtypeStruct((NB, WIN, KVH * HD), f32)
    return pl.pallas_call(
        _attn_sample_kernel,
        grid=(NB // BB,),
        in_specs=[
            pl.BlockSpec((BB, rows, KVH * HD), lambda i: (i, 0, 0)),
            cache_spec, cache_spec, new_spec, new_spec,
            pl.BlockSpec((rows, 1), lambda i: (0, 0)),
        ],
        out_specs=[pl.BlockSpec((BB, rows, KVH * HD), lambda i: (i, 0, 0)),
                   cache_spec, cache_spec],
        out_shape=[jax.ShapeDtypeStruct((NB, rows, KVH * HD), f32), cache_shape, cache_shape],
        compiler_params=_params("parallel"),
        name="attn_sample",
    )(qbd, kc, vc, kn, vn, sink_col)


def _dispatch_kernel(x_ref, g_ref, w_ref, wlo_ref, b_ref, upper_ref, xc_ref, meta_ref, cnt_ref):
    tiles = [_dispatch_tile(x_ref.at[k * TM:(k + 1) * TM], g_ref, w_ref, wlo_ref, b_ref,
                            upper_ref, xc_ref.at[k * RC:(k + 1) * RC],
                            meta_ref.at[k * TM:(k + 1) * TM],
                            cnt_ref.at[k * NE:(k + 1) * NE]) for k in range(DT)]
    for _ in range(3):
        for t in tiles:
            next(t, None)


def _dispatch_kernel_1pass(x_ref, g_ref, w_ref, b_ref, upper_ref, xc_ref, meta_ref, cnt_ref):
    _dispatch_kernel(x_ref, g_ref, w_ref, None, b_ref, upper_ref, xc_ref, meta_ref, cnt_ref)


def _dispatch_tile(x_ref, g_ref, w_ref, wlo_ref, b_ref, upper_ref, xc_ref, meta_ref, cnt_ref):
    hf = _rms(x_ref[...], g_ref[...])
    h_hi, h_lo = _split(hf)
    logits = jnp.dot(h_hi, w_ref[...], preferred_element_type=f32)
    if wlo_ref is not None:
        logits = (logits + jnp.dot(h_lo, w_ref[...], preferred_element_type=f32)
                  + jnp.dot(h_hi, wlo_ref[...], preferred_element_type=f32))
    logits = logits + b_ref[...]
    yield
    lt = logits.T
    inf = jnp.inf
    row8 = lax.broadcasted_iota(jnp.int32, (8, TM), 0).astype(f32)
    gl = jnp.where(row8 < NGRP, lt[0:8], -inf)
    gmax = jnp.max(gl, axis=0, keepdims=True)
    gsel = jnp.min(jnp.where(gl == gmax, row8, 8.0), axis=0, keepdims=True)
    g_w = 1.0 / jnp.sum(jnp.exp(gl - gmax), axis=0, keepdims=True)
    row = lax.broadcasted_iota(jnp.int32, (NE, TM), 0)
    row_f = row.astype(f32)
    el = jnp.where((row >> 2).astype(f32) == gsel, lt[ROW_E:ROW_E + NE], -inf)
    v1 = jnp.max(el, axis=0, keepdims=True)
    i1 = jnp.min(jnp.where(el == v1, row_f, float(NE)), axis=0, keepdims=True)
    el2 = jnp.where(row_f == i1, -inf, el)
    v2 = jnp.max(el2, axis=0, keepdims=True)
    i2 = jnp.min(jnp.where(el2 == v2, row_f, float(NE)), axis=0, keepdims=True)
    e1 = jnp.exp(v2 - v1)
    den = 1.0 + e1
    w1 = (1.0 / den) * g_w
    w2 = (e1 / den) * g_w

    m1 = row_f == i1
    m2 = row_f == i2
    sel = jnp.where(m1 | m2, 1.0, 0.0)
    ranks = jnp.dot(sel.astype(bf16), upper_ref[...],
                    preferred_element_type=f32)
    counts = jnp.sum(sel, axis=1, keepdims=True)
    padded = jnp.floor((counts + (UNIT - 1.0)) * (1.0 / UNIT)) * UNIT
    e_i = lax.broadcasted_iota(jnp.int32, (NE, NE), 0)
    f_i = lax.broadcasted_iota(jnp.int32, (NE, NE), 1)
    below = jnp.where(f_i < e_i, 1.0, 0.0).astype(bf16)
    seg = jnp.dot(below, jnp.broadcast_to(padded, (NE, LANES)).astype(bf16),
                  preferred_element_type=f32)[:, 0:1]
    posall = seg + ranks
    pos1 = jnp.sum(jnp.where(m1, posall, 0.0), axis=0, keepdims=True)
    pos2 = jnp.sum(jnp.where(m2, posall, 0.0), axis=0, keepdims=True)
    cnt_ref[...] = jnp.broadcast_to(counts, (NE, LANES))

    w1_hi = w1.astype(bf16).astype(f32)
    w2_hi = w2.astype(bf16).astype(f32)
    slab = jnp.zeros((8, TM), f32)
    for k, r in enumerate((pos1, pos2, w1_hi, w1 - w1_hi, w2_hi, w2 - w2_hi, i1)):
        slab = jnp.where(row8 == k, r, slab)
    meta = jnp.concatenate([slab, jnp.zeros((LANES - 8, TM), f32)], axis=0).T
    meta_ref[...] = meta
    yield

    rr = lax.broadcasted_iota(jnp.int32, (RC, TM), 0).astype(f32)
    onehot = jnp.where((rr == pos1) | (rr == pos2), 1.0, 0.0).astype(bf16)
    h_aug = jnp.concatenate([h_hi, meta.astype(bf16)], axis=1)
    xc_ref[...] = jnp.dot(onehot, h_aug, preferred_element_type=f32).astype(bf16)


def _dispatch(x, g, w, w_lo, b):
    upper = jnp.asarray(np.triu(np.ones((TM, TM), np.float32), 1), bf16)
    w_spec = pl.BlockSpec((D, LANES), lambda i: (0, 0))
    weights = (w,) if w_lo is None else (w, w_lo)
    return pl.pallas_call(
        _dispatch_kernel_1pass if w_lo is None else _dispatch_kernel,
        grid=(NTM // DT,),
        in_specs=[
            pl.BlockSpec((DT * TM, D), lambda i: (i, 0)),
            pl.BlockSpec((1, D), lambda i: (0, 0)),
            *[w_spec] * len(weights),
            pl.BlockSpec((1, LANES), lambda i: (0, 0)),
            pl.BlockSpec((TM, TM), lambda i: (0, 0)),
        ],
        out_specs=[
            pl.BlockSpec((DT * RC, XW), lambda i: (i, 0)),
            pl.BlockSpec((DT * TM, LANES), lambda i: (i, 0)),
            pl.BlockSpec((DT * NE, LANES), lambda i: (i, 0)),
        ],
        out_shape=[
            jax.ShapeDtypeStruct((NTM * RC, XW), bf16),
            jax.ShapeDtypeStruct((N, LANES), f32),
            jax.ShapeDtypeStruct((NTM * NE, LANES), f32),
        ],
        compiler_params=_params("parallel"),
        name="dispatch",
    )(x, g, *weights, b, upper)


def _dispatch_tables(cnt):
    i32 = jnp.int32
    n = cnt.reshape(NTM, NE, LANES)[:, :, 0].astype(i32)
    units = (n + UNIT - 1) // UNIT
    seg_end = jnp.cumsum(units, axis=1)
    seg_start = seg_end - units
    col_end = jnp.cumsum(units, axis=0)
    col_start = col_end - units
    chunks = (col_end[-1] + UPC - 1) // UPC
    ch_end = jnp.cumsum(chunks)
    ch_start = ch_end - chunks
    nused = ch_end[-1]
    c = jnp.arange(NCH, dtype=i32)
    eid = jnp.minimum(jnp.sum((ch_end[None, :] <= c[:, None]).astype(i32), axis=1), NE - 1)

    src0 = jnp.arange(NTM, dtype=i32)[:, None] * UPT + seg_start
    dst0 = ch_start[None, :] * UPC + col_start
    k = jnp.arange((NCH + NSLOT - 1) * UPC, dtype=i32)[:, None, None]
    inside = (k >= dst0[None]) & (k < (dst0 + units)[None])
    found = jnp.sum(inside.astype(i32), axis=(1, 2))
    shift = jnp.sum(jnp.where(inside, (src0 - dst0)[None], 0), axis=(1, 2))
    ffn_src = jnp.where(found > 0, k[:, 0, 0] + shift, ZERO_UNIT_IN)

    v = jnp.arange(UPT, dtype=i32)[None, :, None]
    inside_v = (v >= seg_start[:, None, :]) & (v < seg_end[:, None, :])
    found_v = jnp.sum(inside_v.astype(i32), axis=2)
    shift_v = jnp.sum(jnp.where(inside_v, (dst0 - seg_start)[:, None, :], 0), axis=2)
    comb_src = jnp.where(found_v > 0, v[:, :, 0] + shift_v, ZERO_UNIT_OUT).reshape(-1)
    own_end = jnp.sum(jnp.where(eid[:, None] == jnp.arange(NE, dtype=i32)[None, :],
                                ch_end[None, :], 0), axis=1)
    eid_at = jnp.sum(jnp.where(c[None, :] == own_end[:, None], eid[None, :], 0), axis=1)
    nxt = jnp.where(own_end < nused, eid_at, -1)
    return eid, nused.reshape(1), ffn_src, nxt, comb_src


def _unit_copies(src_ref, base, n_units, src_hbm, stage, slot, sem):
    out = []
    for j in range(n_units):
        row = pl.multiple_of(src_ref[base + j] * UNIT, UNIT)
        out.append(pltpu.make_async_copy(
            src_hbm.at[pl.ds(row, UNIT), :],
            stage.at[slot, pl.ds(j * UNIT, UNIT), :],
            sem.at[slot]))
    return out


def _start_gathers(step, n_steps, n_units, src_ref, src_hbm, stage, sem):
    ahead = NSLOT - 1

    def start(s, slot):
        for cp in _unit_copies(src_ref, s * n_units, n_units, src_hbm, stage, slot, sem):
            cp.start()

    for s in range(ahead):
        @pl.when((step == 0) & (s < n_steps))
        def _():
            start(s, s)

    @pl.when(step + ahead < n_steps)
    def _():
        start(step + ahead, lax.rem(step + ahead, NSLOT))


def _wait_gather(step, n_units, src_ref, src_hbm, stage, sem):
    slot = lax.rem(step, NSLOT)
    for cp in _unit_copies(src_ref, step * n_units, n_units, src_hbm, stage, slot, sem):
        cp.wait()
    return slot


def _ffn_kernel(layer, eid_ref, nused_ref, src_ref, nxt_ref, xc_hbm, wg_hbm, wu_hbm, wd_hbm,
                o_hbm, stage, sem, wg_f, wu_f, wd_f, w_sem, wg_b, wu_b, wd_b, obuf, o_sem):
    nused = nused_ref[0]
    ahead = NSLOT - 1

    def start(s):
        for cp in _unit_copies(src_ref, s * UPC, UPC, xc_hbm, stage, lax.rem(s, NSLOT), sem):
            cp.start()

    def weight_copies(e, slot):
        return [pltpu.make_async_copy(hbm.at[layer, e], buf.at[slot], w_sem.at[slot])
                for hbm, buf in ((wg_hbm, wg_f), (wu_hbm, wu_f), (wd_hbm, wd_f))]

    def out_copy(c, slot):
        row = pl.multiple_of(c * CH, CH)
        return pltpu.make_async_copy(obuf.at[slot], o_hbm.at[pl.ds(row, CH), :], o_sem.at[slot])

    for s in range(ahead):
        start(s)

    @pl.when(nused > 0)
    def _():
        for cp in weight_copies(eid_ref[0], 0):
            cp.start()

    def chunk(c, n_experts):
        e = eid_ref[c]
        first = (c == 0) | (e != eid_ref[jnp.maximum(c - 1, 0)])
        wslot = lax.rem(n_experts, 2)

        @pl.when(first)
        def _():
            for cp in weight_copies(e, wslot):
                cp.wait()
            wg_b[...] = wg_f[wslot].astype(bf16)
            wu_b[...] = wu_f[wslot].astype(bf16)
            wd_b[...] = wd_f[wslot].astype(bf16)

            @pl.when(nxt_ref[c] >= 0)
            def _():
                for cp in weight_copies(nxt_ref[c], 1 - wslot):
                    cp.start()

        slot = _wait_gather(c, UPC, src_ref, xc_hbm, stage, sem)
        oslot = lax.rem(c, 2)

        @pl.when(c >= 2)
        def _():
            out_copy(c - 2, oslot).wait()

        def block(r):
            xs = stage[slot, r:r + CHB, :]
            x = xs[:, :D]
            a = jnp.dot(x, wg_b[...], preferred_element_type=f32)
            u = jnp.dot(x, wu_b[...], preferred_element_type=f32)
            yield
            gb = xs[:, D:].astype(f32)
            is_first = gb[:, 6:7] == e.astype(f32)
            gate = jnp.where(is_first, gb[:, 2:3] + gb[:, 3:4], gb[:, 4:5] + gb[:, 5:6])
            act = (a * (1.0 / (1.0 + jnp.exp(-a)))) * u * gate
            obuf[oslot, r:r + CHB, :] = jnp.dot(act.astype(bf16), wd_b[...],
                                                preferred_element_type=f32).astype(bf16)

        blocks = [block(r) for r in range(0, CH, CHB)]
        for _ in range(2):
            for blk in blocks:
                next(blk, None)
        out_copy(c, oslot).start()
        start(c + ahead)
        return n_experts + first.astype(jnp.int32)

    lax.fori_loop(0, nused, chunk, jnp.int32(0))

    for s in range(ahead):
        _wait_gather(nused + s, UPC, src_ref, xc_hbm, stage, sem)
    for back in (1, 2):
        @pl.when(nused >= back)
        def _():
            out_copy(nused - back, lax.rem(nused - back, 2)).wait()

    obuf[0] = jnp.zeros((CH, D), bf16)

    def zero_start(c, carry):
        out_copy(c, 0).start()
        return carry

    def zero_wait(c, carry):
        out_copy(c, 0).wait()
        return carry

    lax.fori_loop(nused, NCH, zero_start, 0)
    lax.fori_loop(nused, NCH, zero_wait, 0)


def _ffn(layer, eid, nused, ffn_src, nxt, xc, wg, wu, wd):
    any_spec = pl.BlockSpec(memory_space=pl.ANY)
    return pl.pallas_call(
        functools.partial(_ffn_kernel, layer),
        grid_spec=pltpu.PrefetchScalarGridSpec(
            num_scalar_prefetch=4,
            grid=(1,),
            in_specs=[any_spec, any_spec, any_spec, any_spec],
            out_specs=any_spec,
            scratch_shapes=[
                pltpu.VMEM((NSLOT, CH, XW), bf16),
                pltpu.SemaphoreType.DMA((NSLOT,)),
                pltpu.VMEM((2, D, F), f32),
                pltpu.VMEM((2, D, F), f32),
                pltpu.VMEM((2, F, D), f32),
                pltpu.SemaphoreType.DMA((2,)),
                pltpu.VMEM((D, F), bf16),
                pltpu.VMEM((D, F), bf16),
                pltpu.VMEM((F, D), bf16),
                pltpu.VMEM((2, CH, D), bf16),
                pltpu.SemaphoreType.DMA((2,)),
            ],
        ),
        out_shape=jax.ShapeDtypeStruct((NCH * CH, D), bf16),
        compiler_params=_params("arbitrary"),
        name="ffn",
    )(eid, nused, ffn_src, nxt, xc, wg, wu, wd)


def _combine_tile(x, meta, rows):
    li = lax.broadcasted_iota(jnp.int32, (TM, RC), 1).astype(f32)
    pt = jnp.where((li == meta[:, 0:1]) | (li == meta[:, 1:2]), 1.0, 0.0).astype(bf16)
    return x + jnp.dot(pt, rows, preferred_element_type=f32)


def _combine_kernel(src_ref, x_ref, meta_ref, o_hbm, yp_ref, ys_ref, stage, sem):
    i = pl.program_id(0)
    tiles = T // TM
    _start_gathers(i, NT, tiles * UPT, src_ref, o_hbm, stage, sem)
    slot = _wait_gather(i, tiles * UPT, src_ref, o_hbm, stage, sem)
    ys = [_combine_tile(x_ref[k * TM:(k + 1) * TM, :], meta_ref[k * TM:(k + 1) * TM, :],
                        stage[slot, k * RC:(k + 1) * RC, :]) for k in range(tiles)]
    y = jnp.concatenate(ys, axis=0)

    @pl.when(i < NT - 1)
    def _():
        yp_ref[...] = y

    @pl.when(i == NT - 1)
    def _():
        ys_ref[...] = y


def _combine(comb_src, x, meta, o_sorted):
    out_specs = [pl.BlockSpec((T, D), lambda i, src: (jnp.minimum(i, NT - 2), 0)),
                 pl.BlockSpec((T, D), lambda i, src: (0, 0))]
    out_shape = [jax.ShapeDtypeStruct((NP, D), f32), jax.ShapeDtypeStruct((NS, D), f32)]
    return pl.pallas_call(
        _combine_kernel,
        grid_spec=pltpu.PrefetchScalarGridSpec(
            num_scalar_prefetch=1,
            grid=(NT,),
            in_specs=[
                pl.BlockSpec((T, D), lambda i, src: (i, 0)),
                pl.BlockSpec((T, LANES), lambda i, src: (i, 0)),
                pl.BlockSpec(memory_space=pl.ANY),
            ],
            out_specs=out_specs,
            scratch_shapes=[
                pltpu.VMEM((NSLOT, (T // TM) * RC, D), bf16),
                pltpu.SemaphoreType.DMA((NSLOT,)),
            ],
        ),
        out_shape=out_shape,
        compiler_params=_params("arbitrary"),
        name="combine",
    )(comb_src, x, meta, o_sorted)


def _conv_kernel(src_ref, x1_ref, meta_ref, o_hbm, g_ref, win_ref, cw_ref, wout_ref,
                 p1_ref, p2_ref, y_ref, cu_ref, pad_ref, x_ref, stage, sem):
    i = pl.program_id(0)

    @pl.when(i == 0)
    def _():
        pad_ref[0:8, :] = jnp.zeros((8, D), f32)

    tiles = T // TM
    ahead = NSLOT - 1

    def start(s):
        for cp in _unit_copies(src_ref, s * tiles * UPT, tiles * UPT, o_hbm, stage,
                               lax.rem(s, NSLOT), sem):
            cp.start()

    @pl.when(i == 0)
    def _():
        for s in range(ahead):
            start(s)

    slot = _wait_gather(i, tiles * UPT, src_ref, o_hbm, stage, sem)

    for k in range(tiles):
        rows = slice(k * TM, (k + 1) * TM)
        x_ref[rows, :] = _combine_tile(x1_ref[rows, :], meta_ref[rows, :],
                                       stage[slot, k * RC:(k + 1) * RC, :])

    blocks = [(r, r + T // 2) for r in (0, T // 2)]
    gates = []
    for r0, r1 in blocks:
        h = _rms(x_ref[r0:r1, :], g_ref[...]).astype(bf16)
        bcu = jnp.dot(h, win_ref[...], preferred_element_type=f32)
        gates.append(bcu[:, :D])
        cu = bcu[:, D:2 * D] * bcu[:, 2 * D:]
        pad_ref[8 + r0:8 + r1, :] = cu
        cu_ref[r0:r1, :] = cu
    t = lax.broadcasted_iota(jnp.int32, (T // 2, 1), 0) & (TS - 1)
    is_sample = i == NT - 1
    cw = cw_ref[...]
    for (r0, r1), b in zip(blocks, gates):
        m1 = jnp.where(is_sample & (t == 0), p1_ref[r0:r1, :], pad_ref[7 + r0:7 + r1, :])
        m2 = jnp.where(is_sample & (t < 2), p2_ref[r0:r1, :], pad_ref[6 + r0:6 + r1, :])
        conv = cw[0:1] * m2 + cw[1:2] * m1 + cw[2:3] * pad_ref[8 + r0:8 + r1, :]
        y = jnp.dot((b * conv).astype(bf16), wout_ref[...], preferred_element_type=f32)
        y_ref[r0:r1, :] = x_ref[r0:r1, :] + y
    pad_ref[0:8, :] = pad_ref[T:T + 8, :]
    start(i + ahead)

    @pl.when(i == NT - 1)
    def _():
        for s in range(ahead):
            _wait_gather(i + 1 + s, tiles * UPT, src_ref, o_hbm, stage, sem)


def _conv(comb_src, x1, meta, o_sorted, g, win, cw, wout, p1, p2):
    const = dict(pipeline_mode=pl.Buffered(1))
    comb_src = jnp.concatenate([
        comb_src, jnp.full(((NSLOT - 1) * (T // TM) * UPT,), ZERO_UNIT_OUT, jnp.int32)])
    return pl.pallas_call(
        _conv_kernel,
        grid_spec=pltpu.PrefetchScalarGridSpec(
            num_scalar_prefetch=1,
            grid=(NT,),
            in_specs=[
                pl.BlockSpec((T, D), lambda i, src: (i, 0)),
                pl.BlockSpec((T, LANES), lambda i, src: (i, 0)),
                pl.BlockSpec(memory_space=pl.ANY),
                pl.BlockSpec((1, D), lambda i, src: (0, 0)),
                pl.BlockSpec((D, 3 * D), lambda i, src: (0, 0), **const),
                pl.BlockSpec((3, D), lambda i, src: (0, 0)),
                pl.BlockSpec((D, D), lambda i, src: (0, 0), **const),
                pl.BlockSpec((T, D), lambda i, src: (0, 0), **const),
                pl.BlockSpec((T, D), lambda i, src: (0, 0), **const),
            ],
            out_specs=[
                pl.BlockSpec((T, D), lambda i, src: (i, 0)),
                pl.BlockSpec((T, D), lambda i, src: (jnp.where(i == NT - 1, 1, 0), 0)),
            ],
            scratch_shapes=[
                pltpu.VMEM((T + 8, D), f32),
                pltpu.VMEM((T, D), f32),
                pltpu.VMEM((NSLOT, (T // TM) * RC, D), bf16),
                pltpu.SemaphoreType.DMA((NSLOT,)),
            ],
        ),
        out_shape=[
            jax.ShapeDtypeStruct((N, D), f32),
            jax.ShapeDtypeStruct((2 * T, D), f32),
        ],
        compiler_params=_params("arbitrary"),
        name="conv",
    )(comb_src, x1, meta, o_sorted, g, win, cw, wout, p1, p2)


def _rope_tables():
    inv_freq = THETA ** (-jnp.arange(HALF, dtype=f32) / HALF)
    pos = jnp.concatenate([
        jnp.arange(NP, dtype=jnp.int32),
        PAST + jnp.tile(jnp.arange(TS, dtype=jnp.int32), NB),
    ]).astype(f32)
    ang = inv_freq[:, None] * pos[None, :]
    return jnp.cos(ang), jnp.sin(ang)


def _router_weights(w_group, b_group, w_router, b_router):
    gap = ROW_E - NGRP
    pad = LANES - ROW_E - NE
    w = jnp.concatenate([w_group, jnp.zeros((D, gap), f32), w_router,
                         jnp.zeros((D, pad), f32)], axis=1)
    b = jnp.concatenate([b_group, jnp.zeros((gap,), f32), b_router,
                         jnp.zeros((pad,), f32)])[None, :]
    return w, b


def _split_weights(w):
    hi = w.astype(bf16)
    return hi, (w - hi.astype(f32)).astype(bf16)


def _moe_experts(x, i, norm_ffn, w_group, b_group, w_router, b_router, w_gate, w_up, w_down):
    w, b = _router_weights(w_group[i], b_group[i], w_router[i], b_router[i])
    w, w_lo = _split_weights(w) if i == 0 else (w.astype(bf16), None)
    xc, meta, cnt = _dispatch(x, norm_ffn[i][None, :], w, w_lo, b)
    eid, nused, ffn_src, nxt, comb_src = _dispatch_tables(cnt)
    o_sorted = _ffn(i, eid, nused, ffn_src, nxt, xc, w_gate, w_up, w_down)
    return comb_src, meta, o_sorted


def kernel(x_prompt, x_sample, cache_k, cache_v, state_conv, norm_mix, w_qkv, q_norm, k_norm,
           sinks, w_o, w_in, conv_w, w_out, norm_ffn, w_group, b_group, w_router, b_router,
           w_gate, w_up, w_down):
    xp = x_prompt.reshape(NP, D)
    xs = x_sample.reshape(NS, D)
    moe_w = (norm_ffn, w_group, b_group, w_router, b_router, w_gate, w_up, w_down)

    cos, sin = _rope_tables()
    wT, wT_lo = _split_weights(w_qkv[0].T)
    qT, ktok, vtok, vT, q_s = _qkv(xp, xs, norm_mix[0][None, :], wT, wT_lo,
                                   q_norm[0][:, None], k_norm[0][:, None], cos, sin)
    sink_rows = jnp.repeat(sinks[0].reshape(KVH, G), TQ, axis=1)[:, None, :]

    qs = q_s.reshape(KVH, G, HD, NB, TS).transpose(3, 0, 1, 4, 2)
    zq = jnp.zeros_like(qs[:, 0])
    qbd = jnp.stack([jnp.concatenate([qs[:, 0], zq], axis=-1),
                     jnp.concatenate([zq, qs[:, 1]], axis=-1)], axis=1)
    qbd = qbd.reshape(NB, H * TS, KVH * HD)
    k_new = ktok[NP:].reshape(NB, TS, KVH * HD)
    v_new = vtok[NP:].reshape(NB, TS, KVH * HD)
    pad4 = jnp.zeros((NB, 8 - TS, KVH * HD), f32)
    kc = cache_k[0].reshape(NB, WIN, KVH * HD)
    vc = cache_v[0].reshape(NB, WIN, KVH * HD)
    sink_col = jnp.repeat(sinks[0], TS)[:, None]
    o_s, kc_new, vc_new = _attn_sample(qbd, kc, vc, jnp.concatenate([k_new, pad4], axis=1),
                                       jnp.concatenate([v_new, pad4], axis=1), sink_col)
    o_s = o_s.reshape(NB, KVH, G, TS, KVH, HD)
    o_s = jnp.stack([o_s[:, 0, :, :, 0], o_s[:, 1, :, :, 1]], axis=1)
    o_s = o_s.transpose(0, 3, 1, 2, 4).reshape(NS, H * HD)
    x = _attn_prompt(qT, ktok, vT, sink_rows, xp, xs, o_s, *_split_weights(w_o[0]))
    comb_src, meta, o_sorted = _moe_experts(x, 0, *moe_w)

    new_k_prompt = ktok[NP - WIN:NP].reshape(1, 1, WIN, KVH, HD)
    new_v_prompt = vtok[NP - WIN:NP].reshape(1, 1, WIN, KVH, HD)
    new_k_sample = kc_new.reshape(1, NB, WIN, KVH, HD)
    new_v_sample = vc_new.reshape(1, NB, WIN, KVH, HD)

    st = state_conv[0]
    z = jnp.zeros((NB, 1, D), f32)
    p1 = jnp.concatenate([st[:, 1:2], z, z, z], axis=1).reshape(NS, D)
    p2 = jnp.concatenate([st[:, 0:1], st[:, 1:2], z, z], axis=1).reshape(NS, D)
    x, cu = _conv(comb_src, x, meta, o_sorted, norm_mix[1][None, :], w_in[0].astype(bf16),
                  conv_w[0], w_out[0].astype(bf16), p1, p2)
    comb_src, meta, o_sorted = _moe_experts(x, 1, *moe_w)
    y_prompt, y_sample = _combine(comb_src, x, meta, o_sorted)

    new_conv_prompt = cu[T - 2:T].reshape(1, 1, 2, D)
    new_conv_sample = cu[T:].reshape(NB, TS, D)[:, TS - 2:][None]

    y_prompt = y_prompt.reshape(1, NP, D)
    y_sample = y_sample.reshape(NB, TS, D)
    return (y_prompt, y_sample, new_k_prompt, new_v_prompt, new_conv_prompt,
            new_k_sample, new_v_sample, new_conv_sample)
```

```python
import functools

import jax
import jax.numpy as jnp
import numpy as np
from jax import lax
from jax.experimental import pallas as pl
from jax.experimental.pallas import tpu as pltpu

D = 1024
NP = 16384
NB = 128
TS = 4
NS = NB * TS
N = NP + NS
PAST = 16384
H = 16
KVH = 2
G = H // KVH
HD = 64
HALF = HD // 2
QKV = (H + 2 * KVH) * HD
WIN = 128
THETA = 10000.0
NGRP = 4
EPG = 4
NE = NGRP * EPG
TOPK = 2
ROW_E = 8
F = 512
EPS = 1e-6
SCALE = HD ** -0.5

T = 512
NT = N // T
TQ = 128
NQ = NP // TQ
QB = 4
BB = 16
LANES = 128
V7X_VMEM_BYTES = 64 * 1024 * 1024
VMEM_LIMIT = V7X_VMEM_BYTES - 14 * 1024 * 1024

TM = 256
NTM = N // TM
DT = 6
UNIT = 16
RC = TOPK * TM + NE * UNIT
UPT = RC // UNIT
XW = D + LANES
CH = 512
CHB = 256
UPC = CH // UNIT
NSLOT = 3
NCH = -(-(NTM * (UPT - 1) + NE * (UPC - 1)) // UPC) + NSLOT - 1
ZERO_UNIT_IN = UPT - 1
ZERO_UNIT_OUT = (NCH - 1) * UPC

f32 = jnp.float32
bf16 = jnp.bfloat16


def _params(*sem):
    return pltpu.CompilerParams(dimension_semantics=sem, vmem_limit_bytes=VMEM_LIMIT)


def _rms(x, g):
    ms = jnp.mean(x * x, axis=-1, keepdims=True)
    return x * lax.rsqrt(ms + EPS) * g


def _pick(i, prompt_ref, sample_ref):
    return jnp.where(i == NT - 1, sample_ref[...], prompt_ref[...])


def _stream_specs():
    return [pl.BlockSpec((T, D), lambda i: (jnp.minimum(i, NT - 2), 0)),
            pl.BlockSpec((T, D), lambda i: (0, 0))]


def _split(x):
    hi = x.astype(bf16)
    return hi, (x - hi.astype(f32)).astype(bf16)


def _qkv_kernel(xp_ref, xs_ref, g_ref, wT_ref, wTlo_ref, qn_ref, kn_ref, cos_ref, sin_ref,
                qT_ref, ktok_ref, vtok_ref, vT_ref, qs_ref, acc_ref):
    i = pl.program_id(0)
    hf = _rms(_pick(i, xp_ref, xs_ref), g_ref[...])
    h, h_lo = _split(hf)
    nt = (((1,), (1,)), ((), ()))
    acc_ref[...] = lax.dot_general(wT_ref[...], h, nt, preferred_element_type=f32)

    @pl.when(i == NT - 1)
    def _():
        acc_ref[...] += (lax.dot_general(wT_ref[...], h_lo, nt, preferred_element_type=f32)
                         + lax.dot_general(wTlo_ref[...], h, nt, preferred_element_type=f32))

    qkvT = acc_ref
    cos = cos_ref[...]
    sin = sin_ref[...]

    def norm_rope(blk, gcol):
        ms = jnp.mean(blk * blk, axis=0, keepdims=True)
        y = blk * lax.rsqrt(ms + EPS) * gcol
        y1 = y[:HALF]
        y2 = y[HALF:]
        return y1 * cos - y2 * sin, y2 * cos + y1 * sin

    qn = qn_ref[...]
    for hd in range(H):
        o1, o2 = norm_rope(qkvT[hd * HD:(hd + 1) * HD], qn)
        qT_ref[hd * HD:hd * HD + HALF, :] = (o1 * SCALE).astype(bf16)
        qT_ref[hd * HD + HALF:(hd + 1) * HD, :] = (o2 * SCALE).astype(bf16)

    @pl.when(i == NT - 1)
    def _():
        for hd in range(H):
            o1, o2 = norm_rope(qkvT[hd * HD:(hd + 1) * HD], qn)
            qs_ref[hd * HD:hd * HD + HALF, :] = o1 * SCALE
            qs_ref[hd * HD + HALF:(hd + 1) * HD, :] = o2 * SCALE

    kn = kn_ref[...]
    ks = []
    for j in range(KVH):
        o1, o2 = norm_rope(qkvT[H * HD + j * HD:H * HD + (j + 1) * HD], kn)
        ks += [o1, o2]
    kT = jnp.concatenate(ks, axis=0)
    ktok_ref[...] = kT.T
    vT = qkvT[(H + KVH) * HD:]
    vtok_ref[...] = vT.T
    vT_ref[...] = vT.astype(bf16)


def _qkv(xp, xs, g, wT, wT_lo, qn, kn, cos, sin):
    return pl.pallas_call(
        _qkv_kernel,
        grid=(NT,),
        in_specs=_stream_specs() + [
            pl.BlockSpec((1, D), lambda i: (0, 0)),
            pl.BlockSpec((QKV, D), lambda i: (0, 0)),
            pl.BlockSpec((QKV, D), lambda i: (0, 0)),
            pl.BlockSpec((HD, 1), lambda i: (0, 0)),
            pl.BlockSpec((HD, 1), lambda i: (0, 0)),
            pl.BlockSpec((HALF, T), lambda i: (0, i)),
            pl.BlockSpec((HALF, T), lambda i: (0, i)),
        ],
        out_specs=[
            pl.BlockSpec((None, H * HD, T), lambda i: (i, 0, 0)),
            pl.BlockSpec((T, KVH * HD), lambda i: (i, 0)),
            pl.BlockSpec((T, KVH * HD), lambda i: (i, 0)),
            pl.BlockSpec((None, KVH * HD, T), lambda i: (i, 0, 0)),
            pl.BlockSpec((H * HD, T), lambda i: (0, 0)),
        ],
        out_shape=[
            jax.ShapeDtypeStruct((NT, H * HD, T), bf16),
            jax.ShapeDtypeStruct((N, KVH * HD), f32),
            jax.ShapeDtypeStruct((N, KVH * HD), f32),
            jax.ShapeDtypeStruct((NT, KVH * HD, T), bf16),
            jax.ShapeDtypeStruct((H * HD, T), f32),
        ],
        scratch_shapes=[pltpu.VMEM((QKV, T), f32)],
        compiler_params=_params("arbitrary"),
        name="qkv",
    )(xp, xs, g, wT, wT_lo, qn, kn, cos, sin)


def _attn_prompt_kernel(qT_ref, kp_ref, kc_ref, vp_ref, vc_ref, sink_ref, bias0_ref, bias_ref,
                        xp_ref, xs_ref, os_ref, wo_ref, wolo_ref, y_ref):
    j = pl.program_id(0)

    @pl.when(j >= NQ // QB)
    def _():
        o_hi, o_lo = _split(os_ref[...])
        wo = wo_ref[...]
        y_ref[...] = (xs_ref[...] + jnp.dot(o_hi, wo, preferred_element_type=f32)
                      + jnp.dot(o_lo, wo, preferred_element_type=f32)
                      + jnp.dot(o_hi, wolo_ref[...], preferred_element_type=f32))

    @pl.when(j < NQ // QB)
    def _():
        k_all = jnp.concatenate([kp_ref[...], kc_ref[...]], axis=0)
        v_all = jnp.concatenate([vp_ref[...], vc_ref[...]], axis=1)
        blocks = []
        for k in range(QB):
            rows = slice(k * TQ, (k + 1) * TQ)
            blocks.append(_attend_block(
                qT_ref.at[:, rows], k_all[k * TQ:(k + 2) * TQ], v_all[:, k * TQ:(k + 2) * TQ],
                sink_ref, bias0_ref if k == 0 else bias_ref, xp_ref.at[rows], wo_ref,
                y_ref.at[rows]))
        for _ in range(3):
            for blk in blocks:
                next(blk, None)


def _attend_block(qT_ref, kk, vv, sink_ref, bias_ref, x_ref, wo_ref, y_ref):
    kk = kk.astype(bf16)
    bias = jnp.concatenate([bias_ref[...]] * H, axis=1)
    qg = [jnp.concatenate([qT_ref[(g * G + hh) * HD:(g * G + hh + 1) * HD, :]
                           for hh in range(G)], axis=1) for g in range(KVH)]
    zeros = jnp.zeros_like(qg[0])
    rhs = jnp.concatenate([jnp.concatenate([qg[0], zeros], axis=1),
                           jnp.concatenate([zeros, qg[1]], axis=1)], axis=0)
    sT = jnp.dot(kk, rhs, preferred_element_type=f32) + bias
    yield
    sink = jnp.concatenate([sink_ref[0], sink_ref[1]], axis=1)
    m = jnp.maximum(jnp.max(sT, axis=0, keepdims=True), sink)
    p = jnp.exp(sT - m)
    l = jnp.sum(p, axis=0, keepdims=True) + jnp.exp(sink - m)
    p = (p * (1.0 / l)).astype(bf16)
    yield
    pieces = []
    for g in range(KVH):
        oT = jnp.dot(vv[g * HD:(g + 1) * HD, :], p[:, g * G * TQ:(g + 1) * G * TQ],
                     preferred_element_type=f32)
        pieces += [oT[:, hh * TQ:(hh + 1) * TQ] for hh in range(G)]
    oT_all = jnp.concatenate(pieces, axis=0)
    y_ref[...] = x_ref[...] + jnp.dot(oT_all.T.astype(bf16), wo_ref[...],
                                      preferred_element_type=f32)


def _band_bias():
    s = np.arange(2 * TQ)[:, None]
    t = np.arange(TQ)[None, :]
    dist = t + TQ - s
    band = (dist >= 0) & (dist <= WIN)
    first = band & (s >= TQ)
    return jnp.asarray(np.where(np.stack([first, band]), 0.0, -np.inf), f32)


def _attn_prompt(qT, ktok, vT, sink_rows, xp, xs, o_s, wo, wo_lo):
    steps = NQ // QB
    rows = QB * TQ
    assert rows == T

    def cur(j):
        return jnp.minimum(j, steps - 1)

    def prev(j):
        return jnp.maximum(cur(j) * QB - 1, 0)

    def prev_tile(j):
        return jnp.maximum(cur(j) - 1, 0)

    def sample(j):
        return jnp.maximum(j - steps, 0)

    bias = _band_bias()
    return pl.pallas_call(
        _attn_prompt_kernel,
        grid=(N // rows,),
        in_specs=[
            pl.BlockSpec((None, H * HD, rows), lambda j: (cur(j), 0, 0)),
            pl.BlockSpec((TQ, KVH * HD), lambda j: (prev(j), 0)),
            pl.BlockSpec((rows, KVH * HD), lambda j: (cur(j), 0)),
            pl.BlockSpec((None, KVH * HD, TQ), lambda j: (prev_tile(j), 0, QB - 1)),
            pl.BlockSpec((None, KVH * HD, rows), lambda j: (cur(j), 0, 0)),
            pl.BlockSpec((KVH, 1, G * TQ), lambda j: (0, 0, 0)),
            pl.BlockSpec((None, 2 * TQ, TQ), lambda j: (jnp.minimum(j, 1), 0, 0)),
            pl.BlockSpec((None, 2 * TQ, TQ), lambda j: (1, 0, 0)),
            pl.BlockSpec((rows, D), lambda j: (cur(j), 0)),
            pl.BlockSpec((rows, D), lambda j: (sample(j), 0)),
            pl.BlockSpec((rows, H * HD), lambda j: (sample(j), 0)),
            pl.BlockSpec((D, D), lambda j: (0, 0)),
            pl.BlockSpec((D, D), lambda j: (0, 0)),
        ],
        out_specs=pl.BlockSpec((rows, D), lambda j: (j, 0)),
        out_shape=jax.ShapeDtypeStruct((N, D), f32),
        compiler_params=_params("parallel"),
        name="attn_prompt",
    )(qT, ktok, ktok, vT, vT, sink_rows, bias, bias, xp, xs, o_s, wo, wo_lo)


def _attn_sample_kernel(q_ref, kc_ref, vc_ref, kn_ref, vn_ref, sink_ref,
                        o_ref, knew_ref, vnew_ref):
    rows = H * TS
    t1 = lax.broadcasted_iota(jnp.int32, (1, rows, WIN), 1) & (TS - 1)
    s1 = lax.broadcasted_iota(jnp.int32, (1, rows, WIN), 2)
    valid1 = s1 >= t1
    t2 = lax.broadcasted_iota(jnp.int32, (1, rows, 8), 1) & (TS - 1)
    s2 = lax.broadcasted_iota(jnp.int32, (1, rows, 8), 2)
    valid2 = s2 <= t2
    sink = sink_ref[...][None]
    q = q_ref[...]
    kc = kc_ref[...]
    vc = vc_ref[...]
    kn = kn_ref[...]
    vn = vn_ref[...]
    def dot3(eq, a, b):
        a_hi, a_lo = _split(a)
        b_hi, b_lo = _split(b)
        return (jnp.einsum(eq, a_hi, b_hi, preferred_element_type=f32)
                + jnp.einsum(eq, a_lo, b_hi, preferred_element_type=f32)
                + jnp.einsum(eq, a_hi, b_lo, preferred_element_type=f32))

    sc = dot3('bqd,bkd->bqk', q, kc)
    sn = dot3('bqd,bkd->bqk', q, kn)
    sc = jnp.where(valid1, sc, -jnp.inf)
    sn = jnp.where(valid2, sn, -jnp.inf)
    m = jnp.maximum(jnp.maximum(jnp.max(sc, axis=-1, keepdims=True),
                                jnp.max(sn, axis=-1, keepdims=True)), sink)
    pc = jnp.exp(sc - m)
    pn = jnp.exp(sn - m)
    l = (jnp.sum(pc, axis=-1, keepdims=True) + jnp.sum(pn, axis=-1, keepdims=True)
         + jnp.exp(sink - m))
    o_ref[...] = (dot3('bqk,bkd->bqd', pc, vc) + dot3('bqk,bkd->bqd', pn, vn)) / l
    knew_ref[:, :WIN - TS, :] = kc[:, TS:, :]
    knew_ref[:, WIN - TS:, :] = kn[:, :TS, :]
    vnew_ref[:, :WIN - TS, :] = vc[:, TS:, :]
    vnew_ref[:, WIN - TS:, :] = vn[:, :TS, :]


def _attn_sample(qbd, kc, vc, kn, vn, sink_col):
    rows = H * TS
    cache_spec = pl.BlockSpec((BB, WIN, KVH * HD), lambda i: (i, 0, 0))
    new_spec = pl.BlockSpec((BB, 8, KVH * HD), lambda i: (i, 0, 0))
    cache_shape = jax.ShapeDtypeStruct((NB, WIN, KVH * HD), f32)
    return pl.pallas_call(
        _attn_sample_kernel,
        grid=(NB // BB,),
        in_specs=[
            pl.BlockSpec((BB, rows, KVH * HD), lambda i: (i, 0, 0)),
            cache_spec, cache_spec, new_spec, new_spec,
            pl.BlockSpec((rows, 1), lambda i: (0, 0)),
        ],
        out_specs=[pl.BlockSpec((BB, rows, KVH * HD), lambda i: (i, 0, 0)),
                   cache_spec, cache_spec],
        out_shape=[jax.ShapeDtypeStruct((NB, rows, KVH * HD), f32), cache_shape, cache_shape],
        compiler_params=_params("parallel"),
        name="attn_sample",
    )(qbd, kc, vc, kn, vn, sink_col)


def _dispatch_kernel(x_ref, g_ref, w_ref, wlo_ref, b_ref, upper_ref, xc_ref, meta_ref, cnt_ref):
    last = pl.program_id(0) == NTM // DT - 1
    tiles = [_dispatch_tile(x_ref.at[k * TM:(k + 1) * TM], g_ref, w_ref,
                            wlo_ref if k >= DT - NS // TM else None, last, b_ref,
                            upper_ref, xc_ref.at[k * RC:(k + 1) * RC],
                            meta_ref.at[k * TM:(k + 1) * TM],
                            cnt_ref.at[k * NE:(k + 1) * NE]) for k in range(DT)]
    for _ in range(3):
        for t in tiles:
            next(t, None)


def _dispatch_kernel_1pass(x_ref, g_ref, w_ref, b_ref, upper_ref, xc_ref, meta_ref, cnt_ref):
    _dispatch_kernel(x_ref, g_ref, w_ref, None, b_ref, upper_ref, xc_ref, meta_ref, cnt_ref)


def _dispatch_tile(x_ref, g_ref, w_ref, wlo_ref, precise, b_ref, upper_ref,
                   xc_ref, meta_ref, cnt_ref):
    hf = _rms(x_ref[...], g_ref[...])
    h_hi, h_lo = _split(hf)
    logits = jnp.dot(h_hi, w_ref[...], preferred_element_type=f32)
    if wlo_ref is not None:
        logits = lax.cond(
            precise,
            lambda: (logits + jnp.dot(h_lo, w_ref[...], preferred_element_type=f32)
                     + jnp.dot(h_hi, wlo_ref[...], preferred_element_type=f32)),
            lambda: logits)
    logits = logits + b_ref[...]
    yield
    lt = logits.T
    inf = jnp.inf
    row8 = lax.broadcasted_iota(jnp.int32, (8, TM), 0).astype(f32)
    gl = jnp.where(row8 < NGRP, lt[0:8], -inf)
    gmax = jnp.max(gl, axis=0, keepdims=True)
    gsel = jnp.min(jnp.where(gl == gmax, row8, 8.0), axis=0, keepdims=True)
    g_w = 1.0 / jnp.sum(jnp.exp(gl - gmax), axis=0, keepdims=True)
    row = lax.broadcasted_iota(jnp.int32, (NE, TM), 0)
    row_f = row.astype(f32)
    el = jnp.where((row >> 2).astype(f32) == gsel, lt[ROW_E:ROW_E + NE], -inf)
    v1 = jnp.max(el, axis=0, keepdims=True)
    i1 = jnp.min(jnp.where(el == v1, row_f, float(NE)), axis=0, keepdims=True)
    el2 = jnp.where(row_f == i1, -inf, el)
    v2 = jnp.max(el2, axis=0, keepdims=True)
    i2 = jnp.min(jnp.where(el2 == v2, row_f, float(NE)), axis=0, keepdims=True)
    e1 = jnp.exp(v2 - v1)
    den = 1.0 + e1
    w1 = (1.0 / den) * g_w
    w2 = (e1 / den) * g_w

    m1 = row_f == i1
    m2 = row_f == i2
    sel = jnp.where(m1 | m2, 1.0, 0.0)
    ranks = jnp.dot(sel.astype(bf16), upper_ref[...],
                    preferred_element_type=f32)
    counts = jnp.sum(sel, axis=1, keepdims=True)
    padded = jnp.floor((counts + (UNIT - 1.0)) * (1.0 / UNIT)) * UNIT
    e_i = lax.broadcasted_iota(jnp.int32, (NE, NE), 0)
    f_i = lax.broadcasted_iota(jnp.int32, (NE, NE), 1)
    below = jnp.where(f_i < e_i, 1.0, 0.0).astype(bf16)
    seg = jnp.dot(below, jnp.broadcast_to(padded, (NE, LANES)).astype(bf16),
                  preferred_element_type=f32)[:, 0:1]
    posall = seg + ranks
    pos1 = jnp.sum(jnp.where(m1, posall, 0.0), axis=0, keepdims=True)
    pos2 = jnp.sum(jnp.where(m2, posall, 0.0), axis=0, keepdims=True)
    cnt_ref[...] = jnp.broadcast_to(counts, (NE, LANES))

    w1_hi = w1.astype(bf16).astype(f32)
    w2_hi = w2.astype(bf16).astype(f32)
    slab = jnp.zeros((8, TM), f32)
    for k, r in enumerate((pos1, pos2, w1_hi, w1 - w1_hi, w2_hi, w2 - w2_hi, i1)):
        slab = jnp.where(row8 == k, r, slab)
    meta = jnp.concatenate([slab, jnp.zeros((LANES - 8, TM), f32)], axis=0).T
    meta_ref[...] = meta
    yield

    rr = lax.broadcasted_iota(jnp.int32, (RC, TM), 0).astype(f32)
    onehot = jnp.where((rr == pos1) | (rr == pos2), 1.0, 0.0).astype(bf16)
    h_aug = jnp.concatenate([h_hi, meta.astype(bf16)], axis=1)
    xc_ref[...] = jnp.dot(onehot, h_aug, preferred_element_type=f32).astype(bf16)


def _dispatch(x, g, w, w_lo, b):
    upper = jnp.asarray(np.triu(np.ones((TM, TM), np.float32), 1), bf16)
    w_spec = pl.BlockSpec((D, LANES), lambda i: (0, 0))
    weights = (w,) if w_lo is None else (w, w_lo)
    return pl.pallas_call(
        _dispatch_kernel_1pass if w_lo is None else _dispatch_kernel,
        grid=(NTM // DT,),
        in_specs=[
            pl.BlockSpec((DT * TM, D), lambda i: (i, 0)),
            pl.BlockSpec((1, D), lambda i: (0, 0)),
            *[w_spec] * len(weights),
            pl.BlockSpec((1, LANES), lambda i: (0, 0)),
            pl.BlockSpec((TM, TM), lambda i: (0, 0)),
        ],
        out_specs=[
            pl.BlockSpec((DT * RC, XW), lambda i: (i, 0)),
            pl.BlockSpec((DT * TM, LANES), lambda i: (i, 0)),
            pl.BlockSpec((DT * NE, LANES), lambda i: (i, 0)),
        ],
        out_shape=[
            jax.ShapeDtypeStruct((NTM * RC, XW), bf16),
            jax.ShapeDtypeStruct((N, LANES), f32),
            jax.ShapeDtypeStruct((NTM * NE, LANES), f32),
        ],
        compiler_params=_params("parallel"),
        name="dispatch",
    )(x, g, *weights, b, upper)


def _dispatch_tables(cnt):
    i32 = jnp.int32
    n = cnt.reshape(NTM, NE, LANES)[:, :, 0].astype(i32)
    units = (n + UNIT - 1) // UNIT
    seg_end = jnp.cumsum(units, axis=1)
    seg_start = seg_end - units
    col_end = jnp.cumsum(units, axis=0)
    col_start = col_end - units
    chunks = (col_end[-1] + UPC - 1) // UPC
    ch_end = jnp.cumsum(chunks)
    ch_start = ch_end - chunks
    nused = ch_end[-1]
    c = jnp.arange(NCH, dtype=i32)
    eid = jnp.minimum(jnp.sum((ch_end[None, :] <= c[:, None]).astype(i32), axis=1), NE - 1)

    src0 = jnp.arange(NTM, dtype=i32)[:, None] * UPT + seg_start
    dst0 = ch_start[None, :] * UPC + col_start
    k = jnp.arange((NCH + NSLOT - 1) * UPC, dtype=i32)[:, None, None]
    inside = (k >= dst0[None]) & (k < (dst0 + units)[None])
    found = jnp.sum(inside.astype(i32), axis=(1, 2))
    shift = jnp.sum(jnp.where(inside, (src0 - dst0)[None], 0), axis=(1, 2))
    ffn_src = jnp.where(found > 0, k[:, 0, 0] + shift, ZERO_UNIT_IN)

    v = jnp.arange(UPT, dtype=i32)[None, :, None]
    inside_v = (v >= seg_start[:, None, :]) & (v < seg_end[:, None, :])
    found_v = jnp.sum(inside_v.astype(i32), axis=2)
    shift_v = jnp.sum(jnp.where(inside_v, (dst0 - seg_start)[:, None, :], 0), axis=2)
    comb_src = jnp.where(found_v > 0, v[:, :, 0] + shift_v, ZERO_UNIT_OUT).reshape(-1)
    own_end = jnp.sum(jnp.where(eid[:, None] == jnp.arange(NE, dtype=i32)[None, :],
                                ch_end[None, :], 0), axis=1)
    eid_at = jnp.sum(jnp.where(c[None, :] == own_end[:, None], eid[None, :], 0), axis=1)
    nxt = jnp.where(own_end < nused, eid_at, -1)
    return eid, nused.reshape(1), ffn_src, nxt, comb_src


def _unit_copies(src_ref, base, n_units, src_hbm, stage, slot, sem):
    out = []
    for j in range(n_units):
        row = pl.multiple_of(src_ref[base + j] * UNIT, UNIT)
        out.append(pltpu.make_async_copy(
            src_hbm.at[pl.ds(row, UNIT), :],
            stage.at[slot, pl.ds(j * UNIT, UNIT), :],
            sem.at[slot]))
    return out


def _start_gathers(step, n_steps, n_units, src_ref, src_hbm, stage, sem):
    ahead = NSLOT - 1

    def start(s, slot):
        for cp in _unit_copies(src_ref, s * n_units, n_units, src_hbm, stage, slot, sem):
            cp.start()

    for s in range(ahead):
        @pl.when((step == 0) & (s < n_steps))
        def _():
            start(s, s)

    @pl.when(step + ahead < n_steps)
    def _():
        start(step + ahead, lax.rem(step + ahead, NSLOT))


def _wait_gather(step, n_units, src_ref, src_hbm, stage, sem):
    slot = lax.rem(step, NSLOT)
    for cp in _unit_copies(src_ref, step * n_units, n_units, src_hbm, stage, slot, sem):
        cp.wait()
    return slot


def _ffn_kernel(layer, eid_ref, nused_ref, src_ref, nxt_ref, xc_hbm, wg_hbm, wu_hbm, wd_hbm,
                o_hbm, stage, sem, wg_f, wu_f, wd_f, w_sem, wg_b, wu_b, wd_b, obuf, o_sem):
    nused = nused_ref[0]
    ahead = NSLOT - 1

    def start(s):
        for cp in _unit_copies(src_ref, s * UPC, UPC, xc_hbm, stage, lax.rem(s, NSLOT), sem):
            cp.start()

    def weight_copies(e, slot):
        return [pltpu.make_async_copy(hbm.at[layer, e], buf.at[slot], w_sem.at[slot])
                for hbm, buf in ((wg_hbm, wg_f), (wu_hbm, wu_f), (wd_hbm, wd_f))]

    def out_copy(c, slot):
        row = pl.multiple_of(c * CH, CH)
        return pltpu.make_async_copy(obuf.at[slot], o_hbm.at[pl.ds(row, CH), :], o_sem.at[slot])

    for s in range(ahead):
        start(s)

    @pl.when(nused > 0)
    def _():
        for cp in weight_copies(eid_ref[0], 0):
            cp.start()

    def chunk(c, n_experts):
        e = eid_ref[c]
        first = (c == 0) | (e != eid_ref[jnp.maximum(c - 1, 0)])
        wslot = lax.rem(n_experts, 2)

        @pl.when(first)
        def _():
            for cp in weight_copies(e, wslot):
                cp.wait()
            wg_b[...] = wg_f[wslot].astype(bf16)
            wu_b[...] = wu_f[wslot].astype(bf16)
            wd_b[...] = wd_f[wslot].astype(bf16)

            @pl.when(nxt_ref[c] >= 0)
            def _():
                for cp in weight_copies(nxt_ref[c], 1 - wslot):
                    cp.start()

        slot = _wait_gather(c, UPC, src_ref, xc_hbm, stage, sem)
        oslot = lax.rem(c, 2)

        @pl.when(c >= 2)
        def _():
            out_copy(c - 2, oslot).wait()

        def block(r):
            xs = stage[slot, r:r + CHB, :]
            x = xs[:, :D]
            a = jnp.dot(x, wg_b[...], preferred_element_type=f32)
            u = jnp.dot(x, wu_b[...], preferred_element_type=f32)
            yield
            gb = xs[:, D:].astype(f32)
            is_first = gb[:, 6:7] == e.astype(f32)
            gate = jnp.where(is_first, gb[:, 2:3] + gb[:, 3:4], gb[:, 4:5] + gb[:, 5:6])
            act = (a * (1.0 / (1.0 + jnp.exp(-a)))) * u * gate
            obuf[oslot, r:r + CHB, :] = jnp.dot(act.astype(bf16), wd_b[...],
                                                preferred_element_type=f32).astype(bf16)

        blocks = [block(r) for r in range(0, CH, CHB)]
        for _ in range(2):
            for blk in blocks:
                next(blk, None)
        out_copy(c, oslot).start()
        start(c + ahead)
        return n_experts + first.astype(jnp.int32)

    lax.fori_loop(0, nused, chunk, jnp.int32(0))

    for s in range(ahead):
        _wait_gather(nused + s, UPC, src_ref, xc_hbm, stage, sem)
    for back in (1, 2):
        @pl.when(nused >= back)
        def _():
            out_copy(nused - back, lax.rem(nused - back, 2)).wait()

    obuf[0] = jnp.zeros((CH, D), bf16)

    def zero_start(c, carry):
        out_copy(c, 0).start()
        return carry

    def zero_wait(c, carry):
        out_copy(c, 0).wait()
        return carry

    lax.fori_loop(nused, NCH, zero_start, 0)
    lax.fori_loop(nused, NCH, zero_wait, 0)


def _ffn(layer, eid, nused, ffn_src, nxt, xc, wg, wu, wd):
    any_spec = pl.BlockSpec(memory_space=pl.ANY)
    return pl.pallas_call(
        functools.partial(_ffn_kernel, layer),
        grid_spec=pltpu.PrefetchScalarGridSpec(
            num_scalar_prefetch=4,
            grid=(1,),
            in_specs=[any_spec, any_spec, any_spec, any_spec],
            out_specs=any_spec,
            scratch_shapes=[
                pltpu.VMEM((NSLOT, CH, XW), bf16),
                pltpu.SemaphoreType.DMA((NSLOT,)),
                pltpu.VMEM((2, D, F), f32),
                pltpu.VMEM((2, D, F), f32),
                pltpu.VMEM((2, F, D), f32),
                pltpu.SemaphoreType.DMA((2,)),
                pltpu.VMEM((D, F), bf16),
                pltpu.VMEM((D, F), bf16),
                pltpu.VMEM((F, D), bf16),
                pltpu.VMEM((2, CH, D), bf16),
                pltpu.SemaphoreType.DMA((2,)),
            ],
        ),
        out_shape=jax.ShapeDtypeStruct((NCH * CH, D), bf16),
        compiler_params=_params("arbitrary"),
        name="ffn",
    )(eid, nused, ffn_src, nxt, xc, wg, wu, wd)


def _combine_tile(x, meta, rows):
    li = lax.broadcasted_iota(jnp.int32, (TM, RC), 1).astype(f32)
    pt = jnp.where((li == meta[:, 0:1]) | (li == meta[:, 1:2]), 1.0, 0.0).astype(bf16)
    return x + jnp.dot(pt, rows, preferred_element_type=f32)


def _combine_kernel(src_ref, x_ref, meta_ref, o_hbm, yp_ref, ys_ref, stage, sem):
    i = pl.program_id(0)
    tiles = T // TM
    _start_gathers(i, NT, tiles * UPT, src_ref, o_hbm, stage, sem)
    slot = _wait_gather(i, tiles * UPT, src_ref, o_hbm, stage, sem)
    ys = [_combine_tile(x_ref[k * TM:(k + 1) * TM, :], meta_ref[k * TM:(k + 1) * TM, :],
                        stage[slot, k * RC:(k + 1) * RC, :]) for k in range(tiles)]
    y = jnp.concatenate(ys, axis=0)

    @pl.when(i < NT - 1)
    def _():
        yp_ref[...] = y

    @pl.when(i == NT - 1)
    def _():
        ys_ref[...] = y


def _combine(comb_src, x, meta, o_sorted):
    out_specs = [pl.BlockSpec((T, D), lambda i, src: (jnp.minimum(i, NT - 2), 0)),
                 pl.BlockSpec((T, D), lambda i, src: (0, 0))]
    out_shape = [jax.ShapeDtypeStruct((NP, D), f32), jax.ShapeDtypeStruct((NS, D), f32)]
    return pl.pallas_call(
        _combine_kernel,
        grid_spec=pltpu.PrefetchScalarGridSpec(
            num_scalar_prefetch=1,
            grid=(NT,),
            in_specs=[
                pl.BlockSpec((T, D), lambda i, src: (i, 0)),
                pl.BlockSpec((T, LANES), lambda i, src: (i, 0)),
                pl.BlockSpec(memory_space=pl.ANY),
            ],
            out_specs=out_specs,
            scratch_shapes=[
                pltpu.VMEM((NSLOT, (T // TM) * RC, D), bf16),
                pltpu.SemaphoreType.DMA((NSLOT,)),
            ],
        ),
        out_shape=out_shape,
        compiler_params=_params("arbitrary"),
        name="combine",
    )(comb_src, x, meta, o_sorted)


def _conv_kernel(src_ref, x1_ref, meta_ref, o_hbm, g_ref, win_ref, cw_ref, wout_ref,
                 p1_ref, p2_ref, y_ref, cu_ref, pad_ref, x_ref, stage, sem):
    i = pl.program_id(0)

    @pl.when(i == 0)
    def _():
        pad_ref[0:8, :] = jnp.zeros((8, D), f32)

    tiles = T // TM
    ahead = NSLOT - 1

    def start(s):
        for cp in _unit_copies(src_ref, s * tiles * UPT, tiles * UPT, o_hbm, stage,
                               lax.rem(s, NSLOT), sem):
            cp.start()

    @pl.when(i == 0)
    def _():
        for s in range(ahead):
            start(s)

    slot = _wait_gather(i, tiles * UPT, src_ref, o_hbm, stage, sem)

    for k in range(tiles):
        rows = slice(k * TM, (k + 1) * TM)
        x_ref[rows, :] = _combine_tile(x1_ref[rows, :], meta_ref[rows, :],
                                       stage[slot, k * RC:(k + 1) * RC, :])

    blocks = [(r, r + T // 2) for r in (0, T // 2)]
    gates = []
    for r0, r1 in blocks:
        h = _rms(x_ref[r0:r1, :], g_ref[...]).astype(bf16)
        bcu = jnp.dot(h, win_ref[...], preferred_element_type=f32)
        gates.append(bcu[:, :D])
        cu = bcu[:, D:2 * D] * bcu[:, 2 * D:]
        pad_ref[8 + r0:8 + r1, :] = cu
        cu_ref[r0:r1, :] = cu
    t = lax.broadcasted_iota(jnp.int32, (T // 2, 1), 0) & (TS - 1)
    is_sample = i == NT - 1
    cw = cw_ref[...]
    for (r0, r1), b in zip(blocks, gates):
        m1 = jnp.where(is_sample & (t == 0), p1_ref[r0:r1, :], pad_ref[7 + r0:7 + r1, :])
        m2 = jnp.where(is_sample & (t < 2), p2_ref[r0:r1, :], pad_ref[6 + r0:6 + r1, :])
        conv = cw[0:1] * m2 + cw[1:2] * m1 + cw[2:3] * pad_ref[8 + r0:8 + r1, :]
        y = jnp.dot((b * conv).astype(bf16), wout_ref[...], preferred_element_type=f32)
        y_ref[r0:r1, :] = x_ref[r0:r1, :] + y
    pad_ref[0:8, :] = pad_ref[T:T + 8, :]
    start(i + ahead)

    @pl.when(i == NT - 1)
    def _():
        for s in range(ahead):
            _wait_gather(i + 1 + s, tiles * UPT, src_ref, o_hbm, stage, sem)


def _conv(comb_src, x1, meta, o_sorted, g, win, cw, wout, p1, p2):
    const = dict(pipeline_mode=pl.Buffered(1))
    comb_src = jnp.concatenate([
        comb_src, jnp.full(((NSLOT - 1) * (T // TM) * UPT,), ZERO_UNIT_OUT, jnp.int32)])
    return pl.pallas_call(
        _conv_kernel,
        grid_spec=pltpu.PrefetchScalarGridSpec(
            num_scalar_prefetch=1,
            grid=(NT,),
            in_specs=[
                pl.BlockSpec((T, D), lambda i, src: (i, 0)),
                pl.BlockSpec((T, LANES), lambda i, src: (i, 0)),
                pl.BlockSpec(memory_space=pl.ANY),
                pl.BlockSpec((1, D), lambda i, src: (0, 0)),
                pl.BlockSpec((D, 3 * D), lambda i, src: (0, 0), **const),
                pl.BlockSpec((3, D), lambda i, src: (0, 0)),
                pl.BlockSpec((D, D), lambda i, src: (0, 0), **const),
                pl.BlockSpec((T, D), lambda i, src: (0, 0), **const),
                pl.BlockSpec((T, D), lambda i, src: (0, 0), **const),
            ],
            out_specs=[
                pl.BlockSpec((T, D), lambda i, src: (i, 0)),
                pl.BlockSpec((T, D), lambda i, src: (jnp.where(i == NT - 1, 1, 0), 0)),
            ],
            scratch_shapes=[
                pltpu.VMEM((T + 8, D), f32),
                pltpu.VMEM((T, D), f32),
                pltpu.VMEM((NSLOT, (T // TM) * RC, D), bf16),
                pltpu.SemaphoreType.DMA((NSLOT,)),
            ],
        ),
        out_shape=[
            jax.ShapeDtypeStruct((N, D), f32),
            jax.ShapeDtypeStruct((2 * T, D), f32),
        ],
        compiler_params=_params("arbitrary"),
        name="conv",
    )(comb_src, x1, meta, o_sorted, g, win, cw, wout, p1, p2)


def _rope_tables():
    inv_freq = THETA ** (-jnp.arange(HALF, dtype=f32) / HALF)
    pos = jnp.concatenate([
        jnp.arange(NP, dtype=jnp.int32),
        PAST + jnp.tile(jnp.arange(TS, dtype=jnp.int32), NB),
    ]).astype(f32)
    ang = inv_freq[:, None] * pos[None, :]
    return jnp.cos(ang), jnp.sin(ang)


def _router_weights(w_group, b_group, w_router, b_router):
    gap = ROW_E - NGRP
    pad = LANES - ROW_E - NE
    w = jnp.concatenate([w_group, jnp.zeros((D, gap), f32), w_router,
                         jnp.zeros((D, pad), f32)], axis=1)
    b = jnp.concatenate([b_group, jnp.zeros((gap,), f32), b_router,
                         jnp.zeros((pad,), f32)])[None, :]
    return w, b


def _split_weights(w):
    hi = w.astype(bf16)
    return hi, (w - hi.astype(f32)).astype(bf16)


def _moe_experts(x, i, norm_ffn, w_group, b_group, w_router, b_router, w_gate, w_up, w_down):
    w, b = _router_weights(w_group[i], b_group[i], w_router[i], b_router[i])
    w, w_lo = _split_weights(w) if i == 0 else (w.astype(bf16), None)
    xc, meta, cnt = _dispatch(x, norm_ffn[i][None, :], w, w_lo, b)
    eid, nused, ffn_src, nxt, comb_src = _dispatch_tables(cnt)
    o_sorted = _ffn(i, eid, nused, ffn_src, nxt, xc, w_gate, w_up, w_down)
    return comb_src, meta, o_sorted


def kernel(x_prompt, x_sample, cache_k, cache_v, state_conv, norm_mix, w_qkv, q_norm, k_norm,
           sinks, w_o, w_in, conv_w, w_out, norm_ffn, w_group, b_group, w_router, b_router,
           w_gate, w_up, w_down):
    xp = x_prompt.reshape(NP, D)
    xs = x_sample.reshape(NS, D)
    moe_w = (norm_ffn, w_group, b_group, w_router, b_router, w_gate, w_up, w_down)

    cos, sin = _rope_tables()
    wT, wT_lo = _split_weights(w_qkv[0].T)
    qT, ktok, vtok, vT, q_s = _qkv(xp, xs, norm_mix[0][None, :], wT, wT_lo,
                                   q_norm[0][:, None], k_norm[0][:, None], cos, sin)
    sink_rows = jnp.repeat(sinks[0].reshape(KVH, G), TQ, axis=1)[:, None, :]

    qs = q_s.reshape(KVH, G, HD, NB, TS).transpose(3, 0, 1, 4, 2)
    zq = jnp.zeros_like(qs[:, 0])
    qbd = jnp.stack([jnp.concatenate([qs[:, 0], zq], axis=-1),
                     jnp.concatenate([zq, qs[:, 1]], axis=-1)], axis=1)
    qbd = qbd.reshape(NB, H * TS, KVH * HD)
    k_new = ktok[NP:].reshape(NB, TS, KVH * HD)
    v_new = vtok[NP:].reshape(NB, TS, KVH * HD)
    pad4 = jnp.zeros((NB, 8 - TS, KVH * HD), f32)
    kc = cache_k[0].reshape(NB, WIN, KVH * HD)
    vc = cache_v[0].reshape(NB, WIN, KVH * HD)
    sink_col = jnp.repeat(sinks[0], TS)[:, None]
    o_s, kc_new, vc_new = _attn_sample(qbd, kc, vc, jnp.concatenate([k_new, pad4], axis=1),
                                       jnp.concatenate([v_new, pad4], axis=1), sink_col)
    o_s = o_s.reshape(NB, KVH, G, TS, KVH, HD)
    o_s = jnp.stack([o_s[:, 0, :, :, 0], o_s[:, 1, :, :, 1]], axis=1)
    o_s = o_s.transpose(0, 3, 1, 2, 4).reshape(NS, H * HD)
    x = _attn_prompt(qT, ktok, vT, sink_rows, xp, xs, o_s, *_split_weights(w_o[0]))
    comb_src, meta, o_sorted = _moe_experts(x, 0, *moe_w)

    new_k_prompt = ktok[NP - WIN:NP].reshape(1, 1, WIN, KVH, HD)
    new_v_prompt = vtok[NP - WIN:NP].reshape(1, 1, WIN, KVH, HD)
    new_k_sample = kc_new.reshape(1, NB, WIN, KVH, HD)
    new_v_sample = vc_new.reshape(1, NB, WIN, KVH, HD)

    st = state_conv[0]
    z = jnp.zeros((NB, 1, D), f32)
    p1 = jnp.concatenate([st[:, 1:2], z, z, z], axis=1).reshape(NS, D)
    p2 = jnp.concatenate([st[:, 0:1], st[:, 1:2], z, z], axis=1).reshape(NS, D)
    x, cu = _conv(comb_src, x, meta, o_sorted, norm_mix[1][None, :], w_in[0].astype(bf16),
                  conv_w[0], w_out[0].astype(bf16), p1, p2)
    comb_src, meta, o_sorted = _moe_experts(x, 1, *moe_w)
    y_prompt, y_sample = _combine(comb_src, x, meta, o_sorted)

    new_conv_prompt = cu[T - 2:T].reshape(1, 1, 2, D)
    new_conv_sample = cu[T:].reshape(NB, TS, D)[:, TS - 2:][None]

    y_prompt = y_prompt.reshape(1, NP, D)
    y_sample = y_sample.reshape(NB, TS, D)
    return (y_prompt, y_sample, new_k_prompt, new_v_prompt, new_conv_prompt,
            new_k_sample, new_v_sample, new_conv_sample)
```

```python
import functools

import jax
import jax.numpy as jnp
import numpy as np
from jax import lax
from jax.experimental import pallas as pl
from jax.experimental.pallas import tpu as pltpu

D = 1024
NP = 16384
NB = 128
TS = 4
NS = NB * TS
N = NP + NS
PAST = 16384
H = 16
KVH = 2
G = H // KVH
HD = 64
HALF = HD // 2
QKV = (H + 2 * KVH) * HD
WIN = 128
THETA = 10000.0
NGRP = 4
EPG = 4
NE = NGRP * EPG
TOPK = 2
ROW_E = 8
F = 512
EPS = 1e-6
SCALE = HD ** -0.5

T = 512
NT = N // T
TQ = 128
NQ = NP // TQ
QB = 4
BB = 16
LANES = 128
V7X_VMEM_BYTES = 64 * 1024 * 1024
VMEM_LIMIT = V7X_VMEM_BYTES - 14 * 1024 * 1024

TM = 256
NTM = N // TM
DT = 6
UNIT = 16
RC = TOPK * TM + NE * UNIT
UPT = RC // UNIT
XW = D + LANES
CH = 512
CHB = 256
UPC = CH // UNIT
NSLOT = 3
NCH = -(-(NTM * (UPT - 1) + NE * (UPC - 1)) // UPC) + NSLOT - 1
ZERO_UNIT_IN = UPT - 1
ZERO_UNIT_OUT = (NCH - 1) * UPC

f32 = jnp.float32
bf16 = jnp.bfloat16


def _params(*sem):
    return pltpu.CompilerParams(dimension_semantics=sem, vmem_limit_bytes=VMEM_LIMIT)


def _rms(x, g):
    ms = jnp.mean(x * x, axis=-1, keepdims=True)
    return x * lax.rsqrt(ms + EPS) * g


def _qkv_prompt_kernel(x_ref, g_ref, wT_ref, qn_ref, kn_ref, cos_ref, sin_ref,
                       qT_ref, ktok_ref, vtok_ref, vT_ref):
    _qkv_tile(x_ref, g_ref, wT_ref, qn_ref, kn_ref, cos_ref, sin_ref,
              qT_ref, ktok_ref, vtok_ref, vT_ref)


def _qkv_sample_kernel(x_ref, g_ref, wT_ref, qn_ref, kn_ref, cos_ref, sin_ref,
                       q_ref, ktok_ref, vtok_ref):
    _qkv_tile(x_ref, g_ref, wT_ref, qn_ref, kn_ref, cos_ref, sin_ref,
              q_ref, ktok_ref, vtok_ref, None)


def _qkv_tile(x_ref, g_ref, wT_ref, qn_ref, kn_ref, cos_ref, sin_ref,
              q_ref, ktok_ref, vtok_ref, vT_ref):
    h = _rms(x_ref[...], g_ref[...]).astype(bf16)
    qkvT = lax.dot_general(wT_ref[...], h, (((1,), (1,)), ((), ())),
                           preferred_element_type=f32)
    cos = cos_ref[...]
    sin = sin_ref[...]

    def norm_rope(blk, gcol):
        ms = jnp.mean(blk * blk, axis=0, keepdims=True)
        y = blk * lax.rsqrt(ms + EPS) * gcol
        y1 = y[:HALF]
        y2 = y[HALF:]
        return y1 * cos - y2 * sin, y2 * cos + y1 * sin

    qn = qn_ref[...]
    for hd in range(H):
        o1, o2 = norm_rope(qkvT[hd * HD:(hd + 1) * HD], qn)
        q_ref[hd * HD:hd * HD + HALF, :] = (o1 * SCALE).astype(bf16)
        q_ref[hd * HD + HALF:(hd + 1) * HD, :] = (o2 * SCALE).astype(bf16)
    kn = kn_ref[...]
    ks = []
    for j in range(KVH):
        o1, o2 = norm_rope(qkvT[H * HD + j * HD:H * HD + (j + 1) * HD], kn)
        ks += [o1, o2]
    kT = jnp.concatenate(ks, axis=0)
    ktok_ref[...] = kT.T
    vT = qkvT[(H + KVH) * HD:]
    vtok_ref[...] = vT.T
    if vT_ref is not None:
        vT_ref[...] = vT.astype(bf16)


def _qkv(xp, xs, g, wT, qn, kn, cos, sin):
    const = [pl.BlockSpec((HD, 1), lambda i: (0, 0)), pl.BlockSpec((HD, 1), lambda i: (0, 0)),
             pl.BlockSpec((HALF, T), lambda i: (0, i)), pl.BlockSpec((HALF, T), lambda i: (0, i))]
    x_spec = pl.BlockSpec((T, D), lambda i: (i, 0))
    g_spec = pl.BlockSpec((1, D), lambda i: (0, 0))
    w_spec = pl.BlockSpec((QKV, D), lambda i: (0, 0))
    tok_spec = pl.BlockSpec((T, KVH * HD), lambda i: (i, 0))
    nt = NP // T
    prompt = pl.pallas_call(
        _qkv_prompt_kernel,
        grid=(nt,),
        in_specs=[x_spec, g_spec, w_spec] + const,
        out_specs=[pl.BlockSpec((None, H * HD, T), lambda i: (i, 0, 0)), tok_spec, tok_spec,
                   pl.BlockSpec((None, KVH * HD, T), lambda i: (i, 0, 0))],
        out_shape=[jax.ShapeDtypeStruct((nt, H * HD, T), bf16),
                   jax.ShapeDtypeStruct((NP, KVH * HD), f32),
                   jax.ShapeDtypeStruct((NP, KVH * HD), f32),
                   jax.ShapeDtypeStruct((nt, KVH * HD, T), bf16)],
        compiler_params=_params("parallel"),
        name="qkv",
    )(xp, g, wT, qn, kn, cos[:, :NP], sin[:, :NP])
    sample = pl.pallas_call(
        _qkv_sample_kernel,
        grid=(NS // T,),
        in_specs=[x_spec, g_spec, w_spec] + const,
        out_specs=[pl.BlockSpec((H * HD, T), lambda i: (0, i)), tok_spec, tok_spec],
        out_shape=[jax.ShapeDtypeStruct((H * HD, NS), bf16),
                   jax.ShapeDtypeStruct((NS, KVH * HD), f32),
                   jax.ShapeDtypeStruct((NS, KVH * HD), f32)],
        compiler_params=_params("parallel"),
        name="qkv_sample",
    )(xs, g, wT, qn, kn, cos[:, NP:], sin[:, NP:])
    return prompt, sample


def _attn_prompt_kernel(qT_ref, kp_ref, kc_ref, vp_ref, vc_ref, sink_ref, bias0_ref, bias_ref,
                        xp_ref, xs_ref, os_ref, wo_ref, y_ref):
    j = pl.program_id(0)

    @pl.when(j >= NQ // QB)
    def _():
        y_ref[...] = xs_ref[...] + jnp.dot(os_ref[...], wo_ref[...],
                                           preferred_element_type=f32)

    @pl.when(j < NQ // QB)
    def _():
        k_all = jnp.concatenate([kp_ref[...], kc_ref[...]], axis=0)
        v_all = jnp.concatenate([vp_ref[...], vc_ref[...]], axis=1)
        blocks = []
        for k in range(QB):
            rows = slice(k * TQ, (k + 1) * TQ)
            blocks.append(_attend_block(
                qT_ref.at[:, rows], k_all[k * TQ:(k + 2) * TQ], v_all[:, k * TQ:(k + 2) * TQ],
                sink_ref, bias0_ref if k == 0 else bias_ref, xp_ref.at[rows], wo_ref,
                y_ref.at[rows]))
        for _ in range(3):
            for blk in blocks:
                next(blk, None)


def _attend_block(qT_ref, kk, vv, sink_ref, bias_ref, x_ref, wo_ref, y_ref):
    kk = kk.astype(bf16)
    bias = jnp.concatenate([bias_ref[...]] * H, axis=1)
    qg = [jnp.concatenate([qT_ref[(g * G + hh) * HD:(g * G + hh + 1) * HD, :]
                           for hh in range(G)], axis=1) for g in range(KVH)]
    zeros = jnp.zeros_like(qg[0])
    rhs = jnp.concatenate([jnp.concatenate([qg[0], zeros], axis=1),
                           jnp.concatenate([zeros, qg[1]], axis=1)], axis=0)
    sT = jnp.dot(kk, rhs, preferred_element_type=f32) + bias
    yield
    sink = jnp.concatenate([sink_ref[0], sink_ref[1]], axis=1)
    m = jnp.maximum(jnp.max(sT, axis=0, keepdims=True), sink)
    p = jnp.exp(sT - m)
    l = jnp.sum(p, axis=0, keepdims=True) + jnp.exp(sink - m)
    p = (p * (1.0 / l)).astype(bf16)
    yield
    pieces = []
    for g in range(KVH):
        oT = jnp.dot(vv[g * HD:(g + 1) * HD, :], p[:, g * G * TQ:(g + 1) * G * TQ],
                     preferred_element_type=f32)
        pieces += [oT[:, hh * TQ:(hh + 1) * TQ] for hh in range(G)]
    oT_all = jnp.concatenate(pieces, axis=0)
    y_ref[...] = x_ref[...] + jnp.dot(oT_all.T.astype(bf16), wo_ref[...],
                                      preferred_element_type=f32)


def _band_bias():
    s = np.arange(2 * TQ)[:, None]
    t = np.arange(TQ)[None, :]
    dist = t + TQ - s
    band = (dist >= 0) & (dist <= WIN)
    first = band & (s >= TQ)
    return jnp.asarray(np.where(np.stack([first, band]), 0.0, -np.inf), f32)


def _attn_prompt(qT, ktok, vT, sink_rows, xp, xs, o_s, wo):
    steps = NQ // QB
    rows = QB * TQ
    assert rows == T

    def cur(j):
        return jnp.minimum(j, steps - 1)

    def prev(j):
        return jnp.maximum(cur(j) * QB - 1, 0)

    def prev_tile(j):
        return jnp.maximum(cur(j) - 1, 0)

    def sample(j):
        return jnp.maximum(j - steps, 0)

    bias = _band_bias()
    return pl.pallas_call(
        _attn_prompt_kernel,
        grid=(N // rows,),
        in_specs=[
            pl.BlockSpec((None, H * HD, rows), lambda j: (cur(j), 0, 0)),
            pl.BlockSpec((TQ, KVH * HD), lambda j: (prev(j), 0)),
            pl.BlockSpec((rows, KVH * HD), lambda j: (cur(j), 0)),
            pl.BlockSpec((None, KVH * HD, TQ), lambda j: (prev_tile(j), 0, QB - 1)),
            pl.BlockSpec((None, KVH * HD, rows), lambda j: (cur(j), 0, 0)),
            pl.BlockSpec((KVH, 1, G * TQ), lambda j: (0, 0, 0)),
            pl.BlockSpec((None, 2 * TQ, TQ), lambda j: (jnp.minimum(j, 1), 0, 0)),
            pl.BlockSpec((None, 2 * TQ, TQ), lambda j: (1, 0, 0)),
            pl.BlockSpec((rows, D), lambda j: (cur(j), 0)),
            pl.BlockSpec((rows, D), lambda j: (sample(j), 0)),
            pl.BlockSpec((rows, H * HD), lambda j: (sample(j), 0)),
            pl.BlockSpec((D, D), lambda j: (0, 0)),
        ],
        out_specs=pl.BlockSpec((rows, D), lambda j: (j, 0)),
        out_shape=jax.ShapeDtypeStruct((N, D), f32),
        compiler_params=_params("parallel"),
        name="attn_prompt",
    )(qT, ktok, ktok, vT, vT, sink_rows, bias, bias, xp, xs, o_s, wo)


def _attn_sample_kernel(q_ref, kc_ref, vc_ref, kn_ref, vn_ref, sink_ref,
                        o_ref, knew_ref, vnew_ref):
    rows = H * TS
    t1 = lax.broadcasted_iota(jnp.int32, (1, rows, WIN), 1) & (TS - 1)
    s1 = lax.broadcasted_iota(jnp.int32, (1, rows, WIN), 2)
    valid1 = s1 >= t1
    t2 = lax.broadcasted_iota(jnp.int32, (1, rows, 8), 1) & (TS - 1)
    s2 = lax.broadcasted_iota(jnp.int32, (1, rows, 8), 2)
    valid2 = s2 <= t2
    sink = sink_ref[...][None]
    q = q_ref[...]
    kc = kc_ref[...]
    vc = vc_ref[...]
    kn = kn_ref[...]
    vn = vn_ref[...]
    sc = jnp.einsum('bqd,bkd->bqk', q, kc.astype(bf16), preferred_element_type=f32)
    sn = jnp.einsum('bqd,bkd->bqk', q, kn.astype(bf16), preferred_element_type=f32)
    sc = jnp.where(valid1, sc, -jnp.inf)
    sn = jnp.where(valid2, sn, -jnp.inf)
    m = jnp.maximum(jnp.maximum(jnp.max(sc, axis=-1, keepdims=True),
                                jnp.max(sn, axis=-1, keepdims=True)), sink)
    pc = jnp.exp(sc - m)
    pn = jnp.exp(sn - m)
    l = (jnp.sum(pc, axis=-1, keepdims=True) + jnp.sum(pn, axis=-1, keepdims=True)
         + jnp.exp(sink - m))
    r = 1.0 / l
    o_ref[...] = (jnp.einsum('bqk,bkd->bqd', (pc * r).astype(bf16), vc.astype(bf16),
                             preferred_element_type=f32)
                  + jnp.einsum('bqk,bkd->bqd', (pn * r).astype(bf16), vn.astype(bf16),
                               preferred_element_type=f32))
    knew_ref[:, :WIN - TS, :] = kc[:, TS:, :]
    knew_ref[:, WIN - TS:, :] = kn[:, :TS, :]
    vnew_ref[:, :WIN - TS, :] = vc[:, TS:, :]
    vnew_ref[:, WIN - TS:, :] = vn[:, :TS, :]


def _attn_sample(qbd, kc, vc, kn, vn, sink_col):
    rows = H * TS
    cache_spec = pl.BlockSpec((BB, WIN, KVH * HD), lambda i: (i, 0, 0))
    new_spec = pl.BlockSpec((BB, 8, KVH * HD), lambda i: (i, 0, 0))
    cache_shape = jax.ShapeDtypeStruct((NB, WIN, KVH * HD), f32)
    return pl.pallas_call(
        _attn_sample_kernel,
        grid=(NB // BB,),
        in_specs=[
            pl.BlockSpec((BB, rows, KVH * HD), lambda i: (i, 0, 0)),
            cache_spec, cache_spec, new_spec, new_spec,
            pl.BlockSpec((rows, 1), lambda i: (0, 0)),
        ],
        out_specs=[pl.BlockSpec((BB, rows, KVH * HD), lambda i: (i, 0, 0)),
                   cache_spec, cache_spec],
        out_shape=[jax.ShapeDtypeStruct((NB, rows, KVH * HD), f32), cache_shape, cache_shape],
        compiler_params=_params("parallel"),
        name="attn_sample",
    )(qbd, kc, vc, kn, vn, sink_col)


def _dispatch_kernel(x_ref, g_ref, w_ref, b_ref, upper_ref, xc_ref, meta_ref, cnt_ref):
    tiles = [_dispatch_tile(x_ref.at[k * TM:(k + 1) * TM], g_ref, w_ref, b_ref, upper_ref,
                            xc_ref.at[k * RC:(k + 1) * RC], meta_ref.at[k * TM:(k + 1) * TM],
                            cnt_ref.at[k * NE:(k + 1) * NE]) for k in range(DT)]
    for _ in range(3):
        for t in tiles:
            next(t, None)


def _dispatch_tile(x_ref, g_ref, w_ref, b_ref, upper_ref, xc_ref, meta_ref, cnt_ref):
    h_hi = _rms(x_ref[...], g_ref[...]).astype(bf16)
    logits = jnp.dot(h_hi, w_ref[...], preferred_element_type=f32) + b_ref[...]
    yield
    lt = logits.T
    inf = jnp.inf
    row8 = lax.broadcasted_iota(jnp.int32, (8, TM), 0).astype(f32)
    gl = jnp.where(row8 < NGRP, lt[0:8], -inf)
    gmax = jnp.max(gl, axis=0, keepdims=True)
    gsel = jnp.min(jnp.where(gl == gmax, row8, 8.0), axis=0, keepdims=True)
    g_w = 1.0 / jnp.sum(jnp.exp(gl - gmax), axis=0, keepdims=True)
    row = lax.broadcasted_iota(jnp.int32, (NE, TM), 0)
    row_f = row.astype(f32)
    el = jnp.where((row >> 2).astype(f32) == gsel, lt[ROW_E:ROW_E + NE], -inf)
    v1 = jnp.max(el, axis=0, keepdims=True)
    i1 = jnp.min(jnp.where(el == v1, row_f, float(NE)), axis=0, keepdims=True)
    el2 = jnp.where(row_f == i1, -inf, el)
    v2 = jnp.max(el2, axis=0, keepdims=True)
    i2 = jnp.min(jnp.where(el2 == v2, row_f, float(NE)), axis=0, keepdims=True)
    e1 = jnp.exp(v2 - v1)
    den = 1.0 + e1
    w1 = (1.0 / den) * g_w
    w2 = (e1 / den) * g_w

    m1 = row_f == i1
    m2 = row_f == i2
    sel = jnp.where(m1 | m2, 1.0, 0.0)
    ranks = jnp.dot(sel.astype(bf16), upper_ref[...],
                    preferred_element_type=f32)
    counts = jnp.sum(sel, axis=1, keepdims=True)
    padded = jnp.floor((counts + (UNIT - 1.0)) * (1.0 / UNIT)) * UNIT
    e_i = lax.broadcasted_iota(jnp.int32, (NE, NE), 0)
    f_i = lax.broadcasted_iota(jnp.int32, (NE, NE), 1)
    below = jnp.where(f_i < e_i, 1.0, 0.0).astype(bf16)
    seg = jnp.dot(below, jnp.broadcast_to(padded, (NE, LANES)).astype(bf16),
                  preferred_element_type=f32)[:, 0:1]
    posall = seg + ranks
    pos1 = jnp.sum(jnp.where(m1, posall, 0.0), axis=0, keepdims=True)
    pos2 = jnp.sum(jnp.where(m2, posall, 0.0), axis=0, keepdims=True)
    cnt_ref[...] = jnp.broadcast_to(counts, (NE, LANES))

    w1_hi = w1.astype(bf16).astype(f32)
    w2_hi = w2.astype(bf16).astype(f32)
    slab = jnp.zeros((8, TM), f32)
    for k, r in enumerate((pos1, pos2, w1_hi, w1 - w1_hi, w2_hi, w2 - w2_hi, i1)):
        slab = jnp.where(row8 == k, r, slab)
    meta = jnp.concatenate([slab, jnp.zeros((LANES - 8, TM), f32)], axis=0).T
    meta_ref[...] = meta
    yield

    rr = lax.broadcasted_iota(jnp.int32, (RC, TM), 0).astype(f32)
    onehot = jnp.where((rr == pos1) | (rr == pos2), 1.0, 0.0).astype(bf16)
    h_aug = jnp.concatenate([h_hi, meta.astype(bf16)], axis=1)
    xc_ref[...] = jnp.dot(onehot, h_aug, preferred_element_type=f32).astype(bf16)


def _dispatch(x, g, w, b):
    upper = jnp.asarray(np.triu(np.ones((TM, TM), np.float32), 1), bf16)
    return pl.pallas_call(
        _dispatch_kernel,
        grid=(NTM // DT,),
        in_specs=[
            pl.BlockSpec((DT * TM, D), lambda i: (i, 0)),
            pl.BlockSpec((1, D), lambda i: (0, 0)),
            pl.BlockSpec((D, LANES), lambda i: (0, 0)),
            pl.BlockSpec((1, LANES), lambda i: (0, 0)),
            pl.BlockSpec((TM, TM), lambda i: (0, 0)),
        ],
        out_specs=[
            pl.BlockSpec((DT * RC, XW), lambda i: (i, 0)),
            pl.BlockSpec((DT * TM, LANES), lambda i: (i, 0)),
            pl.BlockSpec((DT * NE, LANES), lambda i: (i, 0)),
        ],
        out_shape=[
            jax.ShapeDtypeStruct((NTM * RC, XW), bf16),
            jax.ShapeDtypeStruct((N, LANES), f32),
            jax.ShapeDtypeStruct((NTM * NE, LANES), f32),
        ],
        compiler_params=_params("parallel"),
        name="dispatch",
    )(x, g, w, b, upper)


def _dispatch_tables(cnt):
    i32 = jnp.int32
    n = cnt.reshape(NTM, NE, LANES)[:, :, 0].astype(i32)
    units = (n + UNIT - 1) // UNIT
    seg_end = jnp.cumsum(units, axis=1)
    seg_start = seg_end - units
    col_end = jnp.cumsum(units, axis=0)
    col_start = col_end - units
    chunks = (col_end[-1] + UPC - 1) // UPC
    ch_end = jnp.cumsum(chunks)
    ch_start = ch_end - chunks
    nused = ch_end[-1]
    c = jnp.arange(NCH, dtype=i32)
    eid = jnp.minimum(jnp.sum((ch_end[None, :] <= c[:, None]).astype(i32), axis=1), NE - 1)

    src0 = jnp.arange(NTM, dtype=i32)[:, None] * UPT + seg_start
    dst0 = ch_start[None, :] * UPC + col_start
    k = jnp.arange((NCH + NSLOT - 1) * UPC, dtype=i32)[:, None, None]
    inside = (k >= dst0[None]) & (k < (dst0 + units)[None])
    found = jnp.sum(inside.astype(i32), axis=(1, 2))
    shift = jnp.sum(jnp.where(inside, (src0 - dst0)[None], 0), axis=(1, 2))
    ffn_src = jnp.where(found > 0, k[:, 0, 0] + shift, ZERO_UNIT_IN)

    v = jnp.arange(UPT, dtype=i32)[None, :, None]
    inside_v = (v >= seg_start[:, None, :]) & (v < seg_end[:, None, :])
    found_v = jnp.sum(inside_v.astype(i32), axis=2)
    shift_v = jnp.sum(jnp.where(inside_v, (dst0 - seg_start)[:, None, :], 0), axis=2)
    comb_src = jnp.where(found_v > 0, v[:, :, 0] + shift_v, ZERO_UNIT_OUT).reshape(-1)
    own_end = jnp.sum(jnp.where(eid[:, None] == jnp.arange(NE, dtype=i32)[None, :],
                                ch_end[None, :], 0), axis=1)
    eid_at = jnp.sum(jnp.where(c[None, :] == own_end[:, None], eid[None, :], 0), axis=1)
    nxt = jnp.where(own_end < nused, eid_at, -1)
    return eid, nused.reshape(1), ffn_src, nxt, comb_src


def _unit_copies(src_ref, base, n_units, src_hbm, stage, slot, sem):
    out = []
    for j in range(n_units):
        row = pl.multiple_of(src_ref[base + j] * UNIT, UNIT)
        out.append(pltpu.make_async_copy(
            src_hbm.at[pl.ds(row, UNIT), :],
            stage.at[slot, pl.ds(j * UNIT, UNIT), :],
            sem.at[slot]))
    return out


def _start_gathers(step, n_steps, n_units, src_ref, src_hbm, stage, sem):
    ahead = NSLOT - 1

    def start(s, slot):
        for cp in _unit_copies(src_ref, s * n_units, n_units, src_hbm, stage, slot, sem):
            cp.start()

    for s in range(ahead):
        @pl.when((step == 0) & (s < n_steps))
        def _():
            start(s, s)

    @pl.when(step + ahead < n_steps)
    def _():
        start(step + ahead, lax.rem(step + ahead, NSLOT))


def _wait_gather(step, n_units, src_ref, src_hbm, stage, sem):
    slot = lax.rem(step, NSLOT)
    for cp in _unit_copies(src_ref, step * n_units, n_units, src_hbm, stage, slot, sem):
        cp.wait()
    return slot


def _ffn_kernel(layer, eid_ref, nused_ref, src_ref, nxt_ref, xc_hbm, wg_hbm, wu_hbm, wd_hbm,
                o_hbm, stage, sem, wg_f, wu_f, wd_f, w_sem, wg_b, wu_b, wd_b, obuf, o_sem):
    nused = nused_ref[0]
    ahead = NSLOT - 1

    def start(s):
        for cp in _unit_copies(src_ref, s * UPC, UPC, xc_hbm, stage, lax.rem(s, NSLOT), sem):
            cp.start()

    def weight_copies(e, slot):
        return [pltpu.make_async_copy(hbm.at[layer, e], buf.at[slot], w_sem.at[slot])
                for hbm, buf in ((wg_hbm, wg_f), (wu_hbm, wu_f), (wd_hbm, wd_f))]

    def out_copy(c, slot):
        row = pl.multiple_of(c * CH, CH)
        return pltpu.make_async_copy(obuf.at[slot], o_hbm.at[pl.ds(row, CH), :], o_sem.at[slot])

    for s in range(ahead):
        start(s)

    @pl.when(nused > 0)
    def _():
        for cp in weight_copies(eid_ref[0], 0):
            cp.start()

    def chunk(c, n_experts):
        e = eid_ref[c]
        first = (c == 0) | (e != eid_ref[jnp.maximum(c - 1, 0)])
        wslot = lax.rem(n_experts, 2)

        @pl.when(first)
        def _():
            for cp in weight_copies(e, wslot):
                cp.wait()
            wg_b[...] = wg_f[wslot].astype(bf16)
            wu_b[...] = wu_f[wslot].astype(bf16)
            wd_b[...] = wd_f[wslot].astype(bf16)

            @pl.when(nxt_ref[c] >= 0)
            def _():
                for cp in weight_copies(nxt_ref[c], 1 - wslot):
                    cp.start()

        slot = _wait_gather(c, UPC, src_ref, xc_hbm, stage, sem)
        oslot = lax.rem(c, 2)

        @pl.when(c >= 2)
        def _():
            out_copy(c - 2, oslot).wait()

        def block(r):
            xs = stage[slot, r:r + CHB, :]
            x = xs[:, :D]
            a = jnp.dot(x, wg_b[...], preferred_element_type=f32)
            u = jnp.dot(x, wu_b[...], preferred_element_type=f32)
            yield
            gb = xs[:, D:].astype(f32)
            is_first = gb[:, 6:7] == e.astype(f32)
            gate = jnp.where(is_first, gb[:, 2:3] + gb[:, 3:4], gb[:, 4:5] + gb[:, 5:6])
            act = (a * (1.0 / (1.0 + jnp.exp(-a)))) * u * gate
            obuf[oslot, r:r + CHB, :] = jnp.dot(act.astype(bf16), wd_b[...],
                                                preferred_element_type=f32).astype(bf16)

        blocks = [block(r) for r in range(0, CH, CHB)]
        for _ in range(2):
            for blk in blocks:
                next(blk, None)
        out_copy(c, oslot).start()
        start(c + ahead)
        return n_experts + first.astype(jnp.int32)

    lax.fori_loop(0, nused, chunk, jnp.int32(0))

    for s in range(ahead):
        _wait_gather(nused + s, UPC, src_ref, xc_hbm, stage, sem)
    for back in (1, 2):
        @pl.when(nused >= back)
        def _():
            out_copy(nused - back, lax.rem(nused - back, 2)).wait()

    obuf[0] = jnp.zeros((CH, D), bf16)

    def zero_start(c, carry):
        out_copy(c, 0).start()
        return carry

    def zero_wait(c, carry):
        out_copy(c, 0).wait()
        return carry

    lax.fori_loop(nused, NCH, zero_start, 0)
    lax.fori_loop(nused, NCH, zero_wait, 0)


def _ffn(layer, eid, nused, ffn_src, nxt, xc, wg, wu, wd):
    any_spec = pl.BlockSpec(memory_space=pl.ANY)
    return pl.pallas_call(
        functools.partial(_ffn_kernel, layer),
        grid_spec=pltpu.PrefetchScalarGridSpec(
            num_scalar_prefetch=4,
            grid=(1,),
            in_specs=[any_spec, any_spec, any_spec, any_spec],
            out_specs=any_spec,
            scratch_shapes=[
                pltpu.VMEM((NSLOT, CH, XW), bf16),
                pltpu.SemaphoreType.DMA((NSLOT,)),
                pltpu.VMEM((2, D, F), f32),
                pltpu.VMEM((2, D, F), f32),
                pltpu.VMEM((2, F, D), f32),
                pltpu.SemaphoreType.DMA((2,)),
                pltpu.VMEM((D, F), bf16),
                pltpu.VMEM((D, F), bf16),
                pltpu.VMEM((F, D), bf16),
                pltpu.VMEM((2, CH, D), bf16),
                pltpu.SemaphoreType.DMA((2,)),
            ],
        ),
        out_shape=jax.ShapeDtypeStruct((NCH * CH, D), bf16),
        compiler_params=_params("arbitrary"),
        name="ffn",
    )(eid, nused, ffn_src, nxt, xc, wg, wu, wd)


def _combine_tile(x, meta, rows):
    li = lax.broadcasted_iota(jnp.int32, (TM, RC), 1).astype(f32)
    pt = jnp.where((li == meta[:, 0:1]) | (li == meta[:, 1:2]), 1.0, 0.0).astype(bf16)
    return x + jnp.dot(pt, rows, preferred_element_type=f32)


def _combine_kernel(src_ref, x_ref, meta_ref, o_hbm, yp_ref, ys_ref, stage, sem):
    i = pl.program_id(0)
    tiles = T // TM
    _start_gathers(i, NT, tiles * UPT, src_ref, o_hbm, stage, sem)
    slot = _wait_gather(i, tiles * UPT, src_ref, o_hbm, stage, sem)
    ys = [_combine_tile(x_ref[k * TM:(k + 1) * TM, :], meta_ref[k * TM:(k + 1) * TM, :],
                        stage[slot, k * RC:(k + 1) * RC, :]) for k in range(tiles)]
    y = jnp.concatenate(ys, axis=0)

    @pl.when(i < NT - 1)
    def _():
        yp_ref[...] = y

    @pl.when(i == NT - 1)
    def _():
        ys_ref[...] = y


def _combine(comb_src, x, meta, o_sorted):
    out_specs = [pl.BlockSpec((T, D), lambda i, src: (jnp.minimum(i, NT - 2), 0)),
                 pl.BlockSpec((T, D), lambda i, src: (0, 0))]
    out_shape = [jax.ShapeDtypeStruct((NP, D), f32), jax.ShapeDtypeStruct((NS, D), f32)]
    return pl.pallas_call(
        _combine_kernel,
        grid_spec=pltpu.PrefetchScalarGridSpec(
            num_scalar_prefetch=1,
            grid=(NT,),
            in_specs=[
                pl.BlockSpec((T, D), lambda i, src: (i, 0)),
                pl.BlockSpec((T, LANES), lambda i, src: (i, 0)),
                pl.BlockSpec(memory_space=pl.ANY),
            ],
            out_specs=out_specs,
            scratch_shapes=[
                pltpu.VMEM((NSLOT, (T // TM) * RC, D), bf16),
                pltpu.SemaphoreType.DMA((NSLOT,)),
            ],
        ),
        out_shape=out_shape,
        compiler_params=_params("arbitrary"),
        name="combine",
    )(comb_src, x, meta, o_sorted)


def _conv_kernel(src_ref, x1_ref, meta_ref, o_hbm, g_ref, win_ref, cw_ref, wout_ref,
                 p1_ref, p2_ref, y_ref, cu_ref, pad_ref, x_ref, stage, sem):
    i = pl.program_id(0)

    @pl.when(i == 0)
    def _():
        pad_ref[0:8, :] = jnp.zeros((8, D), f32)

    tiles = T // TM
    ahead = NSLOT - 1

    def start(s):
        for cp in _unit_copies(src_ref, s * tiles * UPT, tiles * UPT, o_hbm, stage,
                               lax.rem(s, NSLOT), sem):
            cp.start()

    @pl.when(i == 0)
    def _():
        for s in range(ahead):
            start(s)

    slot = _wait_gather(i, tiles * UPT, src_ref, o_hbm, stage, sem)

    for k in range(tiles):
        rows = slice(k * TM, (k + 1) * TM)
        x_ref[rows, :] = _combine_tile(x1_ref[rows, :], meta_ref[rows, :],
                                       stage[slot, k * RC:(k + 1) * RC, :])

    blocks = [(r, r + T // 2) for r in (0, T // 2)]
    gates = []
    for r0, r1 in blocks:
        h = _rms(x_ref[r0:r1, :], g_ref[...]).astype(bf16)
        bcu = jnp.dot(h, win_ref[...], preferred_element_type=f32)
        gates.append(bcu[:, :D])
        cu = bcu[:, D:2 * D] * bcu[:, 2 * D:]
        pad_ref[8 + r0:8 + r1, :] = cu
        cu_ref[r0:r1, :] = cu
    t = lax.broadcasted_iota(jnp.int32, (T // 2, 1), 0) & (TS - 1)
    is_sample = i == NT - 1
    cw = cw_ref[...]
    for (r0, r1), b in zip(blocks, gates):
        m1 = jnp.where(is_sample & (t == 0), p1_ref[r0:r1, :], pad_ref[7 + r0:7 + r1, :])
        m2 = jnp.where(is_sample & (t < 2), p2_ref[r0:r1, :], pad_ref[6 + r0:6 + r1, :])
        conv = cw[0:1] * m2 + cw[1:2] * m1 + cw[2:3] * pad_ref[8 + r0:8 + r1, :]
        y = jnp.dot((b * conv).astype(bf16), wout_ref[...], preferred_element_type=f32)
        y_ref[r0:r1, :] = x_ref[r0:r1, :] + y
    pad_ref[0:8, :] = pad_ref[T:T + 8, :]
    start(i + ahead)

    @pl.when(i == NT - 1)
    def _():
        for s in range(ahead):
            _wait_gather(i + 1 + s, tiles * UPT, src_ref, o_hbm, stage, sem)


def _conv(comb_src, x1, meta, o_sorted, g, win, cw, wout, p1, p2):
    const = dict(pipeline_mode=pl.Buffered(1))
    comb_src = jnp.concatenate([
        comb_src, jnp.full(((NSLOT - 1) * (T // TM) * UPT,), ZERO_UNIT_OUT, jnp.int32)])
    return pl.pallas_call(
        _conv_kernel,
        grid_spec=pltpu.PrefetchScalarGridSpec(
            num_scalar_prefetch=1,
            grid=(NT,),
            in_specs=[
                pl.BlockSpec((T, D), lambda i, src: (i, 0)),
                pl.BlockSpec((T, LANES), lambda i, src: (i, 0)),
                pl.BlockSpec(memory_space=pl.ANY),
                pl.BlockSpec((1, D), lambda i, src: (0, 0)),
                pl.BlockSpec((D, 3 * D), lambda i, src: (0, 0), **const),
                pl.BlockSpec((3, D), lambda i, src: (0, 0)),
                pl.BlockSpec((D, D), lambda i, src: (0, 0), **const),
                pl.BlockSpec((T, D), lambda i, src: (0, 0), **const),
                pl.BlockSpec((T, D), lambda i, src: (0, 0), **const),
            ],
            out_specs=[
                pl.BlockSpec((T, D), lambda i, src: (i, 0)),
                pl.BlockSpec((T, D), lambda i, src: (jnp.where(i == NT - 1, 1, 0), 0)),
            ],
            scratch_shapes=[
                pltpu.VMEM((T + 8, D), f32),
                pltpu.VMEM((T, D), f32),
                pltpu.VMEM((NSLOT, (T // TM) * RC, D), bf16),
                pltpu.SemaphoreType.DMA((NSLOT,)),
            ],
        ),
        out_shape=[
            jax.ShapeDtypeStruct((N, D), f32),
            jax.ShapeDtypeStruct((2 * T, D), f32),
        ],
        compiler_params=_params("arbitrary"),
        name="conv",
    )(comb_src, x1, meta, o_sorted, g, win, cw, wout, p1, p2)


def _rope_tables():
    inv_freq = THETA ** (-jnp.arange(HALF, dtype=f32) / HALF)
    pos = jnp.concatenate([
        jnp.arange(NP, dtype=jnp.int32),
        PAST + jnp.tile(jnp.arange(TS, dtype=jnp.int32), NB),
    ]).astype(f32)
    ang = inv_freq[:, None] * pos[None, :]
    return jnp.cos(ang), jnp.sin(ang)


def _router_weights(w_group, b_group, w_router, b_router):
    gap = ROW_E - NGRP
    pad = LANES - ROW_E - NE
    w = jnp.concatenate([w_group, jnp.zeros((D, gap), f32), w_router,
                         jnp.zeros((D, pad), f32)], axis=1)
    b = jnp.concatenate([b_group, jnp.zeros((gap,), f32), b_router,
                         jnp.zeros((pad,), f32)])[None, :]
    return w.astype(bf16), b


def _moe_experts(x, i, norm_ffn, w_group, b_group, w_router, b_router, w_gate, w_up, w_down):
    w, b = _router_weights(w_group[i], b_group[i], w_router[i], b_router[i])
    xc, meta, cnt = _dispatch(x, norm_ffn[i][None, :], w, b)
    eid, nused, ffn_src, nxt, comb_src = _dispatch_tables(cnt)
    o_sorted = _ffn(i, eid, nused, ffn_src, nxt, xc, w_gate, w_up, w_down)
    return comb_src, meta, o_sorted


def kernel(x_prompt, x_sample, cache_k, cache_v, state_conv, norm_mix, w_qkv, q_norm, k_norm,
           sinks, w_o, w_in, conv_w, w_out, norm_ffn, w_group, b_group, w_router, b_router,
           w_gate, w_up, w_down):
    xp = x_prompt.reshape(NP, D)
    xs = x_sample.reshape(NS, D)
    moe_w = (norm_ffn, w_group, b_group, w_router, b_router, w_gate, w_up, w_down)

    cos, sin = _rope_tables()
    (qT, ktok, vtok, vT), (q_s, ktok_s, vtok_s) = _qkv(
        xp, xs, norm_mix[0][None, :], w_qkv[0].T.astype(bf16), q_norm[0][:, None],
        k_norm[0][:, None], cos, sin)
    sink_rows = jnp.repeat(sinks[0].reshape(KVH, G), TQ, axis=1)[:, None, :]

    qs = q_s.reshape(KVH, G, HD, NB, TS).transpose(3, 0, 1, 4, 2)
    zq = jnp.zeros_like(qs[:, 0])
    qbd = jnp.stack([jnp.concatenate([qs[:, 0], zq], axis=-1),
                     jnp.concatenate([zq, qs[:, 1]], axis=-1)], axis=1)
    qbd = qbd.reshape(NB, H * TS, KVH * HD)
    k_new = ktok_s.reshape(NB, TS, KVH * HD)
    v_new = vtok_s.reshape(NB, TS, KVH * HD)
    pad4 = jnp.zeros((NB, 8 - TS, KVH * HD), f32)
    kc = cache_k[0].reshape(NB, WIN, KVH * HD)
    vc = cache_v[0].reshape(NB, WIN, KVH * HD)
    sink_col = jnp.repeat(sinks[0], TS)[:, None]
    o_s, kc_new, vc_new = _attn_sample(qbd, kc, vc, jnp.concatenate([k_new, pad4], axis=1),
                                       jnp.concatenate([v_new, pad4], axis=1), sink_col)
    o_s = o_s.reshape(NB, KVH, G, TS, KVH, HD)
    o_s = jnp.stack([o_s[:, 0, :, :, 0], o_s[:, 1, :, :, 1]], axis=1)
    o_s = o_s.transpose(0, 3, 1, 2, 4).reshape(NS, H * HD).astype(bf16)
    x = _attn_prompt(qT, ktok, vT, sink_rows, xp, xs, o_s, w_o[0].astype(bf16))
    comb_src, meta, o_sorted = _moe_experts(x, 0, *moe_w)

    new_k_prompt = ktok[NP - WIN:NP].reshape(1, 1, WIN, KVH, HD)
    new_v_prompt = vtok[NP - WIN:NP].reshape(1, 1, WIN, KVH, HD)
    new_k_sample = kc_new.reshape(1, NB, WIN, KVH, HD)
    new_v_sample = vc_new.reshape(1, NB, WIN, KVH, HD)

    st = state_conv[0]
    z = jnp.zeros((NB, 1, D), f32)
    p1 = jnp.concatenate([st[:, 1:2], z, z, z], axis=1).reshape(NS, D)
    p2 = jnp.concatenate([st[:, 0:1], st[:, 1:2], z, z], axis=1).reshape(NS, D)
    x, cu = _conv(comb_src, x, meta, o_sorted, norm_mix[1][None, :], w_in[0].astype(bf16),
                  conv_w[0], w_out[0].astype(bf16), p1, p2)
    comb_src, meta, o_sorted = _moe_experts(x, 1, *moe_w)
    y_prompt, y_sample = _combine(comb_src, x, meta, o_sorted)

    new_conv_prompt = cu[T - 2:T].reshape(1, 1, 2, D)
    new_conv_sample = cu[T:].reshape(NB, TS, D)[:, TS - 2:][None]

    y_prompt = y_prompt.reshape(1, NP, D)
    y_sample = y_sample.reshape(NB, TS, D)
    return (y_prompt, y_sample, new_k_prompt, new_v_prompt, new_conv_prompt,
            new_k_sample, new_v_sample, new_conv_sample)
```

```python
import functools

import jax
import jax.numpy as jnp
import numpy as np
from jax import lax
from jax.experimental import pallas as pl
from jax.experimental.pallas import tpu as pltpu

D = 1024
NP = 16384
NB = 128
TS = 4
NS = NB * TS
N = NP + NS
PAST = 16384
H = 16
KVH = 2
G = H // KVH
HD = 64
HALF = HD // 2
QKV = (H + 2 * KVH) * HD
WIN = 128
THETA = 10000.0
NGRP = 4
EPG = 4
NE = NGRP * EPG
TOPK = 2
ROW_E = 8
F = 512
EPS = 1e-6
SCALE = HD ** -0.5

T = 512
NT = N // T
TQ = 128
NQ = NP // TQ
QB = 4
BB = 16
LANES = 128
V7X_VMEM_BYTES = 64 * 1024 * 1024
VMEM_LIMIT = V7X_VMEM_BYTES - 14 * 1024 * 1024

TM = 256
NTM = N // TM
DT = 6
UNIT = 16
RC = TOPK * TM + NE * UNIT
UPT = RC // UNIT
XW = D + LANES
CH = 512
CHB = 256
UPC = CH // UNIT
NSLOT = 3
NCH = -(-(NTM * (UPT - 1) + NE * (UPC - 1)) // UPC) + NSLOT - 1
ZERO_UNIT_IN = UPT - 1
ZERO_UNIT_OUT = (NCH - 1) * UPC

f32 = jnp.float32
bf16 = jnp.bfloat16


def _params(*sem):
    return pltpu.CompilerParams(dimension_semantics=sem, vmem_limit_bytes=VMEM_LIMIT)


def _rms(x, g):
    ms = jnp.mean(x * x, axis=-1, keepdims=True)
    return x * lax.rsqrt(ms + EPS) * g


def _qkv_prompt_kernel(x_ref, g_ref, wT_ref, qn_ref, kn_ref, cos_ref, sin_ref,
                       qT_ref, ktok_ref, vtok_ref, vT_ref):
    _qkv_tile(x_ref, g_ref, wT_ref, qn_ref, kn_ref, cos_ref, sin_ref,
              qT_ref, ktok_ref, vtok_ref, vT_ref)


def _qkv_sample_kernel(x_ref, g_ref, wT_ref, qn_ref, kn_ref, cos_ref, sin_ref,
                       q_ref, ktok_ref, vtok_ref):
    _qkv_tile(x_ref, g_ref, wT_ref, qn_ref, kn_ref, cos_ref, sin_ref,
              q_ref, ktok_ref, vtok_ref, None)


def _qkv_tile(x_ref, g_ref, wT_ref, qn_ref, kn_ref, cos_ref, sin_ref,
              q_ref, ktok_ref, vtok_ref, vT_ref):
    h = _rms(x_ref[...], g_ref[...]).astype(bf16)
    qkvT = lax.dot_general(wT_ref[...], h, (((1,), (1,)), ((), ())),
                           preferred_element_type=f32)
    cos = cos_ref[...]
    sin = sin_ref[...]

    def norm_rope(blk, gcol):
        ms = jnp.mean(blk * blk, axis=0, keepdims=True)
        y = blk * lax.rsqrt(ms + EPS) * gcol
        y1 = y[:HALF]
        y2 = y[HALF:]
        return y1 * cos - y2 * sin, y2 * cos + y1 * sin

    qn = qn_ref[...]
    for hd in range(H):
        o1, o2 = norm_rope(qkvT[hd * HD:(hd + 1) * HD], qn)
        q_ref[hd * HD:hd * HD + HALF, :] = (o1 * SCALE).astype(bf16)
        q_ref[hd * HD + HALF:(hd + 1) * HD, :] = (o2 * SCALE).astype(bf16)
    kn = kn_ref[...]
    ks = []
    for j in range(KVH):
        o1, o2 = norm_rope(qkvT[H * HD + j * HD:H * HD + (j + 1) * HD], kn)
        ks += [o1, o2]
    kT = jnp.concatenate(ks, axis=0)
    ktok_ref[...] = kT.T
    vT = qkvT[(H + KVH) * HD:]
    vtok_ref[...] = vT.T
    if vT_ref is not None:
        vT_ref[...] = vT.astype(bf16)


def _qkv(xp, xs, g, wT, qn, kn, cos, sin):
    const = [pl.BlockSpec((HD, 1), lambda i: (0, 0)), pl.BlockSpec((HD, 1), lambda i: (0, 0)),
             pl.BlockSpec((HALF, T), lambda i: (0, i)), pl.BlockSpec((HALF, T), lambda i: (0, i))]
    x_spec = pl.BlockSpec((T, D), lambda i: (i, 0))
    g_spec = pl.BlockSpec((1, D), lambda i: (0, 0))
    w_spec = pl.BlockSpec((QKV, D), lambda i: (0, 0))
    tok_spec = pl.BlockSpec((T, KVH * HD), lambda i: (i, 0))
    nt = NP // T
    prompt = pl.pallas_call(
        _qkv_prompt_kernel,
        grid=(nt,),
        in_specs=[x_spec, g_spec, w_spec] + const,
        out_specs=[pl.BlockSpec((None, H * HD, T), lambda i: (i, 0, 0)), tok_spec, tok_spec,
                   pl.BlockSpec((None, KVH * HD, T), lambda i: (i, 0, 0))],
        out_shape=[jax.ShapeDtypeStruct((nt, H * HD, T), bf16),
                   jax.ShapeDtypeStruct((NP, KVH * HD), f32),
                   jax.ShapeDtypeStruct((NP, KVH * HD), f32),
                   jax.ShapeDtypeStruct((nt, KVH * HD, T), bf16)],
        compiler_params=_params("parallel"),
        name="qkv",
    )(xp, g, wT, qn, kn, cos[:, :NP], sin[:, :NP])
    sample = pl.pallas_call(
        _qkv_sample_kernel,
        grid=(NS // T,),
        in_specs=[x_spec, g_spec, w_spec] + const,
        out_specs=[pl.BlockSpec((H * HD, T), lambda i: (0, i)), tok_spec, tok_spec],
        out_shape=[jax.ShapeDtypeStruct((H * HD, NS), bf16),
                   jax.ShapeDtypeStruct((NS, KVH * HD), f32),
                   jax.ShapeDtypeStruct((NS, KVH * HD), f32)],
        compiler_params=_params("parallel"),
        name="qkv_sample",
    )(xs, g, wT, qn, kn, cos[:, NP:], sin[:, NP:])
    return prompt, sample


def _attn_prompt_kernel(qT_ref, kp_ref, kc_ref, vp_ref, vc_ref, sink_ref, bias0_ref, bias_ref,
                        xp_ref, xs_ref, os_ref, wo_ref, y_ref):
    j = pl.program_id(0)

    @pl.when(j >= NQ // QB)
    def _():
        y_ref[...] = xs_ref[...] + jnp.dot(os_ref[...], wo_ref[...],
                                           preferred_element_type=f32)

    @pl.when(j < NQ // QB)
    def _():
        k_all = jnp.concatenate([kp_ref[...], kc_ref[...]], axis=0)
        v_all = jnp.concatenate([vp_ref[...], vc_ref[...]], axis=1)
        blocks = []
        for k in range(QB):
            rows = slice(k * TQ, (k + 1) * TQ)
            blocks.append(_attend_block(
                qT_ref.at[:, rows], k_all[k * TQ:(k + 2) * TQ], v_all[:, k * TQ:(k + 2) * TQ],
                sink_ref, bias0_ref if k == 0 else bias_ref, xp_ref.at[rows], wo_ref,
                y_ref.at[rows]))
        for _ in range(3):
            for blk in blocks:
                next(blk, None)


def _attend_block(qT_ref, kk, vv, sink_ref, bias_ref, x_ref, wo_ref, y_ref):
    kk = kk.astype(bf16)
    bias = jnp.concatenate([bias_ref[...]] * H, axis=1)
    qg = [jnp.concatenate([qT_ref[(g * G + hh) * HD:(g * G + hh + 1) * HD, :]
                           for hh in range(G)], axis=1) for g in range(KVH)]
    zeros = jnp.zeros_like(qg[0])
    rhs = jnp.concatenate([jnp.concatenate([qg[0], zeros], axis=1),
                           jnp.concatenate([zeros, qg[1]], axis=1)], axis=0)
    sT = jnp.dot(kk, rhs, preferred_element_type=f32) + bias
    yield
    sink = jnp.concatenate([sink_ref[0], sink_ref[1]], axis=1)
    m = jnp.maximum(jnp.max(sT, axis=0, keepdims=True), sink)
    p = jnp.exp(sT - m)
    l = jnp.sum(p, axis=0, keepdims=True) + jnp.exp(sink - m)
    p = (p * (1.0 / l)).astype(bf16)
    yield
    pieces = []
    for g in range(KVH):
        oT = jnp.dot(vv[g * HD:(g + 1) * HD, :], p[:, g * G * TQ:(g + 1) * G * TQ],
                     preferred_element_type=f32)
        pieces += [oT[:, hh * TQ:(hh + 1) * TQ] for hh in range(G)]
    oT_all = jnp.concatenate(pieces, axis=0)
    y_ref[...] = x_ref[...] + jnp.dot(oT_all.T.astype(bf16), wo_ref[...],
                                      preferred_element_type=f32)


def _band_bias():
    s = np.arange(2 * TQ)[:, None]
    t = np.arange(TQ)[None, :]
    dist = t + TQ - s
    band = (dist >= 0) & (dist <= WIN)
    first = band & (s >= TQ)
    return jnp.asarray(np.where(np.stack([first, band]), 0.0, -np.inf), f32)


def _attn_prompt(qT, ktok, vT, sink_rows, xp, xs, o_s, wo):
    steps = NQ // QB
    rows = QB * TQ
    assert rows == T

    def cur(j):
        return jnp.minimum(j, steps - 1)

    def prev(j):
        return jnp.maximum(cur(j) * QB - 1, 0)

    def prev_tile(j):
        return jnp.maximum(cur(j) - 1, 0)

    def sample(j):
        return jnp.maximum(j - steps, 0)

    bias = _band_bias()
    return pl.pallas_call(
        _attn_prompt_kernel,
        grid=(N // rows,),
        in_specs=[
            pl.BlockSpec((None, H * HD, rows), lambda j: (cur(j), 0, 0)),
            pl.BlockSpec((TQ, KVH * HD), lambda j: (prev(j), 0)),
            pl.BlockSpec((rows, KVH * HD), lambda j: (cur(j), 0)),
            pl.BlockSpec((None, KVH * HD, TQ), lambda j: (prev_tile(j), 0, QB - 1)),
            pl.BlockSpec((None, KVH * HD, rows), lambda j: (cur(j), 0, 0)),
            pl.BlockSpec((KVH, 1, G * TQ), lambda j: (0, 0, 0)),
            pl.BlockSpec((None, 2 * TQ, TQ), lambda j: (jnp.minimum(j, 1), 0, 0)),
            pl.BlockSpec((None, 2 * TQ, TQ), lambda j: (1, 0, 0)),
            pl.BlockSpec((rows, D), lambda j: (cur(j), 0)),
            pl.BlockSpec((rows, D), lambda j: (sample(j), 0)),
            pl.BlockSpec((rows, H * HD), lambda j: (sample(j), 0)),
            pl.BlockSpec((D, D), lambda j: (0, 0)),
        ],
        out_specs=pl.BlockSpec((rows, D), lambda j: (j, 0)),
        out_shape=jax.ShapeDtypeStruct((N, D), f32),
        compiler_params=_params("parallel"),
        name="attn_prompt",
    )(qT, ktok, ktok, vT, vT, sink_rows, bias, bias, xp, xs, o_s, wo)


def _attn_sample_kernel(q_ref, kc_ref, vc_ref, kn_ref, vn_ref, sink_ref,
                        o_ref, knew_ref, vnew_ref):
    rows = H * TS
    t1 = lax.broadcasted_iota(jnp.int32, (1, rows, WIN), 1) & (TS - 1)
    s1 = lax.broadcasted_iota(jnp.int32, (1, rows, WIN), 2)
    valid1 = s1 >= t1
    t2 = lax.broadcasted_iota(jnp.int32, (1, rows, 8), 1) & (TS - 1)
    s2 = lax.broadcasted_iota(jnp.int32, (1, rows, 8), 2)
    valid2 = s2 <= t2
    sink = sink_ref[...][None]
    q = q_ref[...]
    kc = kc_ref[...]
    vc = vc_ref[...]
    kn = kn_ref[...]
    vn = vn_ref[...]
    sc = jnp.einsum('bqd,bkd->bqk', q, kc.astype(bf16), preferred_element_type=f32)
    sn = jnp.einsum('bqd,bkd->bqk', q, kn.astype(bf16), preferred_element_type=f32)
    sc = jnp.where(valid1, sc, -jnp.inf)
    sn = jnp.where(valid2, sn, -jnp.inf)
    m = jnp.maximum(jnp.maximum(jnp.max(sc, axis=-1, keepdims=True),
                                jnp.max(sn, axis=-1, keepdims=True)), sink)
    pc = jnp.exp(sc - m)
    pn = jnp.exp(sn - m)
    l = (jnp.sum(pc, axis=-1, keepdims=True) + jnp.sum(pn, axis=-1, keepdims=True)
         + jnp.exp(sink - m))
    r = 1.0 / l
    o_ref[...] = (jnp.einsum('bqk,bkd->bqd', (pc * r).astype(bf16), vc.astype(bf16),
                             preferred_element_type=f32)
                  + jnp.einsum('bqk,bkd->bqd', (pn * r).astype(bf16), vn.astype(bf16),
                               preferred_element_type=f32))
    knew_ref[:, :WIN - TS, :] = kc[:, TS:, :]
    knew_ref[:, WIN - TS:, :] = kn[:, :TS, :]
    vnew_ref[:, :WIN - TS, :] = vc[:, TS:, :]
    vnew_ref[:, WIN - TS:, :] = vn[:, :TS, :]


def _attn_sample(qbd, kc, vc, kn, vn, sink_col):
    rows = H * TS
    cache_spec = pl.BlockSpec((BB, WIN, KVH * HD), lambda i: (i, 0, 0))
    new_spec = pl.BlockSpec((BB, 8, KVH * HD), lambda i: (i, 0, 0))
    cache_shape = jax.ShapeDtypeStruct((NB, WIN, KVH * HD), f32)
    return pl.pallas_call(
        _attn_sample_kernel,
        grid=(NB // BB,),
        in_specs=[
            pl.BlockSpec((BB, rows, KVH * HD), lambda i: (i, 0, 0)),
            cache_spec, cache_spec, new_spec, new_spec,
            pl.BlockSpec((rows, 1), lambda i: (0, 0)),
        ],
        out_specs=[pl.BlockSpec((BB, rows, KVH * HD), lambda i: (i, 0, 0)),
                   cache_spec, cache_spec],
        out_shape=[jax.ShapeDtypeStruct((NB, rows, KVH * HD), f32), cache_shape, cache_shape],
        compiler_params=_params("parallel"),
        name="attn_sample",
    )(qbd, kc, vc, kn, vn, sink_col)


def _dispatch_kernel(x_ref, g_ref, w_ref, b_ref, upper_ref, xc_ref, meta_ref, cnt_ref):
    tiles = [_dispatch_tile(x_ref.at[k * TM:(k + 1) * TM], g_ref, w_ref, b_ref, upper_ref,
                            xc_ref.at[k * RC:(k + 1) * RC], meta_ref.at[k * TM:(k + 1) * TM],
                            cnt_ref.at[k * NE:(k + 1) * NE]) for k in range(DT)]
    for _ in range(3):
        for t in tiles:
            next(t, None)


def _dispatch_tile(x_ref, g_ref, w_ref, b_ref, upper_ref, xc_ref, meta_ref, cnt_ref):
    h_hi = _rms(x_ref[...], g_ref[...]).astype(bf16)
    logits = jnp.dot(h_hi, w_ref[...], preferred_element_type=f32) + b_ref[...]
    yield
    lt = logits.T
    inf = jnp.inf
    row8 = lax.broadcasted_iota(jnp.int32, (8, TM), 0).astype(f32)
    gl = jnp.where(row8 < NGRP, lt[0:8], -inf)
    gmax = jnp.max(gl, axis=0, keepdims=True)
    gsel = jnp.min(jnp.where(gl == gmax, row8, 8.0), axis=0, keepdims=True)
    g_w = 1.0 / jnp.sum(jnp.exp(gl - gmax), axis=0, keepdims=True)
    row = lax.broadcasted_iota(jnp.int32, (NE, TM), 0)
    row_f = row.astype(f32)
    el = jnp.where((row >> 2).astype(f32) == gsel, lt[ROW_E:ROW_E + NE], -inf)
    v1 = jnp.max(el, axis=0, keepdims=True)
    i1 = jnp.min(jnp.where(el == v1, row_f, float(NE)), axis=0, keepdims=True)
    el2 = jnp.where(row_f == i1, -inf, el)
    v2 = jnp.max(el2, axis=0, keepdims=True)
    i2 = jnp.min(jnp.where(el2 == v2, row_f, float(NE)), axis=0, keepdims=True)
    e1 = jnp.exp(v2 - v1)
    den = 1.0 + e1
    w1 = (1.0 / den) * g_w
    w2 = (e1 / den) * g_w

    m1 = row_f == i1
    m2 = row_f == i2
    sel = jnp.where(m1 | m2, 1.0, 0.0)
    ranks = jnp.dot(sel.astype(bf16), upper_ref[...],
                    preferred_element_type=f32)
    counts = jnp.sum(sel, axis=1, keepdims=True)
    padded = jnp.floor((counts + (UNIT - 1.0)) * (1.0 / UNIT)) * UNIT
    e_i = lax.broadcasted_iota(jnp.int32, (NE, NE), 0)
    f_i = lax.broadcasted_iota(jnp.int32, (NE, NE), 1)
    below = jnp.where(f_i < e_i, 1.0, 0.0).astype(bf16)
    seg = jnp.dot(below, jnp.broadcast_to(padded, (NE, LANES)).astype(bf16),
                  preferred_element_type=f32)[:, 0:1]
    posall = seg + ranks
    pos1 = jnp.sum(jnp.where(m1, posall, 0.0), axis=0, keepdims=True)
    pos2 = jnp.sum(jnp.where(m2, posall, 0.0), axis=0, keepdims=True)
    cnt_ref[...] = jnp.broadcast_to(counts, (NE, LANES))

    w1_hi = w1.astype(bf16).astype(f32)
    w2_hi = w2.astype(bf16).astype(f32)
    slab = jnp.zeros((8, TM), f32)
    for k, r in enumerate((pos1, pos2, w1_hi, w1 - w1_hi, w2_hi, w2 - w2_hi, i1)):
        slab = jnp.where(row8 == k, r, slab)
    meta = jnp.concatenate([slab, jnp.zeros((LANES - 8, TM), f32)], axis=0).T
    meta_ref[...] = meta
    yield

    rr = lax.broadcasted_iota(jnp.int32, (RC, TM), 0).astype(f32)
    onehot = jnp.where((rr == pos1) | (rr == pos2), 1.0, 0.0).astype(bf16)
    h_aug = jnp.concatenate([h_hi, meta.astype(bf16)], axis=1)
    xc_ref[...] = jnp.dot(onehot, h_aug, preferred_element_type=f32).astype(bf16)


def _dispatch(x, g, w, b):
    upper = jnp.asarray(np.triu(np.ones((TM, TM), np.float32), 1), bf16)
    return pl.pallas_call(
        _dispatch_kernel,
        grid=(NTM // DT,),
        in_specs=[
            pl.BlockSpec((DT * TM, D), lambda i: (i, 0)),
            pl.BlockSpec((1, D), lambda i: (0, 0)),
            pl.BlockSpec((D, LANES), lambda i: (0, 0)),
            pl.BlockSpec((1, LANES), lambda i: (0, 0)),
            pl.BlockSpec((TM, TM), lambda i: (0, 0)),
        ],
        out_specs=[
            pl.BlockSpec((DT * RC, XW), lambda i: (i, 0)),
            pl.BlockSpec((DT * TM, LANES), lambda i: (i, 0)),
            pl.BlockSpec((DT * NE, LANES), lambda i: (i, 0)),
        ],
        out_shape=[
            jax.ShapeDtypeStruct((NTM * RC, XW), bf16),
            jax.ShapeDtypeStruct((N, LANES), f32),
            jax.ShapeDtypeStruct((NTM * NE, LANES), f32),
        ],
        compiler_params=_params("parallel"),
        name="dispatch",
    )(x, g, w, b, upper)


def _dispatch_tables(cnt):
    i32 = jnp.int32
    n = cnt.reshape(NTM, NE, LANES)[:, :, 0].astype(i32)
    units = (n + UNIT - 1) // UNIT
    seg_end = jnp.cumsum(units, axis=1)
    seg_start = seg_end - units
    col_end = jnp.cumsum(units, axis=0)
    col_start = col_end - units
    chunks = (col_end[-1] + UPC - 1) // UPC
    ch_end = jnp.cumsum(chunks)
    ch_start = ch_end - chunks
    nused = ch_end[-1]
    c = jnp.arange(NCH, dtype=i32)
    eid = jnp.minimum(jnp.sum((ch_end[None, :] <= c[:, None]).astype(i32), axis=1), NE - 1)

    src0 = jnp.arange(NTM, dtype=i32)[:, None] * UPT + seg_start
    dst0 = ch_start[None, :] * UPC + col_start
    k = jnp.arange((NCH + NSLOT - 1) * UPC, dtype=i32)[:, None, None]
    inside = (k >= dst0[None]) & (k < (dst0 + units)[None])
    found = jnp.sum(inside.astype(i32), axis=(1, 2))
    shift = jnp.sum(jnp.where(inside, (src0 - dst0)[None], 0), axis=(1, 2))
    ffn_src = jnp.where(found > 0, k[:, 0, 0] + shift, ZERO_UNIT_IN)

    v = jnp.arange(UPT, dtype=i32)[None, :, None]
    inside_v = (v >= seg_start[:, None, :]) & (v < seg_end[:, None, :])
    found_v = jnp.sum(inside_v.astype(i32), axis=2)
    shift_v = jnp.sum(jnp.where(inside_v, (dst0 - seg_start)[:, None, :], 0), axis=2)
    comb_src = jnp.where(found_v > 0, v[:, :, 0] + shift_v, ZERO_UNIT_OUT).reshape(-1)
    own_end = jnp.sum(jnp.where(eid[:, None] == jnp.arange(NE, dtype=i32)[None, :],
                                ch_end[None, :], 0), axis=1)
    eid_at = jnp.sum(jnp.where(c[None, :] == own_end[:, None], eid[None, :], 0), axis=1)
    nxt = jnp.where(own_end < nused, eid_at, -1)
    own = eid[:, None] == jnp.arange(NE, dtype=i32)[None, :]
    own_start = jnp.sum(jnp.where(own, ch_start[None, :], 0), axis=1)
    own_units = jnp.sum(jnp.where(own, col_end[-1][None, :], 0), axis=1)
    half = (own_units - (c - own_start) * UPC <= UPC // 2).astype(i32)
    return eid, nused.reshape(1), ffn_src, nxt, half, comb_src


def _unit_copies(src_ref, base, n_units, src_hbm, stage, slot, sem):
    out = []
    for j in range(n_units):
        row = pl.multiple_of(src_ref[base + j] * UNIT, UNIT)
        out.append(pltpu.make_async_copy(
            src_hbm.at[pl.ds(row, UNIT), :],
            stage.at[slot, pl.ds(j * UNIT, UNIT), :],
            sem.at[slot]))
    return out


def _start_gathers(step, n_steps, n_units, src_ref, src_hbm, stage, sem):
    ahead = NSLOT - 1

    def start(s, slot):
        for cp in _unit_copies(src_ref, s * n_units, n_units, src_hbm, stage, slot, sem):
            cp.start()

    for s in range(ahead):
        @pl.when((step == 0) & (s < n_steps))
        def _():
            start(s, s)

    @pl.when(step + ahead < n_steps)
    def _():
        start(step + ahead, lax.rem(step + ahead, NSLOT))


def _wait_gather(step, n_units, src_ref, src_hbm, stage, sem):
    slot = lax.rem(step, NSLOT)
    for cp in _unit_copies(src_ref, step * n_units, n_units, src_hbm, stage, slot, sem):
        cp.wait()
    return slot


def _ffn_kernel(layer, eid_ref, nused_ref, src_ref, nxt_ref, half_ref,
                xc_hbm, wg_hbm, wu_hbm, wd_hbm,
                o_hbm, stage, sem, wg_f, wu_f, wd_f, w_sem, wg_b, wu_b, wd_b, obuf, o_sem):
    nused = nused_ref[0]
    ahead = NSLOT - 1

    def start(s):
        for cp in _unit_copies(src_ref, s * UPC, UPC, xc_hbm, stage, lax.rem(s, NSLOT), sem):
            cp.start()

    def weight_copies(e, slot):
        return [pltpu.make_async_copy(hbm.at[layer, e], buf.at[slot], w_sem.at[slot])
                for hbm, buf in ((wg_hbm, wg_f), (wu_hbm, wu_f), (wd_hbm, wd_f))]

    def out_copy(c, slot):
        row = pl.multiple_of(c * CH, CH)
        return pltpu.make_async_copy(obuf.at[slot], o_hbm.at[pl.ds(row, CH), :], o_sem.at[slot])

    for s in range(ahead):
        start(s)

    @pl.when(nused > 0)
    def _():
        for cp in weight_copies(eid_ref[0], 0):
            cp.start()

    def chunk(c, n_experts):
        e = eid_ref[c]
        first = (c == 0) | (e != eid_ref[jnp.maximum(c - 1, 0)])
        wslot = lax.rem(n_experts, 2)

        @pl.when(first)
        def _():
            for cp in weight_copies(e, wslot):
                cp.wait()
            wg_b[...] = wg_f[wslot].astype(bf16)
            wu_b[...] = wu_f[wslot].astype(bf16)
            wd_b[...] = wd_f[wslot].astype(bf16)

            @pl.when(nxt_ref[c] >= 0)
            def _():
                for cp in weight_copies(nxt_ref[c], 1 - wslot):
                    cp.start()

        slot = _wait_gather(c, UPC, src_ref, xc_hbm, stage, sem)
        oslot = lax.rem(c, 2)

        @pl.when(c >= 2)
        def _():
            out_copy(c - 2, oslot).wait()

        def block(r):
            xs = stage[slot, r:r + CHB, :]
            x = xs[:, :D]
            a = jnp.dot(x, wg_b[...], preferred_element_type=f32)
            u = jnp.dot(x, wu_b[...], preferred_element_type=f32)
            yield
            gb = xs[:, D:].astype(f32)
            is_first = gb[:, 6:7] == e.astype(f32)
            gate = jnp.where(is_first, gb[:, 2:3] + gb[:, 3:4], gb[:, 4:5] + gb[:, 5:6])
            act = (a * (1.0 / (1.0 + jnp.exp(-a)))) * u * gate
            obuf[oslot, r:r + CHB, :] = jnp.dot(act.astype(bf16), wd_b[...],
                                                preferred_element_type=f32).astype(bf16)

        def run(starts):
            blocks = [block(r) for r in starts]
            for _ in range(2):
                for blk in blocks:
                    next(blk, None)

        @pl.when(half_ref[c] == 0)
        def _():
            run(range(0, CH, CHB))

        @pl.when(half_ref[c] != 0)
        def _():
            run(range(0, CH // 2, CHB))
            obuf[oslot, CH // 2:, :] = jnp.zeros((CH // 2, D), bf16)

        out_copy(c, oslot).start()
        start(c + ahead)
        return n_experts + first.astype(jnp.int32)

    lax.fori_loop(0, nused, chunk, jnp.int32(0))

    for s in range(ahead):
        _wait_gather(nused + s, UPC, src_ref, xc_hbm, stage, sem)
    for back in (1, 2):
        @pl.when(nused >= back)
        def _():
            out_copy(nused - back, lax.rem(nused - back, 2)).wait()

    obuf[0] = jnp.zeros((CH, D), bf16)

    def zero_start(c, carry):
        out_copy(c, 0).start()
        return carry

    def zero_wait(c, carry):
        out_copy(c, 0).wait()
        return carry

    lax.fori_loop(nused, NCH, zero_start, 0)
    lax.fori_loop(nused, NCH, zero_wait, 0)


def _ffn(layer, eid, nused, ffn_src, nxt, half, xc, wg, wu, wd):
    any_spec = pl.BlockSpec(memory_space=pl.ANY)
    return pl.pallas_call(
        functools.partial(_ffn_kernel, layer),
        grid_spec=pltpu.PrefetchScalarGridSpec(
            num_scalar_prefetch=5,
            grid=(1,),
            in_specs=[any_spec, any_spec, any_spec, any_spec],
            out_specs=any_spec,
            scratch_shapes=[
                pltpu.VMEM((NSLOT, CH, XW), bf16),
                pltpu.SemaphoreType.DMA((NSLOT,)),
                pltpu.VMEM((2, D, F), f32),
                pltpu.VMEM((2, D, F), f32),
                pltpu.VMEM((2, F, D), f32),
                pltpu.SemaphoreType.DMA((2,)),
                pltpu.VMEM((D, F), bf16),
                pltpu.VMEM((D, F), bf16),
                pltpu.VMEM((F, D), bf16),
                pltpu.VMEM((2, CH, D), bf16),
                pltpu.SemaphoreType.DMA((2,)),
            ],
        ),
        out_shape=jax.ShapeDtypeStruct((NCH * CH, D), bf16),
        compiler_params=_params("arbitrary"),
        name="ffn",
    )(eid, nused, ffn_src, nxt, half, xc, wg, wu, wd)


def _combine_tile(x, meta, rows):
    li = lax.broadcasted_iota(jnp.int32, (TM, RC), 1).astype(f32)
    pt = jnp.where((li == meta[:, 0:1]) | (li == meta[:, 1:2]), 1.0, 0.0).astype(bf16)
    return x + jnp.dot(pt, rows, preferred_element_type=f32)


def _combine_kernel(src_ref, x_ref, meta_ref, o_hbm, yp_ref, ys_ref, stage, sem):
    i = pl.program_id(0)
    tiles = T // TM
    _start_gathers(i, NT, tiles * UPT, src_ref, o_hbm, stage, sem)
    slot = _wait_gather(i, tiles * UPT, src_ref, o_hbm, stage, sem)
    ys = [_combine_tile(x_ref[k * TM:(k + 1) * TM, :], meta_ref[k * TM:(k + 1) * TM, :],
                        stage[slot, k * RC:(k + 1) * RC, :]) for k in range(tiles)]
    y = jnp.concatenate(ys, axis=0)

    @pl.when(i < NT - 1)
    def _():
        yp_ref[...] = y

    @pl.when(i == NT - 1)
    def _():
        ys_ref[...] = y


def _combine(comb_src, x, meta, o_sorted):
    out_specs = [pl.BlockSpec((T, D), lambda i, src: (jnp.minimum(i, NT - 2), 0)),
                 pl.BlockSpec((T, D), lambda i, src: (0, 0))]
    out_shape = [jax.ShapeDtypeStruct((NP, D), f32), jax.ShapeDtypeStruct((NS, D), f32)]
    return pl.pallas_call(
        _combine_kernel,
        grid_spec=pltpu.PrefetchScalarGridSpec(
            num_scalar_prefetch=1,
            grid=(NT,),
            in_specs=[
                pl.BlockSpec((T, D), lambda i, src: (i, 0)),
                pl.BlockSpec((T, LANES), lambda i, src: (i, 0)),
                pl.BlockSpec(memory_space=pl.ANY),
            ],
            out_specs=out_specs,
            scratch_shapes=[
                pltpu.VMEM((NSLOT, (T // TM) * RC, D), bf16),
                pltpu.SemaphoreType.DMA((NSLOT,)),
            ],
        ),
        out_shape=out_shape,
        compiler_params=_params("arbitrary"),
        name="combine",
    )(comb_src, x, meta, o_sorted)


def _conv_kernel(src_ref, x1_ref, meta_ref, o_hbm, g_ref, win_ref, cw_ref, wout_ref,
                 p1_ref, p2_ref, y_ref, cu_ref, pad_ref, x_ref, stage, sem):
    i = pl.program_id(0)

    @pl.when(i == 0)
    def _():
        pad_ref[0:8, :] = jnp.zeros((8, D), f32)

    tiles = T // TM
    ahead = NSLOT - 1

    def start(s):
        for cp in _unit_copies(src_ref, s * tiles * UPT, tiles * UPT, o_hbm, stage,
                               lax.rem(s, NSLOT), sem):
            cp.start()

    @pl.when(i == 0)
    def _():
        for s in range(ahead):
            start(s)

    slot = _wait_gather(i, tiles * UPT, src_ref, o_hbm, stage, sem)

    for k in range(tiles):
        rows = slice(k * TM, (k + 1) * TM)
        x_ref[rows, :] = _combine_tile(x1_ref[rows, :], meta_ref[rows, :],
                                       stage[slot, k * RC:(k + 1) * RC, :])

    blocks = [(r, r + T // 2) for r in (0, T // 2)]
    gates = []
    for r0, r1 in blocks:
        h = _rms(x_ref[r0:r1, :], g_ref[...]).astype(bf16)
        bcu = jnp.dot(h, win_ref[...], preferred_element_type=f32)
        gates.append(bcu[:, :D])
        cu = bcu[:, D:2 * D] * bcu[:, 2 * D:]
        pad_ref[8 + r0:8 + r1, :] = cu
        cu_ref[r0:r1, :] = cu
    t = lax.broadcasted_iota(jnp.int32, (T // 2, 1), 0) & (TS - 1)
    is_sample = i == NT - 1
    cw = cw_ref[...]
    for (r0, r1), b in zip(blocks, gates):
        m1 = jnp.where(is_sample & (t == 0), p1_ref[r0:r1, :], pad_ref[7 + r0:7 + r1, :])
        m2 = jnp.where(is_sample & (t < 2), p2_ref[r0:r1, :], pad_ref[6 + r0:6 + r1, :])
        conv = cw[0:1] * m2 + cw[1:2] * m1 + cw[2:3] * pad_ref[8 + r0:8 + r1, :]
        y = jnp.dot((b * conv).astype(bf16), wout_ref[...], preferred_element_type=f32)
        y_ref[r0:r1, :] = x_ref[r0:r1, :] + y
    pad_ref[0:8, :] = pad_ref[T:T + 8, :]
    start(i + ahead)

    @pl.when(i == NT - 1)
    def _():
        for s in range(ahead):
            _wait_gather(i + 1 + s, tiles * UPT, src_ref, o_hbm, stage, sem)


def _conv(comb_src, x1, meta, o_sorted, g, win, cw, wout, p1, p2):
    const = dict(pipeline_mode=pl.Buffered(1))
    comb_src = jnp.concatenate([
        comb_src, jnp.full(((NSLOT - 1) * (T // TM) * UPT,), ZERO_UNIT_OUT, jnp.int32)])
    return pl.pallas_call(
        _conv_kernel,
        grid_spec=pltpu.PrefetchScalarGridSpec(
            num_scalar_prefetch=1,
            grid=(NT,),
            in_specs=[
                pl.BlockSpec((T, D), lambda i, src: (i, 0)),
                pl.BlockSpec((T, LANES), lambda i, src: (i, 0)),
                pl.BlockSpec(memory_space=pl.ANY),
                pl.BlockSpec((1, D), lambda i, src: (0, 0)),
                pl.BlockSpec((D, 3 * D), lambda i, src: (0, 0), **const),
                pl.BlockSpec((3, D), lambda i, src: (0, 0)),
                pl.BlockSpec((D, D), lambda i, src: (0, 0), **const),
                pl.BlockSpec((T, D), lambda i, src: (0, 0), **const),
                pl.BlockSpec((T, D), lambda i, src: (0, 0), **const),
            ],
            out_specs=[
                pl.BlockSpec((T, D), lambda i, src: (i, 0)),
                pl.BlockSpec((T, D), lambda i, src: (jnp.where(i == NT - 1, 1, 0), 0)),
            ],
            scratch_shapes=[
                pltpu.VMEM((T + 8, D), f32),
                pltpu.VMEM((T, D), f32),
                pltpu.VMEM((NSLOT, (T // TM) * RC, D), bf16),
                pltpu.SemaphoreType.DMA((NSLOT,)),
            ],
        ),
        out_shape=[
            jax.ShapeDtypeStruct((N, D), f32),
            jax.ShapeDtypeStruct((2 * T, D), f32),
        ],
        compiler_params=_params("arbitrary"),
        name="conv",
    )(comb_src, x1, meta, o_sorted, g, win, cw, wout, p1, p2)


def _rope_tables():
    inv_freq = THETA ** (-jnp.arange(HALF, dtype=f32) / HALF)
    pos = jnp.concatenate([
        jnp.arange(NP, dtype=jnp.int32),
        PAST + jnp.tile(jnp.arange(TS, dtype=jnp.int32), NB),
    ]).astype(f32)
    ang = inv_freq[:, None] * pos[None, :]
    return jnp.cos(ang), jnp.sin(ang)


def _router_weights(w_group, b_group, w_router, b_router):
    gap = ROW_E - NGRP
    pad = LANES - ROW_E - NE
    w = jnp.concatenate([w_group, jnp.zeros((D, gap), f32), w_router,
                         jnp.zeros((D, pad), f32)], axis=1)
    b = jnp.concatenate([b_group, jnp.zeros((gap,), f32), b_router,
                         jnp.zeros((pad,), f32)])[None, :]
    return w.astype(bf16), b


def _moe_experts(x, i, norm_ffn, w_group, b_group, w_router, b_router, w_gate, w_up, w_down):
    w, b = _router_weights(w_group[i], b_group[i], w_router[i], b_router[i])
    xc, meta, cnt = _dispatch(x, norm_ffn[i][None, :], w, b)
    eid, nused, ffn_src, nxt, half, comb_src = _dispatch_tables(cnt)
    o_sorted = _ffn(i, eid, nused, ffn_src, nxt, half, xc, w_gate, w_up, w_down)
    return comb_src, meta, o_sorted


def kernel(x_prompt, x_sample, cache_k, cache_v, state_conv, norm_mix, w_qkv, q_norm, k_norm,
           sinks, w_o, w_in, conv_w, w_out, norm_ffn, w_group, b_group, w_router, b_router,
           w_gate, w_up, w_down):
    xp = x_prompt.reshape(NP, D)
    xs = x_sample.reshape(NS, D)
    moe_w = (norm_ffn, w_group, b_group, w_router, b_router, w_gate, w_up, w_down)

    cos, sin = _rope_tables()
    (qT, ktok, vtok, vT), (q_s, ktok_s, vtok_s) = _qkv(
        xp, xs, norm_mix[0][None, :], w_qkv[0].T.astype(bf16), q_norm[0][:, None],
        k_norm[0][:, None], cos, sin)
    sink_rows = jnp.repeat(sinks[0].reshape(KVH, G), TQ, axis=1)[:, None, :]

    qs = q_s.reshape(KVH, G, HD, NB, TS).transpose(3, 0, 1, 4, 2)
    zq = jnp.zeros_like(qs[:, 0])
    qbd = jnp.stack([jnp.concatenate([qs[:, 0], zq], axis=-1),
                     jnp.concatenate([zq, qs[:, 1]], axis=-1)], axis=1)
    qbd = qbd.reshape(NB, H * TS, KVH * HD)
    k_new = ktok_s.reshape(NB, TS, KVH * HD)
    v_new = vtok_s.reshape(NB, TS, KVH * HD)
    pad4 = jnp.zeros((NB, 8 - TS, KVH * HD), f32)
    kc = cache_k[0].reshape(NB, WIN, KVH * HD)
    vc = cache_v[0].reshape(NB, WIN, KVH * HD)
    sink_col = jnp.repeat(sinks[0], TS)[:, None]
    o_s, kc_new, vc_new = _attn_sample(qbd, kc, vc, jnp.concatenate([k_new, pad4], axis=1),
                                       jnp.concatenate([v_new, pad4], axis=1), sink_col)
    o_s = o_s.reshape(NB, KVH, G, TS, KVH, HD)
    o_s = jnp.stack([o_s[:, 0, :, :, 0], o_s[:, 1, :, :, 1]], axis=1)
    o_s = o_s.transpose(0, 3, 1, 2, 4).reshape(NS, H * HD).astype(bf16)
    x = _attn_prompt(qT, ktok, vT, sink_rows, xp, xs, o_s, w_o[0].astype(bf16))
    comb_src, meta, o_sorted = _moe_experts(x, 0, *moe_w)

    new_k_prompt = ktok[NP - WIN:NP].reshape(1, 1, WIN, KVH, HD)
    new_v_prompt = vtok[NP - WIN:NP].reshape(1, 1, WIN, KVH, HD)
    new_k_sample = kc_new.reshape(1, NB, WIN, KVH, HD)
    new_v_sample = vc_new.reshape(1, NB, WIN, KVH, HD)

    st = state_conv[0]
    z = jnp.zeros((NB, 1, D), f32)
    p1 = jnp.concatenate([st[:, 1:2], z, z, z], axis=1).reshape(NS, D)
    p2 = jnp.concatenate([st[:, 0:1], st[:, 1:2], z, z], axis=1).reshape(NS, D)
    x, cu = _conv(comb_src, x, meta, o_sorted, norm_mix[1][None, :], w_in[0].astype(bf16),
                  conv_w[0], w_out[0].astype(bf16), p1, p2)
    comb_src, meta, o_sorted = _moe_experts(x, 1, *moe_w)
    y_prompt, y_sample = _combine(comb_src, x, meta, o_sorted)

    new_conv_prompt = cu[T - 2:T].reshape(1, 1, 2, D)
    new_conv_sample = cu[T:].reshape(NB, TS, D)[:, TS - 2:][None]

    y_prompt = y_prompt.reshape(1, NP, D)
    y_sample = y_sample.reshape(NB, TS, D)
    return (y_prompt, y_sample, new_k_prompt, new_v_prompt, new_conv_prompt,
            new_k_sample, new_v_sample, new_conv_sample)
```

```python
import functools

import jax
import jax.numpy as jnp
import numpy as np
from jax import lax
from jax.experimental import pallas as pl
from jax.experimental.pallas import tpu as pltpu

D = 1024
NP = 16384
NB = 128
TS = 4
NS = NB * TS
N = NP + NS
PAST = 16384
H = 16
KVH = 2
G = H // KVH
HD = 64
HALF = HD // 2
QKV = (H + 2 * KVH) * HD
WIN = 128
THETA = 10000.0
NGRP = 4
EPG = 4
NE = NGRP * EPG
TOPK = 2
ROW_E = 8
F = 512
EPS = 1e-6
SCALE = HD ** -0.5

T = 512
NT = N // T
TQ = 128
NQ = NP // TQ
QB = 4
BB = 16
LANES = 128
V7X_VMEM_BYTES = 64 * 1024 * 1024
VMEM_LIMIT = V7X_VMEM_BYTES - 14 * 1024 * 1024

TM = 256
NTM = N // TM
DT = 6
UNIT = 16
RC = TOPK * TM + NE * UNIT
UPT = RC // UNIT
XW = D + LANES
CH = 512
CHB = 256
UPC = CH // UNIT
NSLOT = 4
NCH = -(-(NTM * (UPT - 1) + NE * (UPC - 1)) // UPC) + NSLOT - 1
ZERO_UNIT_IN = UPT - 1
ZERO_UNIT_OUT = (NCH - 1) * UPC

f32 = jnp.float32
bf16 = jnp.bfloat16


def _params(*sem):
    return pltpu.CompilerParams(dimension_semantics=sem, vmem_limit_bytes=VMEM_LIMIT)


def _rms(x, g):
    ms = jnp.mean(x * x, axis=-1, keepdims=True)
    return x * lax.rsqrt(ms + EPS) * g


def _qkv_prompt_kernel(x_ref, g_ref, wT_ref, qn_ref, kn_ref, cos_ref, sin_ref,
                       qT_ref, ktok_ref, vtok_ref, vT_ref):
    _qkv_tile(x_ref, g_ref, wT_ref, qn_ref, kn_ref, cos_ref, sin_ref,
              qT_ref, ktok_ref, vtok_ref, vT_ref)


def _qkv_sample_kernel(x_ref, g_ref, wT_ref, qn_ref, kn_ref, cos_ref, sin_ref,
                       q_ref, ktok_ref, vtok_ref):
    _qkv_tile(x_ref, g_ref, wT_ref, qn_ref, kn_ref, cos_ref, sin_ref,
              q_ref, ktok_ref, vtok_ref, None)


def _qkv_tile(x_ref, g_ref, wT_ref, qn_ref, kn_ref, cos_ref, sin_ref,
              q_ref, ktok_ref, vtok_ref, vT_ref):
    h = _rms(x_ref[...], g_ref[...]).astype(bf16)
    qkvT = lax.dot_general(wT_ref[...], h, (((1,), (1,)), ((), ())),
                           preferred_element_type=f32)
    cos = cos_ref[...]
    sin = sin_ref[...]

    def norm_rope(blk, gcol):
        ms = jnp.mean(blk * blk, axis=0, keepdims=True)
        y = blk * lax.rsqrt(ms + EPS) * gcol
        y1 = y[:HALF]
        y2 = y[HALF:]
        return y1 * cos - y2 * sin, y2 * cos + y1 * sin

    qn = qn_ref[...]
    for hd in range(H):
        o1, o2 = norm_rope(qkvT[hd * HD:(hd + 1) * HD], qn)
        q_ref[hd * HD:hd * HD + HALF, :] = (o1 * SCALE).astype(bf16)
        q_ref[hd * HD + HALF:(hd + 1) * HD, :] = (o2 * SCALE).astype(bf16)
    kn = kn_ref[...]
    ks = []
    for j in range(KVH):
        o1, o2 = norm_rope(qkvT[H * HD + j * HD:H * HD + (j + 1) * HD], kn)
        ks += [o1, o2]
    kT = jnp.concatenate(ks, axis=0)
    ktok_ref[...] = kT.T
    vT = qkvT[(H + KVH) * HD:]
    vtok_ref[...] = vT.T
    if vT_ref is not None:
        vT_ref[...] = vT.astype(bf16)


def _qkv(xp, xs, g, wT, qn, kn, cos, sin):
    const = [pl.BlockSpec((HD, 1), lambda i: (0, 0)), pl.BlockSpec((HD, 1), lambda i: (0, 0)),
             pl.BlockSpec((HALF, T), lambda i: (0, i)), pl.BlockSpec((HALF, T), lambda i: (0, i))]
    x_spec = pl.BlockSpec((T, D), lambda i: (i, 0))
    g_spec = pl.BlockSpec((1, D), lambda i: (0, 0))
    w_spec = pl.BlockSpec((QKV, D), lambda i: (0, 0))
    tok_spec = pl.BlockSpec((T, KVH * HD), lambda i: (i, 0))
    nt = NP // T
    prompt = pl.pallas_call(
        _qkv_prompt_kernel,
        grid=(nt,),
        in_specs=[x_spec, g_spec, w_spec] + const,
        out_specs=[pl.BlockSpec((None, H * HD, T), lambda i: (i, 0, 0)), tok_spec, tok_spec,
                   pl.BlockSpec((None, KVH * HD, T), lambda i: (i, 0, 0))],
        out_shape=[jax.ShapeDtypeStruct((nt, H * HD, T), bf16),
                   jax.ShapeDtypeStruct((NP, KVH * HD), f32),
                   jax.ShapeDtypeStruct((NP, KVH * HD), f32),
                   jax.ShapeDtypeStruct((nt, KVH * HD, T), bf16)],
        compiler_params=_params("parallel"),
        name="qkv",
    )(xp, g, wT, qn, kn, cos[:, :NP], sin[:, :NP])
    sample = pl.pallas_call(
        _qkv_sample_kernel,
        grid=(NS // T,),
        in_specs=[x_spec, g_spec, w_spec] + const,
        out_specs=[pl.BlockSpec((H * HD, T), lambda i: (0, i)), tok_spec, tok_spec],
        out_shape=[jax.ShapeDtypeStruct((H * HD, NS), bf16),
                   jax.ShapeDtypeStruct((NS, KVH * HD), f32),
                   jax.ShapeDtypeStruct((NS, KVH * HD), f32)],
        compiler_params=_params("parallel"),
        name="qkv_sample",
    )(xs, g, wT, qn, kn, cos[:, NP:], sin[:, NP:])
    return prompt, sample


def _attn_prompt_kernel(qT_ref, kp_ref, kc_ref, vp_ref, vc_ref, sink_ref, bias0_ref, bias_ref,
                        xp_ref, xs_ref, os_ref, wo_ref, y_ref):
    j = pl.program_id(0)

    @pl.when(j >= NQ // QB)
    def _():
        y_ref[...] = xs_ref[...] + jnp.dot(os_ref[...], wo_ref[...],
                                           preferred_element_type=f32)

    @pl.when(j < NQ // QB)
    def _():
        k_all = jnp.concatenate([kp_ref[...], kc_ref[...]], axis=0)
        v_all = jnp.concatenate([vp_ref[...], vc_ref[...]], axis=1)
        blocks = []
        for k in range(QB):
            rows = slice(k * TQ, (k + 1) * TQ)
            blocks.append(_attend_block(
                qT_ref.at[:, rows], k_all[k * TQ:(k + 2) * TQ], v_all[:, k * TQ:(k + 2) * TQ],
                sink_ref, bias0_ref if k == 0 else bias_ref, xp_ref.at[rows], wo_ref,
                y_ref.at[rows]))
        for _ in range(3):
            for blk in blocks:
                next(blk, None)


def _attend_block(qT_ref, kk, vv, sink_ref, bias_ref, x_ref, wo_ref, y_ref):
    kk = kk.astype(bf16)
    bias = jnp.concatenate([bias_ref[...]] * H, axis=1)
    qg = [jnp.concatenate([qT_ref[(g * G + hh) * HD:(g * G + hh + 1) * HD, :]
                           for hh in range(G)], axis=1) for g in range(KVH)]
    zeros = jnp.zeros_like(qg[0])
    rhs = jnp.concatenate([jnp.concatenate([qg[0], zeros], axis=1),
                           jnp.concatenate([zeros, qg[1]], axis=1)], axis=0)
    sT = jnp.dot(kk, rhs, preferred_element_type=f32) + bias
    yield
    sink = jnp.concatenate([sink_ref[0], sink_ref[1]], axis=1)
    m = jnp.maximum(jnp.max(sT, axis=0, keepdims=True), sink)
    p = jnp.exp(sT - m)
    l = jnp.sum(p, axis=0, keepdims=True) + jnp.exp(sink - m)
    p = (p * (1.0 / l)).astype(bf16)
    yield
    pieces = []
    for g in range(KVH):
        oT = jnp.dot(vv[g * HD:(g + 1) * HD, :], p[:, g * G * TQ:(g + 1) * G * TQ],
                     preferred_element_type=f32)
        pieces += [oT[:, hh * TQ:(hh + 1) * TQ] for hh in range(G)]
    oT_all = jnp.concatenate(pieces, axis=0)
    y_ref[...] = x_ref[...] + jnp.dot(oT_all.T.astype(bf16), wo_ref[...],
                                      preferred_element_type=f32)


def _band_bias():
    s = np.arange(2 * TQ)[:, None]
    t = np.arange(TQ)[None, :]
    dist = t + TQ - s
    band = (dist >= 0) & (dist <= WIN)
    first = band & (s >= TQ)
    return jnp.asarray(np.where(np.stack([first, band]), 0.0, -np.inf), f32)


def _attn_prompt(qT, ktok, vT, sink_rows, xp, xs, o_s, wo):
    steps = NQ // QB
    rows = QB * TQ
    assert rows == T

    def cur(j):
        return jnp.minimum(j, steps - 1)

    def prev(j):
        return jnp.maximum(cur(j) * QB - 1, 0)

    def prev_tile(j):
        return jnp.maximum(cur(j) - 1, 0)

    def sample(j):
        return jnp.maximum(j - steps, 0)

    bias = _band_bias()
    return pl.pallas_call(
        _attn_prompt_kernel,
        grid=(N // rows,),
        in_specs=[
            pl.BlockSpec((None, H * HD, rows), lambda j: (cur(j), 0, 0)),
            pl.BlockSpec((TQ, KVH * HD), lambda j: (prev(j), 0)),
            pl.BlockSpec((rows, KVH * HD), lambda j: (cur(j), 0)),
            pl.BlockSpec((None, KVH * HD, TQ), lambda j: (prev_tile(j), 0, QB - 1)),
            pl.BlockSpec((None, KVH * HD, rows), lambda j: (cur(j), 0, 0)),
            pl.BlockSpec((KVH, 1, G * TQ), lambda j: (0, 0, 0)),
            pl.BlockSpec((None, 2 * TQ, TQ), lambda j: (jnp.minimum(j, 1), 0, 0)),
            pl.BlockSpec((None, 2 * TQ, TQ), lambda j: (1, 0, 0)),
            pl.BlockSpec((rows, D), lambda j: (cur(j), 0)),
            pl.BlockSpec((rows, D), lambda j: (sample(j), 0)),
            pl.BlockSpec((rows, H * HD), lambda j: (sample(j), 0)),
            pl.BlockSpec((D, D), lambda j: (0, 0)),
        ],
        out_specs=pl.BlockSpec((rows, D), lambda j: (j, 0)),
        out_shape=jax.ShapeDtypeStruct((N, D), f32),
        compiler_params=_params("parallel"),
        name="attn_prompt",
    )(qT, ktok, ktok, vT, vT, sink_rows, bias, bias, xp, xs, o_s, wo)


def _attn_sample_kernel(q_ref, kc_ref, vc_ref, kn_ref, vn_ref, sink_ref,
                        o_ref, knew_ref, vnew_ref):
    rows = H * TS
    t1 = lax.broadcasted_iota(jnp.int32, (1, rows, WIN), 1) & (TS - 1)
    s1 = lax.broadcasted_iota(jnp.int32, (1, rows, WIN), 2)
    valid1 = s1 >= t1
    t2 = lax.broadcasted_iota(jnp.int32, (1, rows, 8), 1) & (TS - 1)
    s2 = lax.broadcasted_iota(jnp.int32, (1, rows, 8), 2)
    valid2 = s2 <= t2
    sink = sink_ref[...][None]
    q = q_ref[...]
    kc = kc_ref[...]
    vc = vc_ref[...]
    kn = kn_ref[...]
    vn = vn_ref[...]
    sc = jnp.einsum('bqd,bkd->bqk', q, kc.astype(bf16), preferred_element_type=f32)
    sn = jnp.einsum('bqd,bkd->bqk', q, kn.astype(bf16), preferred_element_type=f32)
    sc = jnp.where(valid1, sc, -jnp.inf)
    sn = jnp.where(valid2, sn, -jnp.inf)
    m = jnp.maximum(jnp.maximum(jnp.max(sc, axis=-1, keepdims=True),
                                jnp.max(sn, axis=-1, keepdims=True)), sink)
    pc = jnp.exp(sc - m)
    pn = jnp.exp(sn - m)
    l = (jnp.sum(pc, axis=-1, keepdims=True) + jnp.sum(pn, axis=-1, keepdims=True)
         + jnp.exp(sink - m))
    r = 1.0 / l
    o_ref[...] = (jnp.einsum('bqk,bkd->bqd', (pc * r).astype(bf16), vc.astype(bf16),
                             preferred_element_type=f32)
                  + jnp.einsum('bqk,bkd->bqd', (pn * r).astype(bf16), vn.astype(bf16),
                               preferred_element_type=f32))
    knew_ref[:, :WIN - TS, :] = kc[:, TS:, :]
    knew_ref[:, WIN - TS:, :] = kn[:, :TS, :]
    vnew_ref[:, :WIN - TS, :] = vc[:, TS:, :]
    vnew_ref[:, WIN - TS:, :] = vn[:, :TS, :]


def _attn_sample(qbd, kc, vc, kn, vn, sink_col):
    rows = H * TS
    cache_spec = pl.BlockSpec((BB, WIN, KVH * HD), lambda i: (i, 0, 0))
    new_spec = pl.BlockSpec((BB, 8, KVH * HD), lambda i: (i, 0, 0))
    cache_shape = jax.ShapeDtypeStruct((NB, WIN, KVH * HD), f32)
    return pl.pallas_call(
        _attn_sample_kernel,
        grid=(NB // BB,),
        in_specs=[
            pl.BlockSpec((BB, rows, KVH * HD), lambda i: (i, 0, 0)),
            cache_spec, cache_spec, new_spec, new_spec,
            pl.BlockSpec((rows, 1), lambda i: (0, 0)),
        ],
        out_specs=[pl.BlockSpec((BB, rows, KVH * HD), lambda i: (i, 0, 0)),
                   cache_spec, cache_spec],
        out_shape=[jax.ShapeDtypeStruct((NB, rows, KVH * HD), f32), cache_shape, cache_shape],
        compiler_params=_params("parallel"),
        name="attn_sample",
    )(qbd, kc, vc, kn, vn, sink_col)


def _dispatch_kernel(x_ref, g_ref, w_ref, b_ref, upper_ref, xc_ref, meta_ref, cnt_ref):
    tiles = [_dispatch_tile(x_ref.at[k * TM:(k + 1) * TM], g_ref, w_ref, b_ref, upper_ref,
                            xc_ref.at[k * RC:(k + 1) * RC], meta_ref.at[k * TM:(k + 1) * TM],
                            cnt_ref.at[k * NE:(k + 1) * NE]) for k in range(DT)]
    for _ in range(3):
        for t in tiles:
            next(t, None)


def _dispatch_tile(x_ref, g_ref, w_ref, b_ref, upper_ref, xc_ref, meta_ref, cnt_ref):
    h_hi = _rms(x_ref[...], g_ref[...]).astype(bf16)
    logits = jnp.dot(h_hi, w_ref[...], preferred_element_type=f32) + b_ref[...]
    yield
    lt = logits.T
    inf = jnp.inf
    row8 = lax.broadcasted_iota(jnp.int32, (8, TM), 0).astype(f32)
    gl = jnp.where(row8 < NGRP, lt[0:8], -inf)
    gmax = jnp.max(gl, axis=0, keepdims=True)
    gsel = jnp.min(jnp.where(gl == gmax, row8, 8.0), axis=0, keepdims=True)
    g_w = 1.0 / jnp.sum(jnp.exp(gl - gmax), axis=0, keepdims=True)
    row = lax.broadcasted_iota(jnp.int32, (NE, TM), 0)
    row_f = row.astype(f32)
    el = jnp.where((row >> 2).astype(f32) == gsel, lt[ROW_E:ROW_E + NE], -inf)
    v1 = jnp.max(el, axis=0, keepdims=True)
    i1 = jnp.min(jnp.where(el == v1, row_f, float(NE)), axis=0, keepdims=True)
    el2 = jnp.where(row_f == i1, -inf, el)
    v2 = jnp.max(el2, axis=0, keepdims=True)
    i2 = jnp.min(jnp.where(el2 == v2, row_f, float(NE)), axis=0, keepdims=True)
    e1 = jnp.exp(v2 - v1)
    den = 1.0 + e1
    w1 = (1.0 / den) * g_w
    w2 = (e1 / den) * g_w

    m1 = row_f == i1
    m2 = row_f == i2
    sel = jnp.where(m1 | m2, 1.0, 0.0)
    ranks = jnp.dot(sel.astype(bf16), upper_ref[...],
                    preferred_element_type=f32)
    counts = jnp.sum(sel, axis=1, keepdims=True)
    padded = jnp.floor((counts + (UNIT - 1.0)) * (1.0 / UNIT)) * UNIT
    e_i = lax.broadcasted_iota(jnp.int32, (NE, NE), 0)
    f_i = lax.broadcasted_iota(jnp.int32, (NE, NE), 1)
    below = jnp.where(f_i < e_i, 1.0, 0.0).astype(bf16)
    seg = jnp.dot(below, jnp.broadcast_to(padded, (NE, LANES)).astype(bf16),
                  preferred_element_type=f32)[:, 0:1]
    posall = seg + ranks
    pos1 = jnp.sum(jnp.where(m1, posall, 0.0), axis=0, keepdims=True)
    pos2 = jnp.sum(jnp.where(m2, posall, 0.0), axis=0, keepdims=True)
    cnt_ref[...] = jnp.broadcast_to(counts, (NE, LANES))

    w1_hi = w1.astype(bf16).astype(f32)
    w2_hi = w2.astype(bf16).astype(f32)
    slab = jnp.zeros((8, TM), f32)
    for k, r in enumerate((pos1, pos2, w1_hi, w1 - w1_hi, w2_hi, w2 - w2_hi, i1)):
        slab = jnp.where(row8 == k, r, slab)
    meta = jnp.concatenate([slab, jnp.zeros((LANES - 8, TM), f32)], axis=0).T
    meta_ref[...] = meta
    yield

    rr = lax.broadcasted_iota(jnp.int32, (RC, TM), 0).astype(f32)
    onehot = jnp.where((rr == pos1) | (rr == pos2), 1.0, 0.0).astype(bf16)
    h_aug = jnp.concatenate([h_hi, meta.astype(bf16)], axis=1)
    xc_ref[...] = jnp.dot(onehot, h_aug, preferred_element_type=f32).astype(bf16)


def _dispatch(x, g, w, b):
    upper = jnp.asarray(np.triu(np.ones((TM, TM), np.float32), 1), bf16)
    return pl.pallas_call(
        _dispatch_kernel,
        grid=(NTM // DT,),
        in_specs=[
            pl.BlockSpec((DT * TM, D), lambda i: (i, 0)),
            pl.BlockSpec((1, D), lambda i: (0, 0)),
            pl.BlockSpec((D, LANES), lambda i: (0, 0)),
            pl.BlockSpec((1, LANES), lambda i: (0, 0)),
            pl.BlockSpec((TM, TM), lambda i: (0, 0)),
        ],
        out_specs=[
            pl.BlockSpec((DT * RC, XW), lambda i: (i, 0)),
            pl.BlockSpec((DT * TM, LANES), lambda i: (i, 0)),
            pl.BlockSpec((DT * NE, LANES), lambda i: (i, 0)),
        ],
        out_shape=[
            jax.ShapeDtypeStruct((NTM * RC, XW), bf16),
            jax.ShapeDtypeStruct((N, LANES), f32),
            jax.ShapeDtypeStruct((NTM * NE, LANES), f32),
        ],
        compiler_params=_params("parallel"),
        name="dispatch",
    )(x, g, w, b, upper)


def _dispatch_tables(cnt):
    i32 = jnp.int32
    n = cnt.reshape(NTM, NE, LANES)[:, :, 0].astype(i32)
    units = (n + UNIT - 1) // UNIT
    seg_end = jnp.cumsum(units, axis=1)
    seg_start = seg_end - units
    col_end = jnp.cumsum(units, axis=0)
    col_start = col_end - units
    chunks = (col_end[-1] + UPC - 1) // UPC
    ch_end = jnp.cumsum(chunks)
    ch_start = ch_end - chunks
    nused = ch_end[-1]
    c = jnp.arange(NCH, dtype=i32)
    eid = jnp.minimum(jnp.sum((ch_end[None, :] <= c[:, None]).astype(i32), axis=1), NE - 1)

    src0 = jnp.arange(NTM, dtype=i32)[:, None] * UPT + seg_start
    dst0 = ch_start[None, :] * UPC + col_start
    k = jnp.arange((NCH + NSLOT - 1) * UPC, dtype=i32)[:, None, None]
    inside = (k >= dst0[None]) & (k < (dst0 + units)[None])
    found = jnp.sum(inside.astype(i32), axis=(1, 2))
    shift = jnp.sum(jnp.where(inside, (src0 - dst0)[None], 0), axis=(1, 2))
    ffn_src = jnp.where(found > 0, k[:, 0, 0] + shift, ZERO_UNIT_IN)

    v = jnp.arange(UPT, dtype=i32)[None, :, None]
    inside_v = (v >= seg_start[:, None, :]) & (v < seg_end[:, None, :])
    found_v = jnp.sum(inside_v.astype(i32), axis=2)
    shift_v = jnp.sum(jnp.where(inside_v, (dst0 - seg_start)[:, None, :], 0), axis=2)
    comb_src = jnp.where(found_v > 0, v[:, :, 0] + shift_v, ZERO_UNIT_OUT).reshape(-1)
    own_end = jnp.sum(jnp.where(eid[:, None] == jnp.arange(NE, dtype=i32)[None, :],
                                ch_end[None, :], 0), axis=1)
    eid_at = jnp.sum(jnp.where(c[None, :] == own_end[:, None], eid[None, :], 0), axis=1)
    nxt = jnp.where(own_end < nused, eid_at, -1)
    return eid, nused.reshape(1), ffn_src, nxt, comb_src


def _unit_copies(src_ref, base, n_units, src_hbm, stage, slot, sem):
    out = []
    for j in range(n_units):
        row = pl.multiple_of(src_ref[base + j] * UNIT, UNIT)
        out.append(pltpu.make_async_copy(
            src_hbm.at[pl.ds(row, UNIT), :],
            stage.at[slot, pl.ds(j * UNIT, UNIT), :],
            sem.at[slot]))
    return out


def _start_gathers(step, n_steps, n_units, src_ref, src_hbm, stage, sem):
    ahead = NSLOT - 1

    def start(s, slot):
        for cp in _unit_copies(src_ref, s * n_units, n_units, src_hbm, stage, slot, sem):
            cp.start()

    for s in range(ahead):
        @pl.when((step == 0) & (s < n_steps))
        def _():
            start(s, s)

    @pl.when(step + ahead < n_steps)
    def _():
        start(step + ahead, lax.rem(step + ahead, NSLOT))


def _wait_gather(step, n_units, src_ref, src_hbm, stage, sem):
    slot = lax.rem(step, NSLOT)
    for cp in _unit_copies(src_ref, step * n_units, n_units, src_hbm, stage, slot, sem):
        cp.wait()
    return slot


def _ffn_kernel(layer, eid_ref, nused_ref, src_ref, nxt_ref, xc_hbm, wg_hbm, wu_hbm, wd_hbm,
                o_hbm, stage, sem, wg_f, wu_f, wd_f, w_sem, wg_b, wu_b, wd_b, obuf, o_sem):
    nused = nused_ref[0]
    ahead = NSLOT - 1

    def start(s):
        for cp in _unit_copies(src_ref, s * UPC, UPC, xc_hbm, stage, lax.rem(s, NSLOT), sem):
            cp.start()

    def weight_copies(e, slot):
        return [pltpu.make_async_copy(hbm.at[layer, e], buf.at[slot], w_sem.at[slot])
                for hbm, buf in ((wg_hbm, wg_f), (wu_hbm, wu_f), (wd_hbm, wd_f))]

    def out_copy(c, slot):
        row = pl.multiple_of(c * CH, CH)
        return pltpu.make_async_copy(obuf.at[slot], o_hbm.at[pl.ds(row, CH), :], o_sem.at[slot])

    for s in range(ahead):
        start(s)

    @pl.when(nused > 0)
    def _():
        for cp in weight_copies(eid_ref[0], 0):
            cp.start()

    def chunk(c, n_experts):
        e = eid_ref[c]
        first = (c == 0) | (e != eid_ref[jnp.maximum(c - 1, 0)])
        wslot = lax.rem(n_experts, 2)

        @pl.when(first)
        def _():
            for cp in weight_copies(e, wslot):
                cp.wait()
            wg_b[...] = wg_f[wslot].astype(bf16)
            wu_b[...] = wu_f[wslot].astype(bf16)
            wd_b[...] = wd_f[wslot].astype(bf16)

            @pl.when(nxt_ref[c] >= 0)
            def _():
                for cp in weight_copies(nxt_ref[c], 1 - wslot):
                    cp.start()

        slot = _wait_gather(c, UPC, src_ref, xc_hbm, stage, sem)
        oslot = lax.rem(c, 2)

        @pl.when(c >= 2)
        def _():
            out_copy(c - 2, oslot).wait()

        def block(r):
            xs = stage[slot, r:r + CHB, :]
            x = xs[:, :D]
            a = jnp.dot(x, wg_b[...], preferred_element_type=f32)
            u = jnp.dot(x, wu_b[...], preferred_element_type=f32)
            yield
            gb = xs[:, D:].astype(f32)
            is_first = gb[:, 6:7] == e.astype(f32)
            gate = jnp.where(is_first, gb[:, 2:3] + gb[:, 3:4], gb[:, 4:5] + gb[:, 5:6])
            act = (a * (1.0 / (1.0 + jnp.exp(-a)))) * u * gate
            obuf[oslot, r:r + CHB, :] = jnp.dot(act.astype(bf16), wd_b[...],
                                                preferred_element_type=f32).astype(bf16)

        blocks = [block(r) for r in range(0, CH, CHB)]
        for _ in range(2):
            for blk in blocks:
                next(blk, None)
        out_copy(c, oslot).start()
        start(c + ahead)
        return n_experts + first.astype(jnp.int32)

    lax.fori_loop(0, nused, chunk, jnp.int32(0))

    for s in range(ahead):
        _wait_gather(nused + s, UPC, src_ref, xc_hbm, stage, sem)
    for back in (1, 2):
        @pl.when(nused >= back)
        def _():
            out_copy(nused - back, lax.rem(nused - back, 2)).wait()

    obuf[0] = jnp.zeros((CH, D), bf16)

    def zero_start(c, carry):
        out_copy(c, 0).start()
        return carry

    def zero_wait(c, carry):
        out_copy(c, 0).wait()
        return carry

    lax.fori_loop(nused, NCH, zero_start, 0)
    lax.fori_loop(nused, NCH, zero_wait, 0)


def _ffn(layer, eid, nused, ffn_src, nxt, xc, wg, wu, wd):
    any_spec = pl.BlockSpec(memory_space=pl.ANY)
    return pl.pallas_call(
        functools.partial(_ffn_kernel, layer),
        grid_spec=pltpu.PrefetchScalarGridSpec(
            num_scalar_prefetch=4,
            grid=(1,),
            in_specs=[any_spec, any_spec, any_spec, any_spec],
            out_specs=any_spec,
            scratch_shapes=[
                pltpu.VMEM((NSLOT, CH, XW), bf16),
                pltpu.SemaphoreType.DMA((NSLOT,)),
                pltpu.VMEM((2, D, F), f32),
                pltpu.VMEM((2, D, F), f32),
                pltpu.VMEM((2, F, D), f32),
                pltpu.SemaphoreType.DMA((2,)),
                pltpu.VMEM((D, F), bf16),
                pltpu.VMEM((D, F), bf16),
                pltpu.VMEM((F, D), bf16),
                pltpu.VMEM((2, CH, D), bf16),
                pltpu.SemaphoreType.DMA((2,)),
            ],
        ),
        out_shape=jax.ShapeDtypeStruct((NCH * CH, D), bf16),
        compiler_params=_params("arbitrary"),
        name="ffn",
    )(eid, nused, ffn_src, nxt, xc, wg, wu, wd)


def _combine_tile(x, meta, rows):
    li = lax.broadcasted_iota(jnp.int32, (TM, RC), 1).astype(f32)
    pt = jnp.where((li == meta[:, 0:1]) | (li == meta[:, 1:2]), 1.0, 0.0).astype(bf16)
    return x + jnp.dot(pt, rows, preferred_element_type=f32)


def _combine_kernel(src_ref, x_ref, meta_ref, o_hbm, yp_ref, ys_ref, stage, sem):
    i = pl.program_id(0)
    tiles = T // TM
    _start_gathers(i, NT, tiles * UPT, src_ref, o_hbm, stage, sem)
    slot = _wait_gather(i, tiles * UPT, src_ref, o_hbm, stage, sem)
    ys = [_combine_tile(x_ref[k * TM:(k + 1) * TM, :], meta_ref[k * TM:(k + 1) * TM, :],
                        stage[slot, k * RC:(k + 1) * RC, :]) for k in range(tiles)]
    y = jnp.concatenate(ys, axis=0)

    @pl.when(i < NT - 1)
    def _():
        yp_ref[...] = y

    @pl.when(i == NT - 1)
    def _():
        ys_ref[...] = y


def _combine(comb_src, x, meta, o_sorted):
    out_specs = [pl.BlockSpec((T, D), lambda i, src: (jnp.minimum(i, NT - 2), 0)),
                 pl.BlockSpec((T, D), lambda i, src: (0, 0))]
    out_shape = [jax.ShapeDtypeStruct((NP, D), f32), jax.ShapeDtypeStruct((NS, D), f32)]
    return pl.pallas_call(
        _combine_kernel,
        grid_spec=pltpu.PrefetchScalarGridSpec(
            num_scalar_prefetch=1,
            grid=(NT,),
            in_specs=[
                pl.BlockSpec((T, D), lambda i, src: (i, 0)),
                pl.BlockSpec((T, LANES), lambda i, src: (i, 0)),
                pl.BlockSpec(memory_space=pl.ANY),
            ],
            out_specs=out_specs,
            scratch_shapes=[
                pltpu.VMEM((NSLOT, (T // TM) * RC, D), bf16),
                pltpu.SemaphoreType.DMA((NSLOT,)),
            ],
        ),
        out_shape=out_shape,
        compiler_params=_params("arbitrary"),
        name="combine",
    )(comb_src, x, meta, o_sorted)


def _conv_kernel(src_ref, x1_ref, meta_ref, o_hbm, g_ref, win_ref, cw_ref, wout_ref,
                 p1_ref, p2_ref, y_ref, cu_ref, pad_ref, x_ref, stage, sem):
    i = pl.program_id(0)

    @pl.when(i == 0)
    def _():
        pad_ref[0:8, :] = jnp.zeros((8, D), f32)

    tiles = T // TM
    ahead = NSLOT - 1

    def start(s):
        for cp in _unit_copies(src_ref, s * tiles * UPT, tiles * UPT, o_hbm, stage,
                               lax.rem(s, NSLOT), sem):
            cp.start()

    @pl.when(i == 0)
    def _():
        for s in range(ahead):
            start(s)

    slot = _wait_gather(i, tiles * UPT, src_ref, o_hbm, stage, sem)

    for k in range(tiles):
        rows = slice(k * TM, (k + 1) * TM)
        x_ref[rows, :] = _combine_tile(x1_ref[rows, :], meta_ref[rows, :],
                                       stage[slot, k * RC:(k + 1) * RC, :])

    blocks = [(r, r + T // 2) for r in (0, T // 2)]
    gates = []
    for r0, r1 in blocks:
        h = _rms(x_ref[r0:r1, :], g_ref[...]).astype(bf16)
        bcu = jnp.dot(h, win_ref[...], preferred_element_type=f32)
        gates.append(bcu[:, :D])
        cu = bcu[:, D:2 * D] * bcu[:, 2 * D:]
        pad_ref[8 + r0:8 + r1, :] = cu
        cu_ref[r0:r1, :] = cu
    t = lax.broadcasted_iota(jnp.int32, (T // 2, 1), 0) & (TS - 1)
    is_sample = i == NT - 1
    cw = cw_ref[...]
    for (r0, r1), b in zip(blocks, gates):
        m1 = jnp.where(is_sample & (t == 0), p1_ref[r0:r1, :], pad_ref[7 + r0:7 + r1, :])
        m2 = jnp.where(is_sample & (t < 2), p2_ref[r0:r1, :], pad_ref[6 + r0:6 + r1, :])
        conv = cw[0:1] * m2 + cw[1:2] * m1 + cw[2:3] * pad_ref[8 + r0:8 + r1, :]
        y = jnp.dot((b * conv).astype(bf16), wout_ref[...], preferred_element_type=f32)
        y_ref[r0:r1, :] = x_ref[r0:r1, :] + y
    pad_ref[0:8, :] = pad_ref[T:T + 8, :]
    start(i + ahead)

    @pl.when(i == NT - 1)
    def _():
        for s in range(ahead):
            _wait_gather(i + 1 + s, tiles * UPT, src_ref, o_hbm, stage, sem)


def _conv(comb_src, x1, meta, o_sorted, g, win, cw, wout, p1, p2):
    const = dict(pipeline_mode=pl.Buffered(1))
    comb_src = jnp.concatenate([
        comb_src, jnp.full(((NSLOT - 1) * (T // TM) * UPT,), ZERO_UNIT_OUT, jnp.int32)])
    return pl.pallas_call(
        _conv_kernel,
        grid_spec=pltpu.PrefetchScalarGridSpec(
            num_scalar_prefetch=1,
            grid=(NT,),
            in_specs=[
                pl.BlockSpec((T, D), lambda i, src: (i, 0)),
                pl.BlockSpec((T, LANES), lambda i, src: (i, 0)),
                pl.BlockSpec(memory_space=pl.ANY),
                pl.BlockSpec((1, D), lambda i, src: (0, 0)),
                pl.BlockSpec((D, 3 * D), lambda i, src: (0, 0), **const),
                pl.BlockSpec((3, D), lambda i, src: (0, 0)),
                pl.BlockSpec((D, D), lambda i, src: (0, 0), **const),
                pl.BlockSpec((T, D), lambda i, src: (0, 0), **const),
                pl.BlockSpec((T, D), lambda i, src: (0, 0), **const),
            ],
            out_specs=[
                pl.BlockSpec((T, D), lambda i, src: (i, 0)),
                pl.BlockSpec((T, D), lambda i, src: (jnp.where(i == NT - 1, 1, 0), 0)),
            ],
            scratch_shapes=[
                pltpu.VMEM((T + 8, D), f32),
                pltpu.VMEM((T, D), f32),
                pltpu.VMEM((NSLOT, (T // TM) * RC, D), bf16),
                pltpu.SemaphoreType.DMA((NSLOT,)),
            ],
        ),
        out_shape=[
            jax.ShapeDtypeStruct((N, D), f32),
            jax.ShapeDtypeStruct((2 * T, D), f32),
        ],
        compiler_params=_params("arbitrary"),
        name="conv",
    )(comb_src, x1, meta, o_sorted, g, win, cw, wout, p1, p2)


def _rope_tables():
    inv_freq = THETA ** (-jnp.arange(HALF, dtype=f32) / HALF)
    pos = jnp.concatenate([
        jnp.arange(NP, dtype=jnp.int32),
        PAST + jnp.tile(jnp.arange(TS, dtype=jnp.int32), NB),
    ]).astype(f32)
    ang = inv_freq[:, None] * pos[None, :]
    return jnp.cos(ang), jnp.sin(ang)


def _router_weights(w_group, b_group, w_router, b_router):
    gap = ROW_E - NGRP
    pad = LANES - ROW_E - NE
    w = jnp.concatenate([w_group, jnp.zeros((D, gap), f32), w_router,
                         jnp.zeros((D, pad), f32)], axis=1)
    b = jnp.concatenate([b_group, jnp.zeros((gap,), f32), b_router,
                         jnp.zeros((pad,), f32)])[None, :]
    return w.astype(bf16), b


def _moe_experts(x, i, norm_ffn, w_group, b_group, w_router, b_router, w_gate, w_up, w_down):
    w, b = _router_weights(w_group[i], b_group[i], w_router[i], b_router[i])
    xc, meta, cnt = _dispatch(x, norm_ffn[i][None, :], w, b)
    eid, nused, ffn_src, nxt, comb_src = _dispatch_tables(cnt)
    o_sorted = _ffn(i, eid, nused, ffn_src, nxt, xc, w_gate, w_up, w_down)
    return comb_src, meta, o_sorted


def kernel(x_prompt, x_sample, cache_k, cache_v, state_conv, norm_mix, w_qkv, q_norm, k_norm,
           sinks, w_o, w_in, conv_w, w_out, norm_ffn, w_group, b_group, w_router, b_router,
           w_gate, w_up, w_down):
    xp = x_prompt.reshape(NP, D)
    xs = x_sample.reshape(NS, D)
    moe_w = (norm_ffn, w_group, b_group, w_router, b_router, w_gate, w_up, w_down)

    cos, sin = _rope_tables()
    (qT, ktok, vtok, vT), (q_s, ktok_s, vtok_s) = _qkv(
        xp, xs, norm_mix[0][None, :], w_qkv[0].T.astype(bf16), q_norm[0][:, None],
        k_norm[0][:, None], cos, sin)
    sink_rows = jnp.repeat(sinks[0].reshape(KVH, G), TQ, axis=1)[:, None, :]

    qs = q_s.reshape(KVH, G, HD, NB, TS).transpose(3, 0, 1, 4, 2)
    zq = jnp.zeros_like(qs[:, 0])
    qbd = jnp.stack([jnp.concatenate([qs[:, 0], zq], axis=-1),
                     jnp.concatenate([zq, qs[:, 1]], axis=-1)], axis=1)
    qbd = qbd.reshape(NB, H * TS, KVH * HD)
    k_new = ktok_s.reshape(NB, TS, KVH * HD)
    v_new = vtok_s.reshape(NB, TS, KVH * HD)
    pad4 = jnp.zeros((NB, 8 - TS, KVH * HD), f32)
    kc = cache_k[0].reshape(NB, WIN, KVH * HD)
    vc = cache_v[0].reshape(NB, WIN, KVH * HD)
    sink_col = jnp.repeat(sinks[0], TS)[:, None]
    o_s, kc_new, vc_new = _attn_sample(qbd, kc, vc, jnp.concatenate([k_new, pad4], axis=1),
                                       jnp.concatenate([v_new, pad4], axis=1), sink_col)
    o_s = o_s.reshape(NB, KVH, G, TS, KVH, HD)
    o_s = jnp.stack([o_s[:, 0, :, :, 0], o_s[:, 1, :, :, 1]], axis=1)
    o_s = o_s.transpose(0, 3, 1, 2, 4).reshape(NS, H * HD).astype(bf16)
    x = _attn_prompt(qT, ktok, vT, sink_rows, xp, xs, o_s, w_o[0].astype(bf16))
    comb_src, meta, o_sorted = _moe_experts(x, 0, *moe_w)

    new_k_prompt = ktok[NP - WIN:NP].reshape(1, 1, WIN, KVH, HD)
    new_v_prompt = vtok[NP - WIN:NP].reshape(1, 1, WIN, KVH, HD)
    new_k_sample = kc_new.reshape(1, NB, WIN, KVH, HD)
    new_v_sample = vc_new.reshape(1, NB, WIN, KVH, HD)

    st = state_conv[0]
    z = jnp.zeros((NB, 1, D), f32)
    p1 = jnp.concatenate([st[:, 1:2], z, z, z], axis=1).reshape(NS, D)
    p2 = jnp.concatenate([st[:, 0:1], st[:, 1:2], z, z], axis=1).reshape(NS, D)
    x, cu = _conv(comb_src, x, meta, o_sorted, norm_mix[1][None, :], w_in[0].astype(bf16),
                  conv_w[0], w_out[0].astype(bf16), p1, p2)
    comb_src, meta, o_sorted = _moe_experts(x, 1, *moe_w)
    y_prompt, y_sample = _combine(comb_src, x, meta, o_sorted)

    new_conv_prompt = cu[T - 2:T].reshape(1, 1, 2, D)
    new_conv_sample = cu[T:].reshape(NB, TS, D)[:, TS - 2:][None]

    y_prompt = y_prompt.reshape(1, NP, D)
    y_sample = y_sample.reshape(NB, TS, D)
    return (y_prompt, y_sample, new_k_prompt, new_v_prompt, new_conv_prompt,
            new_k_sample, new_v_sample, new_conv_sample)
```

```python
import functools

import jax
import jax.numpy as jnp
import numpy as np
from jax import lax
from jax.experimental import pallas as pl
from jax.experimental.pallas import tpu as pltpu

D = 1024
NP = 16384
NB = 128
TS = 4
NS = NB * TS
N = NP + NS
PAST = 16384
H = 16
KVH = 2
G = H // KVH
HD = 64
HALF = HD // 2
QKV = (H + 2 * KVH) * HD
WIN = 128
THETA = 10000.0
NGRP = 4
EPG = 4
NE = NGRP * EPG
TOPK = 2
ROW_E = 8
F = 512
EPS = 1e-6
SCALE = HD ** -0.5

T = 512
NT = N // T
TQ = 128
NQ = NP // TQ
QB = 2
BB = 16
LANES = 128
V7X_VMEM_BYTES = 64 * 1024 * 1024
VMEM_LIMIT = V7X_VMEM_BYTES - 14 * 1024 * 1024

TM = 256
NTM = N // TM
DT = 6
UNIT = 16
RC = TOPK * TM + NE * UNIT
UPT = RC // UNIT
XW = D + LANES
CH = 512
CHB = 256
UPC = CH // UNIT
NSLOT = 4
NCH = -(-(NTM * (UPT - 1) + NE * (UPC - 1)) // UPC) + NSLOT - 1
ZERO_UNIT_IN = UPT - 1
ZERO_UNIT_OUT = (NCH - 1) * UPC

f32 = jnp.float32
bf16 = jnp.bfloat16


def _params(*sem):
    return pltpu.CompilerParams(dimension_semantics=sem, vmem_limit_bytes=VMEM_LIMIT)


def _rms(x, g):
    ms = jnp.mean(x * x, axis=-1, keepdims=True)
    return x * lax.rsqrt(ms + EPS) * g


def _qkv_prompt_kernel(x_ref, g_ref, wT_ref, qn_ref, kn_ref, cos_ref, sin_ref,
                       qT_ref, ktok_ref, vtok_ref, vT_ref):
    _qkv_tile(x_ref, g_ref, wT_ref, qn_ref, kn_ref, cos_ref, sin_ref,
              qT_ref, ktok_ref, vtok_ref, vT_ref)


def _qkv_sample_kernel(x_ref, g_ref, wT_ref, qn_ref, kn_ref, cos_ref, sin_ref,
                       q_ref, ktok_ref, vtok_ref):
    _qkv_tile(x_ref, g_ref, wT_ref, qn_ref, kn_ref, cos_ref, sin_ref,
              q_ref, ktok_ref, vtok_ref, None)


def _qkv_tile(x_ref, g_ref, wT_ref, qn_ref, kn_ref, cos_ref, sin_ref,
              q_ref, ktok_ref, vtok_ref, vT_ref):
    h = _rms(x_ref[...], g_ref[...]).astype(bf16)
    qkvT = lax.dot_general(wT_ref[...], h, (((1,), (1,)), ((), ())),
                           preferred_element_type=f32)
    cos = cos_ref[...]
    sin = sin_ref[...]

    def norm_rope(blk, gcol):
        ms = jnp.mean(blk * blk, axis=0, keepdims=True)
        y = blk * lax.rsqrt(ms + EPS) * gcol
        y1 = y[:HALF]
        y2 = y[HALF:]
        return y1 * cos - y2 * sin, y2 * cos + y1 * sin

    qn = qn_ref[...]
    for hd in range(H):
        o1, o2 = norm_rope(qkvT[hd * HD:(hd + 1) * HD], qn)
        q_ref[hd * HD:hd * HD + HALF, :] = (o1 * SCALE).astype(bf16)
        q_ref[hd * HD + HALF:(hd + 1) * HD, :] = (o2 * SCALE).astype(bf16)
    kn = kn_ref[...]
    ks = []
    for j in range(KVH):
        o1, o2 = norm_rope(qkvT[H * HD + j * HD:H * HD + (j + 1) * HD], kn)
        ks += [o1, o2]
    kT = jnp.concatenate(ks, axis=0)
    ktok_ref[...] = kT.T
    vT = qkvT[(H + KVH) * HD:]
    vtok_ref[...] = vT.T
    if vT_ref is not None:
        vT_ref[...] = vT.astype(bf16)


def _qkv(xp, xs, g, wT, qn, kn, cos, sin):
    const = [pl.BlockSpec((HD, 1), lambda i: (0, 0)), pl.BlockSpec((HD, 1), lambda i: (0, 0)),
             pl.BlockSpec((HALF, T), lambda i: (0, i)), pl.BlockSpec((HALF, T), lambda i: (0, i))]
    x_spec = pl.BlockSpec((T, D), lambda i: (i, 0))
    g_spec = pl.BlockSpec((1, D), lambda i: (0, 0))
    w_spec = pl.BlockSpec((QKV, D), lambda i: (0, 0))
    tok_spec = pl.BlockSpec((T, KVH * HD), lambda i: (i, 0))
    nt = NP // T
    prompt = pl.pallas_call(
        _qkv_prompt_kernel,
        grid=(nt,),
        in_specs=[x_spec, g_spec, w_spec] + const,
        out_specs=[pl.BlockSpec((None, H * HD, T), lambda i: (i, 0, 0)), tok_spec, tok_spec,
                   pl.BlockSpec((None, KVH * HD, T), lambda i: (i, 0, 0))],
        out_shape=[jax.ShapeDtypeStruct((nt, H * HD, T), bf16),
                   jax.ShapeDtypeStruct((NP, KVH * HD), f32),
                   jax.ShapeDtypeStruct((NP, KVH * HD), f32),
                   jax.ShapeDtypeStruct((nt, KVH * HD, T), bf16)],
        compiler_params=_params("parallel"),
        name="qkv",
    )(xp, g, wT, qn, kn, cos[:, :NP], sin[:, :NP])
    sample = pl.pallas_call(
        _qkv_sample_kernel,
        grid=(NS // T,),
        in_specs=[x_spec, g_spec, w_spec] + const,
        out_specs=[pl.BlockSpec((H * HD, T), lambda i: (0, i)), tok_spec, tok_spec],
        out_shape=[jax.ShapeDtypeStruct((H * HD, NS), bf16),
                   jax.ShapeDtypeStruct((NS, KVH * HD), f32),
                   jax.ShapeDtypeStruct((NS, KVH * HD), f32)],
        compiler_params=_params("parallel"),
        name="qkv_sample",
    )(xs, g, wT, qn, kn, cos[:, NP:], sin[:, NP:])
    return prompt, sample


def _attn_prompt_kernel(qT_ref, kp_ref, kc_ref, vp_ref, vc_ref, sink_ref, bias0_ref, bias_ref,
                        xp_ref, xs_ref, os_ref, wo_ref, y_ref):
    j = pl.program_id(0)

    @pl.when(j >= NQ // QB)
    def _():
        y_ref[...] = xs_ref[...] + jnp.dot(os_ref[...], wo_ref[...],
                                           preferred_element_type=f32)

    @pl.when(j < NQ // QB)
    def _():
        k_all = jnp.concatenate([kp_ref[...], kc_ref[...]], axis=0)
        v_all = jnp.concatenate([vp_ref[...], vc_ref[...]], axis=1)
        blocks = []
        for k in range(QB):
            rows = slice(k * TQ, (k + 1) * TQ)
            blocks.append(_attend_block(
                qT_ref.at[:, rows], k_all[k * TQ:(k + 2) * TQ], v_all[:, k * TQ:(k + 2) * TQ],
                sink_ref, bias0_ref if k == 0 else bias_ref, xp_ref.at[rows], wo_ref,
                y_ref.at[rows]))
        for _ in range(3):
            for blk in blocks:
                next(blk, None)


def _attend_block(qT_ref, kk, vv, sink_ref, bias_ref, x_ref, wo_ref, y_ref):
    kk = kk.astype(bf16)
    bias = jnp.concatenate([bias_ref[...]] * H, axis=1)
    qg = [jnp.concatenate([qT_ref[(g * G + hh) * HD:(g * G + hh + 1) * HD, :]
                           for hh in range(G)], axis=1) for g in range(KVH)]
    zeros = jnp.zeros_like(qg[0])
    rhs = jnp.concatenate([jnp.concatenate([qg[0], zeros], axis=1),
                           jnp.concatenate([zeros, qg[1]], axis=1)], axis=0)
    sT = jnp.dot(kk, rhs, preferred_element_type=f32) + bias
    yield
    sink = jnp.concatenate([sink_ref[0], sink_ref[1]], axis=1)
    m = jnp.maximum(jnp.max(sT, axis=0, keepdims=True), sink)
    p = jnp.exp(sT - m)
    l = jnp.sum(p, axis=0, keepdims=True) + jnp.exp(sink - m)
    p = (p * (1.0 / l)).astype(bf16)
    yield
    pieces = []
    for g in range(KVH):
        oT = jnp.dot(vv[g * HD:(g + 1) * HD, :], p[:, g * G * TQ:(g + 1) * G * TQ],
                     preferred_element_type=f32)
        pieces += [oT[:, hh * TQ:(hh + 1) * TQ] for hh in range(G)]
    oT_all = jnp.concatenate(pieces, axis=0)
    y_ref[...] = x_ref[...] + jnp.dot(oT_all.T.astype(bf16), wo_ref[...],
                                      preferred_element_type=f32)


def _band_bias():
    s = np.arange(2 * TQ)[:, None]
    t = np.arange(TQ)[None, :]
    dist = t + TQ - s
    band = (dist >= 0) & (dist <= WIN)
    first = band & (s >= TQ)
    return jnp.asarray(np.where(np.stack([first, band]), 0.0, -np.inf), f32)


def _attn_prompt(qT, ktok, vT, sink_rows, xp, xs, o_s, wo):
    steps = NQ // QB
    rows = QB * TQ
    spt = T // rows
    bpt = T // TQ

    def cur(j):
        return jnp.minimum(j, steps - 1)

    def prev(j):
        return jnp.maximum(cur(j) * QB - 1, 0)

    def sample(j):
        return jnp.maximum(j - steps, 0)

    bias = _band_bias()
    return pl.pallas_call(
        _attn_prompt_kernel,
        grid=(N // rows,),
        in_specs=[
            pl.BlockSpec((None, H * HD, rows), lambda j: (cur(j) // spt, 0, cur(j) % spt)),
            pl.BlockSpec((TQ, KVH * HD), lambda j: (prev(j), 0)),
            pl.BlockSpec((rows, KVH * HD), lambda j: (cur(j), 0)),
            pl.BlockSpec((None, KVH * HD, TQ), lambda j: (prev(j) // bpt, 0, prev(j) % bpt)),
            pl.BlockSpec((None, KVH * HD, rows), lambda j: (cur(j) // spt, 0, cur(j) % spt)),
            pl.BlockSpec((KVH, 1, G * TQ), lambda j: (0, 0, 0)),
            pl.BlockSpec((None, 2 * TQ, TQ), lambda j: (jnp.minimum(j, 1), 0, 0)),
            pl.BlockSpec((None, 2 * TQ, TQ), lambda j: (1, 0, 0)),
            pl.BlockSpec((rows, D), lambda j: (cur(j), 0)),
            pl.BlockSpec((rows, D), lambda j: (sample(j), 0)),
            pl.BlockSpec((rows, H * HD), lambda j: (sample(j), 0)),
            pl.BlockSpec((D, D), lambda j: (0, 0)),
        ],
        out_specs=pl.BlockSpec((rows, D), lambda j: (j, 0)),
        out_shape=jax.ShapeDtypeStruct((N, D), f32),
        compiler_params=_params("parallel"),
        name="attn_prompt",
    )(qT, ktok, ktok, vT, vT, sink_rows, bias, bias, xp, xs, o_s, wo)


def _attn_sample_kernel(q_ref, kc_ref, vc_ref, kn_ref, vn_ref, sink_ref,
                        o_ref, knew_ref, vnew_ref):
    rows = H * TS
    t1 = lax.broadcasted_iota(jnp.int32, (1, rows, WIN), 1) & (TS - 1)
    s1 = lax.broadcasted_iota(jnp.int32, (1, rows, WIN), 2)
    valid1 = s1 >= t1
    t2 = lax.broadcasted_iota(jnp.int32, (1, rows, 8), 1) & (TS - 1)
    s2 = lax.broadcasted_iota(jnp.int32, (1, rows, 8), 2)
    valid2 = s2 <= t2
    sink = sink_ref[...][None]
    q = q_ref[...]
    kc = kc_ref[...]
    vc = vc_ref[...]
    kn = kn_ref[...]
    vn = vn_ref[...]
    sc = jnp.einsum('bqd,bkd->bqk', q, kc.astype(bf16), preferred_element_type=f32)
    sn = jnp.einsum('bqd,bkd->bqk', q, kn.astype(bf16), preferred_element_type=f32)
    sc = jnp.where(valid1, sc, -jnp.inf)
    sn = jnp.where(valid2, sn, -jnp.inf)
    m = jnp.maximum(jnp.maximum(jnp.max(sc, axis=-1, keepdims=True),
                                jnp.max(sn, axis=-1, keepdims=True)), sink)
    pc = jnp.exp(sc - m)
    pn = jnp.exp(sn - m)
    l = (jnp.sum(pc, axis=-1, keepdims=True) + jnp.sum(pn, axis=-1, keepdims=True)
         + jnp.exp(sink - m))
    r = 1.0 / l
    o_ref[...] = (jnp.einsum('bqk,bkd->bqd', (pc * r).astype(bf16), vc.astype(bf16),
                             preferred_element_type=f32)
                  + jnp.einsum('bqk,bkd->bqd', (pn * r).astype(bf16), vn.astype(bf16),
                               preferred_element_type=f32))
    knew_ref[:, :WIN - TS, :] = kc[:, TS:, :]
    knew_ref[:, WIN - TS:, :] = kn[:, :TS, :]
    vnew_ref[:, :WIN - TS, :] = vc[:, TS:, :]
    vnew_ref[:, WIN - TS:, :] = vn[:, :TS, :]


def _attn_sample(qbd, kc, vc, kn, vn, sink_col):
    rows = H * TS
    cache_spec = pl.BlockSpec((BB, WIN, KVH * HD), lambda i: (i, 0, 0))
    new_spec = pl.BlockSpec((BB, 8, KVH * HD), lambda i: (i, 0, 0))
    cache_shape = jax.ShapeDtypeStruct((NB, WIN, KVH * HD), f32)
    return pl.pallas_call(
        _attn_sample_kernel,
        grid=(NB // BB,),
        in_specs=[
            pl.BlockSpec((BB, rows, KVH * HD), lambda i: (i, 0, 0)),
            cache_spec, cache_spec, new_spec, new_spec,
            pl.BlockSpec((rows, 1), lambda i: (0, 0)),
        ],
        out_specs=[pl.BlockSpec((BB, rows, KVH * HD), lambda i: (i, 0, 0)),
                   cache_spec, cache_spec],
        out_shape=[jax.ShapeDtypeStruct((NB, rows, KVH * HD), f32), cache_shape, cache_shape],
        compiler_params=_params("parallel"),
        name="attn_sample",
    )(qbd, kc, vc, kn, vn, sink_col)


def _dispatch_kernel(x_ref, g_ref, w_ref, b_ref, upper_ref, xc_ref, meta_ref, cnt_ref):
    tiles = [_dispatch_tile(x_ref.at[k * TM:(k + 1) * TM], g_ref, w_ref, b_ref, upper_ref,
                            xc_ref.at[k * RC:(k + 1) * RC], meta_ref.at[k * TM:(k + 1) * TM],
                            cnt_ref.at[k * NE:(k + 1) * NE]) for k in range(DT)]
    for _ in range(3):
        for t in tiles:
            next(t, None)


def _dispatch_tile(x_ref, g_ref, w_ref, b_ref, upper_ref, xc_ref, meta_ref, cnt_ref):
    h_hi = _rms(x_ref[...], g_ref[...]).astype(bf16)
    logits = jnp.dot(h_hi, w_ref[...], preferred_element_type=f32) + b_ref[...]
    yield
    lt = logits.T
    inf = jnp.inf
    row8 = lax.broadcasted_iota(jnp.int32, (8, TM), 0).astype(f32)
    gl = jnp.where(row8 < NGRP, lt[0:8], -inf)
    gmax = jnp.max(gl, axis=0, keepdims=True)
    gsel = jnp.min(jnp.where(gl == gmax, row8, 8.0), axis=0, keepdims=True)
    g_w = 1.0 / jnp.sum(jnp.exp(gl - gmax), axis=0, keepdims=True)
    row = lax.broadcasted_iota(jnp.int32, (NE, TM), 0)
    row_f = row.astype(f32)
    el = jnp.where((row >> 2).astype(f32) == gsel, lt[ROW_E:ROW_E + NE], -inf)
    v1 = jnp.max(el, axis=0, keepdims=True)
    i1 = jnp.min(jnp.where(el == v1, row_f, float(NE)), axis=0, keepdims=True)
    el2 = jnp.where(row_f == i1, -inf, el)
    v2 = jnp.max(el2, axis=0, keepdims=True)
    i2 = jnp.min(jnp.where(el2 == v2, row_f, float(NE)), axis=0, keepdims=True)
    e1 = jnp.exp(v2 - v1)
    den = 1.0 + e1
    w1 = (1.0 / den) * g_w
    w2 = (e1 / den) * g_w

    m1 = row_f == i1
    m2 = row_f == i2
    sel = jnp.where(m1 | m2, 1.0, 0.0)
    ranks = jnp.dot(sel.astype(bf16), upper_ref[...],
                    preferred_element_type=f32)
    counts = jnp.sum(sel, axis=1, keepdims=True)
    padded = jnp.floor((counts + (UNIT - 1.0)) * (1.0 / UNIT)) * UNIT
    e_i = lax.broadcasted_iota(jnp.int32, (NE, NE), 0)
    f_i = lax.broadcasted_iota(jnp.int32, (NE, NE), 1)
    below = jnp.where(f_i < e_i, 1.0, 0.0).astype(bf16)
    seg = jnp.dot(below, jnp.broadcast_to(padded, (NE, LANES)).astype(bf16),
                  preferred_element_type=f32)[:, 0:1]
    posall = seg + ranks
    pos1 = jnp.sum(jnp.where(m1, posall, 0.0), axis=0, keepdims=True)
    pos2 = jnp.sum(jnp.where(m2, posall, 0.0), axis=0, keepdims=True)
    cnt_ref[...] = jnp.broadcast_to(counts, (NE, LANES))

    w1_hi = w1.astype(bf16).astype(f32)
    w2_hi = w2.astype(bf16).astype(f32)
    slab = jnp.zeros((8, TM), f32)
    for k, r in enumerate((pos1, pos2, w1_hi, w1 - w1_hi, w2_hi, w2 - w2_hi, i1)):
        slab = jnp.where(row8 == k, r, slab)
    meta = jnp.concatenate([slab, jnp.zeros((LANES - 8, TM), f32)], axis=0).T
    meta_ref[...] = meta
    yield

    rr = lax.broadcasted_iota(jnp.int32, (RC, TM), 0).astype(f32)
    onehot = jnp.where((rr == pos1) | (rr == pos2), 1.0, 0.0).astype(bf16)
    h_aug = jnp.concatenate([h_hi, meta.astype(bf16)], axis=1)
    xc_ref[...] = jnp.dot(onehot, h_aug, preferred_element_type=f32).astype(bf16)


def _dispatch(x, g, w, b):
    upper = jnp.asarray(np.triu(np.ones((TM, TM), np.float32), 1), bf16)
    return pl.pallas_call(
        _dispatch_kernel,
        grid=(NTM // DT,),
        in_specs=[
            pl.BlockSpec((DT * TM, D), lambda i: (i, 0)),
            pl.BlockSpec((1, D), lambda i: (0, 0)),
            pl.BlockSpec((D, LANES), lambda i: (0, 0)),
            pl.BlockSpec((1, LANES), lambda i: (0, 0)),
            pl.BlockSpec((TM, TM), lambda i: (0, 0)),
        ],
        out_specs=[
            pl.BlockSpec((DT * RC, XW), lambda i: (i, 0)),
            pl.BlockSpec((DT * TM, LANES), lambda i: (i, 0)),
            pl.BlockSpec((DT * NE, LANES), lambda i: (i, 0)),
        ],
        out_shape=[
            jax.ShapeDtypeStruct((NTM * RC, XW), bf16),
            jax.ShapeDtypeStruct((N, LANES), f32),
            jax.ShapeDtypeStruct((NTM * NE, LANES), f32),
        ],
        compiler_params=_params("parallel"),
        name="dispatch",
    )(x, g, w, b, upper)


def _dispatch_tables(cnt):
    i32 = jnp.int32
    n = cnt.reshape(NTM, NE, LANES)[:, :, 0].astype(i32)
    units = (n + UNIT - 1) // UNIT
    seg_end = jnp.cumsum(units, axis=1)
    seg_start = seg_end - units
    col_end = jnp.cumsum(units, axis=0)
    col_start = col_end - units
    chunks = (col_end[-1] + UPC - 1) // UPC
    ch_end = jnp.cumsum(chunks)
    ch_start = ch_end - chunks
    nused = ch_end[-1]
    c = jnp.arange(NCH, dtype=i32)
    eid = jnp.minimum(jnp.sum((ch_end[None, :] <= c[:, None]).astype(i32), axis=1), NE - 1)

    src0 = jnp.arange(NTM, dtype=i32)[:, None] * UPT + seg_start
    dst0 = ch_start[None, :] * UPC + col_start
    k = jnp.arange((NCH + NSLOT - 1) * UPC, dtype=i32)[:, None, None]
    inside = (k >= dst0[None]) & (k < (dst0 + units)[None])
    found = jnp.sum(inside.astype(i32), axis=(1, 2))
    shift = jnp.sum(jnp.where(inside, (src0 - dst0)[None], 0), axis=(1, 2))
    ffn_src = jnp.where(found > 0, k[:, 0, 0] + shift, ZERO_UNIT_IN)

    v = jnp.arange(UPT, dtype=i32)[None, :, None]
    inside_v = (v >= seg_start[:, None, :]) & (v < seg_end[:, None, :])
    found_v = jnp.sum(inside_v.astype(i32), axis=2)
    shift_v = jnp.sum(jnp.where(inside_v, (dst0 - seg_start)[:, None, :], 0), axis=2)
    comb_src = jnp.where(found_v > 0, v[:, :, 0] + shift_v, ZERO_UNIT_OUT).reshape(-1)
    own_end = jnp.sum(jnp.where(eid[:, None] == jnp.arange(NE, dtype=i32)[None, :],
                                ch_end[None, :], 0), axis=1)
    eid_at = jnp.sum(jnp.where(c[None, :] == own_end[:, None], eid[None, :], 0), axis=1)
    nxt = jnp.where(own_end < nused, eid_at, -1)
    return eid, nused.reshape(1), ffn_src, nxt, comb_src


def _unit_copies(src_ref, base, n_units, src_hbm, stage, slot, sem):
    out = []
    for j in range(n_units):
        row = pl.multiple_of(src_ref[base + j] * UNIT, UNIT)
        out.append(pltpu.make_async_copy(
            src_hbm.at[pl.ds(row, UNIT), :],
            stage.at[slot, pl.ds(j * UNIT, UNIT), :],
            sem.at[slot]))
    return out


def _start_gathers(step, n_steps, n_units, src_ref, src_hbm, stage, sem):
    ahead = NSLOT - 1

    def start(s, slot):
        for cp in _unit_copies(src_ref, s * n_units, n_units, src_hbm, stage, slot, sem):
            cp.start()

    for s in range(ahead):
        @pl.when((step == 0) & (s < n_steps))
        def _():
            start(s, s)

    @pl.when(step + ahead < n_steps)
    def _():
        start(step + ahead, lax.rem(step + ahead, NSLOT))


def _wait_gather(step, n_units, src_ref, src_hbm, stage, sem):
    slot = lax.rem(step, NSLOT)
    for cp in _unit_copies(src_ref, step * n_units, n_units, src_hbm, stage, slot, sem):
        cp.wait()
    return slot


def _ffn_kernel(layer, eid_ref, nused_ref, src_ref, nxt_ref, xc_hbm, wg_hbm, wu_hbm, wd_hbm,
                o_hbm, stage, sem, wg_f, wu_f, wd_f, w_sem, wg_b, wu_b, wd_b, obuf, o_sem):
    nused = nused_ref[0]
    ahead = NSLOT - 1

    def start(s):
        for cp in _unit_copies(src_ref, s * UPC, UPC, xc_hbm, stage, lax.rem(s, NSLOT), sem):
            cp.start()

    def weight_copies(e, slot):
        return [pltpu.make_async_copy(hbm.at[layer, e], buf.at[slot], w_sem.at[slot])
                for hbm, buf in ((wg_hbm, wg_f), (wu_hbm, wu_f), (wd_hbm, wd_f))]

    def out_copy(c, slot):
        row = pl.multiple_of(c * CH, CH)
        return pltpu.make_async_copy(obuf.at[slot], o_hbm.at[pl.ds(row, CH), :], o_sem.at[slot])

    for s in range(ahead):
        start(s)

    @pl.when(nused > 0)
    def _():
        for cp in weight_copies(eid_ref[0], 0):
            cp.start()

    def chunk(c, n_experts):
        e = eid_ref[c]
        first = (c == 0) | (e != eid_ref[jnp.maximum(c - 1, 0)])
        wslot = lax.rem(n_experts, 2)

        @pl.when(first)
        def _():
            for cp in weight_copies(e, wslot):
                cp.wait()
            wg_b[...] = wg_f[wslot].astype(bf16)
            wu_b[...] = wu_f[wslot].astype(bf16)
            wd_b[...] = wd_f[wslot].astype(bf16)

            @pl.when(nxt_ref[c] >= 0)
            def _():
                for cp in weight_copies(nxt_ref[c], 1 - wslot):
                    cp.start()

        slot = _wait_gather(c, UPC, src_ref, xc_hbm, stage, sem)
        oslot = lax.rem(c, 2)

        @pl.when(c >= 2)
        def _():
            out_copy(c - 2, oslot).wait()

        def block(r):
            xs = stage[slot, r:r + CHB, :]
            x = xs[:, :D]
            a = jnp.dot(x, wg_b[...], preferred_element_type=f32)
            u = jnp.dot(x, wu_b[...], preferred_element_type=f32)
            yield
            gb = xs[:, D:].astype(f32)
            is_first = gb[:, 6:7] == e.astype(f32)
            gate = jnp.where(is_first, gb[:, 2:3] + gb[:, 3:4], gb[:, 4:5] + gb[:, 5:6])
            act = (a * (1.0 / (1.0 + jnp.exp(-a)))) * u * gate
            obuf[oslot, r:r + CHB, :] = jnp.dot(act.astype(bf16), wd_b[...],
                                                preferred_element_type=f32).astype(bf16)

        blocks = [block(r) for r in range(0, CH, CHB)]
        for _ in range(2):
            for blk in blocks:
                next(blk, None)
        out_copy(c, oslot).start()
        start(c + ahead)
        return n_experts + first.astype(jnp.int32)

    lax.fori_loop(0, nused, chunk, jnp.int32(0))

    for s in range(ahead):
        _wait_gather(nused + s, UPC, src_ref, xc_hbm, stage, sem)
    for back in (1, 2):
        @pl.when(nused >= back)
        def _():
            out_copy(nused - back, lax.rem(nused - back, 2)).wait()

    obuf[0] = jnp.zeros((CH, D), bf16)

    def zero_start(c, carry):
        out_copy(c, 0).start()
        return carry

    def zero_wait(c, carry):
        out_copy(c, 0).wait()
        return carry

    lax.fori_loop(nused, NCH, zero_start, 0)
    lax.fori_loop(nused, NCH, zero_wait, 0)


def _ffn(layer, eid, nused, ffn_src, nxt, xc, wg, wu, wd):
    any_spec = pl.BlockSpec(memory_space=pl.ANY)
    return pl.pallas_call(
        functools.partial(_ffn_kernel, layer),
        grid_spec=pltpu.PrefetchScalarGridSpec(
            num_scalar_prefetch=4,
            grid=(1,),
            in_specs=[any_spec, any_spec, any_spec, any_spec],
            out_specs=any_spec,
            scratch_shapes=[
                pltpu.VMEM((NSLOT, CH, XW), bf16),
                pltpu.SemaphoreType.DMA((NSLOT,)),
                pltpu.VMEM((2, D, F), f32),
                pltpu.VMEM((2, D, F), f32),
                pltpu.VMEM((2, F, D), f32),
                pltpu.SemaphoreType.DMA((2,)),
                pltpu.VMEM((D, F), bf16),
                pltpu.VMEM((D, F), bf16),
                pltpu.VMEM((F, D), bf16),
                pltpu.VMEM((2, CH, D), bf16),
                pltpu.SemaphoreType.DMA((2,)),
            ],
        ),
        out_shape=jax.ShapeDtypeStruct((NCH * CH, D), bf16),
        compiler_params=_params("arbitrary"),
        name="ffn",
    )(eid, nused, ffn_src, nxt, xc, wg, wu, wd)


def _combine_tile(x, meta, rows):
    li = lax.broadcasted_iota(jnp.int32, (TM, RC), 1).astype(f32)
    pt = jnp.where((li == meta[:, 0:1]) | (li == meta[:, 1:2]), 1.0, 0.0).astype(bf16)
    return x + jnp.dot(pt, rows, preferred_element_type=f32)


def _combine_kernel(src_ref, x_ref, meta_ref, o_hbm, yp_ref, ys_ref, stage, sem):
    i = pl.program_id(0)
    tiles = T // TM
    _start_gathers(i, NT, tiles * UPT, src_ref, o_hbm, stage, sem)
    slot = _wait_gather(i, tiles * UPT, src_ref, o_hbm, stage, sem)
    ys = [_combine_tile(x_ref[k * TM:(k + 1) * TM, :], meta_ref[k * TM:(k + 1) * TM, :],
                        stage[slot, k * RC:(k + 1) * RC, :]) for k in range(tiles)]
    y = jnp.concatenate(ys, axis=0)

    @pl.when(i < NT - 1)
    def _():
        yp_ref[...] = y

    @pl.when(i == NT - 1)
    def _():
        ys_ref[...] = y


def _combine(comb_src, x, meta, o_sorted):
    out_specs = [pl.BlockSpec((T, D), lambda i, src: (jnp.minimum(i, NT - 2), 0)),
                 pl.BlockSpec((T, D), lambda i, src: (0, 0))]
    out_shape = [jax.ShapeDtypeStruct((NP, D), f32), jax.ShapeDtypeStruct((NS, D), f32)]
    return pl.pallas_call(
        _combine_kernel,
        grid_spec=pltpu.PrefetchScalarGridSpec(
            num_scalar_prefetch=1,
            grid=(NT,),
            in_specs=[
                pl.BlockSpec((T, D), lambda i, src: (i, 0)),
                pl.BlockSpec((T, LANES), lambda i, src: (i, 0)),
                pl.BlockSpec(memory_space=pl.ANY),
            ],
            out_specs=out_specs,
            scratch_shapes=[
                pltpu.VMEM((NSLOT, (T // TM) * RC, D), bf16),
                pltpu.SemaphoreType.DMA((NSLOT,)),
            ],
        ),
        out_shape=out_shape,
        compiler_params=_params("arbitrary"),
        name="combine",
    )(comb_src, x, meta, o_sorted)


def _conv_kernel(src_ref, x1_ref, meta_ref, o_hbm, g_ref, win_ref, cw_ref, wout_ref,
                 p1_ref, p2_ref, y_ref, cu_ref, pad_ref, x_ref, stage, sem):
    i = pl.program_id(0)

    @pl.when(i == 0)
    def _():
        pad_ref[0:8, :] = jnp.zeros((8, D), f32)

    tiles = T // TM
    ahead = NSLOT - 1

    def start(s):
        for cp in _unit_copies(src_ref, s * tiles * UPT, tiles * UPT, o_hbm, stage,
                               lax.rem(s, NSLOT), sem):
            cp.start()

    @pl.when(i == 0)
    def _():
        for s in range(ahead):
            start(s)

    slot = _wait_gather(i, tiles * UPT, src_ref, o_hbm, stage, sem)

    for k in range(tiles):
        rows = slice(k * TM, (k + 1) * TM)
        x_ref[rows, :] = _combine_tile(x1_ref[rows, :], meta_ref[rows, :],
                                       stage[slot, k * RC:(k + 1) * RC, :])

    blocks = [(r, r + T // 2) for r in (0, T // 2)]
    gates = []
    for r0, r1 in blocks:
        h = _rms(x_ref[r0:r1, :], g_ref[...]).astype(bf16)
        bcu = jnp.dot(h, win_ref[...], preferred_element_type=f32)
        gates.append(bcu[:, :D])
        cu = bcu[:, D:2 * D] * bcu[:, 2 * D:]
        pad_ref[8 + r0:8 + r1, :] = cu
        cu_ref[r0:r1, :] = cu
    t = lax.broadcasted_iota(jnp.int32, (T // 2, 1), 0) & (TS - 1)
    is_sample = i == NT - 1
    cw = cw_ref[...]
    for (r0, r1), b in zip(blocks, gates):
        m1 = jnp.where(is_sample & (t == 0), p1_ref[r0:r1, :], pad_ref[7 + r0:7 + r1, :])
        m2 = jnp.where(is_sample & (t < 2), p2_ref[r0:r1, :], pad_ref[6 + r0:6 + r1, :])
        conv = cw[0:1] * m2 + cw[1:2] * m1 + cw[2:3] * pad_ref[8 + r0:8 + r1, :]
        y = jnp.dot((b * conv).astype(bf16), wout_ref[...], preferred_element_type=f32)
        y_ref[r0:r1, :] = x_ref[r0:r1, :] + y
    pad_ref[0:8, :] = pad_ref[T:T + 8, :]
    start(i + ahead)

    @pl.when(i == NT - 1)
    def _():
        for s in range(ahead):
            _wait_gather(i + 1 + s, tiles * UPT, src_ref, o_hbm, stage, sem)


def _conv(comb_src, x1, meta, o_sorted, g, win, cw, wout, p1, p2):
    const = dict(pipeline_mode=pl.Buffered(1))
    comb_src = jnp.concatenate([
        comb_src, jnp.full(((NSLOT - 1) * (T // TM) * UPT,), ZERO_UNIT_OUT, jnp.int32)])
    return pl.pallas_call(
        _conv_kernel,
        grid_spec=pltpu.PrefetchScalarGridSpec(
            num_scalar_prefetch=1,
            grid=(NT,),
            in_specs=[
                pl.BlockSpec((T, D), lambda i, src: (i, 0)),
                pl.BlockSpec((T, LANES), lambda i, src: (i, 0)),
                pl.BlockSpec(memory_space=pl.ANY),
                pl.BlockSpec((1, D), lambda i, src: (0, 0)),
                pl.BlockSpec((D, 3 * D), lambda i, src: (0, 0), **const),
                pl.BlockSpec((3, D), lambda i, src: (0, 0)),
                pl.BlockSpec((D, D), lambda i, src: (0, 0), **const),
                pl.BlockSpec((T, D), lambda i, src: (0, 0), **const),
                pl.BlockSpec((T, D), lambda i, src: (0, 0), **const),
            ],
            out_specs=[
                pl.BlockSpec((T, D), lambda i, src: (i, 0)),
                pl.BlockSpec((T, D), lambda i, src: (jnp.where(i == NT - 1, 1, 0), 0)),
            ],
            scratch_shapes=[
                pltpu.VMEM((T + 8, D), f32),
                pltpu.VMEM((T, D), f32),
                pltpu.VMEM((NSLOT, (T // TM) * RC, D), bf16),
                pltpu.SemaphoreType.DMA((NSLOT,)),
            ],
        ),
        out_shape=[
            jax.ShapeDtypeStruct((N, D), f32),
            jax.ShapeDtypeStruct((2 * T, D), f32),
        ],
        compiler_params=_params("arbitrary"),
        name="conv",
    )(comb_src, x1, meta, o_sorted, g, win, cw, wout, p1, p2)


def _rope_tables():
    inv_freq = THETA ** (-jnp.arange(HALF, dtype=f32) / HALF)
    pos = jnp.concatenate([
        jnp.arange(NP, dtype=jnp.int32),
        PAST + jnp.tile(jnp.arange(TS, dtype=jnp.int32), NB),
    ]).astype(f32)
    ang = inv_freq[:, None] * pos[None, :]
    return jnp.cos(ang), jnp.sin(ang)


def _router_weights(w_group, b_group, w_router, b_router):
    gap = ROW_E - NGRP
    pad = LANES - ROW_E - NE
    w = jnp.concatenate([w_group, jnp.zeros((D, gap), f32), w_router,
                         jnp.zeros((D, pad), f32)], axis=1)
    b = jnp.concatenate([b_group, jnp.zeros((gap,), f32), b_router,
                         jnp.zeros((pad,), f32)])[None, :]
    return w.astype(bf16), b


def _moe_experts(x, i, norm_ffn, w_group, b_group, w_router, b_router, w_gate, w_up, w_down):
    w, b = _router_weights(w_group[i], b_group[i], w_router[i], b_router[i])
    xc, meta, cnt = _dispatch(x, norm_ffn[i][None, :], w, b)
    eid, nused, ffn_src, nxt, comb_src = _dispatch_tables(cnt)
    o_sorted = _ffn(i, eid, nused, ffn_src, nxt, xc, w_gate, w_up, w_down)
    return comb_src, meta, o_sorted


def kernel(x_prompt, x_sample, cache_k, cache_v, state_conv, norm_mix, w_qkv, q_norm, k_norm,
           sinks, w_o, w_in, conv_w, w_out, norm_ffn, w_group, b_group, w_router, b_router,
           w_gate, w_up, w_down):
    xp = x_prompt.reshape(NP, D)
    xs = x_sample.reshape(NS, D)
    moe_w = (norm_ffn, w_group, b_group, w_router, b_router, w_gate, w_up, w_down)

    cos, sin = _rope_tables()
    (qT, ktok, vtok, vT), (q_s, ktok_s, vtok_s) = _qkv(
        xp, xs, norm_mix[0][None, :], w_qkv[0].T.astype(bf16), q_norm[0][:, None],
        k_norm[0][:, None], cos, sin)
    sink_rows = jnp.repeat(sinks[0].reshape(KVH, G), TQ, axis=1)[:, None, :]

    qs = q_s.reshape(KVH, G, HD, NB, TS).transpose(3, 0, 1, 4, 2)
    zq = jnp.zeros_like(qs[:, 0])
    qbd = jnp.stack([jnp.concatenate([qs[:, 0], zq], axis=-1),
                     jnp.concatenate([zq, qs[:, 1]], axis=-1)], axis=1)
    qbd = qbd.reshape(NB, H * TS, KVH * HD)
    k_new = ktok_s.reshape(NB, TS, KVH * HD)
    v_new = vtok_s.reshape(NB, TS, KVH * HD)
    pad4 = jnp.zeros((NB, 8 - TS, KVH * HD), f32)
    kc = cache_k[0].reshape(NB, WIN, KVH * HD)
    vc = cache_v[0].reshape(NB, WIN, KVH * HD)
    sink_col = jnp.repeat(sinks[0], TS)[:, None]
    o_s, kc_new, vc_new = _attn_sample(qbd, kc, vc, jnp.concatenate([k_new, pad4], axis=1),
                                       jnp.concatenate([v_new, pad4], axis=1), sink_col)
    o_s = o_s.reshape(NB, KVH, G, TS, KVH, HD)
    o_s = jnp.stack([o_s[:, 0, :, :, 0], o_s[:, 1, :, :, 1]], axis=1)
    o_s = o_s.transpose(0, 3, 1, 2, 4).reshape(NS, H * HD).astype(bf16)
    x = _attn_prompt(qT, ktok, vT, sink_rows, xp, xs, o_s, w_o[0].astype(bf16))
    comb_src, meta, o_sorted = _moe_experts(x, 0, *moe_w)

    new_k_prompt = ktok[NP - WIN:NP].reshape(1, 1, WIN, KVH, HD)
    new_v_prompt = vtok[NP - WIN:NP].reshape(1, 1, WIN, KVH, HD)
    new_k_sample = kc_new.reshape(1, NB, WIN, KVH, HD)
    new_v_sample = vc_new.reshape(1, NB, WIN, KVH, HD)

    st = state_conv[0]
    z = jnp.zeros((NB, 1, D), f32)
    p1 = jnp.concatenate([st[:, 1:2], z, z, z], axis=1).reshape(NS, D)
    p2 = jnp.concatenate([st[:, 0:1], st[:, 1:2], z, z], axis=1).reshape(NS, D)
    x, cu = _conv(comb_src, x, meta, o_sorted, norm_mix[1][None, :], w_in[0].astype(bf16),
                  conv_w[0], w_out[0].astype(bf16), p1, p2)
    comb_src, meta, o_sorted = _moe_experts(x, 1, *moe_w)
    y_prompt, y_sample = _combine(comb_src, x, meta, o_sorted)

    new_conv_prompt = cu[T - 2:T].reshape(1, 1, 2, D)
    new_conv_sample = cu[T:].reshape(NB, TS, D)[:, TS - 2:][None]

    y_prompt = y_prompt.reshape(1, NP, D)
    y_sample = y_sample.reshape(NB, TS, D)
    return (y_prompt, y_sample, new_k_prompt, new_v_prompt, new_conv_prompt,
            new_k_sample, new_v_sample, new_conv_sample)
```

```python
import functools

import jax
import jax.numpy as jnp
import numpy as np
from jax import lax
from jax.experimental import pallas as pl
from jax.experimental.pallas import tpu as pltpu

D = 1024
NP = 16384
NB = 128
TS = 4
NS = NB * TS
N = NP + NS
PAST = 16384
H = 16
KVH = 2
G = H // KVH
HD = 64
HALF = HD // 2
QKV = (H + 2 * KVH) * HD
WIN = 128
THETA = 10000.0
NGRP = 4
EPG = 4
NE = NGRP * EPG
TOPK = 2
ROW_E = 8
F = 512
EPS = 1e-6
SCALE = HD ** -0.5

T = 512
NT = N // T
TQ = 128
NQ = NP // TQ
QB = 4
BB = 16
LANES = 128
V7X_VMEM_BYTES = 64 * 1024 * 1024
VMEM_LIMIT = V7X_VMEM_BYTES - 14 * 1024 * 1024

TM = 256
NTM = N // TM
DT = 6
UNIT = 16
RC = TOPK * TM + NE * UNIT
UPT = RC // UNIT
XW = D + LANES
CH = 512
CHB = 256
UPC = CH // UNIT
NSLOT = 4
NCH = -(-(NTM * (UPT - 1) + NE * (UPC - 1)) // UPC) + NSLOT - 1
ZERO_UNIT_IN = UPT - 1
ZERO_UNIT_OUT = (NCH - 1) * UPC

f32 = jnp.float32
bf16 = jnp.bfloat16


def _params(*sem):
    return pltpu.CompilerParams(dimension_semantics=sem, vmem_limit_bytes=VMEM_LIMIT)


def _rms(x, g):
    ms = jnp.mean(x * x, axis=-1, keepdims=True)
    return x * lax.rsqrt(ms + EPS) * g


def _qkv_prompt_kernel(x_ref, g_ref, wT_ref, qn_ref, kn_ref, cos_ref, sin_ref,
                       qT_ref, ktok_ref, vtok_ref, vT_ref):
    _qkv_tile(x_ref, g_ref, wT_ref, qn_ref, kn_ref, cos_ref, sin_ref,
              qT_ref, ktok_ref, vtok_ref, vT_ref)


def _qkv_sample_kernel(x_ref, g_ref, wT_ref, qn_ref, kn_ref, cos_ref, sin_ref,
                       q_ref, ktok_ref, vtok_ref):
    _qkv_tile(x_ref, g_ref, wT_ref, qn_ref, kn_ref, cos_ref, sin_ref,
              q_ref, ktok_ref, vtok_ref, None)


def _qkv_tile(x_ref, g_ref, wT_ref, qn_ref, kn_ref, cos_ref, sin_ref,
              q_ref, ktok_ref, vtok_ref, vT_ref):
    h = _rms(x_ref[...], g_ref[...]).astype(bf16)
    qkvT = lax.dot_general(wT_ref[...], h, (((1,), (1,)), ((), ())),
                           preferred_element_type=f32)
    cos = cos_ref[...]
    sin = sin_ref[...]

    def norm_rope(blk, gcol):
        ms = jnp.mean(blk * blk, axis=0, keepdims=True)
        y = blk * lax.rsqrt(ms + EPS) * gcol
        y1 = y[:HALF]
        y2 = y[HALF:]
        return y1 * cos - y2 * sin, y2 * cos + y1 * sin

    qn = qn_ref[...]
    for hd in range(H):
        o1, o2 = norm_rope(qkvT[hd * HD:(hd + 1) * HD], qn)
        q_ref[hd * HD:hd * HD + HALF, :] = (o1 * SCALE).astype(bf16)
        q_ref[hd * HD + HALF:(hd + 1) * HD, :] = (o2 * SCALE).astype(bf16)
    kn = kn_ref[...]
    ks = []
    for j in range(KVH):
        o1, o2 = norm_rope(qkvT[H * HD + j * HD:H * HD + (j + 1) * HD], kn)
        ks += [o1, o2]
    kT = jnp.concatenate(ks, axis=0)
    ktok_ref[...] = kT.T
    vT = qkvT[(H + KVH) * HD:]
    vtok_ref[...] = vT.T
    if vT_ref is not None:
        vT_ref[...] = vT.astype(bf16)


def _qkv(xp, xs, g, wT, qn, kn, cos, sin):
    const = [pl.BlockSpec((HD, 1), lambda i: (0, 0)), pl.BlockSpec((HD, 1), lambda i: (0, 0)),
             pl.BlockSpec((HALF, T), lambda i: (0, i)), pl.BlockSpec((HALF, T), lambda i: (0, i))]
    x_spec = pl.BlockSpec((T, D), lambda i: (i, 0))
    g_spec = pl.BlockSpec((1, D), lambda i: (0, 0))
    w_spec = pl.BlockSpec((QKV, D), lambda i: (0, 0))
    tok_spec = pl.BlockSpec((T, KVH * HD), lambda i: (i, 0))
    nt = NP // T
    prompt = pl.pallas_call(
        _qkv_prompt_kernel,
        grid=(nt,),
        in_specs=[x_spec, g_spec, w_spec] + const,
        out_specs=[pl.BlockSpec((None, H * HD, T), lambda i: (i, 0, 0)), tok_spec, tok_spec,
                   pl.BlockSpec((None, KVH * HD, T), lambda i: (i, 0, 0))],
        out_shape=[jax.ShapeDtypeStruct((nt, H * HD, T), bf16),
                   jax.ShapeDtypeStruct((NP, KVH * HD), f32),
                   jax.ShapeDtypeStruct((NP, KVH * HD), f32),
                   jax.ShapeDtypeStruct((nt, KVH * HD, T), bf16)],
        compiler_params=_params("parallel"),
        name="qkv",
    )(xp, g, wT, qn, kn, cos[:, :NP], sin[:, :NP])
    sample = pl.pallas_call(
        _qkv_sample_kernel,
        grid=(NS // T,),
        in_specs=[x_spec, g_spec, w_spec] + const,
        out_specs=[pl.BlockSpec((H * HD, T), lambda i: (0, i)), tok_spec, tok_spec],
        out_shape=[jax.ShapeDtypeStruct((H * HD, NS), bf16),
                   jax.ShapeDtypeStruct((NS, KVH * HD), f32),
                   jax.ShapeDtypeStruct((NS, KVH * HD), f32)],
        compiler_params=_params("parallel"),
        name="qkv_sample",
    )(xs, g, wT, qn, kn, cos[:, NP:], sin[:, NP:])
    return prompt, sample


def _attn_prompt_kernel(qT_ref, kp_ref, kc_ref, vp_ref, vc_ref, sink_ref, bias0_ref, bias_ref,
                        xp_ref, xs_ref, os_ref, wo_ref, y_ref):
    j = pl.program_id(0)

    @pl.when(j >= NQ // QB)
    def _():
        y_ref[...] = xs_ref[...] + jnp.dot(os_ref[...], wo_ref[...],
                                           preferred_element_type=f32)

    @pl.when(j < NQ // QB)
    def _():
        k_all = jnp.concatenate([kp_ref[...], kc_ref[...]], axis=0)
        v_all = jnp.concatenate([vp_ref[...], vc_ref[...]], axis=1)
        blocks = []
        for k in range(QB):
            rows = slice(k * TQ, (k + 1) * TQ)
            blocks.append(_attend_block(
                qT_ref.at[:, rows], k_all[k * TQ:(k + 2) * TQ], v_all[:, k * TQ:(k + 2) * TQ],
                sink_ref, bias0_ref if k == 0 else bias_ref, xp_ref.at[rows], wo_ref,
                y_ref.at[rows]))
        for _ in range(3):
            for blk in blocks:
                next(blk, None)


def _attend_block(qT_ref, kk, vv, sink_ref, bias_ref, x_ref, wo_ref, y_ref):
    kk = kk.astype(bf16)
    bias = jnp.concatenate([bias_ref[...]] * H, axis=1)
    qg = [jnp.concatenate([qT_ref[(g * G + hh) * HD:(g * G + hh + 1) * HD, :]
                           for hh in range(G)], axis=1) for g in range(KVH)]
    zeros = jnp.zeros_like(qg[0])
    rhs = jnp.concatenate([jnp.concatenate([qg[0], zeros], axis=1),
                           jnp.concatenate([zeros, qg[1]], axis=1)], axis=0)
    sT = jnp.dot(kk, rhs, preferred_element_type=f32) + bias
    yield
    sink = jnp.concatenate([sink_ref[0], sink_ref[1]], axis=1)
    m = jnp.maximum(jnp.max(sT, axis=0, keepdims=True), sink)
    p = jnp.exp(sT - m)
    l = jnp.sum(p, axis=0, keepdims=True) + jnp.exp(sink - m)
    p = (p * (1.0 / l)).astype(bf16)
    yield
    pieces = []
    for g in range(KVH):
        oT = jnp.dot(vv[g * HD:(g + 1) * HD, :], p[:, g * G * TQ:(g + 1) * G * TQ],
                     preferred_element_type=f32)
        pieces += [oT[:, hh * TQ:(hh + 1) * TQ] for hh in range(G)]
    oT_all = jnp.concatenate(pieces, axis=0)
    y_ref[...] = x_ref[...] + jnp.dot(oT_all.T.astype(bf16), wo_ref[...],
                                      preferred_element_type=f32)


def _band_bias():
    s = np.arange(2 * TQ)[:, None]
    t = np.arange(TQ)[None, :]
    dist = t + TQ - s
    band = (dist >= 0) & (dist <= WIN)
    first = band & (s >= TQ)
    return jnp.asarray(np.where(np.stack([first, band]), 0.0, -np.inf), f32)


def _attn_prompt(qT, ktok, vT, sink_rows, xp, xs, o_s, wo):
    steps = NQ // QB
    rows = QB * TQ
    assert rows == T

    def cur(j):
        return jnp.minimum(j, steps - 1)

    def prev(j):
        return jnp.maximum(cur(j) * QB - 1, 0)

    def prev_tile(j):
        return jnp.maximum(cur(j) - 1, 0)

    def sample(j):
        return jnp.maximum(j - steps, 0)

    bias = _band_bias()
    return pl.pallas_call(
        _attn_prompt_kernel,
        grid=(N // rows,),
        in_specs=[
            pl.BlockSpec((None, H * HD, rows), lambda j: (cur(j), 0, 0)),
            pl.BlockSpec((TQ, KVH * HD), lambda j: (prev(j), 0)),
            pl.BlockSpec((rows, KVH * HD), lambda j: (cur(j), 0)),
            pl.BlockSpec((None, KVH * HD, TQ), lambda j: (prev_tile(j), 0, QB - 1)),
            pl.BlockSpec((None, KVH * HD, rows), lambda j: (cur(j), 0, 0)),
            pl.BlockSpec((KVH, 1, G * TQ), lambda j: (0, 0, 0)),
            pl.BlockSpec((None, 2 * TQ, TQ), lambda j: (jnp.minimum(j, 1), 0, 0)),
            pl.BlockSpec((None, 2 * TQ, TQ), lambda j: (1, 0, 0)),
            pl.BlockSpec((rows, D), lambda j: (cur(j), 0)),
            pl.BlockSpec((rows, D), lambda j: (sample(j), 0)),
            pl.BlockSpec((rows, H * HD), lambda j: (sample(j), 0)),
            pl.BlockSpec((D, D), lambda j: (0, 0)),
        ],
        out_specs=pl.BlockSpec((rows, D), lambda j: (j, 0)),
        out_shape=jax.ShapeDtypeStruct((N, D), f32),
        compiler_params=_params("parallel"),
        name="attn_prompt",
    )(qT, ktok, ktok, vT, vT, sink_rows, bias, bias, xp, xs, o_s, wo)


def _attn_sample_kernel(q_ref, kc_ref, vc_ref, kn_ref, vn_ref, sink_ref,
                        o_ref, knew_ref, vnew_ref):
    rows = H * TS
    t1 = lax.broadcasted_iota(jnp.int32, (1, rows, WIN), 1) & (TS - 1)
    s1 = lax.broadcasted_iota(jnp.int32, (1, rows, WIN), 2)
    valid1 = s1 >= t1
    t2 = lax.broadcasted_iota(jnp.int32, (1, rows, 8), 1) & (TS - 1)
    s2 = lax.broadcasted_iota(jnp.int32, (1, rows, 8), 2)
    valid2 = s2 <= t2
    sink = sink_ref[...][None]
    q = q_ref[...]
    kc = kc_ref[...]
    vc = vc_ref[...]
    kn = kn_ref[...]
    vn = vn_ref[...]
    sc = jnp.einsum('bqd,bkd->bqk', q, kc.astype(bf16), preferred_element_type=f32)
    sn = jnp.einsum('bqd,bkd->bqk', q, kn.astype(bf16), preferred_element_type=f32)
    sc = jnp.where(valid1, sc, -jnp.inf)
    sn = jnp.where(valid2, sn, -jnp.inf)
    m = jnp.maximum(jnp.maximum(jnp.max(sc, axis=-1, keepdims=True),
                                jnp.max(sn, axis=-1, keepdims=True)), sink)
    pc = jnp.exp(sc - m)
    pn = jnp.exp(sn - m)
    l = (jnp.sum(pc, axis=-1, keepdims=True) + jnp.sum(pn, axis=-1, keepdims=True)
         + jnp.exp(sink - m))
    r = 1.0 / l
    o_ref[...] = (jnp.einsum('bqk,bkd->bqd', (pc * r).astype(bf16), vc.astype(bf16),
                             preferred_element_type=f32)
                  + jnp.einsum('bqk,bkd->bqd', (pn * r).astype(bf16), vn.astype(bf16),
                               preferred_element_type=f32))
    knew_ref[:, :WIN - TS, :] = kc[:, TS:, :]
    knew_ref[:, WIN - TS:, :] = kn[:, :TS, :]
    vnew_ref[:, :WIN - TS, :] = vc[:, TS:, :]
    vnew_ref[:, WIN - TS:, :] = vn[:, :TS, :]


def _attn_sample(qbd, kc, vc, kn, vn, sink_col):
    rows = H * TS
    cache_spec = pl.BlockSpec((BB, WIN, KVH * HD), lambda i: (i, 0, 0))
    new_spec = pl.BlockSpec((BB, 8, KVH * HD), lambda i: (i, 0, 0))
    cache_shape = jax.ShapeDtypeStruct((NB, WIN, KVH * HD), f32)
    return pl.pallas_call(
        _attn_sample_kernel,
        grid=(NB // BB,),
        in_specs=[
            pl.BlockSpec((BB, rows, KVH * HD), lambda i: (i, 0, 0)),
            cache_spec, cache_spec, new_spec, new_spec,
            pl.BlockSpec((rows, 1), lambda i: (0, 0)),
        ],
        out_specs=[pl.BlockSpec((BB, rows, KVH * HD), lambda i: (i, 0, 0)),
                   cache_spec, cache_spec],
        out_shape=[jax.ShapeDtypeStruct((NB, rows, KVH * HD), f32), cache_shape, cache_shape],
        compiler_params=_params("parallel"),
        name="attn_sample",
    )(qbd, kc, vc, kn, vn, sink_col)


def _dispatch_kernel(x_ref, g_ref, w_ref, b_ref, upper_ref, xc_ref, meta_ref, cnt_ref):
    tiles = [_dispatch_tile(x_ref.at[k * TM:(k + 1) * TM], g_ref, w_ref, b_ref, upper_ref,
                            xc_ref.at[k * RC:(k + 1) * RC], meta_ref.at[k * TM:(k + 1) * TM],
                            cnt_ref.at[k * NE:(k + 1) * NE]) for k in range(DT)]
    for _ in range(3):
        for t in tiles:
            next(t, None)


def _dispatch_tile(x_ref, g_ref, w_ref, b_ref, upper_ref, xc_ref, meta_ref, cnt_ref):
    h_hi = _rms(x_ref[...], g_ref[...]).astype(bf16)
    logits = jnp.dot(h_hi, w_ref[...], preferred_element_type=f32) + b_ref[...]
    yield
    lt = logits.T
    inf = jnp.inf
    row8 = lax.broadcasted_iota(jnp.int32, (8, TM), 0).astype(f32)
    gl = jnp.where(row8 < NGRP, lt[0:8], -inf)
    gmax = jnp.max(gl, axis=0, keepdims=True)
    gsel = jnp.min(jnp.where(gl == gmax, row8, 8.0), axis=0, keepdims=True)
    g_w = 1.0 / jnp.sum(jnp.exp(gl - gmax), axis=0, keepdims=True)
    row = lax.broadcasted_iota(jnp.int32, (NE, TM), 0)
    row_f = row.astype(f32)
    el = jnp.where((row >> 2).astype(f32) == gsel, lt[ROW_E:ROW_E + NE], -inf)
    v1 = jnp.max(el, axis=0, keepdims=True)
    i1 = jnp.min(jnp.where(el == v1, row_f, float(NE)), axis=0, keepdims=True)
    el2 = jnp.where(row_f == i1, -inf, el)
    v2 = jnp.max(el2, axis=0, keepdims=True)
    i2 = jnp.min(jnp.where(el2 == v2, row_f, float(NE)), axis=0, keepdims=True)
    e1 = jnp.exp(v2 - v1)
    den = 1.0 + e1
    w1 = (1.0 / den) * g_w
    w2 = (e1 / den) * g_w

    m1 = row_f == i1
    m2 = row_f == i2
    sel = jnp.where(m1 | m2, 1.0, 0.0)
    ranks = jnp.dot(sel.astype(bf16), upper_ref[...],
                    preferred_element_type=f32)
    counts = jnp.sum(sel, axis=1, keepdims=True)
    padded = jnp.floor((counts + (UNIT - 1.0)) * (1.0 / UNIT)) * UNIT
    e_i = lax.broadcasted_iota(jnp.int32, (NE, NE), 0)
    f_i = lax.broadcasted_iota(jnp.int32, (NE, NE), 1)
    below = jnp.where(f_i < e_i, 1.0, 0.0).astype(bf16)
    seg = jnp.dot(below, jnp.broadcast_to(padded, (NE, LANES)).astype(bf16),
                  preferred_element_type=f32)[:, 0:1]
    posall = seg + ranks
    pos1 = jnp.sum(jnp.where(m1, posall, 0.0), axis=0, keepdims=True)
    pos2 = jnp.sum(jnp.where(m2, posall, 0.0), axis=0, keepdims=True)
    cnt_ref[...] = jnp.broadcast_to(counts, (NE, LANES))

    w1_hi = w1.astype(bf16).astype(f32)
    w2_hi = w2.astype(bf16).astype(f32)
    slab = jnp.zeros((8, TM), f32)
    for k, r in enumerate((pos1, pos2, w1_hi, w1 - w1_hi, w2_hi, w2 - w2_hi, i1)):
        slab = jnp.where(row8 == k, r, slab)
    meta = jnp.concatenate([slab, jnp.zeros((LANES - 8, TM), f32)], axis=0).T
    meta_ref[...] = meta
    yield

    rr = lax.broadcasted_iota(jnp.int32, (RC, TM), 0).astype(f32)
    onehot = jnp.where((rr == pos1) | (rr == pos2), 1.0, 0.0).astype(bf16)
    h_aug = jnp.concatenate([h_hi, meta.astype(bf16)], axis=1)
    xc_ref[...] = jnp.dot(onehot, h_aug, preferred_element_type=f32).astype(bf16)


def _dispatch(x, g, w, b):
    upper = jnp.asarray(np.triu(np.ones((TM, TM), np.float32), 1), bf16)
    return pl.pallas_call(
        _dispatch_kernel,
        grid=(NTM // DT,),
        in_specs=[
            pl.BlockSpec((DT * TM, D), lambda i: (i, 0)),
            pl.BlockSpec((1, D), lambda i: (0, 0)),
            pl.BlockSpec((D, LANES), lambda i: (0, 0)),
            pl.BlockSpec((1, LANES), lambda i: (0, 0)),
            pl.BlockSpec((TM, TM), lambda i: (0, 0)),
        ],
        out_specs=[
            pl.BlockSpec((DT * RC, XW), lambda i: (i, 0)),
            pl.BlockSpec((DT * TM, LANES), lambda i: (i, 0)),
            pl.BlockSpec((DT * NE, LANES), lambda i: (i, 0)),
        ],
        out_shape=[
            jax.ShapeDtypeStruct((NTM * RC, XW), bf16),
            jax.ShapeDtypeStruct((N, LANES), f32),
            jax.ShapeDtypeStruct((NTM * NE, LANES), f32),
        ],
        compiler_params=_params("parallel"),
        name="dispatch",
    )(x, g, w, b, upper)


def _dispatch_tables(cnt):
    i32 = jnp.int32
    n = cnt.reshape(NTM, NE, LANES)[:, :, 0].astype(i32)
    units = (n + UNIT - 1) // UNIT
    seg_end = jnp.cumsum(units, axis=1)
    seg_start = seg_end - units
    col_end = jnp.cumsum(units, axis=0)
    col_start = col_end - units
    chunks = (col_end[-1] + UPC - 1) // UPC
    ch_end = jnp.cumsum(chunks)
    ch_start = ch_end - chunks
    nused = ch_end[-1]
    c = jnp.arange(NCH, dtype=i32)
    eid = jnp.minimum(jnp.sum((ch_end[None, :] <= c[:, None]).astype(i32), axis=1), NE - 1)

    src0 = jnp.arange(NTM, dtype=i32)[:, None] * UPT + seg_start
    dst0 = ch_start[None, :] * UPC + col_start
    k = jnp.arange((NCH + NSLOT - 1) * UPC, dtype=i32)[:, None, None]
    inside = (k >= dst0[None]) & (k < (dst0 + units)[None])
    found = jnp.sum(inside.astype(i32), axis=(1, 2))
    shift = jnp.sum(jnp.where(inside, (src0 - dst0)[None], 0), axis=(1, 2))
    ffn_src = jnp.where(found > 0, k[:, 0, 0] + shift, ZERO_UNIT_IN)

    v = jnp.arange(UPT, dtype=i32)[None, :, None]
    inside_v = (v >= seg_start[:, None, :]) & (v < seg_end[:, None, :])
    found_v = jnp.sum(inside_v.astype(i32), axis=2)
    shift_v = jnp.sum(jnp.where(inside_v, (dst0 - seg_start)[:, None, :], 0), axis=2)
    comb_src = jnp.where(found_v > 0, v[:, :, 0] + shift_v, ZERO_UNIT_OUT).reshape(-1)
    own_end = jnp.sum(jnp.where(eid[:, None] == jnp.arange(NE, dtype=i32)[None, :],
                                ch_end[None, :], 0), axis=1)
    eid_at = jnp.sum(jnp.where(c[None, :] == own_end[:, None], eid[None, :], 0), axis=1)
    nxt = jnp.where(own_end < nused, eid_at, -1)
    return eid, nused.reshape(1), ffn_src, nxt, comb_src


def _unit_copies(src_ref, base, n_units, src_hbm, stage, slot, sem):
    out = []
    for j in range(n_units):
        row = pl.multiple_of(src_ref[base + j] * UNIT, UNIT)
        out.append(pltpu.make_async_copy(
            src_hbm.at[pl.ds(row, UNIT), :],
            stage.at[slot, pl.ds(j * UNIT, UNIT), :],
            sem.at[slot]))
    return out


def _start_burst(copies):
    for j, cp in enumerate(copies):
        cp.start(priority=j % 2)


def _start_gathers(step, n_steps, n_units, src_ref, src_hbm, stage, sem):
    ahead = NSLOT - 1

    def start(s, slot):
        _start_burst(_unit_copies(src_ref, s * n_units, n_units, src_hbm, stage, slot, sem))

    for s in range(ahead):
        @pl.when((step == 0) & (s < n_steps))
        def _():
            start(s, s)

    @pl.when(step + ahead < n_steps)
    def _():
        start(step + ahead, lax.rem(step + ahead, NSLOT))


def _wait_gather(step, n_units, src_ref, src_hbm, stage, sem):
    slot = lax.rem(step, NSLOT)
    for cp in _unit_copies(src_ref, step * n_units, n_units, src_hbm, stage, slot, sem):
        cp.wait()
    return slot


def _ffn_kernel(layer, eid_ref, nused_ref, src_ref, nxt_ref, xc_hbm, wg_hbm, wu_hbm, wd_hbm,
                o_hbm, stage, sem, wg_f, wu_f, wd_f, w_sem, wg_b, wu_b, wd_b, obuf, o_sem):
    nused = nused_ref[0]
    ahead = NSLOT - 1

    def start(s):
        _start_burst(_unit_copies(src_ref, s * UPC, UPC, xc_hbm, stage, lax.rem(s, NSLOT), sem))

    def weight_copies(e, slot):
        return [pltpu.make_async_copy(hbm.at[layer, e], buf.at[slot], w_sem.at[slot])
                for hbm, buf in ((wg_hbm, wg_f), (wu_hbm, wu_f), (wd_hbm, wd_f))]

    def out_copy(c, slot):
        row = pl.multiple_of(c * CH, CH)
        return pltpu.make_async_copy(obuf.at[slot], o_hbm.at[pl.ds(row, CH), :], o_sem.at[slot])

    for s in range(ahead):
        start(s)

    @pl.when(nused > 0)
    def _():
        for cp in weight_copies(eid_ref[0], 0):
            cp.start()

    def chunk(c, n_experts):
        e = eid_ref[c]
        first = (c == 0) | (e != eid_ref[jnp.maximum(c - 1, 0)])
        wslot = lax.rem(n_experts, 2)

        @pl.when(first)
        def _():
            for cp in weight_copies(e, wslot):
                cp.wait()
            wg_b[...] = wg_f[wslot].astype(bf16)
            wu_b[...] = wu_f[wslot].astype(bf16)
            wd_b[...] = wd_f[wslot].astype(bf16)

            @pl.when(nxt_ref[c] >= 0)
            def _():
                for cp in weight_copies(nxt_ref[c], 1 - wslot):
                    cp.start()

        slot = _wait_gather(c, UPC, src_ref, xc_hbm, stage, sem)
        oslot = lax.rem(c, 2)

        @pl.when(c >= 2)
        def _():
            out_copy(c - 2, oslot).wait()

        def block(r):
            xs = stage[slot, r:r + CHB, :]
            x = xs[:, :D]
            a = jnp.dot(x, wg_b[...], preferred_element_type=f32)
            u = jnp.dot(x, wu_b[...], preferred_element_type=f32)
            yield
            gb = xs[:, D:].astype(f32)
            is_first = gb[:, 6:7] == e.astype(f32)
            gate = jnp.where(is_first, gb[:, 2:3] + gb[:, 3:4], gb[:, 4:5] + gb[:, 5:6])
            act = (a * (1.0 / (1.0 + jnp.exp(-a)))) * u * gate
            obuf[oslot, r:r + CHB, :] = jnp.dot(act.astype(bf16), wd_b[...],
                                                preferred_element_type=f32).astype(bf16)

        blocks = [block(r) for r in range(0, CH, CHB)]
        for _ in range(2):
            for blk in blocks:
                next(blk, None)
        out_copy(c, oslot).start()
        start(c + ahead)
        return n_experts + first.astype(jnp.int32)

    lax.fori_loop(0, nused, chunk, jnp.int32(0))

    for s in range(ahead):
        _wait_gather(nused + s, UPC, src_ref, xc_hbm, stage, sem)
    for back in (1, 2):
        @pl.when(nused >= back)
        def _():
            out_copy(nused - back, lax.rem(nused - back, 2)).wait()

    obuf[0] = jnp.zeros((CH, D), bf16)

    def zero_start(c, carry):
        out_copy(c, 0).start()
        return carry

    def zero_wait(c, carry):
        out_copy(c, 0).wait()
        return carry

    lax.fori_loop(nused, NCH, zero_start, 0)
    lax.fori_loop(nused, NCH, zero_wait, 0)


def _ffn(layer, eid, nused, ffn_src, nxt, xc, wg, wu, wd):
    any_spec = pl.BlockSpec(memory_space=pl.ANY)
    return pl.pallas_call(
        functools.partial(_ffn_kernel, layer),
        grid_spec=pltpu.PrefetchScalarGridSpec(
            num_scalar_prefetch=4,
            grid=(1,),
            in_specs=[any_spec, any_spec, any_spec, any_spec],
            out_specs=any_spec,
            scratch_shapes=[
                pltpu.VMEM((NSLOT, CH, XW), bf16),
                pltpu.SemaphoreType.DMA((NSLOT,)),
                pltpu.VMEM((2, D, F), f32),
                pltpu.VMEM((2, D, F), f32),
                pltpu.VMEM((2, F, D), f32),
                pltpu.SemaphoreType.DMA((2,)),
                pltpu.VMEM((D, F), bf16),
                pltpu.VMEM((D, F), bf16),
                pltpu.VMEM((F, D), bf16),
                pltpu.VMEM((2, CH, D), bf16),
                pltpu.SemaphoreType.DMA((2,)),
            ],
        ),
        out_shape=jax.ShapeDtypeStruct((NCH * CH, D), bf16),
        compiler_params=_params("arbitrary"),
        name="ffn",
    )(eid, nused, ffn_src, nxt, xc, wg, wu, wd)


def _combine_tile(x, meta, rows):
    li = lax.broadcasted_iota(jnp.int32, (TM, RC), 1).astype(f32)
    pt = jnp.where((li == meta[:, 0:1]) | (li == meta[:, 1:2]), 1.0, 0.0).astype(bf16)
    return x + jnp.dot(pt, rows, preferred_element_type=f32)


def _combine_kernel(src_ref, x_ref, meta_ref, o_hbm, yp_ref, ys_ref, stage, sem):
    i = pl.program_id(0)
    tiles = T // TM
    _start_gathers(i, NT, tiles * UPT, src_ref, o_hbm, stage, sem)
    slot = _wait_gather(i, tiles * UPT, src_ref, o_hbm, stage, sem)
    ys = [_combine_tile(x_ref[k * TM:(k + 1) * TM, :], meta_ref[k * TM:(k + 1) * TM, :],
                        stage[slot, k * RC:(k + 1) * RC, :]) for k in range(tiles)]
    y = jnp.concatenate(ys, axis=0)

    @pl.when(i < NT - 1)
    def _():
        yp_ref[...] = y

    @pl.when(i == NT - 1)
    def _():
        ys_ref[...] = y


def _combine(comb_src, x, meta, o_sorted):
    out_specs = [pl.BlockSpec((T, D), lambda i, src: (jnp.minimum(i, NT - 2), 0)),
                 pl.BlockSpec((T, D), lambda i, src: (0, 0))]
    out_shape = [jax.ShapeDtypeStruct((NP, D), f32), jax.ShapeDtypeStruct((NS, D), f32)]
    return pl.pallas_call(
        _combine_kernel,
        grid_spec=pltpu.PrefetchScalarGridSpec(
            num_scalar_prefetch=1,
            grid=(NT,),
            in_specs=[
                pl.BlockSpec((T, D), lambda i, src: (i, 0)),
                pl.BlockSpec((T, LANES), lambda i, src: (i, 0)),
                pl.BlockSpec(memory_space=pl.ANY),
            ],
            out_specs=out_specs,
            scratch_shapes=[
                pltpu.VMEM((NSLOT, (T // TM) * RC, D), bf16),
                pltpu.SemaphoreType.DMA((NSLOT,)),
            ],
        ),
        out_shape=out_shape,
        compiler_params=_params("arbitrary"),
        name="combine",
    )(comb_src, x, meta, o_sorted)


def _conv_kernel(src_ref, x1_ref, meta_ref, o_hbm, g_ref, win_ref, cw_ref, wout_ref,
                 p1_ref, p2_ref, y_ref, cu_ref, pad_ref, x_ref, stage, sem):
    i = pl.program_id(0)

    @pl.when(i == 0)
    def _():
        pad_ref[0:8, :] = jnp.zeros((8, D), f32)

    tiles = T // TM
    ahead = NSLOT - 1

    def start(s):
        _start_burst(_unit_copies(src_ref, s * tiles * UPT, tiles * UPT, o_hbm, stage,
                                  lax.rem(s, NSLOT), sem))

    @pl.when(i == 0)
    def _():
        for s in range(ahead):
            start(s)

    slot = _wait_gather(i, tiles * UPT, src_ref, o_hbm, stage, sem)

    for k in range(tiles):
        rows = slice(k * TM, (k + 1) * TM)
        x_ref[rows, :] = _combine_tile(x1_ref[rows, :], meta_ref[rows, :],
                                       stage[slot, k * RC:(k + 1) * RC, :])

    blocks = [(r, r + T // 2) for r in (0, T // 2)]
    gates = []
    for r0, r1 in blocks:
        h = _rms(x_ref[r0:r1, :], g_ref[...]).astype(bf16)
        bcu = jnp.dot(h, win_ref[...], preferred_element_type=f32)
        gates.append(bcu[:, :D])
        cu = bcu[:, D:2 * D] * bcu[:, 2 * D:]
        pad_ref[8 + r0:8 + r1, :] = cu
        cu_ref[r0:r1, :] = cu
    t = lax.broadcasted_iota(jnp.int32, (T // 2, 1), 0) & (TS - 1)
    is_sample = i == NT - 1
    cw = cw_ref[...]
    for (r0, r1), b in zip(blocks, gates):
        m1 = jnp.where(is_sample & (t == 0), p1_ref[r0:r1, :], pad_ref[7 + r0:7 + r1, :])
        m2 = jnp.where(is_sample & (t < 2), p2_ref[r0:r1, :], pad_ref[6 + r0:6 + r1, :])
        conv = cw[0:1] * m2 + cw[1:2] * m1 + cw[2:3] * pad_ref[8 + r0:8 + r1, :]
        y = jnp.dot((b * conv).astype(bf16), wout_ref[...], preferred_element_type=f32)
        y_ref[r0:r1, :] = x_ref[r0:r1, :] + y
    pad_ref[0:8, :] = pad_ref[T:T + 8, :]
    start(i + ahead)

    @pl.when(i == NT - 1)
    def _():
        for s in range(ahead):
            _wait_gather(i + 1 + s, tiles * UPT, src_ref, o_hbm, stage, sem)


def _conv(comb_src, x1, meta, o_sorted, g, win, cw, wout, p1, p2):
    const = dict(pipeline_mode=pl.Buffered(1))
    comb_src = jnp.concatenate([
        comb_src, jnp.full(((NSLOT - 1) * (T // TM) * UPT,), ZERO_UNIT_OUT, jnp.int32)])
    return pl.pallas_call(
        _conv_kernel,
        grid_spec=pltpu.PrefetchScalarGridSpec(
            num_scalar_prefetch=1,
            grid=(NT,),
            in_specs=[
                pl.BlockSpec((T, D), lambda i, src: (i, 0)),
                pl.BlockSpec((T, LANES), lambda i, src: (i, 0)),
                pl.BlockSpec(memory_space=pl.ANY),
                pl.BlockSpec((1, D), lambda i, src: (0, 0)),
                pl.BlockSpec((D, 3 * D), lambda i, src: (0, 0), **const),
                pl.BlockSpec((3, D), lambda i, src: (0, 0)),
                pl.BlockSpec((D, D), lambda i, src: (0, 0), **const),
                pl.BlockSpec((T, D), lambda i, src: (0, 0), **const),
                pl.BlockSpec((T, D), lambda i, src: (0, 0), **const),
            ],
            out_specs=[
                pl.BlockSpec((T, D), lambda i, src: (i, 0)),
                pl.BlockSpec((T, D), lambda i, src: (jnp.where(i == NT - 1, 1, 0), 0)),
            ],
            scratch_shapes=[
                pltpu.VMEM((T + 8, D), f32),
                pltpu.VMEM((T, D), f32),
                pltpu.VMEM((NSLOT, (T // TM) * RC, D), bf16),
                pltpu.SemaphoreType.DMA((NSLOT,)),
            ],
        ),
        out_shape=[
            jax.ShapeDtypeStruct((N, D), f32),
            jax.ShapeDtypeStruct((2 * T, D), f32),
        ],
        compiler_params=_params("arbitrary"),
        name="conv",
    )(comb_src, x1, meta, o_sorted, g, win, cw, wout, p1, p2)


def _rope_tables():
    inv_freq = THETA ** (-jnp.arange(HALF, dtype=f32) / HALF)
    pos = jnp.concatenate([
        jnp.arange(NP, dtype=jnp.int32),
        PAST + jnp.tile(jnp.arange(TS, dtype=jnp.int32), NB),
    ]).astype(f32)
    ang = inv_freq[:, None] * pos[None, :]
    return jnp.cos(ang), jnp.sin(ang)


def _router_weights(w_group, b_group, w_router, b_router):
    gap = ROW_E - NGRP
    pad = LANES - ROW_E - NE
    w = jnp.concatenate([w_group, jnp.zeros((D, gap), f32), w_router,
                         jnp.zeros((D, pad), f32)], axis=1)
    b = jnp.concatenate([b_group, jnp.zeros((gap,), f32), b_router,
                         jnp.zeros((pad,), f32)])[None, :]
    return w.astype(bf16), b


def _moe_experts(x, i, norm_ffn, w_group, b_group, w_router, b_router, w_gate, w_up, w_down):
    w, b = _router_weights(w_group[i], b_group[i], w_router[i], b_router[i])
    xc, meta, cnt = _dispatch(x, norm_ffn[i][None, :], w, b)
    eid, nused, ffn_src, nxt, comb_src = _dispatch_tables(cnt)
    o_sorted = _ffn(i, eid, nused, ffn_src, nxt, xc, w_gate, w_up, w_down)
    return comb_src, meta, o_sorted


def kernel(x_prompt, x_sample, cache_k, cache_v, state_conv, norm_mix, w_qkv, q_norm, k_norm,
           sinks, w_o, w_in, conv_w, w_out, norm_ffn, w_group, b_group, w_router, b_router,
           w_gate, w_up, w_down):
    xp = x_prompt.reshape(NP, D)
    xs = x_sample.reshape(NS, D)
    moe_w = (norm_ffn, w_group, b_group, w_router, b_router, w_gate, w_up, w_down)

    cos, sin = _rope_tables()
    (qT, ktok, vtok, vT), (q_s, ktok_s, vtok_s) = _qkv(
        xp, xs, norm_mix[0][None, :], w_qkv[0].T.astype(bf16), q_norm[0][:, None],
        k_norm[0][:, None], cos, sin)
    sink_rows = jnp.repeat(sinks[0].reshape(KVH, G), TQ, axis=1)[:, None, :]

    qs = q_s.reshape(KVH, G, HD, NB, TS).transpose(3, 0, 1, 4, 2)
    zq = jnp.zeros_like(qs[:, 0])
    qbd = jnp.stack([jnp.concatenate([qs[:, 0], zq], axis=-1),
                     jnp.concatenate([zq, qs[:, 1]], axis=-1)], axis=1)
    qbd = qbd.reshape(NB, H * TS, KVH * HD)
    k_new = ktok_s.reshape(NB, TS, KVH * HD)
    v_new = vtok_s.reshape(NB, TS, KVH * HD)
    pad4 = jnp.zeros((NB, 8 - TS, KVH * HD), f32)
    kc = cache_k[0].reshape(NB, WIN, KVH * HD)
    vc = cache_v[0].reshape(NB, WIN, KVH * HD)
    sink_col = jnp.repeat(sinks[0], TS)[:, None]
    o_s, kc_new, vc_new = _attn_sample(qbd, kc, vc, jnp.concatenate([k_new, pad4], axis=1),
                                       jnp.concatenate([v_new, pad4], axis=1), sink_col)
    o_s = o_s.reshape(NB, KVH, G, TS, KVH, HD)
    o_s = jnp.stack([o_s[:, 0, :, :, 0], o_s[:, 1, :, :, 1]], axis=1)
    o_s = o_s.transpose(0, 3, 1, 2, 4).reshape(NS, H * HD).astype(bf16)
    x = _attn_prompt(qT, ktok, vT, sink_rows, xp, xs, o_s, w_o[0].astype(bf16))
    comb_src, meta, o_sorted = _moe_experts(x, 0, *moe_w)

    new_k_prompt = ktok[NP - WIN:NP].reshape(1, 1, WIN, KVH, HD)
    new_v_prompt = vtok[NP - WIN:NP].reshape(1, 1, WIN, KVH, HD)
    new_k_sample = kc_new.reshape(1, NB, WIN, KVH, HD)
    new_v_sample = vc_new.reshape(1, NB, WIN, KVH, HD)

    st = state_conv[0]
    z = jnp.zeros((NB, 1, D), f32)
    p1 = jnp.concatenate([st[:, 1:2], z, z, z], axis=1).reshape(NS, D)
    p2 = jnp.concatenate([st[:, 0:1], st[:, 1:2], z, z], axis=1).reshape(NS, D)
    x, cu = _conv(comb_src, x, meta, o_sorted, norm_mix[1][None, :], w_in[0].astype(bf16),
                  conv_w[0], w_out[0].astype(bf16), p1, p2)
    comb_src, meta, o_sorted = _moe_experts(x, 1, *moe_w)
    y_prompt, y_sample = _combine(comb_src, x, meta, o_sorted)

    new_conv_prompt = cu[T - 2:T].reshape(1, 1, 2, D)
    new_conv_sample = cu[T:].reshape(NB, TS, D)[:, TS - 2:][None]

    y_prompt = y_prompt.reshape(1, NP, D)
    y_sample = y_sample.reshape(NB, TS, D)
    return (y_prompt, y_sample, new_k_prompt, new_v_prompt, new_conv_prompt,
            new_k_sample, new_v_sample, new_conv_sample)
```

```python
import functools

import jax
import jax.numpy as jnp
import numpy as np
from jax import lax
from jax.experimental import pallas as pl
from jax.experimental.pallas import tpu as pltpu

D = 1024
NP = 16384
NB = 128
TS = 4
NS = NB * TS
N = NP + NS
PAST = 16384
H = 16
KVH = 2
G = H // KVH
HD = 64
HALF = HD // 2
QKV = (H + 2 * KVH) * HD
WIN = 128
THETA = 10000.0
NGRP = 4
EPG = 4
NE = NGRP * EPG
TOPK = 2
ROW_E = 8
F = 512
EPS = 1e-6
SCALE = HD ** -0.5

T = 512
NT = N // T
TQ = 128
NQ = NP // TQ
QB = 4
BB = 16
LANES = 128
V7X_VMEM_BYTES = 64 * 1024 * 1024
VMEM_LIMIT = V7X_VMEM_BYTES - 14 * 1024 * 1024

TM = 256
NTM = N // TM
DT = 6
UNIT = 16
RC = TOPK * TM + NE * UNIT
UPT = RC // UNIT
XW = D + LANES
CH = 512
CHB = 256
UPC = CH // UNIT
NSLOT = 4
NCH = -(-(NTM * (UPT - 1) + NE * (UPC - 1)) // UPC) + NSLOT - 1
ZERO_UNIT_IN = UPT - 1
ZERO_UNIT_OUT = (NCH - 1) * UPC

f32 = jnp.float32
bf16 = jnp.bfloat16


def _params(*sem):
    return pltpu.CompilerParams(dimension_semantics=sem, vmem_limit_bytes=VMEM_LIMIT)


def _rms(x, g):
    ms = jnp.mean(x * x, axis=-1, keepdims=True)
    return x * lax.rsqrt(ms + EPS) * g


def _qkv_prompt_kernel(x_ref, g_ref, wT_ref, qn_ref, kn_ref, cos_ref, sin_ref,
                       qT_ref, ktok_ref, vtok_ref, vT_ref):
    _qkv_tile(x_ref, g_ref, wT_ref, qn_ref, kn_ref, cos_ref, sin_ref,
              qT_ref, ktok_ref, vtok_ref, vT_ref)


def _qkv_sample_kernel(x_ref, g_ref, wT_ref, qn_ref, kn_ref, cos_ref, sin_ref,
                       q_ref, ktok_ref, vtok_ref):
    _qkv_tile(x_ref, g_ref, wT_ref, qn_ref, kn_ref, cos_ref, sin_ref,
              q_ref, ktok_ref, vtok_ref, None)


def _qkv_tile(x_ref, g_ref, wT_ref, qn_ref, kn_ref, cos_ref, sin_ref,
              q_ref, ktok_ref, vtok_ref, vT_ref):
    h = _rms(x_ref[...], g_ref[...]).astype(bf16)
    qkvT = lax.dot_general(wT_ref[...], h, (((1,), (1,)), ((), ())),
                           preferred_element_type=f32)
    cos = cos_ref[...]
    sin = sin_ref[...]

    def norm_rope(blk, gcol):
        ms = jnp.mean(blk * blk, axis=0, keepdims=True)
        y = blk * lax.rsqrt(ms + EPS) * gcol
        y1 = y[:HALF]
        y2 = y[HALF:]
        return y1 * cos - y2 * sin, y2 * cos + y1 * sin

    qn = qn_ref[...]
    for hd in range(H):
        o1, o2 = norm_rope(qkvT[hd * HD:(hd + 1) * HD], qn)
        q_ref[hd * HD:hd * HD + HALF, :] = (o1 * SCALE).astype(bf16)
        q_ref[hd * HD + HALF:(hd + 1) * HD, :] = (o2 * SCALE).astype(bf16)
    kn = kn_ref[...]
    ks = []
    for j in range(KVH):
        o1, o2 = norm_rope(qkvT[H * HD + j * HD:H * HD + (j + 1) * HD], kn)
        ks += [o1, o2]
    kT = jnp.concatenate(ks, axis=0)
    ktok_ref[...] = kT.T
    vT = qkvT[(H + KVH) * HD:]
    vtok_ref[...] = vT.T
    if vT_ref is not None:
        vT_ref[...] = vT.astype(bf16)


def _qkv(xp, xs, g, wT, qn, kn, cos, sin):
    const = [pl.BlockSpec((HD, 1), lambda i: (0, 0)), pl.BlockSpec((HD, 1), lambda i: (0, 0)),
             pl.BlockSpec((HALF, T), lambda i: (0, i)), pl.BlockSpec((HALF, T), lambda i: (0, i))]
    x_spec = pl.BlockSpec((T, D), lambda i: (i, 0))
    g_spec = pl.BlockSpec((1, D), lambda i: (0, 0))
    w_spec = pl.BlockSpec((QKV, D), lambda i: (0, 0))
    tok_spec = pl.BlockSpec((T, KVH * HD), lambda i: (i, 0))
    nt = NP // T
    prompt = pl.pallas_call(
        _qkv_prompt_kernel,
        grid=(nt,),
        in_specs=[x_spec, g_spec, w_spec] + const,
        out_specs=[pl.BlockSpec((None, H * HD, T), lambda i: (i, 0, 0)), tok_spec, tok_spec,
                   pl.BlockSpec((None, KVH * HD, T), lambda i: (i, 0, 0))],
        out_shape=[jax.ShapeDtypeStruct((nt, H * HD, T), bf16),
                   jax.ShapeDtypeStruct((NP, KVH * HD), f32),
                   jax.ShapeDtypeStruct((NP, KVH * HD), f32),
                   jax.ShapeDtypeStruct((nt, KVH * HD, T), bf16)],
        compiler_params=_params("parallel"),
        name="qkv",
    )(xp, g, wT, qn, kn, cos[:, :NP], sin[:, :NP])
    sample = pl.pallas_call(
        _qkv_sample_kernel,
        grid=(NS // T,),
        in_specs=[x_spec, g_spec, w_spec] + const,
        out_specs=[pl.BlockSpec((H * HD, T), lambda i: (0, i)), tok_spec, tok_spec],
        out_shape=[jax.ShapeDtypeStruct((H * HD, NS), bf16),
                   jax.ShapeDtypeStruct((NS, KVH * HD), f32),
                   jax.ShapeDtypeStruct((NS, KVH * HD), f32)],
        compiler_params=_params("parallel"),
        name="qkv_sample",
    )(xs, g, wT, qn, kn, cos[:, NP:], sin[:, NP:])
    return prompt, sample


def _attn_prompt_kernel(qT_ref, kp_ref, kc_ref, vp_ref, vc_ref, sink_ref, bias0_ref, bias_ref,
                        xp_ref, xs_ref, os_ref, wo_ref, y_ref):
    j = pl.program_id(0)

    @pl.when(j >= NQ // QB)
    def _():
        y_ref[...] = xs_ref[...] + jnp.dot(os_ref[...], wo_ref[...],
                                           preferred_element_type=f32)

    @pl.when(j < NQ // QB)
    def _():
        k_all = jnp.concatenate([kp_ref[...], kc_ref[...]], axis=0)
        v_all = jnp.concatenate([vp_ref[...], vc_ref[...]], axis=1)
        blocks = []
        for k in range(QB):
            rows = slice(k * TQ, (k + 1) * TQ)
            blocks.append(_attend_block(
                qT_ref.at[:, rows], k_all[k * TQ:(k + 2) * TQ], v_all[:, k * TQ:(k + 2) * TQ],
                sink_ref, bias0_ref if k == 0 else bias_ref, xp_ref.at[rows], wo_ref,
                y_ref.at[rows]))
        for _ in range(3):
            for blk in blocks:
                next(blk, None)


def _attend_block(qT_ref, kk, vv, sink_ref, bias_ref, x_ref, wo_ref, y_ref):
    kk = kk.astype(bf16)
    bias = jnp.concatenate([bias_ref[...]] * H, axis=1)
    qg = [jnp.concatenate([qT_ref[(g * G + hh) * HD:(g * G + hh + 1) * HD, :]
                           for hh in range(G)], axis=1) for g in range(KVH)]
    zeros = jnp.zeros_like(qg[0])
    rhs = jnp.concatenate([jnp.concatenate([qg[0], zeros], axis=1),
                           jnp.concatenate([zeros, qg[1]], axis=1)], axis=0)
    sT = jnp.dot(kk, rhs, preferred_element_type=f32) + bias
    yield
    sink = jnp.concatenate([sink_ref[0], sink_ref[1]], axis=1)
    m = jnp.maximum(jnp.max(sT, axis=0, keepdims=True), sink)
    p = jnp.exp(sT - m)
    l = jnp.sum(p, axis=0, keepdims=True) + jnp.exp(sink - m)
    p = (p * (1.0 / l)).astype(bf16)
    yield
    pieces = []
    for g in range(KVH):
        oT = jnp.dot(vv[g * HD:(g + 1) * HD, :], p[:, g * G * TQ:(g + 1) * G * TQ],
                     preferred_element_type=f32)
        pieces += [oT[:, hh * TQ:(hh + 1) * TQ] for hh in range(G)]
    oT_all = jnp.concatenate(pieces, axis=0)
    y_ref[...] = x_ref[...] + jnp.dot(oT_all.T.astype(bf16), wo_ref[...],
                                      preferred_element_type=f32)


def _band_bias():
    s = np.arange(2 * TQ)[:, None]
    t = np.arange(TQ)[None, :]
    dist = t + TQ - s
    band = (dist >= 0) & (dist <= WIN)
    first = band & (s >= TQ)
    return jnp.asarray(np.where(np.stack([first, band]), 0.0, -np.inf), f32)


def _attn_prompt(qT, ktok, vT, sink_rows, xp, xs, o_s, wo):
    steps = NQ // QB
    rows = QB * TQ
    assert rows == T

    def cur(j):
        return jnp.minimum(j, steps - 1)

    def prev(j):
        return jnp.maximum(cur(j) * QB - 1, 0)

    def prev_tile(j):
        return jnp.maximum(cur(j) - 1, 0)

    def sample(j):
        return jnp.maximum(j - steps, 0)

    bias = _band_bias()
    return pl.pallas_call(
        _attn_prompt_kernel,
        grid=(N // rows,),
        in_specs=[
            pl.BlockSpec((None, H * HD, rows), lambda j: (cur(j), 0, 0)),
            pl.BlockSpec((TQ, KVH * HD), lambda j: (prev(j), 0)),
            pl.BlockSpec((rows, KVH * HD), lambda j: (cur(j), 0)),
            pl.BlockSpec((None, KVH * HD, TQ), lambda j: (prev_tile(j), 0, QB - 1)),
            pl.BlockSpec((None, KVH * HD, rows), lambda j: (cur(j), 0, 0)),
            pl.BlockSpec((KVH, 1, G * TQ), lambda j: (0, 0, 0)),
            pl.BlockSpec((None, 2 * TQ, TQ), lambda j: (jnp.minimum(j, 1), 0, 0)),
            pl.BlockSpec((None, 2 * TQ, TQ), lambda j: (1, 0, 0)),
            pl.BlockSpec((rows, D), lambda j: (cur(j), 0)),
            pl.BlockSpec((rows, D), lambda j: (sample(j), 0)),
            pl.BlockSpec((rows, H * HD), lambda j: (sample(j), 0)),
            pl.BlockSpec((D, D), lambda j: (0, 0)),
        ],
        out_specs=pl.BlockSpec((rows, D), lambda j: (j, 0)),
        out_shape=jax.ShapeDtypeStruct((N, D), f32),
        compiler_params=_params("parallel"),
        name="attn_prompt",
    )(qT, ktok, ktok, vT, vT, sink_rows, bias, bias, xp, xs, o_s, wo)


def _attn_sample_kernel(q_ref, kc_ref, vc_ref, kn_ref, vn_ref, sink_ref,
                        o_ref, knew_ref, vnew_ref):
    rows = H * TS
    t1 = lax.broadcasted_iota(jnp.int32, (1, rows, WIN), 1) & (TS - 1)
    s1 = lax.broadcasted_iota(jnp.int32, (1, rows, WIN), 2)
    valid1 = s1 >= t1
    t2 = lax.broadcasted_iota(jnp.int32, (1, rows, 8), 1) & (TS - 1)
    s2 = lax.broadcasted_iota(jnp.int32, (1, rows, 8), 2)
    valid2 = s2 <= t2
    sink = sink_ref[...][None]
    q = q_ref[...]
    kc = kc_ref[...]
    vc = vc_ref[...]
    kn = kn_ref[...]
    vn = vn_ref[...]
    sc = jnp.einsum('bqd,bkd->bqk', q, kc.astype(bf16), preferred_element_type=f32)
    sn = jnp.einsum('bqd,bkd->bqk', q, kn.astype(bf16), preferred_element_type=f32)
    sc = jnp.where(valid1, sc, -jnp.inf)
    sn = jnp.where(valid2, sn, -jnp.inf)
    m = jnp.maximum(jnp.maximum(jnp.max(sc, axis=-1, keepdims=True),
                                jnp.max(sn, axis=-1, keepdims=True)), sink)
    pc = jnp.exp(sc - m)
    pn = jnp.exp(sn - m)
    l = (jnp.sum(pc, axis=-1, keepdims=True) + jnp.sum(pn, axis=-1, keepdims=True)
         + jnp.exp(sink - m))
    r = 1.0 / l
    o_ref[...] = (jnp.einsum('bqk,bkd->bqd', (pc * r).astype(bf16), vc.astype(bf16),
                             preferred_element_type=f32)
                  + jnp.einsum('bqk,bkd->bqd', (pn * r).astype(bf16), vn.astype(bf16),
                               preferred_element_type=f32))
    knew_ref[:, :WIN - TS, :] = kc[:, TS:, :]
    knew_ref[:, WIN - TS:, :] = kn[:, :TS, :]
    vnew_ref[:, :WIN - TS, :] = vc[:, TS:, :]
    vnew_ref[:, WIN - TS:, :] = vn[:, :TS, :]


def _attn_sample(qbd, kc, vc, kn, vn, sink_col):
    rows = H * TS
    cache_spec = pl.BlockSpec((BB, WIN, KVH * HD), lambda i: (i, 0, 0))
    new_spec = pl.BlockSpec((BB, 8, KVH * HD), lambda i: (i, 0, 0))
    cache_shape = jax.ShapeDtypeStruct((NB, WIN, KVH * HD), f32)
    return pl.pallas_call(
        _attn_sample_kernel,
        grid=(NB // BB,),
        in_specs=[
            pl.BlockSpec((BB, rows, KVH * HD), lambda i: (i, 0, 0)),
            cache_spec, cache_spec, new_spec, new_spec,
            pl.BlockSpec((rows, 1), lambda i: (0, 0)),
        ],
        out_specs=[pl.BlockSpec((BB, rows, KVH * HD), lambda i: (i, 0, 0)),
                   cache_spec, cache_spec],
        out_shape=[jax.ShapeDtypeStruct((NB, rows, KVH * HD), f32), cache_shape, cache_shape],
        compiler_params=_params("parallel"),
        name="attn_sample",
    )(qbd, kc, vc, kn, vn, sink_col)


def _dispatch_kernel(x_ref, g_ref, w_ref, b_ref, upper_ref, xc_ref, meta_ref, cnt_ref):
    tiles = [_dispatch_tile(x_ref.at[k * TM:(k + 1) * TM], g_ref, w_ref, b_ref, upper_ref,
                            xc_ref.at[k * RC:(k + 1) * RC], meta_ref.at[k * TM:(k + 1) * TM],
                            cnt_ref.at[k * NE:(k + 1) * NE]) for k in range(DT)]
    for _ in range(3):
        for t in tiles:
            next(t, None)


def _dispatch_tile(x_ref, g_ref, w_ref, b_ref, upper_ref, xc_ref, meta_ref, cnt_ref):
    h_hi = _rms(x_ref[...], g_ref[...]).astype(bf16)
    logits = jnp.dot(h_hi, w_ref[...], preferred_element_type=f32) + b_ref[...]
    yield
    lt = logits.T
    inf = jnp.inf
    row8 = lax.broadcasted_iota(jnp.int32, (8, TM), 0).astype(f32)
    gl = jnp.where(row8 < NGRP, lt[0:8], -inf)
    gmax = jnp.max(gl, axis=0, keepdims=True)
    gsel = jnp.min(jnp.where(gl == gmax, row8, 8.0), axis=0, keepdims=True)
    g_w = 1.0 / jnp.sum(jnp.exp(gl - gmax), axis=0, keepdims=True)
    row = lax.broadcasted_iota(jnp.int32, (NE, TM), 0)
    row_f = row.astype(f32)
    el = jnp.where((row >> 2).astype(f32) == gsel, lt[ROW_E:ROW_E + NE], -inf)
    v1 = jnp.max(el, axis=0, keepdims=True)
    i1 = jnp.min(jnp.where(el == v1, row_f, float(NE)), axis=0, keepdims=True)
    el2 = jnp.where(row_f == i1, -inf, el)
    v2 = jnp.max(el2, axis=0, keepdims=True)
    i2 = jnp.min(jnp.where(el2 == v2, row_f, float(NE)), axis=0, keepdims=True)
    e1 = jnp.exp(v2 - v1)
    den = 1.0 + e1
    w1 = (1.0 / den) * g_w
    w2 = (e1 / den) * g_w

    m1 = row_f == i1
    m2 = row_f == i2
    sel = jnp.where(m1 | m2, 1.0, 0.0)
    ranks = jnp.dot(sel.astype(bf16), upper_ref[...],
                    preferred_element_type=f32)
    counts = jnp.sum(sel, axis=1, keepdims=True)
    padded = jnp.floor((counts + (UNIT - 1.0)) * (1.0 / UNIT)) * UNIT
    e_i = lax.broadcasted_iota(jnp.int32, (NE, NE), 0)
    f_i = lax.broadcasted_iota(jnp.int32, (NE, NE), 1)
    below = jnp.where(f_i < e_i, 1.0, 0.0).astype(bf16)
    seg = jnp.dot(below, jnp.broadcast_to(padded, (NE, LANES)).astype(bf16),
                  preferred_element_type=f32)[:, 0:1]
    posall = seg + ranks
    pos1 = jnp.sum(jnp.where(m1, posall, 0.0), axis=0, keepdims=True)
    pos2 = jnp.sum(jnp.where(m2, posall, 0.0), axis=0, keepdims=True)
    cnt_ref[...] = jnp.broadcast_to(counts, (NE, LANES))

    w1_hi = w1.astype(bf16).astype(f32)
    w2_hi = w2.astype(bf16).astype(f32)
    slab = jnp.zeros((8, TM), f32)
    for k, r in enumerate((pos1, pos2, w1_hi, w1 - w1_hi, w2_hi, w2 - w2_hi, i1)):
        slab = jnp.where(row8 == k, r, slab)
    meta = jnp.concatenate([slab, jnp.zeros((LANES - 8, TM), f32)], axis=0).T
    meta_ref[...] = meta
    yield

    rr = lax.broadcasted_iota(jnp.int32, (RC, TM), 0).astype(f32)
    onehot = jnp.where((rr == pos1) | (rr == pos2), 1.0, 0.0).astype(bf16)
    h_aug = jnp.concatenate([h_hi, meta.astype(bf16)], axis=1)
    xc_ref[...] = jnp.dot(onehot, h_aug, preferred_element_type=f32).astype(bf16)


def _dispatch(x, g, w, b):
    upper = jnp.asarray(np.triu(np.ones((TM, TM), np.float32), 1), bf16)
    return pl.pallas_call(
        _dispatch_kernel,
        grid=(NTM // DT,),
        in_specs=[
            pl.BlockSpec((DT * TM, D), lambda i: (i, 0)),
            pl.BlockSpec((1, D), lambda i: (0, 0)),
            pl.BlockSpec((D, LANES), lambda i: (0, 0)),
            pl.BlockSpec((1, LANES), lambda i: (0, 0)),
            pl.BlockSpec((TM, TM), lambda i: (0, 0)),
        ],
        out_specs=[
            pl.BlockSpec((DT * RC, XW), lambda i: (i, 0)),
            pl.BlockSpec((DT * TM, LANES), lambda i: (i, 0)),
            pl.BlockSpec((DT * NE, LANES), lambda i: (i, 0)),
        ],
        out_shape=[
            jax.ShapeDtypeStruct((NTM * RC, XW), bf16),
            jax.ShapeDtypeStruct((N, LANES), f32),
            jax.ShapeDtypeStruct((NTM * NE, LANES), f32),
        ],
        compiler_params=_params("parallel"),
        name="dispatch",
    )(x, g, w, b, upper)


def _dispatch_tables(cnt):
    i32 = jnp.int32
    n = cnt.reshape(NTM, NE, LANES)[:, :, 0].astype(i32)
    units = (n + UNIT - 1) // UNIT
    seg_end = jnp.cumsum(units, axis=1)
    seg_start = seg_end - units
    col_end = jnp.cumsum(units, axis=0)
    col_start = col_end - units
    chunks = (col_end[-1] + UPC - 1) // UPC
    ch_end = jnp.cumsum(chunks)
    ch_start = ch_end - chunks
    nused = ch_end[-1]
    c = jnp.arange(NCH, dtype=i32)
    eid = jnp.minimum(jnp.sum((ch_end[None, :] <= c[:, None]).astype(i32), axis=1), NE - 1)

    src0 = jnp.arange(NTM, dtype=i32)[:, None] * UPT + seg_start
    dst0 = ch_start[None, :] * UPC + col_start
    k = jnp.arange((NCH + NSLOT - 1) * UPC, dtype=i32)[:, None, None]
    inside = (k >= dst0[None]) & (k < (dst0 + units)[None])
    found = jnp.sum(inside.astype(i32), axis=(1, 2))
    shift = jnp.sum(jnp.where(inside, (src0 - dst0)[None], 0), axis=(1, 2))
    ffn_src = jnp.where(found > 0, k[:, 0, 0] + shift, ZERO_UNIT_IN)

    v = jnp.arange(UPT, dtype=i32)[None, :, None]
    inside_v = (v >= seg_start[:, None, :]) & (v < seg_end[:, None, :])
    found_v = jnp.sum(inside_v.astype(i32), axis=2)
    shift_v = jnp.sum(jnp.where(inside_v, (dst0 - seg_start)[:, None, :], 0), axis=2)
    comb_src = jnp.where(found_v > 0, v[:, :, 0] + shift_v, ZERO_UNIT_OUT).reshape(-1)
    own_end = jnp.sum(jnp.where(eid[:, None] == jnp.arange(NE, dtype=i32)[None, :],
                                ch_end[None, :], 0), axis=1)
    eid_at = jnp.sum(jnp.where(c[None, :] == own_end[:, None], eid[None, :], 0), axis=1)
    nxt = jnp.where(own_end < nused, eid_at, -1)
    return eid, nused.reshape(1), ffn_src, nxt, comb_src


def _unit_copies(src_ref, base, n_units, src_hbm, stage, slot, sem):
    out = []
    for j in range(n_units):
        row = pl.multiple_of(src_ref[base + j] * UNIT, UNIT)
        out.append(pltpu.make_async_copy(
            src_hbm.at[pl.ds(row, UNIT), :],
            stage.at[slot, pl.ds(j * UNIT, UNIT), :],
            sem.at[slot]))
    return out


def _start_gathers(step, n_steps, n_units, src_ref, src_hbm, stage, sem):
    ahead = NSLOT - 1

    def start(s, slot):
        for cp in _unit_copies(src_ref, s * n_units, n_units, src_hbm, stage, slot, sem):
            cp.start()

    for s in range(ahead):
        @pl.when((step == 0) & (s < n_steps))
        def _():
            start(s, s)

    @pl.when(step + ahead < n_steps)
    def _():
        start(step + ahead, lax.rem(step + ahead, NSLOT))


def _wait_gather(step, n_units, src_ref, src_hbm, stage, sem):
    slot = lax.rem(step, NSLOT)
    for cp in _unit_copies(src_ref, step * n_units, n_units, src_hbm, stage, slot, sem):
        cp.wait()
    return slot


def _ffn_kernel(layer, eid_ref, nused_ref, src_ref, nxt_ref, xc_hbm, wg_hbm, wu_hbm, wd_hbm,
                o_hbm, stage, sem, wg_f, wu_f, wd_f, w_sem, wg_b, wu_b, wd_b, obuf, o_sem):
    nused = nused_ref[0]
    ahead = NSLOT - 1

    def start(s):
        for cp in _unit_copies(src_ref, s * UPC, UPC, xc_hbm, stage, lax.rem(s, NSLOT), sem):
            cp.start()

    def weight_copies(e, slot):
        return [pltpu.make_async_copy(hbm.at[layer, e], buf.at[slot], w_sem.at[slot])
                for hbm, buf in ((wg_hbm, wg_f), (wu_hbm, wu_f), (wd_hbm, wd_f))]

    def out_copy(c, slot):
        row = pl.multiple_of(c * CH, CH)
        return pltpu.make_async_copy(obuf.at[slot], o_hbm.at[pl.ds(row, CH), :], o_sem.at[slot])

    for s in range(ahead):
        start(s)

    @pl.when(nused > 0)
    def _():
        for cp in weight_copies(eid_ref[0], 0):
            cp.start()

    def chunk(c, n_experts):
        e = eid_ref[c]
        first = (c == 0) | (e != eid_ref[jnp.maximum(c - 1, 0)])
        wslot = lax.rem(n_experts, 2)

        @pl.when(first)
        def _():
            for cp in weight_copies(e, wslot):
                cp.wait()
            wg_b[...] = wg_f[wslot].astype(bf16)
            wu_b[...] = wu_f[wslot].astype(bf16)
            wd_b[...] = wd_f[wslot].astype(bf16)

            @pl.when(nxt_ref[c] >= 0)
            def _():
                for cp in weight_copies(nxt_ref[c], 1 - wslot):
                    cp.start()

        slot = _wait_gather(c, UPC, src_ref, xc_hbm, stage, sem)
        oslot = lax.rem(c, 2)

        @pl.when(c >= 2)
        def _():
            out_copy(c - 2, oslot).wait()

        def block(r):
            xs = stage[slot, r:r + CHB, :]
            x = xs[:, :D]
            a = jnp.dot(x, wg_b[...], preferred_element_type=f32)
            u = jnp.dot(x, wu_b[...], preferred_element_type=f32)
            yield
            gb = xs[:, D:].astype(f32)
            is_first = gb[:, 6:7] == e.astype(f32)
            gate = jnp.where(is_first, gb[:, 2:3] + gb[:, 3:4], gb[:, 4:5] + gb[:, 5:6])
            act = (a * (1.0 / (1.0 + jnp.exp(-a)))) * u * gate
            obuf[oslot, r:r + CHB, :] = jnp.dot(act.astype(bf16), wd_b[...],
                                                preferred_element_type=f32).astype(bf16)

        blocks = [block(r) for r in range(0, CH, CHB)]
        for _ in range(2):
            for blk in blocks:
                next(blk, None)
        out_copy(c, oslot).start()
        start(c + ahead)
        return n_experts + first.astype(jnp.int32)

    lax.fori_loop(0, nused, chunk, jnp.int32(0))

    for s in range(ahead):
        _wait_gather(nused + s, UPC, src_ref, xc_hbm, stage, sem)
    for back in (1, 2):
        @pl.when(nused >= back)
        def _():
            out_copy(nused - back, lax.rem(nused - back, 2)).wait()

    obuf[0] = jnp.zeros((CH, D), bf16)

    def zero_start(c, carry):
        out_copy(c, 0).start()
        return carry

    def zero_wait(c, carry):
        out_copy(c, 0).wait()
        return carry

    lax.fori_loop(nused, NCH, zero_start, 0)
    lax.fori_loop(nused, NCH, zero_wait, 0)


def _ffn(layer, eid, nused, ffn_src, nxt, xc, wg, wu, wd):
    any_spec = pl.BlockSpec(memory_space=pl.ANY)
    return pl.pallas_call(
        functools.partial(_ffn_kernel, layer),
        grid_spec=pltpu.PrefetchScalarGridSpec(
            num_scalar_prefetch=4,
            grid=(1,),
            in_specs=[any_spec, any_spec, any_spec, any_spec],
            out_specs=any_spec,
            scratch_shapes=[
                pltpu.VMEM((NSLOT, CH, XW), bf16),
                pltpu.SemaphoreType.DMA((NSLOT,)),
                pltpu.VMEM((2, D, F), f32),
                pltpu.VMEM((2, D, F), f32),
                pltpu.VMEM((2, F, D), f32),
                pltpu.SemaphoreType.DMA((2,)),
                pltpu.VMEM((D, F), bf16),
                pltpu.VMEM((D, F), bf16),
                pltpu.VMEM((F, D), bf16),
                pltpu.VMEM((2, CH, D), bf16),
                pltpu.SemaphoreType.DMA((2,)),
            ],
        ),
        out_shape=jax.ShapeDtypeStruct((NCH * CH, D), bf16),
        compiler_params=_params("arbitrary"),
        name="ffn",
    )(eid, nused, ffn_src, nxt, xc, wg, wu, wd)


def _combine_tile(x, meta, rows):
    li = lax.broadcasted_iota(jnp.int32, (TM, RC), 1).astype(f32)
    pt = jnp.where((li == meta[:, 0:1]) | (li == meta[:, 1:2]), 1.0, 0.0).astype(bf16)
    return x + jnp.dot(pt, rows, preferred_element_type=f32)


def _combine_kernel(src_ref, x_hbm, meta_ref, o_hbm, yp_ref, ys_ref, stage, sem, xbuf, xsem):
    i = pl.program_id(0)
    tiles = T // TM
    ahead = NSLOT - 1

    def x_copy(s):
        sl = lax.rem(s, NSLOT)
        return pltpu.make_async_copy(x_hbm.at[pl.ds(pl.multiple_of(s * T, T), T), :],
                                     xbuf.at[sl], xsem.at[sl])

    for s in range(ahead):
        @pl.when(i == 0)
        def _():
            x_copy(s).start()

    @pl.when(i + ahead < NT)
    def _():
        x_copy(i + ahead).start()

    _start_gathers(i, NT, tiles * UPT, src_ref, o_hbm, stage, sem)
    slot = _wait_gather(i, tiles * UPT, src_ref, o_hbm, stage, sem)
    x_copy(i).wait()
    x_ref = xbuf.at[slot]
    ys = [_combine_tile(x_ref[k * TM:(k + 1) * TM, :], meta_ref[k * TM:(k + 1) * TM, :],
                        stage[slot, k * RC:(k + 1) * RC, :]) for k in range(tiles)]
    y = jnp.concatenate(ys, axis=0)

    @pl.when(i < NT - 1)
    def _():
        yp_ref[...] = y

    @pl.when(i == NT - 1)
    def _():
        ys_ref[...] = y


def _combine(comb_src, x, meta, o_sorted):
    out_specs = [pl.BlockSpec((T, D), lambda i, src: (jnp.minimum(i, NT - 2), 0)),
                 pl.BlockSpec((T, D), lambda i, src: (0, 0))]
    out_shape = [jax.ShapeDtypeStruct((NP, D), f32), jax.ShapeDtypeStruct((NS, D), f32)]
    return pl.pallas_call(
        _combine_kernel,
        grid_spec=pltpu.PrefetchScalarGridSpec(
            num_scalar_prefetch=1,
            grid=(NT,),
            in_specs=[
                pl.BlockSpec(memory_space=pl.ANY),
                pl.BlockSpec((T, LANES), lambda i, src: (i, 0)),
                pl.BlockSpec(memory_space=pl.ANY),
            ],
            out_specs=out_specs,
            scratch_shapes=[
                pltpu.VMEM((NSLOT, (T // TM) * RC, D), bf16),
                pltpu.SemaphoreType.DMA((NSLOT,)),
                pltpu.VMEM((NSLOT, T, D), f32),
                pltpu.SemaphoreType.DMA((NSLOT,)),
            ],
        ),
        out_shape=out_shape,
        compiler_params=_params("arbitrary"),
        name="combine",
    )(comb_src, x, meta, o_sorted)


def _conv_kernel(src_ref, x1_ref, meta_ref, o_hbm, g_ref, win_ref, cw_ref, wout_ref,
                 p1_ref, p2_ref, y_ref, cu_ref, pad_ref, x_ref, stage, sem):
    i = pl.program_id(0)

    @pl.when(i == 0)
    def _():
        pad_ref[0:8, :] = jnp.zeros((8, D), f32)

    tiles = T // TM
    ahead = NSLOT - 1

    def start(s):
        for cp in _unit_copies(src_ref, s * tiles * UPT, tiles * UPT, o_hbm, stage,
                               lax.rem(s, NSLOT), sem):
            cp.start()

    @pl.when(i == 0)
    def _():
        for s in range(ahead):
            start(s)

    slot = _wait_gather(i, tiles * UPT, src_ref, o_hbm, stage, sem)

    for k in range(tiles):
        rows = slice(k * TM, (k + 1) * TM)
        x_ref[rows, :] = _combine_tile(x1_ref[rows, :], meta_ref[rows, :],
                                       stage[slot, k * RC:(k + 1) * RC, :])

    blocks = [(r, r + T // 2) for r in (0, T // 2)]
    gates = []
    for r0, r1 in blocks:
        h = _rms(x_ref[r0:r1, :], g_ref[...]).astype(bf16)
        bcu = jnp.dot(h, win_ref[...], preferred_element_type=f32)
        gates.append(bcu[:, :D])
        cu = bcu[:, D:2 * D] * bcu[:, 2 * D:]
        pad_ref[8 + r0:8 + r1, :] = cu
        cu_ref[r0:r1, :] = cu
    t = lax.broadcasted_iota(jnp.int32, (T // 2, 1), 0) & (TS - 1)
    is_sample = i == NT - 1
    cw = cw_ref[...]
    for (r0, r1), b in zip(blocks, gates):
        m1 = jnp.where(is_sample & (t == 0), p1_ref[r0:r1, :], pad_ref[7 + r0:7 + r1, :])
        m2 = jnp.where(is_sample & (t < 2), p2_ref[r0:r1, :], pad_ref[6 + r0:6 + r1, :])
        conv = cw[0:1] * m2 + cw[1:2] * m1 + cw[2:3] * pad_ref[8 + r0:8 + r1, :]
        y = jnp.dot((b * conv).astype(bf16), wout_ref[...], preferred_element_type=f32)
        y_ref[r0:r1, :] = x_ref[r0:r1, :] + y
    pad_ref[0:8, :] = pad_ref[T:T + 8, :]
    start(i + ahead)

    @pl.when(i == NT - 1)
    def _():
        for s in range(ahead):
            _wait_gather(i + 1 + s, tiles * UPT, src_ref, o_hbm, stage, sem)


def _conv(comb_src, x1, meta, o_sorted, g, win, cw, wout, p1, p2):
    const = dict(pipeline_mode=pl.Buffered(1))
    comb_src = jnp.concatenate([
        comb_src, jnp.full(((NSLOT - 1) * (T // TM) * UPT,), ZERO_UNIT_OUT, jnp.int32)])
    return pl.pallas_call(
        _conv_kernel,
        grid_spec=pltpu.PrefetchScalarGridSpec(
            num_scalar_prefetch=1,
            grid=(NT,),
            in_specs=[
                pl.BlockSpec((T, D), lambda i, src: (i, 0)),
                pl.BlockSpec((T, LANES), lambda i, src: (i, 0)),
                pl.BlockSpec(memory_space=pl.ANY),
                pl.BlockSpec((1, D), lambda i, src: (0, 0)),
                pl.BlockSpec((D, 3 * D), lambda i, src: (0, 0), **const),
                pl.BlockSpec((3, D), lambda i, src: (0, 0)),
                pl.BlockSpec((D, D), lambda i, src: (0, 0), **const),
                pl.BlockSpec((T, D), lambda i, src: (0, 0), **const),
                pl.BlockSpec((T, D), lambda i, src: (0, 0), **const),
            ],
            out_specs=[
                pl.BlockSpec((T, D), lambda i, src: (i, 0)),
                pl.BlockSpec((T, D), lambda i, src: (jnp.where(i == NT - 1, 1, 0), 0)),
            ],
            scratch_shapes=[
                pltpu.VMEM((T + 8, D), f32),
                pltpu.VMEM((T, D), f32),
                pltpu.VMEM((NSLOT, (T // TM) * RC, D), bf16),
                pltpu.SemaphoreType.DMA((NSLOT,)),
            ],
        ),
        out_shape=[
            jax.ShapeDtypeStruct((N, D), f32),
            jax.ShapeDtypeStruct((2 * T, D), f32),
        ],
        compiler_params=_params("arbitrary"),
        name="conv",
    )(comb_src, x1, meta, o_sorted, g, win, cw, wout, p1, p2)


def _rope_tables():
    inv_freq = THETA ** (-jnp.arange(HALF, dtype=f32) / HALF)
    pos = jnp.concatenate([
        jnp.arange(NP, dtype=jnp.int32),
        PAST + jnp.tile(jnp.arange(TS, dtype=jnp.int32), NB),
    ]).astype(f32)
    ang = inv_freq[:, None] * pos[None, :]
    return jnp.cos(ang), jnp.sin(ang)


def _router_weights(w_group, b_group, w_router, b_router):
    gap = ROW_E - NGRP
    pad = LANES - ROW_E - NE
    w = jnp.concatenate([w_group, jnp.zeros((D, gap), f32), w_router,
                         jnp.zeros((D, pad), f32)], axis=1)
    b = jnp.concatenate([b_group, jnp.zeros((gap,), f32), b_router,
                         jnp.zeros((pad,), f32)])[None, :]
    return w.astype(bf16), b


def _moe_experts(x, i, norm_ffn, w_group, b_group, w_router, b_router, w_gate, w_up, w_down):
    w, b = _router_weights(w_group[i], b_group[i], w_router[i], b_router[i])
    xc, meta, cnt = _dispatch(x, norm_ffn[i][None, :], w, b)
    eid, nused, ffn_src, nxt, comb_src = _dispatch_tables(cnt)
    o_sorted = _ffn(i, eid, nused, ffn_src, nxt, xc, w_gate, w_up, w_down)
    return comb_src, meta, o_sorted


def kernel(x_prompt, x_sample, cache_k, cache_v, state_conv, norm_mix, w_qkv, q_norm, k_norm,
           sinks, w_o, w_in, conv_w, w_out, norm_ffn, w_group, b_group, w_router, b_router,
           w_gate, w_up, w_down):
    xp = x_prompt.reshape(NP, D)
    xs = x_sample.reshape(NS, D)
    moe_w = (norm_ffn, w_group, b_group, w_router, b_router, w_gate, w_up, w_down)

    cos, sin = _rope_tables()
    (qT, ktok, vtok, vT), (q_s, ktok_s, vtok_s) = _qkv(
        xp, xs, norm_mix[0][None, :], w_qkv[0].T.astype(bf16), q_norm[0][:, None],
        k_norm[0][:, None], cos, sin)
    sink_rows = jnp.repeat(sinks[0].reshape(KVH, G), TQ, axis=1)[:, None, :]

    qs = q_s.reshape(KVH, G, HD, NB, TS).transpose(3, 0, 1, 4, 2)
    zq = jnp.zeros_like(qs[:, 0])
    qbd = jnp.stack([jnp.concatenate([qs[:, 0], zq], axis=-1),
                     jnp.concatenate([zq, qs[:, 1]], axis=-1)], axis=1)
    qbd = qbd.reshape(NB, H * TS, KVH * HD)
    k_new = ktok_s.reshape(NB, TS, KVH * HD)
    v_new = vtok_s.reshape(NB, TS, KVH * HD)
    pad4 = jnp.zeros((NB, 8 - TS, KVH * HD), f32)
    kc = cache_k[0].reshape(NB, WIN, KVH * HD)
    vc = cache_v[0].reshape(NB, WIN, KVH * HD)
    sink_col = jnp.repeat(sinks[0], TS)[:, None]
    o_s, kc_new, vc_new = _attn_sample(qbd, kc, vc, jnp.concatenate([k_new, pad4], axis=1),
                                       jnp.concatenate([v_new, pad4], axis=1), sink_col)
    o_s = o_s.reshape(NB, KVH, G, TS, KVH, HD)
    o_s = jnp.stack([o_s[:, 0, :, :, 0], o_s[:, 1, :, :, 1]], axis=1)
    o_s = o_s.transpose(0, 3, 1, 2, 4).reshape(NS, H * HD).astype(bf16)
    x = _attn_prompt(qT, ktok, vT, sink_rows, xp, xs, o_s, w_o[0].astype(bf16))
    comb_src, meta, o_sorted = _moe_experts(x, 0, *moe_w)

    new_k_prompt = ktok[NP - WIN:NP].reshape(1, 1, WIN, KVH, HD)
    new_v_prompt = vtok[NP - WIN:NP].reshape(1, 1, WIN, KVH, HD)
    new_k_sample = kc_new.reshape(1, NB, WIN, KVH, HD)
    new_v_sample = vc_new.reshape(1, NB, WIN, KVH, HD)

    st = state_conv[0]
    z = jnp.zeros((NB, 1, D), f32)
    p1 = jnp.concatenate([st[:, 1:2], z, z, z], axis=1).reshape(NS, D)
    p2 = jnp.concatenate([st[:, 0:1], st[:, 1:2], z, z], axis=1).reshape(NS, D)
    x, cu = _conv(comb_src, x, meta, o_sorted, norm_mix[1][None, :], w_in[0].astype(bf16),
                  conv_w[0], w_out[0].astype(bf16), p1, p2)
    comb_src, meta, o_sorted = _moe_experts(x, 1, *moe_w)
    y_prompt, y_sample = _combine(comb_src, x, meta, o_sorted)

    new_conv_prompt = cu[T - 2:T].reshape(1, 1, 2, D)
    new_conv_sample = cu[T:].reshape(NB, TS, D)[:, TS - 2:][None]

    y_prompt = y_prompt.reshape(1, NP, D)
    y_sample = y_sample.reshape(NB, TS, D)
    return (y_prompt, y_sample, new_k_prompt, new_v_prompt, new_conv_prompt,
            new_k_sample, new_v_sample, new_conv_sample)
```

```python
import functools

import jax
import jax.numpy as jnp
import numpy as np
from jax import lax
from jax.experimental import pallas as pl
from jax.experimental.pallas import tpu as pltpu

D = 1024
NP = 16384
NB = 128
TS = 4
NS = NB * TS
N = NP + NS
PAST = 16384
H = 16
KVH = 2
G = H // KVH
HD = 64
HALF = HD // 2
QKV = (H + 2 * KVH) * HD
WIN = 128
THETA = 10000.0
NGRP = 4
EPG = 4
NE = NGRP * EPG
TOPK = 2
ROW_E = 8
F = 512
EPS = 1e-6
SCALE = HD ** -0.5

T = 512
NT = N // T
TQ = 128
NQ = NP // TQ
QB = 4
BB = 16
LANES = 128
V7X_VMEM_BYTES = 64 * 1024 * 1024
VMEM_LIMIT = V7X_VMEM_BYTES - 14 * 1024 * 1024

TM = 256
NTM = N // TM
DT = 6
XSLOT = 3
UNIT = 16
RC = TOPK * TM + NE * UNIT
UPT = RC // UNIT
XW = D + LANES
CH = 512
CHB = 256
UPC = CH // UNIT
NSLOT = 4
NCH = -(-(NTM * (UPT - 1) + NE * (UPC - 1)) // UPC) + NSLOT - 1
ZERO_UNIT_IN = UPT - 1
ZERO_UNIT_OUT = (NCH - 1) * UPC

f32 = jnp.float32
bf16 = jnp.bfloat16


def _params(*sem):
    return pltpu.CompilerParams(dimension_semantics=sem, vmem_limit_bytes=VMEM_LIMIT)


def _rms(x, g):
    ms = jnp.mean(x * x, axis=-1, keepdims=True)
    return x * lax.rsqrt(ms + EPS) * g


def _qkv_prompt_kernel(x_ref, g_ref, wT_ref, qn_ref, kn_ref, cos_ref, sin_ref,
                       qT_ref, ktok_ref, vtok_ref, vT_ref):
    _qkv_tile(x_ref, g_ref, wT_ref, qn_ref, kn_ref, cos_ref, sin_ref,
              qT_ref, ktok_ref, vtok_ref, vT_ref)


def _qkv_sample_kernel(x_ref, g_ref, wT_ref, qn_ref, kn_ref, cos_ref, sin_ref,
                       q_ref, ktok_ref, vtok_ref):
    _qkv_tile(x_ref, g_ref, wT_ref, qn_ref, kn_ref, cos_ref, sin_ref,
              q_ref, ktok_ref, vtok_ref, None)


def _qkv_tile(x_ref, g_ref, wT_ref, qn_ref, kn_ref, cos_ref, sin_ref,
              q_ref, ktok_ref, vtok_ref, vT_ref):
    h = _rms(x_ref[...], g_ref[...]).astype(bf16)
    qkvT = lax.dot_general(wT_ref[...], h, (((1,), (1,)), ((), ())),
                           preferred_element_type=f32)
    cos = cos_ref[...]
    sin = sin_ref[...]

    def norm_rope(blk, gcol):
        ms = jnp.mean(blk * blk, axis=0, keepdims=True)
        y = blk * lax.rsqrt(ms + EPS) * gcol
        y1 = y[:HALF]
        y2 = y[HALF:]
        return y1 * cos - y2 * sin, y2 * cos + y1 * sin

    qn = qn_ref[...]
    for hd in range(H):
        o1, o2 = norm_rope(qkvT[hd * HD:(hd + 1) * HD], qn)
        q_ref[hd * HD:hd * HD + HALF, :] = (o1 * SCALE).astype(bf16)
        q_ref[hd * HD + HALF:(hd + 1) * HD, :] = (o2 * SCALE).astype(bf16)
    kn = kn_ref[...]
    ks = []
    for j in range(KVH):
        o1, o2 = norm_rope(qkvT[H * HD + j * HD:H * HD + (j + 1) * HD], kn)
        ks += [o1, o2]
    kT = jnp.concatenate(ks, axis=0)
    ktok_ref[...] = kT.T
    vT = qkvT[(H + KVH) * HD:]
    vtok_ref[...] = vT.T
    if vT_ref is not None:
        vT_ref[...] = vT.astype(bf16)


def _qkv(xp, xs, g, wT, qn, kn, cos, sin):
    const = [pl.BlockSpec((HD, 1), lambda i: (0, 0)), pl.BlockSpec((HD, 1), lambda i: (0, 0)),
             pl.BlockSpec((HALF, T), lambda i: (0, i)), pl.BlockSpec((HALF, T), lambda i: (0, i))]
    x_spec = pl.BlockSpec((T, D), lambda i: (i, 0))
    g_spec = pl.BlockSpec((1, D), lambda i: (0, 0))
    w_spec = pl.BlockSpec((QKV, D), lambda i: (0, 0))
    tok_spec = pl.BlockSpec((T, KVH * HD), lambda i: (i, 0))
    nt = NP // T
    prompt = pl.pallas_call(
        _qkv_prompt_kernel,
        grid=(nt,),
        in_specs=[x_spec, g_spec, w_spec] + const,
        out_specs=[pl.BlockSpec((None, H * HD, T), lambda i: (i, 0, 0)), tok_spec, tok_spec,
                   pl.BlockSpec((None, KVH * HD, T), lambda i: (i, 0, 0))],
        out_shape=[jax.ShapeDtypeStruct((nt, H * HD, T), bf16),
                   jax.ShapeDtypeStruct((NP, KVH * HD), f32),
                   jax.ShapeDtypeStruct((NP, KVH * HD), f32),
                   jax.ShapeDtypeStruct((nt, KVH * HD, T), bf16)],
        compiler_params=_params("parallel"),
        name="qkv",
    )(xp, g, wT, qn, kn, cos[:, :NP], sin[:, :NP])
    sample = pl.pallas_call(
        _qkv_sample_kernel,
        grid=(NS // T,),
        in_specs=[x_spec, g_spec, w_spec] + const,
        out_specs=[pl.BlockSpec((H * HD, T), lambda i: (0, i)), tok_spec, tok_spec],
        out_shape=[jax.ShapeDtypeStruct((H * HD, NS), bf16),
                   jax.ShapeDtypeStruct((NS, KVH * HD), f32),
                   jax.ShapeDtypeStruct((NS, KVH * HD), f32)],
        compiler_params=_params("parallel"),
        name="qkv_sample",
    )(xs, g, wT, qn, kn, cos[:, NP:], sin[:, NP:])
    return prompt, sample


def _attn_prompt_kernel(qT_ref, kp_ref, kc_ref, vp_ref, vc_ref, sink_ref, bias0_ref, bias_ref,
                        xp_ref, xs_ref, os_ref, wo_ref, y_ref):
    j = pl.program_id(0)

    @pl.when(j >= NQ // QB)
    def _():
        y_ref[...] = xs_ref[...] + jnp.dot(os_ref[...], wo_ref[...],
                                           preferred_element_type=f32)

    @pl.when(j < NQ // QB)
    def _():
        k_all = jnp.concatenate([kp_ref[...], kc_ref[...]], axis=0)
        v_all = jnp.concatenate([vp_ref[...], vc_ref[...]], axis=1)
        blocks = []
        for k in range(QB):
            rows = slice(k * TQ, (k + 1) * TQ)
            blocks.append(_attend_block(
                qT_ref.at[:, rows], k_all[k * TQ:(k + 2) * TQ], v_all[:, k * TQ:(k + 2) * TQ],
                sink_ref, bias0_ref if k == 0 else bias_ref, xp_ref.at[rows], wo_ref,
                y_ref.at[rows]))
        for _ in range(3):
            for blk in blocks:
                next(blk, None)


def _attend_block(qT_ref, kk, vv, sink_ref, bias_ref, x_ref, wo_ref, y_ref):
    kk = kk.astype(bf16)
    bias = jnp.concatenate([bias_ref[...]] * H, axis=1)
    qg = [jnp.concatenate([qT_ref[(g * G + hh) * HD:(g * G + hh + 1) * HD, :]
                           for hh in range(G)], axis=1) for g in range(KVH)]
    zeros = jnp.zeros_like(qg[0])
    rhs = jnp.concatenate([jnp.concatenate([qg[0], zeros], axis=1),
                           jnp.concatenate([zeros, qg[1]], axis=1)], axis=0)
    sT = jnp.dot(kk, rhs, preferred_element_type=f32) + bias
    yield
    sink = jnp.concatenate([sink_ref[0], sink_ref[1]], axis=1)
    m = jnp.maximum(jnp.max(sT, axis=0, keepdims=True), sink)
    p = jnp.exp(sT - m)
    l = jnp.sum(p, axis=0, keepdims=True) + jnp.exp(sink - m)
    p = (p * (1.0 / l)).astype(bf16)
    yield
    pieces = []
    for g in range(KVH):
        oT = jnp.dot(vv[g * HD:(g + 1) * HD, :], p[:, g * G * TQ:(g + 1) * G * TQ],
                     preferred_element_type=f32)
        pieces += [oT[:, hh * TQ:(hh + 1) * TQ] for hh in range(G)]
    oT_all = jnp.concatenate(pieces, axis=0)
    y_ref[...] = x_ref[...] + jnp.dot(oT_all.T.astype(bf16), wo_ref[...],
                                      preferred_element_type=f32)


def _band_bias():
    s = np.arange(2 * TQ)[:, None]
    t = np.arange(TQ)[None, :]
    dist = t + TQ - s
    band = (dist >= 0) & (dist <= WIN)
    first = band & (s >= TQ)
    return jnp.asarray(np.where(np.stack([first, band]), 0.0, -np.inf), f32)


def _attn_prompt(qT, ktok, vT, sink_rows, xp, xs, o_s, wo):
    steps = NQ // QB
    rows = QB * TQ
    assert rows == T

    def cur(j):
        return jnp.minimum(j, steps - 1)

    def prev(j):
        return jnp.maximum(cur(j) * QB - 1, 0)

    def prev_tile(j):
        return jnp.maximum(cur(j) - 1, 0)

    def sample(j):
        return jnp.maximum(j - steps, 0)

    bias = _band_bias()
    return pl.pallas_call(
        _attn_prompt_kernel,
        grid=(N // rows,),
        in_specs=[
            pl.BlockSpec((None, H * HD, rows), lambda j: (cur(j), 0, 0)),
            pl.BlockSpec((TQ, KVH * HD), lambda j: (prev(j), 0)),
            pl.BlockSpec((rows, KVH * HD), lambda j: (cur(j), 0)),
            pl.BlockSpec((None, KVH * HD, TQ), lambda j: (prev_tile(j), 0, QB - 1)),
            pl.BlockSpec((None, KVH * HD, rows), lambda j: (cur(j), 0, 0)),
            pl.BlockSpec((KVH, 1, G * TQ), lambda j: (0, 0, 0)),
            pl.BlockSpec((None, 2 * TQ, TQ), lambda j: (jnp.minimum(j, 1), 0, 0)),
            pl.BlockSpec((None, 2 * TQ, TQ), lambda j: (1, 0, 0)),
            pl.BlockSpec((rows, D), lambda j: (cur(j), 0)),
            pl.BlockSpec((rows, D), lambda j: (sample(j), 0)),
            pl.BlockSpec((rows, H * HD), lambda j: (sample(j), 0)),
            pl.BlockSpec((D, D), lambda j: (0, 0)),
        ],
        out_specs=pl.BlockSpec((rows, D), lambda j: (j, 0)),
        out_shape=jax.ShapeDtypeStruct((N, D), f32),
        compiler_params=_params("parallel"),
        name="attn_prompt",
    )(qT, ktok, ktok, vT, vT, sink_rows, bias, bias, xp, xs, o_s, wo)


def _attn_sample_kernel(q_ref, kc_ref, vc_ref, kn_ref, vn_ref, sink_ref,
                        o_ref, knew_ref, vnew_ref):
    rows = H * TS
    t1 = lax.broadcasted_iota(jnp.int32, (1, rows, WIN), 1) & (TS - 1)
    s1 = lax.broadcasted_iota(jnp.int32, (1, rows, WIN), 2)
    valid1 = s1 >= t1
    t2 = lax.broadcasted_iota(jnp.int32, (1, rows, 8), 1) & (TS - 1)
    s2 = lax.broadcasted_iota(jnp.int32, (1, rows, 8), 2)
    valid2 = s2 <= t2
    sink = sink_ref[...][None]
    q = q_ref[...]
    kc = kc_ref[...]
    vc = vc_ref[...]
    kn = kn_ref[...]
    vn = vn_ref[...]
    sc = jnp.einsum('bqd,bkd->bqk', q, kc.astype(bf16), preferred_element_type=f32)
    sn = jnp.einsum('bqd,bkd->bqk', q, kn.astype(bf16), preferred_element_type=f32)
    sc = jnp.where(valid1, sc, -jnp.inf)
    sn = jnp.where(valid2, sn, -jnp.inf)
    m = jnp.maximum(jnp.maximum(jnp.max(sc, axis=-1, keepdims=True),
                                jnp.max(sn, axis=-1, keepdims=True)), sink)
    pc = jnp.exp(sc - m)
    pn = jnp.exp(sn - m)
    l = (jnp.sum(pc, axis=-1, keepdims=True) + jnp.sum(pn, axis=-1, keepdims=True)
         + jnp.exp(sink - m))
    r = 1.0 / l
    o_ref[...] = (jnp.einsum('bqk,bkd->bqd', (pc * r).astype(bf16), vc.astype(bf16),
                             preferred_element_type=f32)
                  + jnp.einsum('bqk,bkd->bqd', (pn * r).astype(bf16), vn.astype(bf16),
                               preferred_element_type=f32))
    knew_ref[:, :WIN - TS, :] = kc[:, TS:, :]
    knew_ref[:, WIN - TS:, :] = kn[:, :TS, :]
    vnew_ref[:, :WIN - TS, :] = vc[:, TS:, :]
    vnew_ref[:, WIN - TS:, :] = vn[:, :TS, :]


def _attn_sample(qbd, kc, vc, kn, vn, sink_col):
    rows = H * TS
    cache_spec = pl.BlockSpec((BB, WIN, KVH * HD), lambda i: (i, 0, 0))
    new_spec = pl.BlockSpec((BB, 8, KVH * HD), lambda i: (i, 0, 0))
    cache_shape = jax.ShapeDtypeStruct((NB, WIN, KVH * HD), f32)
    return pl.pallas_call(
        _attn_sample_kernel,
        grid=(NB // BB,),
        in_specs=[
            pl.BlockSpec((BB, rows, KVH * HD), lambda i: (i, 0, 0)),
            cache_spec, cache_spec, new_spec, new_spec,
            pl.BlockSpec((rows, 1), lambda i: (0, 0)),
        ],
        out_specs=[pl.BlockSpec((BB, rows, KVH * HD), lambda i: (i, 0, 0)),
                   cache_spec, cache_spec],
        out_shape=[jax.ShapeDtypeStruct((NB, rows, KVH * HD), f32), cache_shape, cache_shape],
        compiler_params=_params("parallel"),
        name="attn_sample",
    )(qbd, kc, vc, kn, vn, sink_col)


def _dispatch_kernel(x_hbm, g_ref, w_ref, b_ref, upper_ref, xc_ref, meta_ref, cnt_ref,
                     xbuf, xsem):
    i = pl.program_id(0)
    steps = NTM // DT
    rows = DT * TM

    def x_copy(s):
        sl = lax.rem(s, XSLOT)
        return pltpu.make_async_copy(x_hbm.at[pl.ds(pl.multiple_of(s * rows, rows), rows), :],
                                     xbuf.at[sl], xsem.at[sl])

    for s in range(XSLOT - 1):
        @pl.when(i == 0)
        def _():
            x_copy(s).start()

    @pl.when(i + XSLOT - 1 < steps)
    def _():
        x_copy(i + XSLOT - 1).start()

    x_copy(i).wait()
    x_ref = xbuf.at[lax.rem(i, XSLOT)]
    tiles = [_dispatch_tile(x_ref.at[k * TM:(k + 1) * TM], g_ref, w_ref, b_ref, upper_ref,
                            xc_ref.at[k * RC:(k + 1) * RC], meta_ref.at[k * TM:(k + 1) * TM],
                            cnt_ref.at[k * NE:(k + 1) * NE]) for k in range(DT)]
    for _ in range(3):
        for t in tiles:
            next(t, None)


def _dispatch_tile(x_ref, g_ref, w_ref, b_ref, upper_ref, xc_ref, meta_ref, cnt_ref):
    h_hi = _rms(x_ref[...], g_ref[...]).astype(bf16)
    logits = jnp.dot(h_hi, w_ref[...], preferred_element_type=f32) + b_ref[...]
    yield
    lt = logits.T
    inf = jnp.inf
    row8 = lax.broadcasted_iota(jnp.int32, (8, TM), 0).astype(f32)
    gl = jnp.where(row8 < NGRP, lt[0:8], -inf)
    gmax = jnp.max(gl, axis=0, keepdims=True)
    gsel = jnp.min(jnp.where(gl == gmax, row8, 8.0), axis=0, keepdims=True)
    g_w = 1.0 / jnp.sum(jnp.exp(gl - gmax), axis=0, keepdims=True)
    row = lax.broadcasted_iota(jnp.int32, (NE, TM), 0)
    row_f = row.astype(f32)
    el = jnp.where((row >> 2).astype(f32) == gsel, lt[ROW_E:ROW_E + NE], -inf)
    v1 = jnp.max(el, axis=0, keepdims=True)
    i1 = jnp.min(jnp.where(el == v1, row_f, float(NE)), axis=0, keepdims=True)
    el2 = jnp.where(row_f == i1, -inf, el)
    v2 = jnp.max(el2, axis=0, keepdims=True)
    i2 = jnp.min(jnp.where(el2 == v2, row_f, float(NE)), axis=0, keepdims=True)
    e1 = jnp.exp(v2 - v1)
    den = 1.0 + e1
    w1 = (1.0 / den) * g_w
    w2 = (e1 / den) * g_w

    m1 = row_f == i1
    m2 = row_f == i2
    sel = jnp.where(m1 | m2, 1.0, 0.0)
    ranks = jnp.dot(sel.astype(bf16), upper_ref[...],
                    preferred_element_type=f32)
    counts = jnp.sum(sel, axis=1, keepdims=True)
    padded = jnp.floor((counts + (UNIT - 1.0)) * (1.0 / UNIT)) * UNIT
    e_i = lax.broadcasted_iota(jnp.int32, (NE, NE), 0)
    f_i = lax.broadcasted_iota(jnp.int32, (NE, NE), 1)
    below = jnp.where(f_i < e_i, 1.0, 0.0).astype(bf16)
    seg = jnp.dot(below, jnp.broadcast_to(padded, (NE, LANES)).astype(bf16),
                  preferred_element_type=f32)[:, 0:1]
    posall = seg + ranks
    pos1 = jnp.sum(jnp.where(m1, posall, 0.0), axis=0, keepdims=True)
    pos2 = jnp.sum(jnp.where(m2, posall, 0.0), axis=0, keepdims=True)
    cnt_ref[...] = jnp.broadcast_to(counts, (NE, LANES))

    w1_hi = w1.astype(bf16).astype(f32)
    w2_hi = w2.astype(bf16).astype(f32)
    slab = jnp.zeros((8, TM), f32)
    for k, r in enumerate((pos1, pos2, w1_hi, w1 - w1_hi, w2_hi, w2 - w2_hi, i1)):
        slab = jnp.where(row8 == k, r, slab)
    meta = jnp.concatenate([slab, jnp.zeros((LANES - 8, TM), f32)], axis=0).T
    meta_ref[...] = meta
    yield

    rr = lax.broadcasted_iota(jnp.int32, (RC, TM), 0).astype(f32)
    onehot = jnp.where((rr == pos1) | (rr == pos2), 1.0, 0.0).astype(bf16)
    h_aug = jnp.concatenate([h_hi, meta.astype(bf16)], axis=1)
    xc_ref[...] = jnp.dot(onehot, h_aug, preferred_element_type=f32).astype(bf16)


def _dispatch(x, g, w, b):
    upper = jnp.asarray(np.triu(np.ones((TM, TM), np.float32), 1), bf16)
    return pl.pallas_call(
        _dispatch_kernel,
        grid=(NTM // DT,),
        in_specs=[
            pl.BlockSpec(memory_space=pl.ANY),
            pl.BlockSpec((1, D), lambda i: (0, 0)),
            pl.BlockSpec((D, LANES), lambda i: (0, 0)),
            pl.BlockSpec((1, LANES), lambda i: (0, 0)),
            pl.BlockSpec((TM, TM), lambda i: (0, 0)),
        ],
        out_specs=[
            pl.BlockSpec((DT * RC, XW), lambda i: (i, 0)),
            pl.BlockSpec((DT * TM, LANES), lambda i: (i, 0)),
            pl.BlockSpec((DT * NE, LANES), lambda i: (i, 0)),
        ],
        out_shape=[
            jax.ShapeDtypeStruct((NTM * RC, XW), bf16),
            jax.ShapeDtypeStruct((N, LANES), f32),
            jax.ShapeDtypeStruct((NTM * NE, LANES), f32),
        ],
        scratch_shapes=[pltpu.VMEM((XSLOT, DT * TM, D), f32),
                        pltpu.SemaphoreType.DMA((XSLOT,))],
        compiler_params=_params("arbitrary"),
        name="dispatch",
    )(x, g, w, b, upper)


def _dispatch_tables(cnt):
    i32 = jnp.int32
    n = cnt.reshape(NTM, NE, LANES)[:, :, 0].astype(i32)
    units = (n + UNIT - 1) // UNIT
    seg_end = jnp.cumsum(units, axis=1)
    seg_start = seg_end - units
    col_end = jnp.cumsum(units, axis=0)
    col_start = col_end - units
    chunks = (col_end[-1] + UPC - 1) // UPC
    ch_end = jnp.cumsum(chunks)
    ch_start = ch_end - chunks
    nused = ch_end[-1]
    c = jnp.arange(NCH, dtype=i32)
    eid = jnp.minimum(jnp.sum((ch_end[None, :] <= c[:, None]).astype(i32), axis=1), NE - 1)

    src0 = jnp.arange(NTM, dtype=i32)[:, None] * UPT + seg_start
    dst0 = ch_start[None, :] * UPC + col_start
    k = jnp.arange((NCH + NSLOT - 1) * UPC, dtype=i32)[:, None, None]
    inside = (k >= dst0[None]) & (k < (dst0 + units)[None])
    found = jnp.sum(inside.astype(i32), axis=(1, 2))
    shift = jnp.sum(jnp.where(inside, (src0 - dst0)[None], 0), axis=(1, 2))
    ffn_src = jnp.where(found > 0, k[:, 0, 0] + shift, ZERO_UNIT_IN)

    v = jnp.arange(UPT, dtype=i32)[None, :, None]
    inside_v = (v >= seg_start[:, None, :]) & (v < seg_end[:, None, :])
    found_v = jnp.sum(inside_v.astype(i32), axis=2)
    shift_v = jnp.sum(jnp.where(inside_v, (dst0 - seg_start)[:, None, :], 0), axis=2)
    comb_src = jnp.where(found_v > 0, v[:, :, 0] + shift_v, ZERO_UNIT_OUT).reshape(-1)
    own_end = jnp.sum(jnp.where(eid[:, None] == jnp.arange(NE, dtype=i32)[None, :],
                                ch_end[None, :], 0), axis=1)
    eid_at = jnp.sum(jnp.where(c[None, :] == own_end[:, None], eid[None, :], 0), axis=1)
    nxt = jnp.where(own_end < nused, eid_at, -1)
    return eid, nused.reshape(1), ffn_src, nxt, comb_src


def _unit_copies(src_ref, base, n_units, src_hbm, stage, slot, sem):
    out = []
    for j in range(n_units):
        row = pl.multiple_of(src_ref[base + j] * UNIT, UNIT)
        out.append(pltpu.make_async_copy(
            src_hbm.at[pl.ds(row, UNIT), :],
            stage.at[slot, pl.ds(j * UNIT, UNIT), :],
            sem.at[slot]))
    return out


def _start_gathers(step, n_steps, n_units, src_ref, src_hbm, stage, sem):
    ahead = NSLOT - 1

    def start(s, slot):
        for cp in _unit_copies(src_ref, s * n_units, n_units, src_hbm, stage, slot, sem):
            cp.start()

    for s in range(ahead):
        @pl.when((step == 0) & (s < n_steps))
        def _():
            start(s, s)

    @pl.when(step + ahead < n_steps)
    def _():
        start(step + ahead, lax.rem(step + ahead, NSLOT))


def _wait_gather(step, n_units, src_ref, src_hbm, stage, sem):
    slot = lax.rem(step, NSLOT)
    for cp in _unit_copies(src_ref, step * n_units, n_units, src_hbm, stage, slot, sem):
        cp.wait()
    return slot


def _ffn_kernel(layer, eid_ref, nused_ref, src_ref, nxt_ref, xc_hbm, wg_hbm, wu_hbm, wd_hbm,
                o_hbm, stage, sem, wg_f, wu_f, wd_f, w_sem, wg_b, wu_b, wd_b, obuf, o_sem):
    nused = nused_ref[0]
    ahead = NSLOT - 1

    def start(s):
        for cp in _unit_copies(src_ref, s * UPC, UPC, xc_hbm, stage, lax.rem(s, NSLOT), sem):
            cp.start()

    def weight_copies(e, slot):
        return [pltpu.make_async_copy(hbm.at[layer, e], buf.at[slot], w_sem.at[slot])
                for hbm, buf in ((wg_hbm, wg_f), (wu_hbm, wu_f), (wd_hbm, wd_f))]

    def out_copy(c, slot):
        row = pl.multiple_of(c * CH, CH)
        return pltpu.make_async_copy(obuf.at[slot], o_hbm.at[pl.ds(row, CH), :], o_sem.at[slot])

    for s in range(ahead):
        start(s)

    @pl.when(nused > 0)
    def _():
        for cp in weight_copies(eid_ref[0], 0):
            cp.start()

    def chunk(c, n_experts):
        e = eid_ref[c]
        first = (c == 0) | (e != eid_ref[jnp.maximum(c - 1, 0)])
        wslot = lax.rem(n_experts, 2)

        @pl.when(first)
        def _():
            for cp in weight_copies(e, wslot):
                cp.wait()
            wg_b[...] = wg_f[wslot].astype(bf16)
            wu_b[...] = wu_f[wslot].astype(bf16)
            wd_b[...] = wd_f[wslot].astype(bf16)

            @pl.when(nxt_ref[c] >= 0)
            def _():
                for cp in weight_copies(nxt_ref[c], 1 - wslot):
                    cp.start()

        slot = _wait_gather(c, UPC, src_ref, xc_hbm, stage, sem)
        oslot = lax.rem(c, 2)

        @pl.when(c >= 2)
        def _():
            out_copy(c - 2, oslot).wait()

        def block(r):
            xs = stage[slot, r:r + CHB, :]
            x = xs[:, :D]
            a = jnp.dot(x, wg_b[...], preferred_element_type=f32)
            u = jnp.dot(x, wu_b[...], preferred_element_type=f32)
            yield
            gb = xs[:, D:].astype(f32)
            is_first = gb[:, 6:7] == e.astype(f32)
            gate = jnp.where(is_first, gb[:, 2:3] + gb[:, 3:4], gb[:, 4:5] + gb[:, 5:6])
            act = (a * (1.0 / (1.0 + jnp.exp(-a)))) * u * gate
            obuf[oslot, r:r + CHB, :] = jnp.dot(act.astype(bf16), wd_b[...],
                                                preferred_element_type=f32).astype(bf16)

        blocks = [block(r) for r in range(0, CH, CHB)]
        for _ in range(2):
            for blk in blocks:
                next(blk, None)
        out_copy(c, oslot).start()
        start(c + ahead)
        return n_experts + first.astype(jnp.int32)

    lax.fori_loop(0, nused, chunk, jnp.int32(0))

    for s in range(ahead):
        _wait_gather(nused + s, UPC, src_ref, xc_hbm, stage, sem)
    for back in (1, 2):
        @pl.when(nused >= back)
        def _():
            out_copy(nused - back, lax.rem(nused - back, 2)).wait()

    obuf[0] = jnp.zeros((CH, D), bf16)

    def zero_start(c, carry):
        out_copy(c, 0).start()
        return carry

    def zero_wait(c, carry):
        out_copy(c, 0).wait()
        return carry

    lax.fori_loop(nused, NCH, zero_start, 0)
    lax.fori_loop(nused, NCH, zero_wait, 0)


def _ffn(layer, eid, nused, ffn_src, nxt, xc, wg, wu, wd):
    any_spec = pl.BlockSpec(memory_space=pl.ANY)
    return pl.pallas_call(
        functools.partial(_ffn_kernel, layer),
        grid_spec=pltpu.PrefetchScalarGridSpec(
            num_scalar_prefetch=4,
            grid=(1,),
            in_specs=[any_spec, any_spec, any_spec, any_spec],
            out_specs=any_spec,
            scratch_shapes=[
                pltpu.VMEM((NSLOT, CH, XW), bf16),
                pltpu.SemaphoreType.DMA((NSLOT,)),
                pltpu.VMEM((2, D, F), f32),
                pltpu.VMEM((2, D, F), f32),
                pltpu.VMEM((2, F, D), f32),
                pltpu.SemaphoreType.DMA((2,)),
                pltpu.VMEM((D, F), bf16),
                pltpu.VMEM((D, F), bf16),
                pltpu.VMEM((F, D), bf16),
                pltpu.VMEM((2, CH, D), bf16),
                pltpu.SemaphoreType.DMA((2,)),
            ],
        ),
        out_shape=jax.ShapeDtypeStruct((NCH * CH, D), bf16),
        compiler_params=_params("arbitrary"),
        name="ffn",
    )(eid, nused, ffn_src, nxt, xc, wg, wu, wd)


def _combine_tile(x, meta, rows):
    li = lax.broadcasted_iota(jnp.int32, (TM, RC), 1).astype(f32)
    pt = jnp.where((li == meta[:, 0:1]) | (li == meta[:, 1:2]), 1.0, 0.0).astype(bf16)
    return x + jnp.dot(pt, rows, preferred_element_type=f32)


def _combine_kernel(src_ref, x_ref, meta_ref, o_hbm, yp_ref, ys_ref, stage, sem):
    i = pl.program_id(0)
    tiles = T // TM
    _start_gathers(i, NT, tiles * UPT, src_ref, o_hbm, stage, sem)
    slot = _wait_gather(i, tiles * UPT, src_ref, o_hbm, stage, sem)
    ys = [_combine_tile(x_ref[k * TM:(k + 1) * TM, :], meta_ref[k * TM:(k + 1) * TM, :],
                        stage[slot, k * RC:(k + 1) * RC, :]) for k in range(tiles)]
    y = jnp.concatenate(ys, axis=0)

    @pl.when(i < NT - 1)
    def _():
        yp_ref[...] = y

    @pl.when(i == NT - 1)
    def _():
        ys_ref[...] = y


def _combine(comb_src, x, meta, o_sorted):
    out_specs = [pl.BlockSpec((T, D), lambda i, src: (jnp.minimum(i, NT - 2), 0)),
                 pl.BlockSpec((T, D), lambda i, src: (0, 0))]
    out_shape = [jax.ShapeDtypeStruct((NP, D), f32), jax.ShapeDtypeStruct((NS, D), f32)]
    return pl.pallas_call(
        _combine_kernel,
        grid_spec=pltpu.PrefetchScalarGridSpec(
            num_scalar_prefetch=1,
            grid=(NT,),
            in_specs=[
                pl.BlockSpec((T, D), lambda i, src: (i, 0)),
                pl.BlockSpec((T, LANES), lambda i, src: (i, 0)),
                pl.BlockSpec(memory_space=pl.ANY),
            ],
            out_specs=out_specs,
            scratch_shapes=[
                pltpu.VMEM((NSLOT, (T // TM) * RC, D), bf16),
                pltpu.SemaphoreType.DMA((NSLOT,)),
            ],
        ),
        out_shape=out_shape,
        compiler_params=_params("arbitrary"),
        name="combine",
    )(comb_src, x, meta, o_sorted)


def _conv_kernel(src_ref, x1_ref, meta_ref, o_hbm, g_ref, win_ref, cw_ref, wout_ref,
                 p1_ref, p2_ref, y_ref, cu_ref, pad_ref, x_ref, stage, sem):
    i = pl.program_id(0)

    @pl.when(i == 0)
    def _():
        pad_ref[0:8, :] = jnp.zeros((8, D), f32)

    tiles = T // TM
    ahead = NSLOT - 1

    def start(s):
        for cp in _unit_copies(src_ref, s * tiles * UPT, tiles * UPT, o_hbm, stage,
                               lax.rem(s, NSLOT), sem):
            cp.start()

    @pl.when(i == 0)
    def _():
        for s in range(ahead):
            start(s)

    slot = _wait_gather(i, tiles * UPT, src_ref, o_hbm, stage, sem)

    for k in range(tiles):
        rows = slice(k * TM, (k + 1) * TM)
        x_ref[rows, :] = _combine_tile(x1_ref[rows, :], meta_ref[rows, :],
                                       stage[slot, k * RC:(k + 1) * RC, :])

    blocks = [(r, r + T // 2) for r in (0, T // 2)]
    gates = []
    for r0, r1 in blocks:
        h = _rms(x_ref[r0:r1, :], g_ref[...]).astype(bf16)
        bcu = jnp.dot(h, win_ref[...], preferred_element_type=f32)
        gates.append(bcu[:, :D])
        cu = bcu[:, D:2 * D] * bcu[:, 2 * D:]
        pad_ref[8 + r0:8 + r1, :] = cu
        cu_ref[r0:r1, :] = cu
    t = lax.broadcasted_iota(jnp.int32, (T // 2, 1), 0) & (TS - 1)
    is_sample = i == NT - 1
    cw = cw_ref[...]
    for (r0, r1), b in zip(blocks, gates):
        m1 = jnp.where(is_sample & (t == 0), p1_ref[r0:r1, :], pad_ref[7 + r0:7 + r1, :])
        m2 = jnp.where(is_sample & (t < 2), p2_ref[r0:r1, :], pad_ref[6 + r0:6 + r1, :])
        conv = cw[0:1] * m2 + cw[1:2] * m1 + cw[2:3] * pad_ref[8 + r0:8 + r1, :]
        y = jnp.dot((b * conv).astype(bf16), wout_ref[...], preferred_element_type=f32)
        y_ref[r0:r1, :] = x_ref[r0:r1, :] + y
    pad_ref[0:8, :] = pad_ref[T:T + 8, :]
    start(i + ahead)

    @pl.when(i == NT - 1)
    def _():
        for s in range(ahead):
            _wait_gather(i + 1 + s, tiles * UPT, src_ref, o_hbm, stage, sem)


def _conv(comb_src, x1, meta, o_sorted, g, win, cw, wout, p1, p2):
    const = dict(pipeline_mode=pl.Buffered(1))
    comb_src = jnp.concatenate([
        comb_src, jnp.full(((NSLOT - 1) * (T // TM) * UPT,), ZERO_UNIT_OUT, jnp.int32)])
    return pl.pallas_call(
        _conv_kernel,
        grid_spec=pltpu.PrefetchScalarGridSpec(
            num_scalar_prefetch=1,
            grid=(NT,),
            in_specs=[
                pl.BlockSpec((T, D), lambda i, src: (i, 0)),
                pl.BlockSpec((T, LANES), lambda i, src: (i, 0)),
                pl.BlockSpec(memory_space=pl.ANY),
                pl.BlockSpec((1, D), lambda i, src: (0, 0)),
                pl.BlockSpec((D, 3 * D), lambda i, src: (0, 0), **const),
                pl.BlockSpec((3, D), lambda i, src: (0, 0)),
                pl.BlockSpec((D, D), lambda i, src: (0, 0), **const),
                pl.BlockSpec((T, D), lambda i, src: (0, 0), **const),
                pl.BlockSpec((T, D), lambda i, src: (0, 0), **const),
            ],
            out_specs=[
                pl.BlockSpec((T, D), lambda i, src: (i, 0)),
                pl.BlockSpec((T, D), lambda i, src: (jnp.where(i == NT - 1, 1, 0), 0)),
            ],
            scratch_shapes=[
                pltpu.VMEM((T + 8, D), f32),
                pltpu.VMEM((T, D), f32),
                pltpu.VMEM((NSLOT, (T // TM) * RC, D), bf16),
                pltpu.SemaphoreType.DMA((NSLOT,)),
            ],
        ),
        out_shape=[
            jax.ShapeDtypeStruct((N, D), f32),
            jax.ShapeDtypeStruct((2 * T, D), f32),
        ],
        compiler_params=_params("arbitrary"),
        name="conv",
    )(comb_src, x1, meta, o_sorted, g, win, cw, wout, p1, p2)


def _rope_tables():
    inv_freq = THETA ** (-jnp.arange(HALF, dtype=f32) / HALF)
    pos = jnp.concatenate([
        jnp.arange(NP, dtype=jnp.int32),
        PAST + jnp.tile(jnp.arange(TS, dtype=jnp.int32), NB),
    ]).astype(f32)
    ang = inv_freq[:, None] * pos[None, :]
    return jnp.cos(ang), jnp.sin(ang)


def _router_weights(w_group, b_group, w_router, b_router):
    gap = ROW_E - NGRP
    pad = LANES - ROW_E - NE
    w = jnp.concatenate([w_group, jnp.zeros((D, gap), f32), w_router,
                         jnp.zeros((D, pad), f32)], axis=1)
    b = jnp.concatenate([b_group, jnp.zeros((gap,), f32), b_router,
                         jnp.zeros((pad,), f32)])[None, :]
    return w.astype(bf16), b


def _moe_experts(x, i, norm_ffn, w_group, b_group, w_router, b_router, w_gate, w_up, w_down):
    w, b = _router_weights(w_group[i], b_group[i], w_router[i], b_router[i])
    xc, meta, cnt = _dispatch(x, norm_ffn[i][None, :], w, b)
    eid, nused, ffn_src, nxt, comb_src = _dispatch_tables(cnt)
    o_sorted = _ffn(i, eid, nused, ffn_src, nxt, xc, w_gate, w_up, w_down)
    return comb_src, meta, o_sorted


def kernel(x_prompt, x_sample, cache_k, cache_v, state_conv, norm_mix, w_qkv, q_norm, k_norm,
           sinks, w_o, w_in, conv_w, w_out, norm_ffn, w_group, b_group, w_router, b_router,
           w_gate, w_up, w_down):
    xp = x_prompt.reshape(NP, D)
    xs = x_sample.reshape(NS, D)
    moe_w = (norm_ffn, w_group, b_group, w_router, b_router, w_gate, w_up, w_down)

    cos, sin = _rope_tables()
    (qT, ktok, vtok, vT), (q_s, ktok_s, vtok_s) = _qkv(
        xp, xs, norm_mix[0][None, :], w_qkv[0].T.astype(bf16), q_norm[0][:, None],
        k_norm[0][:, None], cos, sin)
    sink_rows = jnp.repeat(sinks[0].reshape(KVH, G), TQ, axis=1)[:, None, :]

    qs = q_s.reshape(KVH, G, HD, NB, TS).transpose(3, 0, 1, 4, 2)
    zq = jnp.zeros_like(qs[:, 0])
    qbd = jnp.stack([jnp.concatenate([qs[:, 0], zq], axis=-1),
                     jnp.concatenate([zq, qs[:, 1]], axis=-1)], axis=1)
    qbd = qbd.reshape(NB, H * TS, KVH * HD)
    k_new = ktok_s.reshape(NB, TS, KVH * HD)
    v_new = vtok_s.reshape(NB, TS, KVH * HD)
    pad4 = jnp.zeros((NB, 8 - TS, KVH * HD), f32)
    kc = cache_k[0].reshape(NB, WIN, KVH * HD)
    vc = cache_v[0].reshape(NB, WIN, KVH * HD)
    sink_col = jnp.repeat(sinks[0], TS)[:, None]
    o_s, kc_new, vc_new = _attn_sample(qbd, kc, vc, jnp.concatenate([k_new, pad4], axis=1),
                                       jnp.concatenate([v_new, pad4], axis=1), sink_col)
    o_s = o_s.reshape(NB, KVH, G, TS, KVH, HD)
    o_s = jnp.stack([o_s[:, 0, :, :, 0], o_s[:, 1, :, :, 1]], axis=1)
    o_s = o_s.transpose(0, 3, 1, 2, 4).reshape(NS, H * HD).astype(bf16)
    x = _attn_prompt(qT, ktok, vT, sink_rows, xp, xs, o_s, w_o[0].astype(bf16))
    comb_src, meta, o_sorted = _moe_experts(x, 0, *moe_w)

    new_k_prompt = ktok[NP - WIN:NP].reshape(1, 1, WIN, KVH, HD)
    new_v_prompt = vtok[NP - WIN:NP].reshape(1, 1, WIN, KVH, HD)
    new_k_sample = kc_new.reshape(1, NB, WIN, KVH, HD)
    new_v_sample = vc_new.reshape(1, NB, WIN, KVH, HD)

    st = state_conv[0]
    z = jnp.zeros((NB, 1, D), f32)
    p1 = jnp.concatenate([st[:, 1:2], z, z, z], axis=1).reshape(NS, D)
    p2 = jnp.concatenate([st[:, 0:1], st[:, 1:2], z, z], axis=1).reshape(NS, D)
    x, cu = _conv(comb_src, x, meta, o_sorted, norm_mix[1][None, :], w_in[0].astype(bf16),
                  conv_w[0], w_out[0].astype(bf16), p1, p2)
    comb_src, meta, o_sorted = _moe_experts(x, 1, *moe_w)
    y_prompt, y_sample = _combine(comb_src, x, meta, o_sorted)

    new_conv_prompt = cu[T - 2:T].reshape(1, 1, 2, D)
    new_conv_sample = cu[T:].reshape(NB, TS, D)[:, TS - 2:][None]

    y_prompt = y_prompt.reshape(1, NP, D)
    y_sample = y_sample.reshape(NB, TS, D)
    return (y_prompt, y_sample, new_k_prompt, new_v_prompt, new_conv_prompt,
            new_k_sample, new_v_sample, new_conv_sample)
```

```python
import functools

import jax
import jax.numpy as jnp
import numpy as np
from jax import lax
from jax.experimental import pallas as pl
from jax.experimental.pallas import tpu as pltpu

D = 1024
NP = 16384
NB = 128
TS = 4
NS = NB * TS
N = NP + NS
PAST = 16384
H = 16
KVH = 2
G = H // KVH
HD = 64
HALF = HD // 2
QKV = (H + 2 * KVH) * HD
WIN = 128
THETA = 10000.0
NGRP = 4
EPG = 4
NE = NGRP * EPG
TOPK = 2
ROW_E = 8
F = 512
EPS = 1e-6
SCALE = HD ** -0.5

T = 512
NT = N // T
TQ = 128
NQ = NP // TQ
QB = 4
BB = 16
LANES = 128
V7X_VMEM_BYTES = 64 * 1024 * 1024
VMEM_LIMIT = V7X_VMEM_BYTES - 14 * 1024 * 1024

TM = 256
NTM = N // TM
DT = 6
XSLOT = 3
UNIT = 16
RC = TOPK * TM + NE * UNIT
UPT = RC // UNIT
XW = D + LANES
CH = 512
CHB = 256
UPC = CH // UNIT
NSLOT = 4
NCH = -(-(NTM * (UPT - 1) + NE * (UPC - 1)) // UPC) + NSLOT - 1
ZERO_UNIT_IN = UPT - 1
ZERO_UNIT_OUT = (NCH - 1) * UPC

f32 = jnp.float32
bf16 = jnp.bfloat16


def _params(*sem):
    return pltpu.CompilerParams(dimension_semantics=sem, vmem_limit_bytes=VMEM_LIMIT)


def _rms(x, g):
    ms = jnp.mean(x * x, axis=-1, keepdims=True)
    return x * lax.rsqrt(ms + EPS) * g


def _qkv_prompt_kernel(x_ref, g_ref, wT_ref, qn_ref, kn_ref, cos_ref, sin_ref,
                       qT_ref, ktok_ref, vtok_ref, vT_ref):
    _qkv_tile(x_ref, g_ref, wT_ref, qn_ref, kn_ref, cos_ref, sin_ref,
              qT_ref, ktok_ref, vtok_ref, vT_ref)


def _qkv_sample_kernel(x_ref, g_ref, wT_ref, qn_ref, kn_ref, cos_ref, sin_ref,
                       q_ref, ktok_ref, vtok_ref):
    _qkv_tile(x_ref, g_ref, wT_ref, qn_ref, kn_ref, cos_ref, sin_ref,
              q_ref, ktok_ref, vtok_ref, None)


def _qkv_tile(x_ref, g_ref, wT_ref, qn_ref, kn_ref, cos_ref, sin_ref,
              q_ref, ktok_ref, vtok_ref, vT_ref):
    h = _rms(x_ref[...], g_ref[...]).astype(bf16)
    qkvT = lax.dot_general(wT_ref[...], h, (((1,), (1,)), ((), ())),
                           preferred_element_type=f32)
    cos = cos_ref[...]
    sin = sin_ref[...]

    def norm_rope(blk, gcol):
        ms = jnp.mean(blk * blk, axis=0, keepdims=True)
        y = blk * lax.rsqrt(ms + EPS) * gcol
        y1 = y[:HALF]
        y2 = y[HALF:]
        return y1 * cos - y2 * sin, y2 * cos + y1 * sin

    qn = qn_ref[...]
    for hd in range(H):
        o1, o2 = norm_rope(qkvT[hd * HD:(hd + 1) * HD], qn)
        q_ref[hd * HD:hd * HD + HALF, :] = (o1 * SCALE).astype(bf16)
        q_ref[hd * HD + HALF:(hd + 1) * HD, :] = (o2 * SCALE).astype(bf16)
    kn = kn_ref[...]
    ks = []
    for j in range(KVH):
        o1, o2 = norm_rope(qkvT[H * HD + j * HD:H * HD + (j + 1) * HD], kn)
        ks += [o1, o2]
    kT = jnp.concatenate(ks, axis=0)
    ktok_ref[...] = kT.T
    vT = qkvT[(H + KVH) * HD:]
    vtok_ref[...] = vT.T
    if vT_ref is not None:
        vT_ref[...] = vT.astype(bf16)


def _qkv(xp, xs, g, wT, qn, kn, cos, sin):
    const = [pl.BlockSpec((HD, 1), lambda i: (0, 0)), pl.BlockSpec((HD, 1), lambda i: (0, 0)),
             pl.BlockSpec((HALF, T), lambda i: (0, i)), pl.BlockSpec((HALF, T), lambda i: (0, i))]
    x_spec = pl.BlockSpec((T, D), lambda i: (i, 0))
    g_spec = pl.BlockSpec((1, D), lambda i: (0, 0))
    w_spec = pl.BlockSpec((QKV, D), lambda i: (0, 0))
    tok_spec = pl.BlockSpec((T, KVH * HD), lambda i: (i, 0))
    nt = NP // T
    prompt = pl.pallas_call(
        _qkv_prompt_kernel,
        grid=(nt,),
        in_specs=[x_spec, g_spec, w_spec] + const,
        out_specs=[pl.BlockSpec((None, H * HD, T), lambda i: (i, 0, 0)), tok_spec, tok_spec,
                   pl.BlockSpec((None, KVH * HD, T), lambda i: (i, 0, 0))],
        out_shape=[jax.ShapeDtypeStruct((nt, H * HD, T), bf16),
                   jax.ShapeDtypeStruct((NP, KVH * HD), f32),
                   jax.ShapeDtypeStruct((NP, KVH * HD), f32),
                   jax.ShapeDtypeStruct((nt, KVH * HD, T), bf16)],
        compiler_params=_params("parallel"),
        name="qkv",
    )(xp, g, wT, qn, kn, cos[:, :NP], sin[:, :NP])
    sample = pl.pallas_call(
        _qkv_sample_kernel,
        grid=(NS // T,),
        in_specs=[x_spec, g_spec, w_spec] + const,
        out_specs=[pl.BlockSpec((H * HD, T), lambda i: (0, i)), tok_spec, tok_spec],
        out_shape=[jax.ShapeDtypeStruct((H * HD, NS), bf16),
                   jax.ShapeDtypeStruct((NS, KVH * HD), f32),
                   jax.ShapeDtypeStruct((NS, KVH * HD), f32)],
        compiler_params=_params("parallel"),
        name="qkv_sample",
    )(xs, g, wT, qn, kn, cos[:, NP:], sin[:, NP:])
    return prompt, sample


def _attn_prompt_kernel(qT_ref, kp_ref, kc_ref, vp_ref, vc_ref, sink_ref, bias0_ref, bias_ref,
                        xp_ref, xs_ref, os_ref, wo_f32_ref, y_ref, wo_ref):
    j = pl.program_id(0)

    @pl.when(j == 0)
    def _():
        wo_ref[...] = wo_f32_ref[...].astype(bf16)

    @pl.when(j >= NQ // QB)
    def _():
        y_ref[...] = xs_ref[...] + jnp.dot(os_ref[...], wo_ref[...],
                                           preferred_element_type=f32)

    @pl.when(j < NQ // QB)
    def _():
        k_all = jnp.concatenate([kp_ref[...], kc_ref[...]], axis=0)
        v_all = jnp.concatenate([vp_ref[...], vc_ref[...]], axis=1)
        blocks = []
        for k in range(QB):
            rows = slice(k * TQ, (k + 1) * TQ)
            blocks.append(_attend_block(
                qT_ref.at[:, rows], k_all[k * TQ:(k + 2) * TQ], v_all[:, k * TQ:(k + 2) * TQ],
                sink_ref, bias0_ref if k == 0 else bias_ref, xp_ref.at[rows], wo_ref,
                y_ref.at[rows]))
        for _ in range(3):
            for blk in blocks:
                next(blk, None)


def _attend_block(qT_ref, kk, vv, sink_ref, bias_ref, x_ref, wo_ref, y_ref):
    kk = kk.astype(bf16)
    bias = jnp.concatenate([bias_ref[...]] * H, axis=1)
    qg = [jnp.concatenate([qT_ref[(g * G + hh) * HD:(g * G + hh + 1) * HD, :]
                           for hh in range(G)], axis=1) for g in range(KVH)]
    zeros = jnp.zeros_like(qg[0])
    rhs = jnp.concatenate([jnp.concatenate([qg[0], zeros], axis=1),
                           jnp.concatenate([zeros, qg[1]], axis=1)], axis=0)
    sT = jnp.dot(kk, rhs, preferred_element_type=f32) + bias
    yield
    sink = jnp.concatenate([sink_ref[0], sink_ref[1]], axis=1)
    m = jnp.maximum(jnp.max(sT, axis=0, keepdims=True), sink)
    p = jnp.exp(sT - m)
    l = jnp.sum(p, axis=0, keepdims=True) + jnp.exp(sink - m)
    p = (p * (1.0 / l)).astype(bf16)
    yield
    pieces = []
    for g in range(KVH):
        oT = jnp.dot(vv[g * HD:(g + 1) * HD, :], p[:, g * G * TQ:(g + 1) * G * TQ],
                     preferred_element_type=f32)
        pieces += [oT[:, hh * TQ:(hh + 1) * TQ] for hh in range(G)]
    oT_all = jnp.concatenate(pieces, axis=0)
    y_ref[...] = x_ref[...] + jnp.dot(oT_all.T.astype(bf16), wo_ref[...],
                                      preferred_element_type=f32)


def _band_bias():
    s = np.arange(2 * TQ)[:, None]
    t = np.arange(TQ)[None, :]
    dist = t + TQ - s
    band = (dist >= 0) & (dist <= WIN)
    first = band & (s >= TQ)
    return jnp.asarray(np.where(np.stack([first, band]), 0.0, -np.inf), f32)


def _attn_prompt(qT, ktok, vT, sink_rows, xp, xs, o_s, wo):
    steps = NQ // QB
    rows = QB * TQ
    assert rows == T

    def cur(j):
        return jnp.minimum(j, steps - 1)

    def prev(j):
        return jnp.maximum(cur(j) * QB - 1, 0)

    def prev_tile(j):
        return jnp.maximum(cur(j) - 1, 0)

    def sample(j):
        return jnp.maximum(j - steps, 0)

    bias = _band_bias()
    return pl.pallas_call(
        _attn_prompt_kernel,
        grid=(N // rows,),
        in_specs=[
            pl.BlockSpec((None, H * HD, rows), lambda j: (cur(j), 0, 0)),
            pl.BlockSpec((TQ, KVH * HD), lambda j: (prev(j), 0)),
            pl.BlockSpec((rows, KVH * HD), lambda j: (cur(j), 0)),
            pl.BlockSpec((None, KVH * HD, TQ), lambda j: (prev_tile(j), 0, QB - 1)),
            pl.BlockSpec((None, KVH * HD, rows), lambda j: (cur(j), 0, 0)),
            pl.BlockSpec((KVH, 1, G * TQ), lambda j: (0, 0, 0)),
            pl.BlockSpec((None, 2 * TQ, TQ), lambda j: (jnp.minimum(j, 1), 0, 0)),
            pl.BlockSpec((None, 2 * TQ, TQ), lambda j: (1, 0, 0)),
            pl.BlockSpec((rows, D), lambda j: (cur(j), 0)),
            pl.BlockSpec((rows, D), lambda j: (sample(j), 0)),
            pl.BlockSpec((rows, H * HD), lambda j: (sample(j), 0)),
            pl.BlockSpec((D, D), lambda j: (0, 0), pipeline_mode=pl.Buffered(1)),
        ],
        out_specs=pl.BlockSpec((rows, D), lambda j: (j, 0)),
        out_shape=jax.ShapeDtypeStruct((N, D), f32),
        scratch_shapes=[pltpu.VMEM((D, D), bf16)],
        compiler_params=_params("arbitrary"),
        name="attn_prompt",
    )(qT, ktok, ktok, vT, vT, sink_rows, bias, bias, xp, xs, o_s, wo)


def _attn_sample_kernel(q_ref, kc_ref, vc_ref, kn_ref, vn_ref, sink_ref,
                        o_ref, knew_ref, vnew_ref):
    rows = H * TS
    t1 = lax.broadcasted_iota(jnp.int32, (1, rows, WIN), 1) & (TS - 1)
    s1 = lax.broadcasted_iota(jnp.int32, (1, rows, WIN), 2)
    valid1 = s1 >= t1
    t2 = lax.broadcasted_iota(jnp.int32, (1, rows, 8), 1) & (TS - 1)
    s2 = lax.broadcasted_iota(jnp.int32, (1, rows, 8), 2)
    valid2 = s2 <= t2
    sink = sink_ref[...][None]
    q = q_ref[...]
    kc = kc_ref[...]
    vc = vc_ref[...]
    kn = kn_ref[...]
    vn = vn_ref[...]
    sc = jnp.einsum('bqd,bkd->bqk', q, kc.astype(bf16), preferred_element_type=f32)
    sn = jnp.einsum('bqd,bkd->bqk', q, kn.astype(bf16), preferred_element_type=f32)
    sc = jnp.where(valid1, sc, -jnp.inf)
    sn = jnp.where(valid2, sn, -jnp.inf)
    m = jnp.maximum(jnp.maximum(jnp.max(sc, axis=-1, keepdims=True),
                                jnp.max(sn, axis=-1, keepdims=True)), sink)
    pc = jnp.exp(sc - m)
    pn = jnp.exp(sn - m)
    l = (jnp.sum(pc, axis=-1, keepdims=True) + jnp.sum(pn, axis=-1, keepdims=True)
         + jnp.exp(sink - m))
    r = 1.0 / l
    o_ref[...] = (jnp.einsum('bqk,bkd->bqd', (pc * r).astype(bf16), vc.astype(bf16),
                             preferred_element_type=f32)
                  + jnp.einsum('bqk,bkd->bqd', (pn * r).astype(bf16), vn.astype(bf16),
                               preferred_element_type=f32))
    knew_ref[:, :WIN - TS, :] = kc[:, TS:, :]
    knew_ref[:, WIN - TS:, :] = kn[:, :TS, :]
    vnew_ref[:, :WIN - TS, :] = vc[:, TS:, :]
    vnew_ref[:, WIN - TS:, :] = vn[:, :TS, :]


def _attn_sample(qbd, kc, vc, kn, vn, sink_col):
    rows = H * TS
    cache_spec = pl.BlockSpec((BB, WIN, KVH * HD), lambda i: (i, 0, 0))
    new_spec = pl.BlockSpec((BB, 8, KVH * HD), lambda i: (i, 0, 0))
    cache_shape = jax.ShapeDtypeStruct((NB, WIN, KVH * HD), f32)
    return pl.pallas_call(
        _attn_sample_kernel,
        grid=(NB // BB,),
        in_specs=[
            pl.BlockSpec((BB, rows, KVH * HD), lambda i: (i, 0, 0)),
            cache_spec, cache_spec, new_spec, new_spec,
            pl.BlockSpec((rows, 1), lambda i: (0, 0)),
        ],
        out_specs=[pl.BlockSpec((BB, rows, KVH * HD), lambda i: (i, 0, 0)),
                   cache_spec, cache_spec],
        out_shape=[jax.ShapeDtypeStruct((NB, rows, KVH * HD), f32), cache_shape, cache_shape],
        compiler_params=_params("parallel"),
        name="attn_sample",
    )(qbd, kc, vc, kn, vn, sink_col)


def _dispatch_kernel(x_hbm, g_ref, w_ref, b_ref, upper_ref, xc_ref, meta_ref, cnt_ref,
                     xbuf, xsem):
    i = pl.program_id(0)
    steps = NTM // DT
    rows = DT * TM

    def x_copy(s):
        sl = lax.rem(s, XSLOT)
        return pltpu.make_async_copy(x_hbm.at[pl.ds(pl.multiple_of(s * rows, rows), rows), :],
                                     xbuf.at[sl], xsem.at[sl])

    for s in range(XSLOT - 1):
        @pl.when(i == 0)
        def _():
            x_copy(s).start()

    @pl.when(i + XSLOT - 1 < steps)
    def _():
        x_copy(i + XSLOT - 1).start()

    x_copy(i).wait()
    x_ref = xbuf.at[lax.rem(i, XSLOT)]
    tiles = [_dispatch_tile(x_ref.at[k * TM:(k + 1) * TM], g_ref, w_ref, b_ref, upper_ref,
                            xc_ref.at[k * RC:(k + 1) * RC], meta_ref.at[k * TM:(k + 1) * TM],
                            cnt_ref.at[k * NE:(k + 1) * NE]) for k in range(DT)]
    for _ in range(3):
        for t in tiles:
            next(t, None)


def _dispatch_tile(x_ref, g_ref, w_ref, b_ref, upper_ref, xc_ref, meta_ref, cnt_ref):
    h_hi = _rms(x_ref[...], g_ref[...]).astype(bf16)
    logits = jnp.dot(h_hi, w_ref[...], preferred_element_type=f32) + b_ref[...]
    yield
    lt = logits.T
    inf = jnp.inf
    row8 = lax.broadcasted_iota(jnp.int32, (8, TM), 0).astype(f32)
    gl = jnp.where(row8 < NGRP, lt[0:8], -inf)
    gmax = jnp.max(gl, axis=0, keepdims=True)
    gsel = jnp.min(jnp.where(gl == gmax, row8, 8.0), axis=0, keepdims=True)
    g_w = 1.0 / jnp.sum(jnp.exp(gl - gmax), axis=0, keepdims=True)
    row = lax.broadcasted_iota(jnp.int32, (NE, TM), 0)
    row_f = row.astype(f32)
    el = jnp.where((row >> 2).astype(f32) == gsel, lt[ROW_E:ROW_E + NE], -inf)
    v1 = jnp.max(el, axis=0, keepdims=True)
    i1 = jnp.min(jnp.where(el == v1, row_f, float(NE)), axis=0, keepdims=True)
    el2 = jnp.where(row_f == i1, -inf, el)
    v2 = jnp.max(el2, axis=0, keepdims=True)
    i2 = jnp.min(jnp.where(el2 == v2, row_f, float(NE)), axis=0, keepdims=True)
    e1 = jnp.exp(v2 - v1)
    den = 1.0 + e1
    w1 = (1.0 / den) * g_w
    w2 = (e1 / den) * g_w

    m1 = row_f == i1
    m2 = row_f == i2
    sel = jnp.where(m1 | m2, 1.0, 0.0)
    ranks = jnp.dot(sel.astype(bf16), upper_ref[...],
                    preferred_element_type=f32)
    counts = jnp.sum(sel, axis=1, keepdims=True)
    padded = jnp.floor((counts + (UNIT - 1.0)) * (1.0 / UNIT)) * UNIT
    e_i = lax.broadcasted_iota(jnp.int32, (NE, NE), 0)
    f_i = lax.broadcasted_iota(jnp.int32, (NE, NE), 1)
    below = jnp.where(f_i < e_i, 1.0, 0.0).astype(bf16)
    seg = jnp.dot(below, jnp.broadcast_to(padded, (NE, LANES)).astype(bf16),
                  preferred_element_type=f32)[:, 0:1]
    posall = seg + ranks
    pos1 = jnp.sum(jnp.where(m1, posall, 0.0), axis=0, keepdims=True)
    pos2 = jnp.sum(jnp.where(m2, posall, 0.0), axis=0, keepdims=True)
    cnt_ref[...] = jnp.broadcast_to(counts, (NE, LANES))

    w1_hi = w1.astype(bf16).astype(f32)
    w2_hi = w2.astype(bf16).astype(f32)
    slab = jnp.zeros((8, TM), f32)
    for k, r in enumerate((pos1, pos2, w1_hi, w1 - w1_hi, w2_hi, w2 - w2_hi, i1)):
        slab = jnp.where(row8 == k, r, slab)
    meta = jnp.concatenate([slab, jnp.zeros((LANES - 8, TM), f32)], axis=0).T
    meta_ref[...] = meta
    yield

    rr = lax.broadcasted_iota(jnp.int32, (RC, TM), 0).astype(f32)
    onehot = jnp.where((rr == pos1) | (rr == pos2), 1.0, 0.0).astype(bf16)
    h_aug = jnp.concatenate([h_hi, meta.astype(bf16)], axis=1)
    xc_ref[...] = jnp.dot(onehot, h_aug, preferred_element_type=f32).astype(bf16)


def _dispatch(x, g, w, b):
    upper = jnp.asarray(np.triu(np.ones((TM, TM), np.float32), 1), bf16)
    return pl.pallas_call(
        _dispatch_kernel,
        grid=(NTM // DT,),
        in_specs=[
            pl.BlockSpec(memory_space=pl.ANY),
            pl.BlockSpec((1, D), lambda i: (0, 0)),
            pl.BlockSpec((D, LANES), lambda i: (0, 0)),
            pl.BlockSpec((1, LANES), lambda i: (0, 0)),
            pl.BlockSpec((TM, TM), lambda i: (0, 0)),
        ],
        out_specs=[
            pl.BlockSpec((DT * RC, XW), lambda i: (i, 0)),
            pl.BlockSpec((DT * TM, LANES), lambda i: (i, 0)),
            pl.BlockSpec((DT * NE, LANES), lambda i: (i, 0)),
        ],
        out_shape=[
            jax.ShapeDtypeStruct((NTM * RC, XW), bf16),
            jax.ShapeDtypeStruct((N, LANES), f32),
            jax.ShapeDtypeStruct((NTM * NE, LANES), f32),
        ],
        scratch_shapes=[pltpu.VMEM((XSLOT, DT * TM, D), f32),
                        pltpu.SemaphoreType.DMA((XSLOT,))],
        compiler_params=_params("arbitrary"),
        name="dispatch",
    )(x, g, w, b, upper)


def _dispatch_tables(cnt):
    i32 = jnp.int32
    n = cnt.reshape(NTM, NE, LANES)[:, :, 0].astype(i32)
    units = (n + UNIT - 1) // UNIT
    seg_end = jnp.cumsum(units, axis=1)
    seg_start = seg_end - units
    col_end = jnp.cumsum(units, axis=0)
    col_start = col_end - units
    chunks = (col_end[-1] + UPC - 1) // UPC
    ch_end = jnp.cumsum(chunks)
    ch_start = ch_end - chunks
    nused = ch_end[-1]
    c = jnp.arange(NCH, dtype=i32)
    eid = jnp.minimum(jnp.sum((ch_end[None, :] <= c[:, None]).astype(i32), axis=1), NE - 1)

    src0 = jnp.arange(NTM, dtype=i32)[:, None] * UPT + seg_start
    dst0 = ch_start[None, :] * UPC + col_start
    k = jnp.arange((NCH + NSLOT - 1) * UPC, dtype=i32)[:, None, None]
    inside = (k >= dst0[None]) & (k < (dst0 + units)[None])
    found = jnp.sum(inside.astype(i32), axis=(1, 2))
    shift = jnp.sum(jnp.where(inside, (src0 - dst0)[None], 0), axis=(1, 2))
    ffn_src = jnp.where(found > 0, k[:, 0, 0] + shift, ZERO_UNIT_IN)

    v = jnp.arange(UPT, dtype=i32)[None, :, None]
    inside_v = (v >= seg_start[:, None, :]) & (v < seg_end[:, None, :])
    found_v = jnp.sum(inside_v.astype(i32), axis=2)
    shift_v = jnp.sum(jnp.where(inside_v, (dst0 - seg_start)[:, None, :], 0), axis=2)
    comb_src = jnp.where(found_v > 0, v[:, :, 0] + shift_v, ZERO_UNIT_OUT).reshape(-1)
    own_end = jnp.sum(jnp.where(eid[:, None] == jnp.arange(NE, dtype=i32)[None, :],
                                ch_end[None, :], 0), axis=1)
    eid_at = jnp.sum(jnp.where(c[None, :] == own_end[:, None], eid[None, :], 0), axis=1)
    nxt = jnp.where(own_end < nused, eid_at, -1)
    return eid, nused.reshape(1), ffn_src, nxt, comb_src


def _unit_copies(src_ref, base, n_units, src_hbm, stage, slot, sem):
    out = []
    for j in range(n_units):
        row = pl.multiple_of(src_ref[base + j] * UNIT, UNIT)
        out.append(pltpu.make_async_copy(
            src_hbm.at[pl.ds(row, UNIT), :],
            stage.at[slot, pl.ds(j * UNIT, UNIT), :],
            sem.at[slot]))
    return out


def _start_gathers(step, n_steps, n_units, src_ref, src_hbm, stage, sem):
    ahead = NSLOT - 1

    def start(s, slot):
        for cp in _unit_copies(src_ref, s * n_units, n_units, src_hbm, stage, slot, sem):
            cp.start()

    for s in range(ahead):
        @pl.when((step == 0) & (s < n_steps))
        def _():
            start(s, s)

    @pl.when(step + ahead < n_steps)
    def _():
        start(step + ahead, lax.rem(step + ahead, NSLOT))


def _wait_gather(step, n_units, src_ref, src_hbm, stage, sem):
    slot = lax.rem(step, NSLOT)
    for cp in _unit_copies(src_ref, step * n_units, n_units, src_hbm, stage, slot, sem):
        cp.wait()
    return slot


def _ffn_kernel(layer, eid_ref, nused_ref, src_ref, nxt_ref, xc_hbm, wg_hbm, wu_hbm, wd_hbm,
                o_hbm, stage, sem, wg_f, wu_f, wd_f, w_sem, wg_b, wu_b, wd_b, obuf, o_sem):
    nused = nused_ref[0]
    ahead = NSLOT - 1

    def start(s):
        for cp in _unit_copies(src_ref, s * UPC, UPC, xc_hbm, stage, lax.rem(s, NSLOT), sem):
            cp.start()

    def weight_copies(e, slot):
        return [pltpu.make_async_copy(hbm.at[layer, e], buf.at[slot], w_sem.at[slot])
                for hbm, buf in ((wg_hbm, wg_f), (wu_hbm, wu_f), (wd_hbm, wd_f))]

    def out_copy(c, slot):
        row = pl.multiple_of(c * CH, CH)
        return pltpu.make_async_copy(obuf.at[slot], o_hbm.at[pl.ds(row, CH), :], o_sem.at[slot])

    for s in range(ahead):
        start(s)

    @pl.when(nused > 0)
    def _():
        for cp in weight_copies(eid_ref[0], 0):
            cp.start()

    def chunk(c, n_experts):
        e = eid_ref[c]
        first = (c == 0) | (e != eid_ref[jnp.maximum(c - 1, 0)])
        wslot = lax.rem(n_experts, 2)

        @pl.when(first)
        def _():
            for cp in weight_copies(e, wslot):
                cp.wait()
            wg_b[...] = wg_f[wslot].astype(bf16)
            wu_b[...] = wu_f[wslot].astype(bf16)
            wd_b[...] = wd_f[wslot].astype(bf16)

            @pl.when(nxt_ref[c] >= 0)
            def _():
                for cp in weight_copies(nxt_ref[c], 1 - wslot):
                    cp.start()

        slot = _wait_gather(c, UPC, src_ref, xc_hbm, stage, sem)
        oslot = lax.rem(c, 2)

        @pl.when(c >= 2)
        def _():
            out_copy(c - 2, oslot).wait()

        def block(r):
            xs = stage[slot, r:r + CHB, :]
            x = xs[:, :D]
            a = jnp.dot(x, wg_b[...], preferred_element_type=f32)
            u = jnp.dot(x, wu_b[...], preferred_element_type=f32)
            yield
            gb = xs[:, D:].astype(f32)
            is_first = gb[:, 6:7] == e.astype(f32)
            gate = jnp.where(is_first, gb[:, 2:3] + gb[:, 3:4], gb[:, 4:5] + gb[:, 5:6])
            act = (a * (1.0 / (1.0 + jnp.exp(-a)))) * u * gate
            obuf[oslot, r:r + CHB, :] = jnp.dot(act.astype(bf16), wd_b[...],
                                                preferred_element_type=f32).astype(bf16)

        blocks = [block(r) for r in range(0, CH, CHB)]
        for _ in range(2):
            for blk in blocks:
                next(blk, None)
        out_copy(c, oslot).start()
        start(c + ahead)
        return n_experts + first.astype(jnp.int32)

    lax.fori_loop(0, nused, chunk, jnp.int32(0))

    for s in range(ahead):
        _wait_gather(nused + s, UPC, src_ref, xc_hbm, stage, sem)
    for back in (1, 2):
        @pl.when(nused >= back)
        def _():
            out_copy(nused - back, lax.rem(nused - back, 2)).wait()

    obuf[0] = jnp.zeros((CH, D), bf16)

    def zero_start(c, carry):
        out_copy(c, 0).start()
        return carry

    def zero_wait(c, carry):
        out_copy(c, 0).wait()
        return carry

    lax.fori_loop(nused, NCH, zero_start, 0)
    lax.fori_loop(nused, NCH, zero_wait, 0)


def _ffn(layer, eid, nused, ffn_src, nxt, xc, wg, wu, wd):
    any_spec = pl.BlockSpec(memory_space=pl.ANY)
    return pl.pallas_call(
        functools.partial(_ffn_kernel, layer),
        grid_spec=pltpu.PrefetchScalarGridSpec(
            num_scalar_prefetch=4,
            grid=(1,),
            in_specs=[any_spec, any_spec, any_spec, any_spec],
            out_specs=any_spec,
            scratch_shapes=[
                pltpu.VMEM((NSLOT, CH, XW), bf16),
                pltpu.SemaphoreType.DMA((NSLOT,)),
                pltpu.VMEM((2, D, F), f32),
                pltpu.VMEM((2, D, F), f32),
                pltpu.VMEM((2, F, D), f32),
                pltpu.SemaphoreType.DMA((2,)),
                pltpu.VMEM((D, F), bf16),
                pltpu.VMEM((D, F), bf16),
                pltpu.VMEM((F, D), bf16),
                pltpu.VMEM((2, CH, D), bf16),
                pltpu.SemaphoreType.DMA((2,)),
            ],
        ),
        out_shape=jax.ShapeDtypeStruct((NCH * CH, D), bf16),
        compiler_params=_params("arbitrary"),
        name="ffn",
    )(eid, nused, ffn_src, nxt, xc, wg, wu, wd)


def _combine_tile(x, meta, rows):
    li = lax.broadcasted_iota(jnp.int32, (TM, RC), 1).astype(f32)
    pt = jnp.where((li == meta[:, 0:1]) | (li == meta[:, 1:2]), 1.0, 0.0).astype(bf16)
    return x + jnp.dot(pt, rows, preferred_element_type=f32)


def _combine_kernel(src_ref, x_ref, meta_ref, o_hbm, yp_ref, ys_ref, stage, sem):
    i = pl.program_id(0)
    tiles = T // TM
    _start_gathers(i, NT, tiles * UPT, src_ref, o_hbm, stage, sem)
    slot = _wait_gather(i, tiles * UPT, src_ref, o_hbm, stage, sem)
    ys = [_combine_tile(x_ref[k * TM:(k + 1) * TM, :], meta_ref[k * TM:(k + 1) * TM, :],
                        stage[slot, k * RC:(k + 1) * RC, :]) for k in range(tiles)]
    y = jnp.concatenate(ys, axis=0)

    @pl.when(i < NT - 1)
    def _():
        yp_ref[...] = y

    @pl.when(i == NT - 1)
    def _():
        ys_ref[...] = y


def _combine(comb_src, x, meta, o_sorted):
    out_specs = [pl.BlockSpec((T, D), lambda i, src: (jnp.minimum(i, NT - 2), 0)),
                 pl.BlockSpec((T, D), lambda i, src: (0, 0))]
    out_shape = [jax.ShapeDtypeStruct((NP, D), f32), jax.ShapeDtypeStruct((NS, D), f32)]
    return pl.pallas_call(
        _combine_kernel,
        grid_spec=pltpu.PrefetchScalarGridSpec(
            num_scalar_prefetch=1,
            grid=(NT,),
            in_specs=[
                pl.BlockSpec((T, D), lambda i, src: (i, 0)),
                pl.BlockSpec((T, LANES), lambda i, src: (i, 0)),
                pl.BlockSpec(memory_space=pl.ANY),
            ],
            out_specs=out_specs,
            scratch_shapes=[
                pltpu.VMEM((NSLOT, (T // TM) * RC, D), bf16),
                pltpu.SemaphoreType.DMA((NSLOT,)),
            ],
        ),
        out_shape=out_shape,
        compiler_params=_params("arbitrary"),
        name="combine",
    )(comb_src, x, meta, o_sorted)


def _conv_kernel(src_ref, x1_ref, meta_ref, o_hbm, g_ref, win_ref, cw_ref, wout_ref,
                 p1_ref, p2_ref, y_ref, cu_ref, pad_ref, x_ref, stage, sem):
    i = pl.program_id(0)

    @pl.when(i == 0)
    def _():
        pad_ref[0:8, :] = jnp.zeros((8, D), f32)

    tiles = T // TM
    ahead = NSLOT - 1

    def start(s):
        for cp in _unit_copies(src_ref, s * tiles * UPT, tiles * UPT, o_hbm, stage,
                               lax.rem(s, NSLOT), sem):
            cp.start()

    @pl.when(i == 0)
    def _():
        for s in range(ahead):
            start(s)

    slot = _wait_gather(i, tiles * UPT, src_ref, o_hbm, stage, sem)

    for k in range(tiles):
        rows = slice(k * TM, (k + 1) * TM)
        x_ref[rows, :] = _combine_tile(x1_ref[rows, :], meta_ref[rows, :],
                                       stage[slot, k * RC:(k + 1) * RC, :])

    blocks = [(r, r + T // 2) for r in (0, T // 2)]
    gates = []
    for r0, r1 in blocks:
        h = _rms(x_ref[r0:r1, :], g_ref[...]).astype(bf16)
        bcu = jnp.dot(h, win_ref[...], preferred_element_type=f32)
        gates.append(bcu[:, :D])
        cu = bcu[:, D:2 * D] * bcu[:, 2 * D:]
        pad_ref[8 + r0:8 + r1, :] = cu
        cu_ref[r0:r1, :] = cu
    t = lax.broadcasted_iota(jnp.int32, (T // 2, 1), 0) & (TS - 1)
    is_sample = i == NT - 1
    cw = cw_ref[...]
    for (r0, r1), b in zip(blocks, gates):
        m1 = jnp.where(is_sample & (t == 0), p1_ref[r0:r1, :], pad_ref[7 + r0:7 + r1, :])
        m2 = jnp.where(is_sample & (t < 2), p2_ref[r0:r1, :], pad_ref[6 + r0:6 + r1, :])
        conv = cw[0:1] * m2 + cw[1:2] * m1 + cw[2:3] * pad_ref[8 + r0:8 + r1, :]
        y = jnp.dot((b * conv).astype(bf16), wout_ref[...], preferred_element_type=f32)
        y_ref[r0:r1, :] = x_ref[r0:r1, :] + y
    pad_ref[0:8, :] = pad_ref[T:T + 8, :]
    start(i + ahead)

    @pl.when(i == NT - 1)
    def _():
        for s in range(ahead):
            _wait_gather(i + 1 + s, tiles * UPT, src_ref, o_hbm, stage, sem)


def _conv(comb_src, x1, meta, o_sorted, g, win, cw, wout, p1, p2):
    const = dict(pipeline_mode=pl.Buffered(1))
    comb_src = jnp.concatenate([
        comb_src, jnp.full(((NSLOT - 1) * (T // TM) * UPT,), ZERO_UNIT_OUT, jnp.int32)])
    return pl.pallas_call(
        _conv_kernel,
        grid_spec=pltpu.PrefetchScalarGridSpec(
            num_scalar_prefetch=1,
            grid=(NT,),
            in_specs=[
                pl.BlockSpec((T, D), lambda i, src: (i, 0)),
                pl.BlockSpec((T, LANES), lambda i, src: (i, 0)),
                pl.BlockSpec(memory_space=pl.ANY),
                pl.BlockSpec((1, D), lambda i, src: (0, 0)),
                pl.BlockSpec((D, 3 * D), lambda i, src: (0, 0), **const),
                pl.BlockSpec((3, D), lambda i, src: (0, 0)),
                pl.BlockSpec((D, D), lambda i, src: (0, 0), **const),
                pl.BlockSpec((T, D), lambda i, src: (0, 0), **const),
                pl.BlockSpec((T, D), lambda i, src: (0, 0), **const),
            ],
            out_specs=[
                pl.BlockSpec((T, D), lambda i, src: (i, 0)),
                pl.BlockSpec((T, D), lambda i, src: (jnp.where(i == NT - 1, 1, 0), 0)),
            ],
            scratch_shapes=[
                pltpu.VMEM((T + 8, D), f32),
                pltpu.VMEM((T, D), f32),
                pltpu.VMEM((NSLOT, (T // TM) * RC, D), bf16),
                pltpu.SemaphoreType.DMA((NSLOT,)),
            ],
        ),
        out_shape=[
            jax.ShapeDtypeStruct((N, D), f32),
            jax.ShapeDtypeStruct((2 * T, D), f32),
        ],
        compiler_params=_params("arbitrary"),
        name="conv",
    )(comb_src, x1, meta, o_sorted, g, win, cw, wout, p1, p2)


def _rope_tables():
    inv_freq = THETA ** (-jnp.arange(HALF, dtype=f32) / HALF)
    pos = jnp.concatenate([
        jnp.arange(NP, dtype=jnp.int32),
        PAST + jnp.tile(jnp.arange(TS, dtype=jnp.int32), NB),
    ]).astype(f32)
    ang = inv_freq[:, None] * pos[None, :]
    return jnp.cos(ang), jnp.sin(ang)


def _router_weights(w_group, b_group, w_router, b_router):
    gap = ROW_E - NGRP
    pad = LANES - ROW_E - NE
    w = jnp.concatenate([w_group, jnp.zeros((D, gap), f32), w_router,
                         jnp.zeros((D, pad), f32)], axis=1)
    b = jnp.concatenate([b_group, jnp.zeros((gap,), f32), b_router,
                         jnp.zeros((pad,), f32)])[None, :]
    return w.astype(bf16), b


def _moe_experts(x, i, norm_ffn, w_group, b_group, w_router, b_router, w_gate, w_up, w_down):
    w, b = _router_weights(w_group[i], b_group[i], w_router[i], b_router[i])
    xc, meta, cnt = _dispatch(x, norm_ffn[i][None, :], w, b)
    eid, nused, ffn_src, nxt, comb_src = _dispatch_tables(cnt)
    o_sorted = _ffn(i, eid, nused, ffn_src, nxt, xc, w_gate, w_up, w_down)
    return comb_src, meta, o_sorted


def kernel(x_prompt, x_sample, cache_k, cache_v, state_conv, norm_mix, w_qkv, q_norm, k_norm,
           sinks, w_o, w_in, conv_w, w_out, norm_ffn, w_group, b_group, w_router, b_router,
           w_gate, w_up, w_down):
    xp = x_prompt.reshape(NP, D)
    xs = x_sample.reshape(NS, D)
    moe_w = (norm_ffn, w_group, b_group, w_router, b_router, w_gate, w_up, w_down)

    cos, sin = _rope_tables()
    (qT, ktok, vtok, vT), (q_s, ktok_s, vtok_s) = _qkv(
        xp, xs, norm_mix[0][None, :], w_qkv[0].T.astype(bf16), q_norm[0][:, None],
        k_norm[0][:, None], cos, sin)
    sink_rows = jnp.repeat(sinks[0].reshape(KVH, G), TQ, axis=1)[:, None, :]

    qs = q_s.reshape(KVH, G, HD, NB, TS).transpose(3, 0, 1, 4, 2)
    zq = jnp.zeros_like(qs[:, 0])
    qbd = jnp.stack([jnp.concatenate([qs[:, 0], zq], axis=-1),
                     jnp.concatenate([zq, qs[:, 1]], axis=-1)], axis=1)
    qbd = qbd.reshape(NB, H * TS, KVH * HD)
    k_new = ktok_s.reshape(NB, TS, KVH * HD)
    v_new = vtok_s.reshape(NB, TS, KVH * HD)
    pad4 = jnp.zeros((NB, 8 - TS, KVH * HD), f32)
    kc = cache_k[0].reshape(NB, WIN, KVH * HD)
    vc = cache_v[0].reshape(NB, WIN, KVH * HD)
    sink_col = jnp.repeat(sinks[0], TS)[:, None]
    o_s, kc_new, vc_new = _attn_sample(qbd, kc, vc, jnp.concatenate([k_new, pad4], axis=1),
                                       jnp.concatenate([v_new, pad4], axis=1), sink_col)
    o_s = o_s.reshape(NB, KVH, G, TS, KVH, HD)
    o_s = jnp.stack([o_s[:, 0, :, :, 0], o_s[:, 1, :, :, 1]], axis=1)
    o_s = o_s.transpose(0, 3, 1, 2, 4).reshape(NS, H * HD).astype(bf16)
    x = _attn_prompt(qT, ktok, vT, sink_rows, xp, xs, o_s, w_o[0])
    comb_src, meta, o_sorted = _moe_experts(x, 0, *moe_w)

    new_k_prompt = ktok[NP - WIN:NP].reshape(1, 1, WIN, KVH, HD)
    new_v_prompt = vtok[NP - WIN:NP].reshape(1, 1, WIN, KVH, HD)
    new_k_sample = kc_new.reshape(1, NB, WIN, KVH, HD)
    new_v_sample = vc_new.reshape(1, NB, WIN, KVH, HD)

    st = state_conv[0]
    z = jnp.zeros((NB, 1, D), f32)
    p1 = jnp.concatenate([st[:, 1:2], z, z, z], axis=1).reshape(NS, D)
    p2 = jnp.concatenate([st[:, 0:1], st[:, 1:2], z, z], axis=1).reshape(NS, D)
    x, cu = _conv(comb_src, x, meta, o_sorted, norm_mix[1][None, :], w_in[0].astype(bf16),
                  conv_w[0], w_out[0].astype(bf16), p1, p2)
    comb_src, meta, o_sorted = _moe_experts(x, 1, *moe_w)
    y_prompt, y_sample = _combine(comb_src, x, meta, o_sorted)

    new_conv_prompt = cu[T - 2:T].reshape(1, 1, 2, D)
    new_conv_sample = cu[T:].reshape(NB, TS, D)[:, TS - 2:][None]

    y_prompt = y_prompt.reshape(1, NP, D)
    y_sample = y_sample.reshape(NB, TS, D)
    return (y_prompt, y_sample, new_k_prompt, new_v_prompt, new_conv_prompt,
            new_k_sample, new_v_sample, new_conv_sample)
```
